```python
import jax, jax.numpy as jnp
from jax import lax
import numpy as np

D_MODEL = 2048
BATCH = 4
SEQ = 2048
DEPTH = 1
DEC_BATCH = 128
DEC_SEQ = 1
PAST_LEN = 16384
PAGE_SIZE = 128

POOL_WIDTH = D_MODEL // 2
POOL_GROUPS = 4
POOL_GROUP_DIM = POOL_WIDTH // POOL_GROUPS
POOL_WINDOWS = (2, 4, 8, 16)
POOL_STATE = max(POOL_WINDOWS) - 1
CONV_WIDTH = D_MODEL // 2
CONV_K = 3
CONV_STATE = CONV_K - 1
D_FF = ((8 * D_MODEL // 3 + 255) // 256) * 256
N_MOD = 9
IN_COLS = POOL_WIDTH + 3 * CONV_WIDTH + 2 * D_MODEL
EPS = 1e-6

kernel_name = "cond_macaron_pool_shortconv_decoder_step"


def rmsnorm(x, g):
    xf = x.astype(jnp.float32)
    inv = lax.rsqrt(jnp.mean(xf * xf, axis=-1, keepdims=True) + EPS)
    return (xf * inv).astype(x.dtype) * g


def modulate(x, g, shift, scale):
    return rmsnorm(x, g) * (1.0 + scale[:, None, :]) + shift[:, None, :]


def swiglu(h, w_gate, w_up, w_down):
    return (jax.nn.silu(h @ w_gate) * (h @ w_up)) @ w_down


def multiscale_pool(buf, n_new, pos0):
    L = buf.shape[1]
    start = L - n_new
    cs = jnp.cumsum(buf.astype(jnp.float32), axis=1)
    cs = jnp.concatenate([jnp.zeros_like(cs[:, :1]), cs], axis=1)
    idx = start + jnp.arange(n_new)
    pos = pos0 + jnp.arange(n_new)
    hi = cs[:, idx + 1]
    outs = []
    for g, w in enumerate(POOL_WINDOWS):
        sl = slice(g * POOL_GROUP_DIM, (g + 1) * POOL_GROUP_DIM)
        lo = cs[:, jnp.maximum(idx + 1 - w, 0), sl]
        cnt = jnp.minimum(pos + 1, w).astype(jnp.float32)[None, :, None]
        outs.append((hi[..., sl] - lo) / cnt)
    pooled = jnp.concatenate(outs, axis=-1).astype(buf.dtype)
    return pooled - buf[:, start:]


def causal_dwconv(buf, n_new, w, b):
    y = b
    for k in range(CONV_K):
        y = y + buf[:, k:k + n_new] * w[k]
    return y


def layer(x, c, pool_prefix, conv_prefix, pos0, w_ada, b_ada, norm1, ffn1_gate, ffn1_up, ffn1_down,
          norm2, w_in, pool_grp, pool_scale, w_branch_a, conv_w, conv_b, w_branch_b, w_o,
          norm3, ffn2_gate, ffn2_up, ffn2_down):
    B, T, _ = x.shape
    mod = jax.nn.silu(c) @ w_ada + b_ada
    sh1, sc1, g1, sh2, sc2, g2, sh3, sc3, g3 = jnp.split(mod, N_MOD, axis=-1)
    h = modulate(x, norm1, sh1, sc1)
    x = x + 0.5 * g1[:, None, :] * swiglu(h, ffn1_gate, ffn1_up, ffn1_down)
    h = modulate(x, norm2, sh2, sc2)
    z = h @ w_in
    p, cb, cc, ch, ga, gb = jnp.split(z, [POOL_WIDTH, POOL_WIDTH + CONV_WIDTH, POOL_WIDTH + 2 * CONV_WIDTH,
                                         POOL_WIDTH + 3 * CONV_WIDTH, POOL_WIDTH + 3 * CONV_WIDTH + D_MODEL], axis=-1)
    pool_buf = jnp.concatenate([pool_prefix, p], axis=1)
    a = multiscale_pool(pool_buf, T, pos0)
    a = jnp.einsum('btgc,gcd->btgd', a.reshape(B, T, POOL_GROUPS, POOL_GROUP_DIM), pool_grp)
    a = a.reshape(B, T, POOL_WIDTH) * pool_scale
    ua = a @ w_branch_a
    conv_buf = jnp.concatenate([conv_prefix, cc * ch], axis=1)
    v = cb * causal_dwconv(conv_buf, T, conv_w, conv_b)
    ub = v @ w_branch_b
    m = jax.nn.sigmoid(ga) * ua + jax.nn.sigmoid(gb) * ub
    x = x + g2[:, None, :] * (m @ w_o)
    h = modulate(x, norm3, sh3, sc3)
    x = x + 0.5 * g3[:, None, :] * swiglu(h, ffn2_gate, ffn2_up, ffn2_down)
    return x, pool_buf[:, -POOL_STATE:], conv_buf[:, -CONV_STATE:]


def setup_inputs(seed: int = 0) -> dict:
    key = jax.random.key(seed)
    ks = jax.random.split(key, 32)
    f32 = jnp.float32
    def nrm(k, shape, scale):
        return jax.random.normal(k, shape, f32) * scale
    L = DEPTH
    return {
        "x_prompt": nrm(ks[0], (BATCH, SEQ, D_MODEL), 1.0),
        "x_sample": nrm(ks[1], (DEC_BATCH, DEC_SEQ, D_MODEL), 1.0),
        "c_prompt": nrm(ks[2], (BATCH, D_MODEL), 1.0),
        "c_sample": nrm(ks[3], (DEC_BATCH, D_MODEL), 1.0),
        "state_pool": nrm(ks[4], (L, DEC_BATCH, POOL_STATE, POOL_WIDTH), 1.0),
        "state_conv": nrm(ks[5], (L, DEC_BATCH, CONV_STATE, CONV_WIDTH), 1.0),
        "w_ada": nrm(ks[6], (L, D_MODEL, N_MOD * D_MODEL), 0.5 * D_MODEL ** -0.5),
        "b_ada": nrm(ks[7], (L, N_MOD * D_MODEL), 0.02),
        "norm1": 1.0 + nrm(ks[8], (L, D_MODEL), 0.02),
        "ffn1_gate": nrm(ks[9], (L, D_MODEL, D_FF), D_MODEL ** -0.5),
        "ffn1_up": nrm(ks[10], (L, D_MODEL, D_FF), D_MODEL ** -0.5),
        "ffn1_down": nrm(ks[11], (L, D_FF, D_MODEL), D_FF ** -0.5),
        "norm2": 1.0 + nrm(ks[12], (L, D_MODEL), 0.02),
        "w_in": nrm(ks[13], (L, D_MODEL, IN_COLS), D_MODEL ** -0.5),
        "pool_grp": nrm(ks[14], (L, POOL_GROUPS, POOL_GROUP_DIM, POOL_GROUP_DIM), POOL_GROUP_DIM ** -0.5),
        "pool_scale": 1.0 + nrm(ks[15], (L, POOL_WIDTH), 0.02),
        "w_branch_a": nrm(ks[16], (L, POOL_WIDTH, D_MODEL), POOL_WIDTH ** -0.5),
        "conv_w": nrm(ks[17], (L, CONV_K, CONV_WIDTH), CONV_K ** -0.5),
        "conv_b": nrm(ks[18], (L, CONV_WIDTH), 0.02),
        "w_branch_b": nrm(ks[19], (L, CONV_WIDTH, D_MODEL), CONV_WIDTH ** -0.5),
        "w_o": nrm(ks[20], (L, D_MODEL, D_MODEL), D_MODEL ** -0.5),
        "norm3": 1.0 + nrm(ks[21], (L, D_MODEL), 0.02),
        "ffn2_gate": nrm(ks[22], (L, D_MODEL, D_FF), D_MODEL ** -0.5),
        "ffn2_up": nrm(ks[23], (L, D_MODEL, D_FF), D_MODEL ** -0.5),
        "ffn2_down": nrm(ks[24], (L, D_FF, D_MODEL), D_FF ** -0.5),
        "norm_final": 1.0 + nrm(ks[25], (D_MODEL,), 0.02),
    }


def reference(x_prompt, x_sample, c_prompt, c_sample, state_pool, state_conv, w_ada, b_ada, norm1,
              ffn1_gate, ffn1_up, ffn1_down, norm2, w_in, pool_grp, pool_scale, w_branch_a, conv_w, conv_b,
              w_branch_b, w_o, norm3, ffn2_gate, ffn2_up, ffn2_down, norm_final):
    xp, xs = x_prompt, x_sample
    pool_p, conv_p, pool_s, conv_s = [], [], [], []
    for d in range(DEPTH):
        params = (w_ada[d], b_ada[d], norm1[d], ffn1_gate[d], ffn1_up[d], ffn1_down[d], norm2[d], w_in[d],
                  pool_grp[d], pool_scale[d], w_branch_a[d], conv_w[d], conv_b[d], w_branch_b[d], w_o[d],
                  norm3[d], ffn2_gate[d], ffn2_up[d], ffn2_down[d])
        zp_pool = jnp.zeros((xp.shape[0], POOL_STATE, POOL_WIDTH), xp.dtype)
        zp_conv = jnp.zeros((xp.shape[0], CONV_STATE, CONV_WIDTH), xp.dtype)
        xp, np_, nc_ = layer(xp, c_prompt, zp_pool, zp_conv, 0, *params)
        xs, ns_, ncs_ = layer(xs, c_sample, state_pool[d], state_conv[d], PAST_LEN, *params)
        pool_p.append(np_); conv_p.append(nc_); pool_s.append(ns_); conv_s.append(ncs_)
    y_prompt = rmsnorm(xp, norm_final)
    y_sample = rmsnorm(xs, norm_final)
    new_pool_prompt = jnp.stack(pool_p, axis=0)
    new_conv_prompt = jnp.stack(conv_p, axis=0)
    new_pool_sample = jnp.stack(pool_s, axis=0)
    new_conv_sample = jnp.stack(conv_s, axis=0)
    return (y_prompt, y_sample, new_pool_prompt, new_conv_prompt, new_pool_sample, new_conv_sample)
```

```python
import functools

import jax
import jax.numpy as jnp
from jax import lax
from jax.experimental import pallas as pl
from jax.experimental.pallas import tpu as pltpu

F32 = jnp.float32
BF16 = jnp.bfloat16

D_MODEL = 2048
D_FF = 5632
POOL_WIDTH = 1024
CONV_WIDTH = 1024
POOL_WINDOWS = (2, 4, 8, 16)
POOL_GROUP_DIM = 256
POOL_STATE = 15
CONV_STATE = 2
N_MOD = 9
EPS = 1e-6
PAST_LEN = 16384

V7X_VMEM_BYTES = 64 * 1024 * 1024
SUBLANES = 8

ADA_TN = 1024
FFN_TF = 256
MIX_CH = 512
MIX_NB = 3
X_ROWS = 128
CARRY_P = 16
CARRY_Q = 8

COL_P, COL_CB, COL_CC, COL_CH, COL_GA, COL_GB = 0, 1024, 2048, 3072, 4096, 6144
MOD_SH1, MOD_SC1, MOD_G1, MOD_SH2, MOD_SC2, MOD_G2, MOD_SH3, MOD_SC3, MOD_G3 = range(9)


def _sigmoid(x):
    return 1.0 / (1.0 + jnp.exp(-x))


def _modulate(x, nrm, sc, sh):
    inv = lax.rsqrt(jnp.mean(x * x, axis=-1, keepdims=True) + EPS)
    return (x * inv) * nrm * (1.0 + sc) + sh


def _bdot(a, b):
    return jnp.dot(a, b, preferred_element_type=F32)


def _ada_kernel(c_ref, w_ref, b_ref, o_ref):
    c = c_ref[...]
    s = (c * _sigmoid(c)).astype(BF16)
    o_ref[...] = _bdot(s, w_ref[...].astype(BF16)) + b_ref[...]


def _ada(c_all, w_ada, b_ada):
    rows = c_all.shape[0]
    cols = w_ada.shape[1]
    return pl.pallas_call(
        _ada_kernel,
        grid=(cols // ADA_TN,),
        in_specs=[
            pl.BlockSpec((rows, D_MODEL), lambda j: (0, 0)),
            pl.BlockSpec((D_MODEL, ADA_TN), lambda j: (0, j)),
            pl.BlockSpec((1, ADA_TN), lambda j: (0, j)),
        ],
        out_specs=pl.BlockSpec((rows, ADA_TN), lambda j: (0, j)),
        out_shape=jax.ShapeDtypeStruct((rows, cols), F32),
        compiler_params=pltpu.CompilerParams(dimension_semantics=("arbitrary",)),
        name="ada_mod",
    )(c_all, w_ada, b_ada)


def _ffn_kernel(x_ref, sh_ref, sc_ref, gt_ref, nrm_ref, wg_ref, wu_ref, wd_ref, nf_ref, o_ref, h_ref,
                *, n_f, final_norm):
    f = pl.program_id(1)
    tm = x_ref.shape[0]
    rows_mod = sh_ref.shape[0]
    n_row_chunks = tm // X_ROWS

    def mod_rows(ref, rows):
        return ref[...] if rows_mod == 1 else ref[rows, :]

    @pl.when(f == 0)
    def _():
        @pl.loop(0, n_row_chunks)
        def _(r):
            rows = pl.ds(pl.multiple_of(r * X_ROWS, X_ROWS), X_ROWS)
            h_ref[rows, :] = _modulate(x_ref[rows, :], nrm_ref[...], mod_rows(sc_ref, rows),
                                       mod_rows(sh_ref, rows)).astype(BF16)
            o_ref[rows, :] = jnp.zeros((X_ROWS, D_MODEL), F32)

    h = h_ref[...]
    g = _bdot(h, wg_ref[...].astype(BF16))
    u = _bdot(h, wu_ref[...].astype(BF16))
    a = ((g * _sigmoid(g)) * u).astype(BF16)
    for half in range(2):
        hcols = pl.ds(half * (D_MODEL // 2), D_MODEL // 2)
        o_ref[:, hcols] += _bdot(a, wd_ref[:, hcols].astype(BF16))

    @pl.when(f == n_f - 1)
    def _():
        @pl.loop(0, n_row_chunks)
        def _(r):
            rows = pl.ds(pl.multiple_of(r * X_ROWS, X_ROWS), X_ROWS)
            xo = x_ref[rows, :] + 0.5 * mod_rows(gt_ref, rows) * o_ref[rows, :]
            if final_norm:
                inv = lax.rsqrt(jnp.mean(xo * xo, axis=-1, keepdims=True) + EPS)
                xo = (xo * inv) * nf_ref[...]
            o_ref[rows, :] = xo


def _ffn(x, mod3, j_sh, j_sc, j_g, nrm, wg, wu, wd, nf, *, tm, tps, final_norm, name):
    n = x.shape[0]
    n_f = D_FF // FFN_TF
    rows_mod = mod3.shape[1]

    def modspec(j):
        return pl.BlockSpec((None, rows_mod, D_MODEL), lambda i, f: (i // tps, 0, j))

    return pl.pallas_call(
        functools.partial(_ffn_kernel, n_f=n_f, final_norm=final_norm),
        grid=(n // tm, n_f),
        in_specs=[
            pl.BlockSpec((tm, D_MODEL), lambda i, f: (i, 0)),
            modspec(j_sh), modspec(j_sc), modspec(j_g),
            pl.BlockSpec((1, D_MODEL), lambda i, f: (0, 0)),
            pl.BlockSpec((D_MODEL, FFN_TF), lambda i, f: (0, f)),
            pl.BlockSpec((D_MODEL, FFN_TF), lambda i, f: (0, f)),
            pl.BlockSpec((FFN_TF, D_MODEL), lambda i, f: (f, 0)),
            pl.BlockSpec((1, D_MODEL), lambda i, f: (0, 0)),
        ],
        out_specs=pl.BlockSpec((tm, D_MODEL), lambda i, f: (i, 0)),
        out_shape=jax.ShapeDtypeStruct((n, D_MODEL), F32),
        scratch_shapes=[pltpu.VMEM((tm, D_MODEL), BF16)],
        compiler_params=pltpu.CompilerParams(
            dimension_semantics=("arbitrary", "arbitrary"),
            vmem_limit_bytes=V7X_VMEM_BYTES - 4 * 1024 * 1024),
        name=name,
    )(x, mod3, mod3, mod3, nrm, wg, wu, wd, nf)


_STAGES = ([("p", 0), ("p", 1)]
           + [(k, c) for c in range(CONV_WIDTH // MIX_CH) for k in ("cc", "ch", "cb")]
           + [(k, c) for c in range(D_MODEL // MIX_CH) for k in ("ga", "gb", "ab")]
           + [("o", c) for c in range(D_MODEL // MIX_CH)])
_N_STAGES = len(_STAGES)
_N_OUT_CHUNKS = D_MODEL // MIX_CH
_FIRST_O = _N_STAGES - _N_OUT_CHUNKS
_IN_COL0 = {"p": COL_P, "cb": COL_CB, "cc": COL_CC, "ch": COL_CH, "ga": COL_GA, "gb": COL_GB}
assert _N_STAGES % MIX_NB == 0


def _mixer_kernel(*refs, tm, tps, n_tiles, prompt, rows_mod):
    if prompt:
        (x1_hbm, x1c, sh, sc, g2c, nrm, grp, pscale, cw, cbias, w_in, w_a, w_b, w_o,
         x2, pst, cst,
         wbuf, wsem, xbuf, xsem, h2, ap, v, m, ybuf, sbuf, e0, e1, qext, pcar, qcar) = refs
    else:
        (x1_hbm, x1c, sh, sc, g2c, nrm, grp, pscale, cw, cbias, stp, stc, w_in, w_a, w_b, w_o,
         x2, pst, cst,
         wbuf, wsem, xbuf, xsem, h2, ap, v, m, ybuf, sbuf) = refs
    i = pl.program_id(0)
    c_step = pl.program_id(1)

    def wcopies(s):
        kind, c = _STAGES[s % _N_STAGES]
        slot = s % MIX_NB
        if kind == "ab":
            half = D_MODEL // 2
            return [
                pltpu.make_async_copy(w_a.at[:, pl.ds(c * MIX_CH, MIX_CH)],
                                      wbuf.at[slot, pl.ds(0, half), :], wsem.at[slot]),
                pltpu.make_async_copy(w_b.at[:, pl.ds(c * MIX_CH, MIX_CH)],
                                      wbuf.at[slot, pl.ds(half, half), :], wsem.at[slot]),
            ]
        if kind == "o":
            src = w_o.at[:, pl.ds(c * MIX_CH, MIX_CH)]
        else:
            src = w_in.at[:, pl.ds(_IN_COL0[kind] + c * MIX_CH, MIX_CH)]
        return [pltpu.make_async_copy(src, wbuf.at[slot], wsem.at[slot])]

    def wstart(s):
        for cp in wcopies(s):
            cp.start()

    def wwait(s):
        for cp in wcopies(s):
            cp.wait()

    def zchunk(s):
        return _bdot(h2[...], wbuf[s % MIX_NB].astype(BF16))

    def prologue():
        n_chunks = tm // X_ROWS

        def xcopy(r):
            return pltpu.make_async_copy(x1_hbm.at[pl.ds(i * tm + r * X_ROWS, X_ROWS), :],
                                         xbuf.at[r % 2], xsem.at[r % 2])

        xcopy(0).start()
        for r in range(n_chunks):
            if r + 1 < n_chunks:
                xcopy(r + 1).start()
            xcopy(r).wait()
            rows = pl.ds(r * X_ROWS, X_ROWS)
            msl = slice(None) if rows_mod == 1 else rows
            h2[rows, :] = _modulate(xbuf[r % 2], nrm[...], sc[msl, :], sh[msl, :]).astype(BF16)

    def pool_stage(c, z):
        for gl in range(MIX_CH // POOL_GROUP_DIM):
            g = c * (MIX_CH // POOL_GROUP_DIM) + gl
            w = POOL_WINDOWS[g]
            gcols = pl.ds(g * POOL_GROUP_DIM, POOL_GROUP_DIM)
            pg = z[:, gl * POOL_GROUP_DIM:(gl + 1) * POOL_GROUP_DIM]
            if prompt:
                pad = jnp.zeros((SUBLANES, POOL_GROUP_DIM), F32)
                e0[pl.ds(SUBLANES, SUBLANES), :] = pad
                e1[pl.ds(SUBLANES, SUBLANES), :] = pad
                e0[pl.ds(CARRY_P, CARRY_P), :] = pcar[:, gcols]
                e0[pl.ds(2 * CARRY_P, tm), :] = pg
                src, dst = e0, e1
                d = 1
                while d < w:
                    dst[pl.ds(CARRY_P, tm + CARRY_P), :] = (src[pl.ds(CARRY_P, tm + CARRY_P), :]
                                                            + src[pl.ds(CARRY_P - d, tm + CARRY_P), :])
                    src, dst = dst, src
                    d *= 2
                wsum = src[pl.ds(2 * CARRY_P, tm), :]
                pos = (i % tps) * tm + lax.broadcasted_iota(jnp.int32, (tm, 1), 0)
                cnt = jnp.minimum(pos + 1, w).astype(F32)
                a = wsum / cnt - pg
                tail = pg[tm - CARRY_P:, :]
                pcar[:, gcols] = tail
                pst[:, gcols] = tail
            else:
                wsum = pg
                for r in range(POOL_STATE + 1 - w, POOL_STATE):
                    wsum = wsum + stp[:, r, gcols]
                a = wsum / float(min(PAST_LEN + 1, w)) - pg
                pst[:, gcols] = pg
            ag = _bdot(a.astype(BF16), grp[g].astype(BF16)) * pscale[:, gcols]
            ap[:, gcols] = ag.astype(BF16)

    def conv_stage(c, z):
        ccols = pl.ds(c * MIX_CH, MIX_CH)
        q = ybuf[...] * z
        if prompt:
            qext[pl.ds(0, CARRY_Q), :] = qcar[:, ccols]
            qext[pl.ds(CARRY_Q, tm), :] = q
            y = cbias[:, ccols] + qext[pl.ds(CARRY_Q - 2, tm), :] * cw[0:1, ccols]
            y = y + qext[pl.ds(CARRY_Q - 1, tm), :] * cw[1:2, ccols]
            tail = q[tm - CARRY_Q:, :]
            qcar[:, ccols] = tail
            cst[:, ccols] = tail
        else:
            y = cbias[:, ccols] + stc[:, 0, ccols] * cw[0:1, ccols]
            y = y + stc[:, 1, ccols] * cw[1:2, ccols]
            cst[:, ccols] = q
        ybuf[...] = y + q * cw[2:3, ccols]

    def compute(s):
        kind, c = _STAGES[s]
        slot = s % MIX_NB
        ccols = pl.ds(c * MIX_CH, MIX_CH)
        if kind == "p":
            pool_stage(c, zchunk(s))
        elif kind == "cc":
            ybuf[...] = zchunk(s)
        elif kind == "ch":
            conv_stage(c, zchunk(s))
        elif kind == "cb":
            v[:, ccols] = (zchunk(s) * ybuf[...]).astype(BF16)
        elif kind == "ga":
            ybuf[...] = _sigmoid(zchunk(s))
        elif kind == "gb":
            sbuf[...] = _sigmoid(zchunk(s))
        elif kind == "ab":
            half = D_MODEL // 2
            ua = _bdot(ap[...], wbuf[slot, pl.ds(0, half), :].astype(BF16))
            ub = _bdot(v[...], wbuf[slot, pl.ds(half, half), :].astype(BF16))
            m[:, ccols] = (ybuf[...] * ua + sbuf[...] * ub).astype(BF16)
        else:
            x2[...] = x1c[...] + g2c[...] * zchunk_m(s)

    def zchunk_m(s):
        return _bdot(m[...], wbuf[s % MIX_NB].astype(BF16))

    @pl.when(c_step == 0)
    def _():
        @pl.when(i == 0)
        def _():
            for s in range(MIX_NB - 1):
                wstart(s)

        if prompt:
            @pl.when(i % tps == 0)
            def _():
                pcar[...] = jnp.zeros(pcar.shape, F32)
                qcar[...] = jnp.zeros(qcar.shape, F32)

        prologue()
        for s in range(_FIRST_O + 1):
            wwait(s)
            wstart(s + MIX_NB - 1)
            compute(s)

    for oc in range(1, _N_OUT_CHUNKS):
        @pl.when(c_step == oc)
        def _(oc=oc):
            s = _FIRST_O + oc
            wwait(s)
            nxt = s + MIX_NB - 1
            if nxt < _N_STAGES:
                wstart(nxt)
            else:
                @pl.when(i < n_tiles - 1)
                def _():
                    wstart(nxt)
            compute(s)


def _mixer(x1, mod3, nrm, grp, pscale, cw, cbias, w_in, w_a, w_b, w_o, state=None, *, tm, tps):
    n = x1.shape[0]
    n_tiles = n // tm
    prompt = state is None
    rows_mod = mod3.shape[1]
    n_seq = n_tiles // tps
    any_spec = pl.BlockSpec(memory_space=pl.ANY)

    def const(shape):
        return pl.BlockSpec(shape, lambda i, c: (0,) * len(shape))

    g2_blk = (MOD_G2 * D_MODEL) // MIX_CH
    in_specs = [
        any_spec,
        pl.BlockSpec((tm, MIX_CH), lambda i, c: (i, c)),
        pl.BlockSpec((None, rows_mod, D_MODEL), lambda i, c: (i // tps, 0, MOD_SH2)),
        pl.BlockSpec((None, rows_mod, D_MODEL), lambda i, c: (i // tps, 0, MOD_SC2)),
        pl.BlockSpec((None, rows_mod, MIX_CH), lambda i, c: (i // tps, 0, g2_blk + c)),
        const((1, D_MODEL)),
        const((len(POOL_WINDOWS), POOL_GROUP_DIM, POOL_GROUP_DIM)),
        const((1, POOL_WIDTH)),
        const((3, CONV_WIDTH)),
        const((1, CONV_WIDTH)),
    ]
    args = [x1, x1, mod3, mod3, mod3, nrm, grp, pscale, cw, cbias]
    if not prompt:
        stp, stc = state
        in_specs += [const(stp.shape), const(stc.shape)]
        args += [stp, stc]
    in_specs += [any_spec] * 4
    args += [w_in, w_a, w_b, w_o]

    if prompt:
        st_shapes = [jax.ShapeDtypeStruct((n_seq, CARRY_P, POOL_WIDTH), F32),
                     jax.ShapeDtypeStruct((n_seq, CARRY_Q, CONV_WIDTH), F32)]
        st_specs = [pl.BlockSpec((None, CARRY_P, POOL_WIDTH), lambda i, c: (i // tps, 0, 0)),
                    pl.BlockSpec((None, CARRY_Q, CONV_WIDTH), lambda i, c: (i // tps, 0, 0))]
    else:
        st_shapes = [jax.ShapeDtypeStruct((n, POOL_WIDTH), F32), jax.ShapeDtypeStruct((n, CONV_WIDTH), F32)]
        st_specs = [const((n, POOL_WIDTH)), const((n, CONV_WIDTH))]

    scratch = [
        pltpu.VMEM((MIX_NB, D_MODEL, MIX_CH), F32),
        pltpu.SemaphoreType.DMA((MIX_NB,)),
        pltpu.VMEM((2, X_ROWS, D_MODEL), F32),
        pltpu.SemaphoreType.DMA((2,)),
        pltpu.VMEM((tm, D_MODEL), BF16),
        pltpu.VMEM((tm, POOL_WIDTH), BF16),
        pltpu.VMEM((tm, CONV_WIDTH), BF16),
        pltpu.VMEM((tm, D_MODEL), BF16),
        pltpu.VMEM((tm, MIX_CH), F32),
        pltpu.VMEM((tm, MIX_CH), F32),
    ]
    if prompt:
        scratch += [
            pltpu.VMEM((tm + 2 * CARRY_P, POOL_GROUP_DIM), F32),
            pltpu.VMEM((tm + 2 * CARRY_P, POOL_GROUP_DIM), F32),
            pltpu.VMEM((tm + CARRY_Q, MIX_CH), F32),
            pltpu.VMEM((CARRY_P, POOL_WIDTH), F32),
            pltpu.VMEM((CARRY_Q, CONV_WIDTH), F32),
        ]

    return pl.pallas_call(
        functools.partial(_mixer_kernel, tm=tm, tps=tps, n_tiles=n_tiles, prompt=prompt, rows_mod=rows_mod),
        grid=(n_tiles, _N_OUT_CHUNKS),
        in_specs=in_specs,
        out_specs=[pl.BlockSpec((tm, MIX_CH), lambda i, c: (i, c))] + st_specs,
        out_shape=[jax.ShapeDtypeStruct((n, D_MODEL), F32)] + st_shapes,
        scratch_shapes=scratch,
        compiler_params=pltpu.CompilerParams(
            dimension_semantics=("arbitrary", "arbitrary"),
            vmem_limit_bytes=V7X_VMEM_BYTES - 4 * 1024 * 1024),
        name="mixer_prompt" if prompt else "mixer_sample",
    )(*args)


def _layer(x, mod3, params, state, *, tm, tps, tag):
    (norm1, f1g, f1u, f1d, norm2, w_in, grp, pscale, w_a, cw, cbias, w_b, w_o, norm3, f2g, f2u, f2d, nf) = params
    x1 = _ffn(x, mod3, MOD_SH1, MOD_SC1, MOD_G1, norm1, f1g, f1u, f1d, nf,
              tm=tm, tps=tps, final_norm=False, name="ffn1_" + tag)
    x2, pst, cst = _mixer(x1, mod3, norm2, grp, pscale, cw, cbias, w_in, w_a, w_b, w_o, state, tm=tm, tps=tps)
    y = _ffn(x2, mod3, MOD_SH3, MOD_SC3, MOD_G3, norm3, f2g, f2u, f2d, nf,
             tm=tm, tps=tps, final_norm=True, name="ffn2_" + tag)
    return y, pst, cst


def kernel(x_prompt, x_sample, c_prompt, c_sample, state_pool, state_conv, w_ada, b_ada, norm1, ffn1_gate, ffn1_up, ffn1_down, norm2, w_in, pool_grp, pool_scale, w_branch_a, conv_w, conv_b, w_branch_b, w_o, norm3, ffn2_gate, ffn2_up, ffn2_down, norm_final):
    assert w_ada.shape[0] == 1, "single-layer step"
    batch, seq, d = x_prompt.shape
    dec_batch = x_sample.shape[0]
    assert d == D_MODEL and x_sample.shape[1] == 1

    n_c = batch + dec_batch
    pad = (-n_c) % SUBLANES
    c_all = jnp.concatenate([c_prompt, c_sample, jnp.zeros((pad, d), F32)], axis=0)
    mod = _ada(c_all, w_ada[0], b_ada)
    mod_p = mod[:batch].reshape(batch, 1, N_MOD * d)
    mod_s = mod[batch:n_c].reshape(1, dec_batch, N_MOD * d)

    row = lambda a: a.reshape(1, -1)
    params = (row(norm1[0]), ffn1_gate[0], ffn1_up[0], ffn1_down[0], row(norm2[0]), w_in[0], pool_grp[0],
              row(pool_scale[0]), w_branch_a[0], conv_w[0], row(conv_b[0]), w_branch_b[0], w_o[0],
              row(norm3[0]), ffn2_gate[0], ffn2_up[0], ffn2_down[0], row(norm_final))

    tm_p = 1024
    yp, pst_p, cst_p = _layer(x_prompt.reshape(batch * seq, d), mod_p, params, None,
                              tm=tm_p, tps=seq // tm_p, tag="prompt")
    ys, p_s, q_s = _layer(x_sample.reshape(dec_batch, d), mod_s, params, (state_pool[0], state_conv[0]),
                          tm=dec_batch, tps=1, tag="sample")

    y_prompt = yp.reshape(batch, seq, d)
    y_sample = ys.reshape(dec_batch, 1, d)
    new_pool_prompt = pst_p[None, :, CARRY_P - POOL_STATE:, :]
    new_conv_prompt = cst_p[None, :, CARRY_Q - CONV_STATE:, :]
    new_pool_sample = jnp.concatenate([state_pool[:, :, 1:, :], p_s[None, :, None, :]], axis=2)
    new_conv_sample = jnp.concatenate([state_conv[:, :, 1:, :], q_s[None, :, None, :]], axis=2)
    return (y_prompt, y_sample, new_pool_prompt, new_conv_prompt, new_pool_sample, new_conv_sample)
```

```python
import functools

import jax
import jax.numpy as jnp
from jax import lax
from jax.experimental import pallas as pl
from jax.experimental.pallas import tpu as pltpu

F32 = jnp.float32
BF16 = jnp.bfloat16

D_MODEL = 2048
D_FF = 5632
POOL_WIDTH = 1024
CONV_WIDTH = 1024
POOL_WINDOWS = (2, 4, 8, 16)
POOL_GROUP_DIM = 256
POOL_STATE = 15
CONV_STATE = 2
N_MOD = 9
EPS = 1e-6
PAST_LEN = 16384

V7X_VMEM_BYTES = 64 * 1024 * 1024
SUBLANES = 8

ADA_TN = 1024
FFN_TF = 256
MIX_CH = 512
MIX_NB = 3
X_ROWS = 128
FFN_ROWS = 128
X_BUFS = 4
MIX_DMA_SPLIT = 8
CARRY_P = 16
CARRY_Q = 8

COL_P, COL_CB, COL_CC, COL_CH, COL_GA, COL_GB = 0, 1024, 2048, 3072, 4096, 6144
MOD_SH1, MOD_SC1, MOD_G1, MOD_SH2, MOD_SC2, MOD_G2, MOD_SH3, MOD_SC3, MOD_G3 = range(9)


def _sigmoid(x):
    return 1.0 / (1.0 + jnp.exp(-x))


def _modulate(x, nrm, sc, sh):
    inv = lax.rsqrt(jnp.mean(x * x, axis=-1, keepdims=True) + EPS)
    return (x * inv) * nrm * (1.0 + sc) + sh


def _bdot(a, b):
    return jnp.dot(a, b, preferred_element_type=F32)


def _ada_kernel(c_ref, w_ref, b_ref, o_ref):
    c = c_ref[...]
    s = (c * _sigmoid(c)).astype(BF16)
    o_ref[...] = _bdot(s, w_ref[...].astype(BF16)) + b_ref[...]


def _ada(c_all, w_ada, b_ada):
    rows = c_all.shape[0]
    cols = w_ada.shape[1]
    return pl.pallas_call(
        _ada_kernel,
        grid=(cols // ADA_TN,),
        in_specs=[
            pl.BlockSpec((rows, D_MODEL), lambda j: (0, 0)),
            pl.BlockSpec((D_MODEL, ADA_TN), lambda j: (0, j)),
            pl.BlockSpec((1, ADA_TN), lambda j: (0, j)),
        ],
        out_specs=pl.BlockSpec((rows, ADA_TN), lambda j: (0, j)),
        out_shape=jax.ShapeDtypeStruct((rows, cols), F32),
        compiler_params=pltpu.CompilerParams(dimension_semantics=("arbitrary",)),
        name="ada_mod",
    )(c_all, w_ada, b_ada)


def _ffn_kernel(x_ref, sh_ref, sc_ref, gt_ref, nrm_ref, wg_ref, wu_ref, wd_ref, nf_ref, o_ref, h_ref,
                *, n_f, final_norm):
    f = pl.program_id(1)
    tm = x_ref.shape[0]
    rows_mod = sh_ref.shape[0]
    n_row_chunks = tm // FFN_ROWS

    def mod_rows(ref, rows):
        return ref[...] if rows_mod == 1 else ref[rows, :]

    @pl.when(f == 0)
    def _():
        @pl.loop(0, n_row_chunks)
        def _(r):
            rows = pl.ds(pl.multiple_of(r * FFN_ROWS, FFN_ROWS), FFN_ROWS)
            h_ref[rows, :] = _modulate(x_ref[rows, :], nrm_ref[...], mod_rows(sc_ref, rows),
                                       mod_rows(sh_ref, rows)).astype(BF16)
            o_ref[rows, :] = jnp.zeros((FFN_ROWS, D_MODEL), F32)

    h = h_ref[...]
    g = _bdot(h, wg_ref[...].astype(BF16))
    u = _bdot(h, wu_ref[...].astype(BF16))
    a = ((g * _sigmoid(g)) * u).astype(BF16)
    for half in range(2):
        hcols = pl.ds(half * (D_MODEL // 2), D_MODEL // 2)
        o_ref[:, hcols] += _bdot(a, wd_ref[:, hcols].astype(BF16))

    @pl.when(f == n_f - 1)
    def _():
        @pl.loop(0, n_row_chunks)
        def _(r):
            rows = pl.ds(pl.multiple_of(r * FFN_ROWS, FFN_ROWS), FFN_ROWS)
            xo = x_ref[rows, :] + 0.5 * mod_rows(gt_ref, rows) * o_ref[rows, :]
            if final_norm:
                inv = lax.rsqrt(jnp.mean(xo * xo, axis=-1, keepdims=True) + EPS)
                xo = (xo * inv) * nf_ref[...]
            o_ref[rows, :] = xo


def _ffn(x, mod3, j_sh, j_sc, j_g, nrm, wg, wu, wd, nf, *, tm, tps, final_norm, name):
    n = x.shape[0]
    n_f = D_FF // FFN_TF
    rows_mod = mod3.shape[1]

    def modspec(j):
        return pl.BlockSpec((None, rows_mod, D_MODEL), lambda i, f: (i // tps, 0, j))

    return pl.pallas_call(
        functools.partial(_ffn_kernel, n_f=n_f, final_norm=final_norm),
        grid=(n // tm, n_f),
        in_specs=[
            pl.BlockSpec((tm, D_MODEL), lambda i, f: (i, 0)),
            modspec(j_sh), modspec(j_sc), modspec(j_g),
            pl.BlockSpec((1, D_MODEL), lambda i, f: (0, 0)),
            pl.BlockSpec((D_MODEL, FFN_TF), lambda i, f: (0, f)),
            pl.BlockSpec((D_MODEL, FFN_TF), lambda i, f: (0, f)),
            pl.BlockSpec((FFN_TF, D_MODEL), lambda i, f: (f, 0)),
            pl.BlockSpec((1, D_MODEL), lambda i, f: (0, 0)),
        ],
        out_specs=pl.BlockSpec((tm, D_MODEL), lambda i, f: (i, 0)),
        out_shape=jax.ShapeDtypeStruct((n, D_MODEL), F32),
        scratch_shapes=[pltpu.VMEM((tm, D_MODEL), BF16)],
        compiler_params=pltpu.CompilerParams(
            dimension_semantics=("arbitrary", "arbitrary"),
            vmem_limit_bytes=V7X_VMEM_BYTES - 4 * 1024 * 1024),
        name=name,
    )(x, mod3, mod3, mod3, nrm, wg, wu, wd, nf)


_STAGES = ([("p", 0), ("p", 1)]
           + [(k, c) for c in range(CONV_WIDTH // MIX_CH) for k in ("cc", "ch", "cb")]
           + [(k, c) for c in range(D_MODEL // MIX_CH) for k in ("ga", "gb", "ab")]
           + [("o", c) for c in range(D_MODEL // MIX_CH)])
_N_STAGES = len(_STAGES)
_N_OUT_CHUNKS = D_MODEL // MIX_CH
_FIRST_O = _N_STAGES - _N_OUT_CHUNKS
_IN_COL0 = {"p": COL_P, "cb": COL_CB, "cc": COL_CC, "ch": COL_CH, "ga": COL_GA, "gb": COL_GB}
assert _N_STAGES % MIX_NB == 0


def _mixer_kernel(*refs, tm, tps, n_tiles, prompt, rows_mod):
    if prompt:
        (x1_hbm, x1c, sh, sc, g2c, nrm, grp, pscale, cw, cbias, w_in, w_a, w_b, w_o,
         x2, pst, cst,
         wbuf, wsem, xbuf, xsem, h2, ap, v, m, ybuf, sbuf, e0, e1, qext, pcar, qcar) = refs
    else:
        (x1_hbm, x1c, sh, sc, g2c, nrm, grp, pscale, cw, cbias, stp, stc, w_in, w_a, w_b, w_o,
         x2, pst, cst,
         wbuf, wsem, xbuf, xsem, h2, ap, v, m, ybuf, sbuf) = refs
    i = pl.program_id(0)
    c_step = pl.program_id(1)

    def wcopies(s):
        kind, c = _STAGES[s % _N_STAGES]
        slot = s % MIX_NB
        cols = pl.ds(_IN_COL0.get(kind, 0) + c * MIX_CH, MIX_CH)
        slab = D_MODEL // MIX_DMA_SPLIT
        copies = []
        for k in range(MIX_DMA_SPLIT):
            r0 = k * slab
            if kind == "ab":
                src, s0 = (w_a, r0) if r0 < D_MODEL // 2 else (w_b, r0 - D_MODEL // 2)
            else:
                src, s0 = (w_o if kind == "o" else w_in), r0
            copies.append(pltpu.make_async_copy(src.at[pl.ds(s0, slab), cols],
                                                wbuf.at[slot, pl.ds(r0, slab), :], wsem.at[slot]))
        return copies

    def wstart(s):
        for cp in wcopies(s):
            cp.start()

    def wwait(s):
        for cp in wcopies(s):
            cp.wait()

    def zchunk(s):
        return _bdot(h2[...], wbuf[s % MIX_NB].astype(BF16))

    n_xchunks = tm // X_ROWS

    def xcopy(tile, r):
        return pltpu.make_async_copy(x1_hbm.at[pl.ds(tile * tm + r * X_ROWS, X_ROWS), :],
                                     xbuf.at[r % X_BUFS], xsem.at[r % X_BUFS])

    def xprefetch(tile):
        for r in range(min(X_BUFS, n_xchunks)):
            xcopy(tile, r).start()

    def prologue():
        @pl.when(i == 0)
        def _():
            xprefetch(0)

        for r in range(n_xchunks):
            xcopy(i, r).wait()
            rows = pl.ds(r * X_ROWS, X_ROWS)
            msl = slice(None) if rows_mod == 1 else rows
            h2[rows, :] = _modulate(xbuf[r % X_BUFS], nrm[...], sc[msl, :], sh[msl, :]).astype(BF16)
            if r + X_BUFS < n_xchunks:
                xcopy(i, r + X_BUFS).start()

    def pool_stage(c, z):
        for gl in range(MIX_CH // POOL_GROUP_DIM):
            g = c * (MIX_CH // POOL_GROUP_DIM) + gl
            w = POOL_WINDOWS[g]
            gcols = pl.ds(g * POOL_GROUP_DIM, POOL_GROUP_DIM)
            pg = z[:, gl * POOL_GROUP_DIM:(gl + 1) * POOL_GROUP_DIM]
            if prompt:
                pad = jnp.zeros((SUBLANES, POOL_GROUP_DIM), F32)
                e0[pl.ds(SUBLANES, SUBLANES), :] = pad
                e1[pl.ds(SUBLANES, SUBLANES), :] = pad
                e0[pl.ds(CARRY_P, CARRY_P), :] = pcar[:, gcols]
                e0[pl.ds(2 * CARRY_P, tm), :] = pg
                src, dst = e0, e1
                d = 1
                while d < w:
                    dst[pl.ds(CARRY_P, tm + CARRY_P), :] = (src[pl.ds(CARRY_P, tm + CARRY_P), :]
                                                            + src[pl.ds(CARRY_P - d, tm + CARRY_P), :])
                    src, dst = dst, src
                    d *= 2
                wsum = src[pl.ds(2 * CARRY_P, tm), :]
                pos = (i % tps) * tm + lax.broadcasted_iota(jnp.int32, (tm, 1), 0)
                cnt = jnp.minimum(pos + 1, w).astype(F32)
                a = wsum / cnt - pg
                tail = pg[tm - CARRY_P:, :]
                pcar[:, gcols] = tail
                pst[:, gcols] = tail
            else:
                wsum = pg
                for r in range(POOL_STATE + 1 - w, POOL_STATE):
                    wsum = wsum + stp[:, r, gcols]
                a = wsum / float(min(PAST_LEN + 1, w)) - pg
                pst[:, gcols] = pg
            ag = _bdot(a.astype(BF16), grp[g].astype(BF16)) * pscale[:, gcols]
            ap[:, gcols] = ag.astype(BF16)

    def conv_stage(c, z):
        ccols = pl.ds(c * MIX_CH, MIX_CH)
        q = ybuf[...] * z
        if prompt:
            qext[pl.ds(0, CARRY_Q), :] = qcar[:, ccols]
            qext[pl.ds(CARRY_Q, tm), :] = q
            y = cbias[:, ccols] + qext[pl.ds(CARRY_Q - 2, tm), :] * cw[0:1, ccols]
            y = y + qext[pl.ds(CARRY_Q - 1, tm), :] * cw[1:2, ccols]
            tail = q[tm - CARRY_Q:, :]
            qcar[:, ccols] = tail
            cst[:, ccols] = tail
        else:
            y = cbias[:, ccols] + stc[:, 0, ccols] * cw[0:1, ccols]
            y = y + stc[:, 1, ccols] * cw[1:2, ccols]
            cst[:, ccols] = q
        ybuf[...] = y + q * cw[2:3, ccols]

    def compute(s):
        kind, c = _STAGES[s]
        slot = s % MIX_NB
        ccols = pl.ds(c * MIX_CH, MIX_CH)
        if kind == "p":
            pool_stage(c, zchunk(s))
        elif kind == "cc":
            ybuf[...] = zchunk(s)
        elif kind == "ch":
            conv_stage(c, zchunk(s))
        elif kind == "cb":
            v[:, ccols] = (zchunk(s) * ybuf[...]).astype(BF16)
        elif kind == "ga":
            ybuf[...] = _sigmoid(zchunk(s))
        elif kind == "gb":
            sbuf[...] = _sigmoid(zchunk(s))
        elif kind == "ab":
            half = D_MODEL // 2
            ua = _bdot(ap[...], wbuf[slot, pl.ds(0, half), :].astype(BF16))
            ub = _bdot(v[...], wbuf[slot, pl.ds(half, half), :].astype(BF16))
            m[:, ccols] = (ybuf[...] * ua + sbuf[...] * ub).astype(BF16)
        else:
            x2[...] = x1c[...] + g2c[...] * zchunk_m(s)

    def zchunk_m(s):
        return _bdot(m[...], wbuf[s % MIX_NB].astype(BF16))

    @pl.when(c_step == 0)
    def _():
        @pl.when(i == 0)
        def _():
            for s in range(MIX_NB - 1):
                wstart(s)

        if prompt:
            @pl.when(i % tps == 0)
            def _():
                pcar[...] = jnp.zeros(pcar.shape, F32)
                qcar[...] = jnp.zeros(qcar.shape, F32)

        prologue()
        for s in range(_FIRST_O + 1):
            wwait(s)
            wstart(s + MIX_NB - 1)
            compute(s)

    for oc in range(1, _N_OUT_CHUNKS):
        @pl.when(c_step == oc)
        def _(oc=oc):
            s = _FIRST_O + oc
            if oc == 1:
                @pl.when(i < n_tiles - 1)
                def _():
                    xprefetch(i + 1)
            wwait(s)
            nxt = s + MIX_NB - 1
            if nxt < _N_STAGES:
                wstart(nxt)
            else:
                @pl.when(i < n_tiles - 1)
                def _():
                    wstart(nxt)
            compute(s)


def _mixer(x1, mod3, nrm, grp, pscale, cw, cbias, w_in, w_a, w_b, w_o, state=None, *, tm, tps):
    n = x1.shape[0]
    n_tiles = n // tm
    prompt = state is None
    rows_mod = mod3.shape[1]
    n_seq = n_tiles // tps
    any_spec = pl.BlockSpec(memory_space=pl.ANY)

    def const(shape):
        return pl.BlockSpec(shape, lambda i, c: (0,) * len(shape))

    g2_blk = (MOD_G2 * D_MODEL) // MIX_CH
    in_specs = [
        any_spec,
        pl.BlockSpec((tm, MIX_CH), lambda i, c: (i, c)),
        pl.BlockSpec((None, rows_mod, D_MODEL), lambda i, c: (i // tps, 0, MOD_SH2)),
        pl.BlockSpec((None, rows_mod, D_MODEL), lambda i, c: (i // tps, 0, MOD_SC2)),
        pl.BlockSpec((None, rows_mod, MIX_CH), lambda i, c: (i // tps, 0, g2_blk + c)),
        const((1, D_MODEL)),
        const((len(POOL_WINDOWS), POOL_GROUP_DIM, POOL_GROUP_DIM)),
        const((1, POOL_WIDTH)),
        const((3, CONV_WIDTH)),
        const((1, CONV_WIDTH)),
    ]
    args = [x1, x1, mod3, mod3, mod3, nrm, grp, pscale, cw, cbias]
    if not prompt:
        stp, stc = state
        in_specs += [const(stp.shape), const(stc.shape)]
        args += [stp, stc]
    in_specs += [any_spec] * 4
    args += [w_in, w_a, w_b, w_o]

    if prompt:
        st_shapes = [jax.ShapeDtypeStruct((n_seq, CARRY_P, POOL_WIDTH), F32),
                     jax.ShapeDtypeStruct((n_seq, CARRY_Q, CONV_WIDTH), F32)]
        st_specs = [pl.BlockSpec((None, CARRY_P, POOL_WIDTH), lambda i, c: (i // tps, 0, 0)),
                    pl.BlockSpec((None, CARRY_Q, CONV_WIDTH), lambda i, c: (i // tps, 0, 0))]
    else:
        st_shapes = [jax.ShapeDtypeStruct((n, POOL_WIDTH), F32), jax.ShapeDtypeStruct((n, CONV_WIDTH), F32)]
        st_specs = [const((n, POOL_WIDTH)), const((n, CONV_WIDTH))]

    scratch = [
        pltpu.VMEM((MIX_NB, D_MODEL, MIX_CH), F32),
        pltpu.SemaphoreType.DMA((MIX_NB,)),
        pltpu.VMEM((X_BUFS, X_ROWS, D_MODEL), F32),
        pltpu.SemaphoreType.DMA((X_BUFS,)),
        pltpu.VMEM((tm, D_MODEL), BF16),
        pltpu.VMEM((tm, POOL_WIDTH), BF16),
        pltpu.VMEM((tm, CONV_WIDTH), BF16),
        pltpu.VMEM((tm, D_MODEL), BF16),
        pltpu.VMEM((tm, MIX_CH), F32),
        pltpu.VMEM((tm, MIX_CH), F32),
    ]
    if prompt:
        scratch += [
            pltpu.VMEM((tm + 2 * CARRY_P, POOL_GROUP_DIM), F32),
            pltpu.VMEM((tm + 2 * CARRY_P, POOL_GROUP_DIM), F32),
            pltpu.VMEM((tm + CARRY_Q, MIX_CH), F32),
            pltpu.VMEM((CARRY_P, POOL_WIDTH), F32),
            pltpu.VMEM((CARRY_Q, CONV_WIDTH), F32),
        ]

    return pl.pallas_call(
        functools.partial(_mixer_kernel, tm=tm, tps=tps, n_tiles=n_tiles, prompt=prompt, rows_mod=rows_mod),
        grid=(n_tiles, _N_OUT_CHUNKS),
        in_specs=in_specs,
        out_specs=[pl.BlockSpec((tm, MIX_CH), lambda i, c: (i, c))] + st_specs,
        out_shape=[jax.ShapeDtypeStruct((n, D_MODEL), F32)] + st_shapes,
        scratch_shapes=scratch,
        compiler_params=pltpu.CompilerParams(
            dimension_semantics=("arbitrary", "arbitrary"),
            vmem_limit_bytes=V7X_VMEM_BYTES - 4 * 1024 * 1024),
        name="mixer_prompt" if prompt else "mixer_sample",
    )(*args)


def _layer(x, mod3, params, state, *, tm, tps, tag):
    (norm1, f1g, f1u, f1d, norm2, w_in, grp, pscale, w_a, cw, cbias, w_b, w_o, norm3, f2g, f2u, f2d, nf) = params
    x1 = _ffn(x, mod3, MOD_SH1, MOD_SC1, MOD_G1, norm1, f1g, f1u, f1d, nf,
              tm=tm, tps=tps, final_norm=False, name="ffn1_" + tag)
    x2, pst, cst = _mixer(x1, mod3, norm2, grp, pscale, cw, cbias, w_in, w_a, w_b, w_o, state, tm=tm, tps=tps)
    y = _ffn(x2, mod3, MOD_SH3, MOD_SC3, MOD_G3, norm3, f2g, f2u, f2d, nf,
             tm=tm, tps=tps, final_norm=True, name="ffn2_" + tag)
    return y, pst, cst


def kernel(x_prompt, x_sample, c_prompt, c_sample, state_pool, state_conv, w_ada, b_ada, norm1, ffn1_gate, ffn1_up, ffn1_down, norm2, w_in, pool_grp, pool_scale, w_branch_a, conv_w, conv_b, w_branch_b, w_o, norm3, ffn2_gate, ffn2_up, ffn2_down, norm_final):
    assert w_ada.shape[0] == 1, "single-layer step"
    batch, seq, d = x_prompt.shape
    dec_batch = x_sample.shape[0]
    assert d == D_MODEL and x_sample.shape[1] == 1

    n_c = batch + dec_batch
    pad = (-n_c) % SUBLANES
    c_all = jnp.concatenate([c_prompt, c_sample, jnp.zeros((pad, d), F32)], axis=0)
    mod = _ada(c_all, w_ada[0], b_ada)
    mod_p = mod[:batch].reshape(batch, 1, N_MOD * d)
    mod_s = mod[batch:n_c].reshape(1, dec_batch, N_MOD * d)

    row = lambda a: a.reshape(1, -1)
    params = (row(norm1[0]), ffn1_gate[0], ffn1_up[0], ffn1_down[0], row(norm2[0]), w_in[0], pool_grp[0],
              row(pool_scale[0]), w_branch_a[0], conv_w[0], row(conv_b[0]), w_branch_b[0], w_o[0],
              row(norm3[0]), ffn2_gate[0], ffn2_up[0], ffn2_down[0], row(norm_final))

    tm_p = 1024
    yp, pst_p, cst_p = _layer(x_prompt.reshape(batch * seq, d), mod_p, params, None,
                              tm=tm_p, tps=seq // tm_p, tag="prompt")
    ys, p_s, q_s = _layer(x_sample.reshape(dec_batch, d), mod_s, params, (state_pool[0], state_conv[0]),
                          tm=dec_batch, tps=1, tag="sample")

    y_prompt = yp.reshape(batch, seq, d)
    y_sample = ys.reshape(dec_batch, 1, d)
    new_pool_prompt = pst_p[None, :, CARRY_P - POOL_STATE:, :]
    new_conv_prompt = cst_p[None, :, CARRY_Q - CONV_STATE:, :]
    new_pool_sample = jnp.concatenate([state_pool[:, :, 1:, :], p_s[None, :, None, :]], axis=2)
    new_conv_sample = jnp.concatenate([state_conv[:, :, 1:, :], q_s[None, :, None, :]], axis=2)
    return (y_prompt, y_sample, new_pool_prompt, new_conv_prompt, new_pool_sample, new_conv_sample)
```

```python
import functools

import jax
import jax.numpy as jnp
from jax import lax
from jax.experimental import pallas as pl
from jax.experimental.pallas import tpu as pltpu

F32 = jnp.float32
BF16 = jnp.bfloat16

D_MODEL = 2048
D_FF = 5632
POOL_WIDTH = 1024
CONV_WIDTH = 1024
POOL_WINDOWS = (2, 4, 8, 16)
POOL_GROUP_DIM = 256
POOL_STATE = 15
CONV_STATE = 2
N_MOD = 9
EPS = 1e-6
PAST_LEN = 16384

V7X_VMEM_BYTES = 64 * 1024 * 1024
SUBLANES = 8

ADA_TN = 1024
FFN_TF = 256
MIX_CH = 512
MIX_NB = 3
X_ROWS = 128
FFN_ROWS = 128
X_BUFS = 4
MIX_DMA_SPLIT = 8
CARRY_P = 16
CARRY_Q = 8

COL_P, COL_CB, COL_CC, COL_CH, COL_GA, COL_GB = 0, 1024, 2048, 3072, 4096, 6144
MOD_SH1, MOD_SC1, MOD_G1, MOD_SH2, MOD_SC2, MOD_G2, MOD_SH3, MOD_SC3, MOD_G3 = range(9)


def _sigmoid(x):
    return 1.0 / (1.0 + jnp.exp(-x))


def _modulate(x, nrm, sc, sh):
    inv = lax.rsqrt(jnp.mean(x * x, axis=-1, keepdims=True) + EPS)
    return (x * inv) * nrm * (1.0 + sc) + sh


def _bdot(a, b):
    return jnp.dot(a, b, preferred_element_type=F32)


def _ada_kernel(c_ref, w_ref, b_ref, o_ref):
    c = c_ref[...]
    s = (c * _sigmoid(c)).astype(BF16)
    o_ref[...] = _bdot(s, w_ref[...].astype(BF16)) + b_ref[...]


def _ada(c_all, w_ada, b_ada):
    rows = c_all.shape[0]
    cols = w_ada.shape[1]
    return pl.pallas_call(
        _ada_kernel,
        grid=(cols // ADA_TN,),
        in_specs=[
            pl.BlockSpec((rows, D_MODEL), lambda j: (0, 0)),
            pl.BlockSpec((D_MODEL, ADA_TN), lambda j: (0, j)),
            pl.BlockSpec((1, ADA_TN), lambda j: (0, j)),
        ],
        out_specs=pl.BlockSpec((rows, ADA_TN), lambda j: (0, j)),
        out_shape=jax.ShapeDtypeStruct((rows, cols), F32),
        compiler_params=pltpu.CompilerParams(dimension_semantics=("arbitrary",)),
        name="ada_mod",
    )(c_all, w_ada, b_ada)


def _ffn_kernel(x_ref, sh_ref, sc_ref, gt_ref, nrm_ref, wg_ref, wu_ref, wd_ref, nf_ref, o_ref, h_ref,
                *, n_f, final_norm):
    f = pl.program_id(1)
    tm = x_ref.shape[0]
    rows_mod = sh_ref.shape[0]
    n_row_chunks = tm // FFN_ROWS

    def mod_rows(ref, rows):
        return ref[...] if rows_mod == 1 else ref[rows, :]

    @pl.when(f == 0)
    def _():
        @pl.loop(0, n_row_chunks)
        def _(r):
            rows = pl.ds(pl.multiple_of(r * FFN_ROWS, FFN_ROWS), FFN_ROWS)
            h_ref[rows, :] = _modulate(x_ref[rows, :], nrm_ref[...], mod_rows(sc_ref, rows),
                                       mod_rows(sh_ref, rows)).astype(BF16)
            o_ref[rows, :] = jnp.zeros((FFN_ROWS, D_MODEL), F32)

    h = h_ref[...]
    g = _bdot(h, wg_ref[...].astype(BF16))
    u = _bdot(h, wu_ref[...].astype(BF16))
    a = ((g * _sigmoid(g)) * u).astype(BF16)
    for half in range(2):
        hcols = pl.ds(half * (D_MODEL // 2), D_MODEL // 2)
        o_ref[:, hcols] += _bdot(a, wd_ref[:, hcols].astype(BF16))

    @pl.when(f == n_f - 1)
    def _():
        @pl.loop(0, n_row_chunks)
        def _(r):
            rows = pl.ds(pl.multiple_of(r * FFN_ROWS, FFN_ROWS), FFN_ROWS)
            xo = x_ref[rows, :] + 0.5 * mod_rows(gt_ref, rows) * o_ref[rows, :]
            if final_norm:
                inv = lax.rsqrt(jnp.mean(xo * xo, axis=-1, keepdims=True) + EPS)
                xo = (xo * inv) * nf_ref[...]
            o_ref[rows, :] = xo


def _ffn(x, mod3, j_sh, j_sc, j_g, nrm, wg, wu, wd, nf, *, tm, tps, final_norm, name):
    n = x.shape[0]
    n_f = D_FF // FFN_TF
    rows_mod = mod3.shape[1]

    def modspec(j):
        return pl.BlockSpec((None, rows_mod, D_MODEL), lambda i, f: (i // tps, 0, j))

    return pl.pallas_call(
        functools.partial(_ffn_kernel, n_f=n_f, final_norm=final_norm),
        grid=(n // tm, n_f),
        in_specs=[
            pl.BlockSpec((tm, D_MODEL), lambda i, f: (i, 0)),
            modspec(j_sh), modspec(j_sc), modspec(j_g),
            pl.BlockSpec((1, D_MODEL), lambda i, f: (0, 0)),
            pl.BlockSpec((D_MODEL, FFN_TF), lambda i, f: (0, f)),
            pl.BlockSpec((D_MODEL, FFN_TF), lambda i, f: (0, f)),
            pl.BlockSpec((FFN_TF, D_MODEL), lambda i, f: (f, 0)),
            pl.BlockSpec((1, D_MODEL), lambda i, f: (0, 0)),
        ],
        out_specs=pl.BlockSpec((tm, D_MODEL), lambda i, f: (i, 0)),
        out_shape=jax.ShapeDtypeStruct((n, D_MODEL), F32),
        scratch_shapes=[pltpu.VMEM((tm, D_MODEL), BF16)],
        compiler_params=pltpu.CompilerParams(
            dimension_semantics=("arbitrary", "arbitrary"),
            vmem_limit_bytes=V7X_VMEM_BYTES - 4 * 1024 * 1024),
        name=name,
    )(x, mod3, mod3, mod3, nrm, wg, wu, wd, nf)


K_P, K_CC, K_CH, K_CB, K_GA, K_GB, K_AB, K_O = range(8)
_N_CONV_CHUNKS = CONV_WIDTH // MIX_CH
_N_OUT_CHUNKS = D_MODEL // MIX_CH
_S_CONV = POOL_WIDTH // MIX_CH
_S_GATE = _S_CONV + 3 * _N_CONV_CHUNKS
_S_OUT = _S_GATE + 3 * _N_OUT_CHUNKS
_N_STAGES = _S_OUT + _N_OUT_CHUNKS
_KIND_COL0 = (COL_P, COL_CC, COL_CH, COL_CB, COL_GA, COL_GB, 0, 0)
assert _N_STAGES % MIX_NB == 0


def _decode_stage(t):
    u1 = t - _S_CONV
    u2 = t - _S_GATE
    c1 = lax.div(jnp.maximum(u1, 0), 3)
    c2 = lax.div(jnp.maximum(u2, 0), 3)
    kind = jnp.where(t < _S_CONV, K_P,
                     jnp.where(t < _S_GATE, K_CC + (u1 - 3 * c1),
                               jnp.where(t < _S_OUT, K_GA + (u2 - 3 * c2), K_O)))
    chunk = jnp.where(t < _S_CONV, t, jnp.where(t < _S_GATE, c1, jnp.where(t < _S_OUT, c2, t - _S_OUT)))
    col0 = jnp.int32(0)
    for k, base in enumerate(_KIND_COL0):
        if base:
            col0 = jnp.where(kind == k, base, col0)
    return kind, chunk, col0 + chunk * MIX_CH


def _mixer_kernel(*refs, tm, tps, n_tiles, prompt, rows_mod):
    if prompt:
        (x1_hbm, x1c, sh, sc, g2c, nrm, grp, pscale, cw, cbias, w_in, w_a, w_b, w_o,
         x2, pst, cst,
         wbuf, wsem, xbuf, xsem, h2, ap, v, m, zbuf, ybuf, sbuf, e0, e1, qext, pcar, qcar) = refs
    else:
        (x1_hbm, x1c, sh, sc, g2c, nrm, grp, pscale, cw, cbias, stp, stc, w_in, w_a, w_b, w_o,
         x2, pst, cst,
         wbuf, wsem, xbuf, xsem, h2, ap, v, m, zbuf, ybuf, sbuf) = refs
    i = pl.program_id(0)
    s = pl.program_id(1)
    kind, chunk, _ = _decode_stage(s)
    slot = lax.rem(s, MIX_NB)
    slab = D_MODEL // MIX_DMA_SPLIT
    half = D_MODEL // 2

    def wcopy(src, src_row, col, dst_slot, dst_row):
        return pltpu.make_async_copy(src.at[pl.ds(src_row, slab), pl.ds(col, MIX_CH)],
                                     wbuf.at[dst_slot, pl.ds(dst_row, slab), :], wsem.at[dst_slot])

    def wstart(t):
        k_t, _, col_t = _decode_stage(t)
        col_t = pl.multiple_of(col_t, MIX_CH)
        slot_t = lax.rem(t, MIX_NB)

        @pl.when(k_t < K_AB)
        def _():
            for k in range(MIX_DMA_SPLIT):
                wcopy(w_in, k * slab, col_t, slot_t, k * slab).start()

        @pl.when(k_t == K_AB)
        def _():
            for k in range(MIX_DMA_SPLIT):
                r0 = k * slab
                src, s0 = (w_a, r0) if r0 < half else (w_b, r0 - half)
                wcopy(src, s0, col_t, slot_t, r0).start()

        @pl.when(k_t == K_O)
        def _():
            for k in range(MIX_DMA_SPLIT):
                wcopy(w_o, k * slab, col_t, slot_t, k * slab).start()

    def wwait():
        for k in range(MIX_DMA_SPLIT):
            wcopy(w_in, k * slab, 0, slot, k * slab).wait()

    n_xchunks = tm // X_ROWS

    def xcopy(tile, r):
        return pltpu.make_async_copy(x1_hbm.at[pl.ds(tile * tm + r * X_ROWS, X_ROWS), :],
                                     xbuf.at[r % X_BUFS], xsem.at[r % X_BUFS])

    def xprefetch(tile):
        for r in range(min(X_BUFS, n_xchunks)):
            xcopy(tile, r).start()

    def prologue():
        for r in range(n_xchunks):
            xcopy(i, r).wait()
            rows = pl.ds(r * X_ROWS, X_ROWS)
            msl = slice(None) if rows_mod == 1 else rows
            h2[rows, :] = _modulate(xbuf[r % X_BUFS], nrm[...], sc[msl, :], sh[msl, :]).astype(BF16)
            if r + X_BUFS < n_xchunks:
                xcopy(i, r + X_BUFS).start()

    def pool_stage(c):
        for gl in range(MIX_CH // POOL_GROUP_DIM):
            g = c * (MIX_CH // POOL_GROUP_DIM) + gl
            w = POOL_WINDOWS[g]
            gcols = pl.ds(g * POOL_GROUP_DIM, POOL_GROUP_DIM)
            pg = zbuf[:, pl.ds(gl * POOL_GROUP_DIM, POOL_GROUP_DIM)]
            if prompt:
                pad = jnp.zeros((SUBLANES, POOL_GROUP_DIM), F32)
                e0[pl.ds(SUBLANES, SUBLANES), :] = pad
                e1[pl.ds(SUBLANES, SUBLANES), :] = pad
                e0[pl.ds(CARRY_P, CARRY_P), :] = pcar[:, gcols]
                e0[pl.ds(2 * CARRY_P, tm), :] = pg
                src, dst = e0, e1
                d = 1
                while d < w:
                    dst[pl.ds(CARRY_P, tm + CARRY_P), :] = (src[pl.ds(CARRY_P, tm + CARRY_P), :]
                                                            + src[pl.ds(CARRY_P - d, tm + CARRY_P), :])
                    src, dst = dst, src
                    d *= 2
                wsum = src[pl.ds(2 * CARRY_P, tm), :]
                pos = (i % tps) * tm + lax.broadcasted_iota(jnp.int32, (tm, 1), 0)
                cnt = jnp.minimum(pos + 1, w).astype(F32)
                a = wsum / cnt - pg
                tail = pg[tm - CARRY_P:, :]
                pcar[:, gcols] = tail
                pst[:, gcols] = tail
            else:
                wsum = pg
                for r in range(POOL_STATE + 1 - w, POOL_STATE):
                    wsum = wsum + stp[:, r, gcols]
                a = wsum / float(min(PAST_LEN + 1, w)) - pg
                pst[:, gcols] = pg
            ag = _bdot(a.astype(BF16), grp[g].astype(BF16)) * pscale[:, gcols]
            ap[g] = ag.astype(BF16)

    def conv_stage(c):
        ccols = pl.ds(c * MIX_CH, MIX_CH)
        q = ybuf[...] * zbuf[...]
        if prompt:
            qext[pl.ds(0, CARRY_Q), :] = qcar[:, ccols]
            qext[pl.ds(CARRY_Q, tm), :] = q
            y = cbias[:, ccols] + qext[pl.ds(CARRY_Q - 2, tm), :] * cw[0:1, ccols]
            y = y + qext[pl.ds(CARRY_Q - 1, tm), :] * cw[1:2, ccols]
            tail = q[tm - CARRY_Q:, :]
            qcar[:, ccols] = tail
            cst[:, ccols] = tail
        else:
            y = cbias[:, ccols] + stc[:, 0, ccols] * cw[0:1, ccols]
            y = y + stc[:, 1, ccols] * cw[1:2, ccols]
            cst[:, ccols] = q
        ybuf[...] = y + q * cw[2:3, ccols]

    @pl.when(s == 0)
    def _():
        @pl.when(i == 0)
        def _():
            for t in range(MIX_NB - 1):
                wstart(jnp.int32(t))
            xprefetch(0)

        if prompt:
            @pl.when(i % tps == 0)
            def _():
                pcar[...] = jnp.zeros(pcar.shape, F32)
                qcar[...] = jnp.zeros(qcar.shape, F32)

        prologue()

    wwait()
    nxt = s + (MIX_NB - 1)

    @pl.when(nxt < _N_STAGES)
    def _():
        wstart(nxt)

    @pl.when(jnp.logical_and(nxt >= _N_STAGES, i < n_tiles - 1))
    def _():
        wstart(nxt - _N_STAGES)

    @pl.when(jnp.logical_and(s == _S_OUT + 1, i < n_tiles - 1))
    def _():
        xprefetch(i + 1)

    @pl.when(kind < K_AB)
    def _():
        zbuf[...] = _bdot(h2[...], wbuf[slot].astype(BF16))

    for c in range(POOL_WIDTH // MIX_CH):
        @pl.when(jnp.logical_and(kind == K_P, chunk == c))
        def _(c=c):
            pool_stage(c)

    @pl.when(kind == K_CC)
    def _():
        ybuf[...] = zbuf[...]

    for c in range(_N_CONV_CHUNKS):
        @pl.when(jnp.logical_and(kind == K_CH, chunk == c))
        def _(c=c):
            conv_stage(c)

    @pl.when(kind == K_CB)
    def _():
        v[chunk] = (zbuf[...] * ybuf[...]).astype(BF16)

    @pl.when(kind == K_GA)
    def _():
        ybuf[...] = _sigmoid(zbuf[...])

    @pl.when(kind == K_GB)
    def _():
        sbuf[...] = _sigmoid(zbuf[...])

    @pl.when(kind == K_AB)
    def _():
        ua = None
        for g in range(len(POOL_WINDOWS)):
            part = _bdot(ap[g], wbuf[slot, pl.ds(g * POOL_GROUP_DIM, POOL_GROUP_DIM), :].astype(BF16))
            ua = part if ua is None else ua + part
        ub = None
        for k in range(_N_CONV_CHUNKS):
            part = _bdot(v[k], wbuf[slot, pl.ds(half + k * MIX_CH, MIX_CH), :].astype(BF16))
            ub = part if ub is None else ub + part
        m[chunk] = (ybuf[...] * ua + sbuf[...] * ub).astype(BF16)

    @pl.when(kind == K_O)
    def _():
        acc = None
        for k in range(_N_OUT_CHUNKS):
            part = _bdot(m[k], wbuf[slot, pl.ds(k * MIX_CH, MIX_CH), :].astype(BF16))
            acc = part if acc is None else acc + part
        x2[...] = x1c[...] + g2c[...] * acc


def _mixer(x1, mod3, nrm, grp, pscale, cw, cbias, w_in, w_a, w_b, w_o, state=None, *, tm, tps):
    n = x1.shape[0]
    n_tiles = n // tm
    prompt = state is None
    rows_mod = mod3.shape[1]
    n_seq = n_tiles // tps
    any_spec = pl.BlockSpec(memory_space=pl.ANY)

    def const(shape):
        return pl.BlockSpec(shape, lambda i, s: (0,) * len(shape))

    def out_chunk(s):
        return jnp.maximum(s - _S_OUT, 0)

    g2_blk = (MOD_G2 * D_MODEL) // MIX_CH
    in_specs = [
        any_spec,
        pl.BlockSpec((tm, MIX_CH), lambda i, s: (i, out_chunk(s))),
        pl.BlockSpec((None, rows_mod, D_MODEL), lambda i, s: (i // tps, 0, MOD_SH2)),
        pl.BlockSpec((None, rows_mod, D_MODEL), lambda i, s: (i // tps, 0, MOD_SC2)),
        pl.BlockSpec((None, rows_mod, MIX_CH), lambda i, s: (i // tps, 0, g2_blk + out_chunk(s))),
        const((1, D_MODEL)),
        const((len(POOL_WINDOWS), POOL_GROUP_DIM, POOL_GROUP_DIM)),
        const((1, POOL_WIDTH)),
        const((3, CONV_WIDTH)),
        const((1, CONV_WIDTH)),
    ]
    args = [x1, x1, mod3, mod3, mod3, nrm, grp, pscale, cw, cbias]
    if not prompt:
        stp, stc = state
        in_specs += [const(stp.shape), const(stc.shape)]
        args += [stp, stc]
    in_specs += [any_spec] * 4
    args += [w_in, w_a, w_b, w_o]

    if prompt:
        st_shapes = [jax.ShapeDtypeStruct((n_seq, CARRY_P, POOL_WIDTH), F32),
                     jax.ShapeDtypeStruct((n_seq, CARRY_Q, CONV_WIDTH), F32)]
        st_specs = [pl.BlockSpec((None, CARRY_P, POOL_WIDTH), lambda i, s: (i // tps, 0, 0)),
                    pl.BlockSpec((None, CARRY_Q, CONV_WIDTH), lambda i, s: (i // tps, 0, 0))]
    else:
        st_shapes = [jax.ShapeDtypeStruct((n, POOL_WIDTH), F32), jax.ShapeDtypeStruct((n, CONV_WIDTH), F32)]
        st_specs = [const((n, POOL_WIDTH)), const((n, CONV_WIDTH))]

    scratch = [
        pltpu.VMEM((MIX_NB, D_MODEL, MIX_CH), F32),
        pltpu.SemaphoreType.DMA((MIX_NB,)),
        pltpu.VMEM((X_BUFS, X_ROWS, D_MODEL), F32),
        pltpu.SemaphoreType.DMA((X_BUFS,)),
        pltpu.VMEM((tm, D_MODEL), BF16),
        pltpu.VMEM((len(POOL_WINDOWS), tm, POOL_GROUP_DIM), BF16),
        pltpu.VMEM((_N_CONV_CHUNKS, tm, MIX_CH), BF16),
        pltpu.VMEM((_N_OUT_CHUNKS, tm, MIX_CH), BF16),
        pltpu.VMEM((tm, MIX_CH), F32),
        pltpu.VMEM((tm, MIX_CH), F32),
        pltpu.VMEM((tm, MIX_CH), F32),
    ]
    if prompt:
        scratch += [
            pltpu.VMEM((tm + 2 * CARRY_P, POOL_GROUP_DIM), F32),
            pltpu.VMEM((tm + 2 * CARRY_P, POOL_GROUP_DIM), F32),
            pltpu.VMEM((tm + CARRY_Q, MIX_CH), F32),
            pltpu.VMEM((CARRY_P, POOL_WIDTH), F32),
            pltpu.VMEM((CARRY_Q, CONV_WIDTH), F32),
        ]

    return pl.pallas_call(
        functools.partial(_mixer_kernel, tm=tm, tps=tps, n_tiles=n_tiles, prompt=prompt, rows_mod=rows_mod),
        grid=(n_tiles, _N_STAGES),
        in_specs=in_specs,
        out_specs=[pl.BlockSpec((tm, MIX_CH), lambda i, s: (i, out_chunk(s)))] + st_specs,
        out_shape=[jax.ShapeDtypeStruct((n, D_MODEL), F32)] + st_shapes,
        scratch_shapes=scratch,
        compiler_params=pltpu.CompilerParams(
            dimension_semantics=("arbitrary", "arbitrary"),
            vmem_limit_bytes=V7X_VMEM_BYTES - 4 * 1024 * 1024),
        name="mixer_prompt" if prompt else "mixer_sample",
    )(*args)


def _layer(x, mod3, params, state, *, tm, tps, tag):
    (norm1, f1g, f1u, f1d, norm2, w_in, grp, pscale, w_a, cw, cbias, w_b, w_o, norm3, f2g, f2u, f2d, nf) = params
    x1 = _ffn(x, mod3, MOD_SH1, MOD_SC1, MOD_G1, norm1, f1g, f1u, f1d, nf,
              tm=tm, tps=tps, final_norm=False, name="ffn1_" + tag)
    x2, pst, cst = _mixer(x1, mod3, norm2, grp, pscale, cw, cbias, w_in, w_a, w_b, w_o, state, tm=tm, tps=tps)
    y = _ffn(x2, mod3, MOD_SH3, MOD_SC3, MOD_G3, norm3, f2g, f2u, f2d, nf,
             tm=tm, tps=tps, final_norm=True, name="ffn2_" + tag)
    return y, pst, cst


def kernel(x_prompt, x_sample, c_prompt, c_sample, state_pool, state_conv, w_ada, b_ada, norm1, ffn1_gate, ffn1_up, ffn1_down, norm2, w_in, pool_grp, pool_scale, w_branch_a, conv_w, conv_b, w_branch_b, w_o, norm3, ffn2_gate, ffn2_up, ffn2_down, norm_final):
    assert w_ada.shape[0] == 1, "single-layer step"
    batch, seq, d = x_prompt.shape
    dec_batch = x_sample.shape[0]
    assert d == D_MODEL and x_sample.shape[1] == 1

    n_c = batch + dec_batch
    pad = (-n_c) % SUBLANES
    c_all = jnp.concatenate([c_prompt, c_sample, jnp.zeros((pad, d), F32)], axis=0)
    mod = _ada(c_all, w_ada[0], b_ada)
    mod_p = mod[:batch].reshape(batch, 1, N_MOD * d)
    mod_s = mod[batch:n_c].reshape(1, dec_batch, N_MOD * d)

    row = lambda a: a.reshape(1, -1)
    params = (row(norm1[0]), ffn1_gate[0], ffn1_up[0], ffn1_down[0], row(norm2[0]), w_in[0], pool_grp[0],
              row(pool_scale[0]), w_branch_a[0], conv_w[0], row(conv_b[0]), w_branch_b[0], w_o[0],
              row(norm3[0]), ffn2_gate[0], ffn2_up[0], ffn2_down[0], row(norm_final))

    tm_p = 1024
    yp, pst_p, cst_p = _layer(x_prompt.reshape(batch * seq, d), mod_p, params, None,
                              tm=tm_p, tps=seq // tm_p, tag="prompt")
    ys, p_s, q_s = _layer(x_sample.reshape(dec_batch, d), mod_s, params, (state_pool[0], state_conv[0]),
                          tm=dec_batch, tps=1, tag="sample")

    y_prompt = yp.reshape(batch, seq, d)
    y_sample = ys.reshape(dec_batch, 1, d)
    new_pool_prompt = pst_p[None, :, CARRY_P - POOL_STATE:, :]
    new_conv_prompt = cst_p[None, :, CARRY_Q - CONV_STATE:, :]
    new_pool_sample = jnp.concatenate([state_pool[:, :, 1:, :], p_s[None, :, None, :]], axis=2)
    new_conv_sample = jnp.concatenate([state_conv[:, :, 1:, :], q_s[None, :, None, :]], axis=2)
    return (y_prompt, y_sample, new_pool_prompt, new_conv_prompt, new_pool_sample, new_conv_sample)
```

```python
import functools

import jax
import jax.numpy as jnp
from jax import lax
from jax.experimental import pallas as pl
from jax.experimental.pallas import tpu as pltpu

F32 = jnp.float32
BF16 = jnp.bfloat16

D_MODEL = 2048
D_FF = 5632
POOL_WIDTH = 1024
CONV_WIDTH = 1024
POOL_WINDOWS = (2, 4, 8, 16)
POOL_GROUP_DIM = 256
POOL_STATE = 15
CONV_STATE = 2
N_MOD = 9
EPS = 1e-6
PAST_LEN = 16384

V7X_VMEM_BYTES = 64 * 1024 * 1024
SUBLANES = 8
VMEM_LIMIT_BYTES = V7X_VMEM_BYTES - 2 * 1024 * 1024

TM_PROMPT = 1024

ADA_TN = 1024
FFN_TF = 256
MIX_CH = 512
MIX_NB = 3
X_ROWS = 128
FFN_ROWS = 128
X_BUFS = 4
MIX_DMA_SPLIT = 8
CARRY_P = 16
CARRY_Q = 8

COL_P, COL_CB, COL_CC, COL_CH, COL_GA, COL_GB = 0, 1024, 2048, 3072, 4096, 6144
MOD_SH1, MOD_SC1, MOD_G1, MOD_SH2, MOD_SC2, MOD_G2, MOD_SH3, MOD_SC3, MOD_G3 = range(9)


def _sigmoid(x):
    return 1.0 / (1.0 + jnp.exp(-x))


def _modulate(x, nrm, sc, sh):
    inv = lax.rsqrt(jnp.mean(x * x, axis=-1, keepdims=True) + EPS)
    return (x * inv) * nrm * (1.0 + sc) + sh


def _bdot(a, b):
    return jnp.dot(a, b, preferred_element_type=F32)


def _ada_kernel(c_ref, w_ref, b_ref, o_ref):
    c = c_ref[...]
    s = (c * _sigmoid(c)).astype(BF16)
    o_ref[...] = _bdot(s, w_ref[...].astype(BF16)) + b_ref[...]


def _ada(c_all, w_ada, b_ada):
    rows = c_all.shape[0]
    cols = w_ada.shape[1]
    return pl.pallas_call(
        _ada_kernel,
        grid=(cols // ADA_TN,),
        in_specs=[
            pl.BlockSpec((rows, D_MODEL), lambda j: (0, 0)),
            pl.BlockSpec((D_MODEL, ADA_TN), lambda j: (0, j)),
            pl.BlockSpec((1, ADA_TN), lambda j: (0, j)),
        ],
        out_specs=pl.BlockSpec((rows, ADA_TN), lambda j: (0, j)),
        out_shape=jax.ShapeDtypeStruct((rows, cols), F32),
        compiler_params=pltpu.CompilerParams(dimension_semantics=("arbitrary",)),
        name="ada_mod",
    )(c_all, w_ada, b_ada)


def _ffn_kernel(xp_ref, xs_ref, shp, scp, gtp, shs, scs, gts, nrm_ref, wg_ref, wu_ref, wd_ref, nf_ref,
                op_ref, os_ref, h_ref, *, n_f, final_norm):
    f = pl.program_id(1)
    tm = xp_ref.shape[0]
    ts = xs_ref.shape[0]

    def head(x, sc, sh):
        return _modulate(x, nrm_ref[...], sc, sh).astype(BF16)

    def tail(x, gt, acc):
        xo = x + 0.5 * gt * acc
        if final_norm:
            inv = lax.rsqrt(jnp.mean(xo * xo, axis=-1, keepdims=True) + EPS)
            xo = (xo * inv) * nf_ref[...]
        return xo

    @pl.when(f == 0)
    def _():
        @pl.loop(0, tm // FFN_ROWS)
        def _(r):
            rows = pl.ds(pl.multiple_of(r * FFN_ROWS, FFN_ROWS), FFN_ROWS)
            h_ref[rows, :] = head(xp_ref[rows, :], scp[...], shp[...])
            op_ref[rows, :] = jnp.zeros((FFN_ROWS, D_MODEL), F32)

        h_ref[pl.ds(tm, ts), :] = head(xs_ref[...], scs[...], shs[...])
        os_ref[...] = jnp.zeros((ts, D_MODEL), F32)

    h = h_ref[...]
    g = _bdot(h, wg_ref[...].astype(BF16))
    u = _bdot(h, wu_ref[...].astype(BF16))
    a = ((g * _sigmoid(g)) * u).astype(BF16)
    for half in range(2):
        hcols = pl.ds(half * (D_MODEL // 2), D_MODEL // 2)
        y = _bdot(a, wd_ref[:, hcols].astype(BF16))
        op_ref[:, hcols] += y[:tm]
        os_ref[:, hcols] += y[tm:]

    @pl.when(f == n_f - 1)
    def _():
        @pl.loop(0, tm // FFN_ROWS)
        def _(r):
            rows = pl.ds(pl.multiple_of(r * FFN_ROWS, FFN_ROWS), FFN_ROWS)
            op_ref[rows, :] = tail(xp_ref[rows, :], gtp[...], op_ref[rows, :])

        os_ref[...] = tail(xs_ref[...], gts[...], os_ref[...])


def _ffn(xp, xs, mod_p3, mod_s, j_sh, j_sc, j_g, nrm, wg, wu, wd, nf, *, tm, ts, tps, final_norm, name):
    n_tiles = xp.shape[0] // tm
    assert xs.shape[0] == n_tiles * ts
    n_f = D_FF // FFN_TF

    def pspec(j):
        return pl.BlockSpec((None, 1, D_MODEL), lambda i, f: (i // tps, 0, j))

    def sspec(j):
        return pl.BlockSpec((ts, D_MODEL), lambda i, f: (i, j))

    return pl.pallas_call(
        functools.partial(_ffn_kernel, n_f=n_f, final_norm=final_norm),
        grid=(n_tiles, n_f),
        in_specs=[
            pl.BlockSpec((tm, D_MODEL), lambda i, f: (i, 0)),
            pl.BlockSpec((ts, D_MODEL), lambda i, f: (i, 0)),
            pspec(j_sh), pspec(j_sc), pspec(j_g),
            sspec(j_sh), sspec(j_sc), sspec(j_g),
            pl.BlockSpec((1, D_MODEL), lambda i, f: (0, 0)),
            pl.BlockSpec((D_MODEL, FFN_TF), lambda i, f: (0, f)),
            pl.BlockSpec((D_MODEL, FFN_TF), lambda i, f: (0, f)),
            pl.BlockSpec((FFN_TF, D_MODEL), lambda i, f: (f, 0)),
            pl.BlockSpec((1, D_MODEL), lambda i, f: (0, 0)),
        ],
        out_specs=[pl.BlockSpec((tm, D_MODEL), lambda i, f: (i, 0)),
                   pl.BlockSpec((ts, D_MODEL), lambda i, f: (i, 0))],
        out_shape=[jax.ShapeDtypeStruct(xp.shape, F32), jax.ShapeDtypeStruct(xs.shape, F32)],
        scratch_shapes=[pltpu.VMEM((tm + ts, D_MODEL), BF16)],
        compiler_params=pltpu.CompilerParams(
            dimension_semantics=("arbitrary", "arbitrary"), vmem_limit_bytes=VMEM_LIMIT_BYTES),
        name=name,
    )(xp, xs, mod_p3, mod_p3, mod_p3, mod_s, mod_s, mod_s, nrm, wg, wu, wd, nf)


K_P, K_CC, K_CH, K_CB, K_GA, K_GB, K_AB, K_O = range(8)
_N_CONV_CHUNKS = CONV_WIDTH // MIX_CH
_N_OUT_CHUNKS = D_MODEL // MIX_CH
_S_CONV = POOL_WIDTH // MIX_CH
_S_GATE = _S_CONV + 3 * _N_CONV_CHUNKS
_S_OUT = _S_GATE + 3 * _N_OUT_CHUNKS
_N_STAGES = _S_OUT + _N_OUT_CHUNKS
_KIND_COL0 = (COL_P, COL_CC, COL_CH, COL_CB, COL_GA, COL_GB, 0, 0)
assert _N_STAGES % MIX_NB == 0


def _decode_stage(t):
    u1 = t - _S_CONV
    u2 = t - _S_GATE
    c1 = lax.div(jnp.maximum(u1, 0), 3)
    c2 = lax.div(jnp.maximum(u2, 0), 3)
    kind = jnp.where(t < _S_CONV, K_P,
                     jnp.where(t < _S_GATE, K_CC + (u1 - 3 * c1),
                               jnp.where(t < _S_OUT, K_GA + (u2 - 3 * c2), K_O)))
    chunk = jnp.where(t < _S_CONV, t, jnp.where(t < _S_GATE, c1, jnp.where(t < _S_OUT, c2, t - _S_OUT)))
    col0 = jnp.int32(0)
    for k, base in enumerate(_KIND_COL0):
        if base:
            col0 = jnp.where(kind == k, base, col0)
    return kind, chunk, col0 + chunk * MIX_CH


def _mixer_kernel(x1p_hbm, x1s, x1pc, x1sc, shp, scp, g2pc, shs, scs, g2sc, nrm, grp, pscale, cw, cbias,
                  stp, stc, w_in, w_a, w_b, w_o,
                  x2p, x2s, pst, cst, nps, ncs,
                  wbuf, wsem, xbuf, xsem, h2, ap, v, m, zbuf, ybuf, sbuf, abf, e0, e1, qext, pcar, qcar,
                  *, tm, ts, tps, n_tiles):
    i = pl.program_id(0)
    s = pl.program_id(1)
    kind, chunk, _ = _decode_stage(s)
    slot = lax.rem(s, MIX_NB)
    slab = D_MODEL // MIX_DMA_SPLIT
    half = D_MODEL // 2
    prows = pl.ds(0, tm)
    srows = pl.ds(tm, ts)

    def wcopy(src, src_row, col, dst_slot, dst_row):
        return pltpu.make_async_copy(src.at[pl.ds(src_row, slab), pl.ds(col, MIX_CH)],
                                     wbuf.at[dst_slot, pl.ds(dst_row, slab), :], wsem.at[dst_slot])

    def wstart(t):
        k_t, _, col_t = _decode_stage(t)
        col_t = pl.multiple_of(col_t, MIX_CH)
        slot_t = lax.rem(t, MIX_NB)

        @pl.when(k_t < K_AB)
        def _():
            for k in range(MIX_DMA_SPLIT):
                wcopy(w_in, k * slab, col_t, slot_t, k * slab).start()

        @pl.when(k_t == K_AB)
        def _():
            for k in range(MIX_DMA_SPLIT):
                r0 = k * slab
                src, s0 = (w_a, r0) if r0 < half else (w_b, r0 - half)
                wcopy(src, s0, col_t, slot_t, r0).start()

        @pl.when(k_t == K_O)
        def _():
            for k in range(MIX_DMA_SPLIT):
                wcopy(w_o, k * slab, col_t, slot_t, k * slab).start()

    def wwait():
        for k in range(MIX_DMA_SPLIT):
            wcopy(w_in, k * slab, 0, slot, k * slab).wait()

    n_xchunks = tm // X_ROWS

    def xcopy(tile, r):
        return pltpu.make_async_copy(x1p_hbm.at[pl.ds(tile * tm + r * X_ROWS, X_ROWS), :],
                                     xbuf.at[r % X_BUFS], xsem.at[r % X_BUFS])

    def xprefetch(tile):
        for r in range(min(X_BUFS, n_xchunks)):
            xcopy(tile, r).start()

    def prologue():
        for r in range(n_xchunks):
            xcopy(i, r).wait()
            h2[pl.ds(r * X_ROWS, X_ROWS), :] = _modulate(xbuf[r % X_BUFS], nrm[...], scp[...], shp[...]).astype(BF16)
            if r + X_BUFS < n_xchunks:
                xcopy(i, r + X_BUFS).start()
        h2[srows, :] = _modulate(x1s[...], nrm[...], scs[...], shs[...]).astype(BF16)
        for r in range(POOL_STATE - 1):
            nps[r] = stp[r + 1]
        for r in range(CONV_STATE - 1):
            ncs[r] = stc[r + 1]

    def pool_stage(c):
        for gl in range(MIX_CH // POOL_GROUP_DIM):
            g = c * (MIX_CH // POOL_GROUP_DIM) + gl
            w = POOL_WINDOWS[g]
            gcols = pl.ds(g * POOL_GROUP_DIM, POOL_GROUP_DIM)
            lcols = pl.ds(gl * POOL_GROUP_DIM, POOL_GROUP_DIM)
            pg = zbuf[prows, lcols]
            pad = jnp.zeros((SUBLANES, POOL_GROUP_DIM), F32)
            e0[pl.ds(SUBLANES, SUBLANES), :] = pad
            e1[pl.ds(SUBLANES, SUBLANES), :] = pad
            e0[pl.ds(CARRY_P, CARRY_P), :] = pcar[:, gcols]
            e0[pl.ds(2 * CARRY_P, tm), :] = pg
            src, dst = e0, e1
            d = 1
            while d < w:
                dst[pl.ds(CARRY_P, tm + CARRY_P), :] = (src[pl.ds(CARRY_P, tm + CARRY_P), :]
                                                        + src[pl.ds(CARRY_P - d, tm + CARRY_P), :])
                src, dst = dst, src
                d *= 2
            wsum = src[pl.ds(2 * CARRY_P, tm), :]
            pos = (i % tps) * tm + lax.broadcasted_iota(jnp.int32, (tm, 1), 0)
            cnt = jnp.minimum(pos + 1, w).astype(F32)
            abf[prows, :] = (wsum / cnt - pg).astype(BF16)
            tail = pg[tm - CARRY_P:, :]
            pcar[:, gcols] = tail
            pst[:, gcols] = tail
            ps = zbuf[srows, lcols]
            ssum = ps
            for r in range(POOL_STATE + 1 - w, POOL_STATE):
                ssum = ssum + stp[r, :, gcols]
            abf[srows, :] = (ssum / float(min(PAST_LEN + 1, w)) - ps).astype(BF16)
            nps[POOL_STATE - 1, :, gcols] = ps
            ag = _bdot(abf[...], grp[g].astype(BF16)) * pscale[:, gcols]
            ap[g] = ag.astype(BF16)

    def conv_stage(c):
        ccols = pl.ds(c * MIX_CH, MIX_CH)
        qp = ybuf[prows, :] * zbuf[prows, :]
        qext[pl.ds(0, CARRY_Q), :] = qcar[:, ccols]
        qext[pl.ds(CARRY_Q, tm), :] = qp
        y = cbias[:, ccols] + qext[pl.ds(CARRY_Q - 2, tm), :] * cw[0:1, ccols]
        y = y + qext[pl.ds(CARRY_Q - 1, tm), :] * cw[1:2, ccols]
        tail = qp[tm - CARRY_Q:, :]
        qcar[:, ccols] = tail
        cst[:, ccols] = tail
        qs = ybuf[srows, :] * zbuf[srows, :]
        ys = cbias[:, ccols] + stc[0, :, ccols] * cw[0:1, ccols]
        ys = ys + stc[1, :, ccols] * cw[1:2, ccols]
        ncs[CONV_STATE - 1, :, ccols] = qs
        ybuf[prows, :] = y + qp * cw[2:3, ccols]
        ybuf[srows, :] = ys + qs * cw[2:3, ccols]

    @pl.when(s == 0)
    def _():
        @pl.when(i == 0)
        def _():
            for t in range(MIX_NB - 1):
                wstart(jnp.int32(t))
            xprefetch(0)

        @pl.when(i % tps == 0)
        def _():
            pcar[...] = jnp.zeros(pcar.shape, F32)
            qcar[...] = jnp.zeros(qcar.shape, F32)

        prologue()

    wwait()
    nxt = s + (MIX_NB - 1)

    @pl.when(nxt < _N_STAGES)
    def _():
        wstart(nxt)

    @pl.when(jnp.logical_and(nxt >= _N_STAGES, i < n_tiles - 1))
    def _():
        wstart(nxt - _N_STAGES)

    @pl.when(jnp.logical_and(s == _S_OUT + 1, i < n_tiles - 1))
    def _():
        xprefetch(i + 1)

    @pl.when(kind < K_AB)
    def _():
        zbuf[...] = _bdot(h2[...], wbuf[slot].astype(BF16))

    for c in range(POOL_WIDTH // MIX_CH):
        @pl.when(jnp.logical_and(kind == K_P, chunk == c))
        def _(c=c):
            pool_stage(c)

    @pl.when(kind == K_CC)
    def _():
        ybuf[...] = zbuf[...]

    for c in range(_N_CONV_CHUNKS):
        @pl.when(jnp.logical_and(kind == K_CH, chunk == c))
        def _(c=c):
            conv_stage(c)

    @pl.when(kind == K_CB)
    def _():
        v[chunk] = (zbuf[...] * ybuf[...]).astype(BF16)

    @pl.when(kind == K_GA)
    def _():
        ybuf[...] = _sigmoid(zbuf[...])

    @pl.when(kind == K_GB)
    def _():
        sbuf[...] = _sigmoid(zbuf[...])

    @pl.when(kind == K_AB)
    def _():
        ua = None
        for g in range(len(POOL_WINDOWS)):
            part = _bdot(ap[g], wbuf[slot, pl.ds(g * POOL_GROUP_DIM, POOL_GROUP_DIM), :].astype(BF16))
            ua = part if ua is None else ua + part
        ub = None
        for k in range(_N_CONV_CHUNKS):
            part = _bdot(v[k], wbuf[slot, pl.ds(half + k * MIX_CH, MIX_CH), :].astype(BF16))
            ub = part if ub is None else ub + part
        m[chunk] = (ybuf[...] * ua + sbuf[...] * ub).astype(BF16)

    @pl.when(kind == K_O)
    def _():
        acc = None
        for k in range(_N_OUT_CHUNKS):
            part = _bdot(m[k], wbuf[slot, pl.ds(k * MIX_CH, MIX_CH), :].astype(BF16))
            acc = part if acc is None else acc + part
        x2p[...] = x1pc[...] + g2pc[...] * acc[:tm]
        x2s[...] = x1sc[...] + g2sc[...] * acc[tm:]


def _mixer(x1p, x1s, mod_p3, mod_s, nrm, grp, pscale, cw, cbias, stp_t, stc_t, w_in, w_a, w_b, w_o, *, tm, ts, tps):
    n_tiles = x1p.shape[0] // tm
    n_seq = n_tiles // tps
    n_s = x1s.shape[0]
    assert n_s == n_tiles * ts
    rows = tm + ts
    any_spec = pl.BlockSpec(memory_space=pl.ANY)

    def const(shape):
        return pl.BlockSpec(shape, lambda i, s: (0,) * len(shape))

    def oc(s):
        return jnp.maximum(s - _S_OUT, 0)

    g2_blk = (MOD_G2 * D_MODEL) // MIX_CH
    in_specs = [
        any_spec,
        pl.BlockSpec((ts, D_MODEL), lambda i, s: (i, 0)),
        pl.BlockSpec((tm, MIX_CH), lambda i, s: (i, oc(s))),
        pl.BlockSpec((ts, MIX_CH), lambda i, s: (i, oc(s))),
        pl.BlockSpec((None, 1, D_MODEL), lambda i, s: (i // tps, 0, MOD_SH2)),
        pl.BlockSpec((None, 1, D_MODEL), lambda i, s: (i // tps, 0, MOD_SC2)),
        pl.BlockSpec((None, 1, MIX_CH), lambda i, s: (i // tps, 0, g2_blk + oc(s))),
        pl.BlockSpec((ts, D_MODEL), lambda i, s: (i, MOD_SH2)),
        pl.BlockSpec((ts, D_MODEL), lambda i, s: (i, MOD_SC2)),
        pl.BlockSpec((ts, MIX_CH), lambda i, s: (i, g2_blk + oc(s))),
        const((1, D_MODEL)),
        const((len(POOL_WINDOWS), POOL_GROUP_DIM, POOL_GROUP_DIM)),
        const((1, POOL_WIDTH)),
        const((3, CONV_WIDTH)),
        const((1, CONV_WIDTH)),
        pl.BlockSpec((POOL_STATE, ts, POOL_WIDTH), lambda i, s: (0, i, 0)),
        pl.BlockSpec((CONV_STATE, ts, CONV_WIDTH), lambda i, s: (0, i, 0)),
        any_spec, any_spec, any_spec, any_spec,
    ]
    out_specs = [
        pl.BlockSpec((tm, MIX_CH), lambda i, s: (i, oc(s))),
        pl.BlockSpec((ts, MIX_CH), lambda i, s: (i, oc(s))),
        pl.BlockSpec((None, CARRY_P, POOL_WIDTH), lambda i, s: (i // tps, 0, 0)),
        pl.BlockSpec((None, CARRY_Q, CONV_WIDTH), lambda i, s: (i // tps, 0, 0)),
        pl.BlockSpec((POOL_STATE, ts, POOL_WIDTH), lambda i, s: (0, i, 0)),
        pl.BlockSpec((CONV_STATE, ts, CONV_WIDTH), lambda i, s: (0, i, 0)),
    ]
    out_shape = [
        jax.ShapeDtypeStruct(x1p.shape, F32),
        jax.ShapeDtypeStruct(x1s.shape, F32),
        jax.ShapeDtypeStruct((n_seq, CARRY_P, POOL_WIDTH), F32),
        jax.ShapeDtypeStruct((n_seq, CARRY_Q, CONV_WIDTH), F32),
        jax.ShapeDtypeStruct((POOL_STATE, n_s, POOL_WIDTH), F32),
        jax.ShapeDtypeStruct((CONV_STATE, n_s, CONV_WIDTH), F32),
    ]
    scratch = [
        pltpu.VMEM((MIX_NB, D_MODEL, MIX_CH), F32),
        pltpu.SemaphoreType.DMA((MIX_NB,)),
        pltpu.VMEM((X_BUFS, X_ROWS, D_MODEL), F32),
        pltpu.SemaphoreType.DMA((X_BUFS,)),
        pltpu.VMEM((rows, D_MODEL), BF16),
        pltpu.VMEM((len(POOL_WINDOWS), rows, POOL_GROUP_DIM), BF16),
        pltpu.VMEM((_N_CONV_CHUNKS, rows, MIX_CH), BF16),
        pltpu.VMEM((_N_OUT_CHUNKS, rows, MIX_CH), BF16),
        pltpu.VMEM((rows, MIX_CH), F32),
        pltpu.VMEM((rows, MIX_CH), F32),
        pltpu.VMEM((rows, MIX_CH), F32),
        pltpu.VMEM((rows, POOL_GROUP_DIM), BF16),
        pltpu.VMEM((tm + 2 * CARRY_P, POOL_GROUP_DIM), F32),
        pltpu.VMEM((tm + 2 * CARRY_P, POOL_GROUP_DIM), F32),
        pltpu.VMEM((tm + CARRY_Q, MIX_CH), F32),
        pltpu.VMEM((CARRY_P, POOL_WIDTH), F32),
        pltpu.VMEM((CARRY_Q, CONV_WIDTH), F32),
    ]
    return pl.pallas_call(
        functools.partial(_mixer_kernel, tm=tm, ts=ts, tps=tps, n_tiles=n_tiles),
        grid=(n_tiles, _N_STAGES),
        in_specs=in_specs,
        out_specs=out_specs,
        out_shape=out_shape,
        scratch_shapes=scratch,
        compiler_params=pltpu.CompilerParams(
            dimension_semantics=("arbitrary", "arbitrary"), vmem_limit_bytes=VMEM_LIMIT_BYTES),
        name="mixer",
    )(x1p, x1s, x1p, x1s, mod_p3, mod_p3, mod_p3, mod_s, mod_s, mod_s, nrm, grp, pscale, cw, cbias,
      stp_t, stc_t, w_in, w_a, w_b, w_o)


def kernel(x_prompt, x_sample, c_prompt, c_sample, state_pool, state_conv, w_ada, b_ada, norm1, ffn1_gate, ffn1_up, ffn1_down, norm2, w_in, pool_grp, pool_scale, w_branch_a, conv_w, conv_b, w_branch_b, w_o, norm3, ffn2_gate, ffn2_up, ffn2_down, norm_final):
    assert w_ada.shape[0] == 1, "single-layer step"
    batch, seq, d = x_prompt.shape
    dec_batch = x_sample.shape[0]
    assert d == D_MODEL and x_sample.shape[1] == 1

    tm = TM_PROMPT
    tps = seq // tm
    n_tiles = batch * tps
    ts = dec_batch // n_tiles
    assert seq % tm == 0 and dec_batch % n_tiles == 0 and ts % (2 * SUBLANES) == 0

    n_c = batch + dec_batch
    pad = (-n_c) % SUBLANES
    c_all = jnp.concatenate([c_sample, c_prompt, jnp.zeros((pad, d), F32)], axis=0)
    mod = _ada(c_all, w_ada[0], b_ada)
    mod_p3 = mod[dec_batch:n_c].reshape(batch, 1, N_MOD * d)

    row = lambda a: a.reshape(1, -1)
    xp = x_prompt.reshape(batch * seq, d)
    xs = x_sample.reshape(dec_batch, d)
    nf = row(norm_final)
    x1p, x1s = _ffn(xp, xs, mod_p3, mod, MOD_SH1, MOD_SC1, MOD_G1, row(norm1[0]), ffn1_gate[0], ffn1_up[0],
                    ffn1_down[0], nf, tm=tm, ts=ts, tps=tps, final_norm=False, name="ffn1")
    stp_t = jnp.transpose(state_pool[0], (1, 0, 2))
    stc_t = jnp.transpose(state_conv[0], (1, 0, 2))
    x2p, x2s, pst, cst, nps_t, ncs_t = _mixer(
        x1p, x1s, mod_p3, mod, row(norm2[0]), pool_grp[0], row(pool_scale[0]), conv_w[0], row(conv_b[0]),
        stp_t, stc_t, w_in[0], w_branch_a[0], w_branch_b[0], w_o[0], tm=tm, ts=ts, tps=tps)
    yp, ys = _ffn(x2p, x2s, mod_p3, mod, MOD_SH3, MOD_SC3, MOD_G3, row(norm3[0]), ffn2_gate[0], ffn2_up[0],
                  ffn2_down[0], nf, tm=tm, ts=ts, tps=tps, final_norm=True, name="ffn2")

    y_prompt = yp.reshape(batch, seq, d)
    y_sample = ys.reshape(dec_batch, 1, d)
    new_pool_prompt = pst[None, :, CARRY_P - POOL_STATE:, :]
    new_conv_prompt = cst[None, :, CARRY_Q - CONV_STATE:, :]
    new_pool_sample = jnp.transpose(nps_t, (1, 0, 2))[None]
    new_conv_sample = jnp.transpose(ncs_t, (1, 0, 2))[None]
    return (y_prompt, y_sample, new_pool_prompt, new_conv_prompt, new_pool_sample, new_conv_sample)
```

```python
import functools

import jax
import jax.numpy as jnp
from jax import lax
from jax.experimental import pallas as pl
from jax.experimental.pallas import tpu as pltpu

F32 = jnp.float32
BF16 = jnp.bfloat16

D_MODEL = 2048
D_FF = 5632
POOL_WIDTH = 1024
CONV_WIDTH = 1024
POOL_WINDOWS = (2, 4, 8, 16)
POOL_GROUP_DIM = 256
POOL_STATE = 15
CONV_STATE = 2
N_MOD = 9
EPS = 1e-6
PAST_LEN = 16384

V7X_VMEM_BYTES = 64 * 1024 * 1024
SUBLANES = 8
VMEM_LIMIT_BYTES = V7X_VMEM_BYTES - 2 * 1024 * 1024

TM_PROMPT = 1024

ADA_TN = 1024
FFN_TF = 256
MIX_CH = 512
MIX_NB = 3
X_ROWS = 128
FFN_ROWS = 128
X_BUFS = 4
MIX_DMA_SPLIT = 8
CARRY_P = 16
CARRY_Q = 8

COL_P, COL_CB, COL_CC, COL_CH, COL_GA, COL_GB = 0, 1024, 2048, 3072, 4096, 6144
MOD_SH1, MOD_SC1, MOD_G1, MOD_SH2, MOD_SC2, MOD_G2, MOD_SH3, MOD_SC3, MOD_G3 = range(9)


def _sigmoid(x):
    return 1.0 / (1.0 + jnp.exp(-x))


def _modulate(x, nrm, sc, sh):
    inv = lax.rsqrt(jnp.mean(x * x, axis=-1, keepdims=True) + EPS)
    return (x * inv) * nrm * (1.0 + sc) + sh


def _bdot(a, b):
    return jnp.dot(a, b, preferred_element_type=F32)


def _ada_kernel(c_ref, w_ref, b_ref, o_ref):
    c = c_ref[...]
    s = (c * _sigmoid(c)).astype(BF16)
    o_ref[...] = _bdot(s, w_ref[...].astype(BF16)) + b_ref[...]


def _ada(c_all, w_ada, b_ada):
    rows = c_all.shape[0]
    cols = w_ada.shape[1]
    return pl.pallas_call(
        _ada_kernel,
        grid=(cols // ADA_TN,),
        in_specs=[
            pl.BlockSpec((rows, D_MODEL), lambda j: (0, 0)),
            pl.BlockSpec((D_MODEL, ADA_TN), lambda j: (0, j)),
            pl.BlockSpec((1, ADA_TN), lambda j: (0, j)),
        ],
        out_specs=pl.BlockSpec((rows, ADA_TN), lambda j: (0, j)),
        out_shape=jax.ShapeDtypeStruct((rows, cols), F32),
        compiler_params=pltpu.CompilerParams(dimension_semantics=("arbitrary",)),
        name="ada_mod",
    )(c_all, w_ada, b_ada)


def _ffn_kernel(xp_ref, xs_ref, shp, scp, gtp, shs, scs, gts, nrm_ref, wg_ref, wu_ref, wd_ref, nf_ref,
                op_ref, os_ref, h_ref, inv_ref, *, n_f, final_norm):
    f = pl.program_id(1)
    tm = xp_ref.shape[0]
    ts = xs_ref.shape[0]

    def head(x, sc, sh):
        return _modulate(x, nrm_ref[...], sc, sh).astype(BF16)

    def tail(x, gt, acc):
        xo = x + 0.5 * gt * acc
        if final_norm:
            inv = lax.rsqrt(jnp.mean(xo * xo, axis=-1, keepdims=True) + EPS)
            xo = (xo * inv) * nf_ref[...]
        return xo

    @pl.when(f == 0)
    def _():
        @pl.loop(0, tm // FFN_ROWS)
        def _(r):
            rows = pl.ds(pl.multiple_of(r * FFN_ROWS, FFN_ROWS), FFN_ROWS)
            x = xp_ref[rows, :]
            inv_ref[rows, :] = lax.rsqrt(jnp.mean(x * x, axis=-1, keepdims=True) + EPS)

        @pl.loop(0, tm // FFN_ROWS)
        def _(r):
            rows = pl.ds(pl.multiple_of(r * FFN_ROWS, FFN_ROWS), FFN_ROWS)
            hx = (xp_ref[rows, :] * inv_ref[rows, :]) * nrm_ref[...] * (1.0 + scp[...]) + shp[...]
            h_ref[rows, :] = hx.astype(BF16)
            op_ref[rows, :] = jnp.zeros((FFN_ROWS, D_MODEL), F32)

        h_ref[pl.ds(tm, ts), :] = head(xs_ref[...], scs[...], shs[...])
        os_ref[...] = jnp.zeros((ts, D_MODEL), F32)

    h = h_ref[...]
    g = _bdot(h, wg_ref[...].astype(BF16))
    u = _bdot(h, wu_ref[...].astype(BF16))
    a = ((g * _sigmoid(g)) * u).astype(BF16)
    for half in range(2):
        hcols = pl.ds(half * (D_MODEL // 2), D_MODEL // 2)
        y = _bdot(a, wd_ref[:, hcols].astype(BF16))
        op_ref[:, hcols] += y[:tm]
        os_ref[:, hcols] += y[tm:]

    @pl.when(f == n_f - 1)
    def _():
        @pl.loop(0, tm // FFN_ROWS)
        def _(r):
            rows = pl.ds(pl.multiple_of(r * FFN_ROWS, FFN_ROWS), FFN_ROWS)
            op_ref[rows, :] = tail(xp_ref[rows, :], gtp[...], op_ref[rows, :])

        os_ref[...] = tail(xs_ref[...], gts[...], os_ref[...])


def _ffn(xp, xs, mod_p3, mod_s, j_sh, j_sc, j_g, nrm, wg, wu, wd, nf, *, tm, ts, tps, final_norm, name):
    n_tiles = xp.shape[0] // tm
    assert xs.shape[0] == n_tiles * ts
    n_f = D_FF // FFN_TF

    def pspec(j):
        return pl.BlockSpec((None, 1, D_MODEL), lambda i, f: (i // tps, 0, j))

    def sspec(j):
        return pl.BlockSpec((ts, D_MODEL), lambda i, f: (i, j))

    return pl.pallas_call(
        functools.partial(_ffn_kernel, n_f=n_f, final_norm=final_norm),
        grid=(n_tiles, n_f),
        in_specs=[
            pl.BlockSpec((tm, D_MODEL), lambda i, f: (i, 0)),
            pl.BlockSpec((ts, D_MODEL), lambda i, f: (i, 0)),
            pspec(j_sh), pspec(j_sc), pspec(j_g),
            sspec(j_sh), sspec(j_sc), sspec(j_g),
            pl.BlockSpec((1, D_MODEL), lambda i, f: (0, 0)),
            pl.BlockSpec((D_MODEL, FFN_TF), lambda i, f: (0, f)),
            pl.BlockSpec((D_MODEL, FFN_TF), lambda i, f: (0, f)),
            pl.BlockSpec((FFN_TF, D_MODEL), lambda i, f: (f, 0)),
            pl.BlockSpec((1, D_MODEL), lambda i, f: (0, 0)),
        ],
        out_specs=[pl.BlockSpec((tm, D_MODEL), lambda i, f: (i, 0)),
                   pl.BlockSpec((ts, D_MODEL), lambda i, f: (i, 0))],
        out_shape=[jax.ShapeDtypeStruct(xp.shape, F32), jax.ShapeDtypeStruct(xs.shape, F32)],
        scratch_shapes=[pltpu.VMEM((tm + ts, D_MODEL), BF16), pltpu.VMEM((tm, 1), F32)],
        compiler_params=pltpu.CompilerParams(
            dimension_semantics=("arbitrary", "arbitrary"), vmem_limit_bytes=VMEM_LIMIT_BYTES),
        name=name,
    )(xp, xs, mod_p3, mod_p3, mod_p3, mod_s, mod_s, mod_s, nrm, wg, wu, wd, nf)


K_P, K_CC, K_CH, K_CB, K_GA, K_GB, K_AB, K_O = range(8)
_N_CONV_CHUNKS = CONV_WIDTH // MIX_CH
_N_OUT_CHUNKS = D_MODEL // MIX_CH
_S_CONV = POOL_WIDTH // MIX_CH
_S_GATE = _S_CONV + 3 * _N_CONV_CHUNKS
_S_OUT = _S_GATE + 3 * _N_OUT_CHUNKS
_N_STAGES = _S_OUT + _N_OUT_CHUNKS
_KIND_COL0 = (COL_P, COL_CC, COL_CH, COL_CB, COL_GA, COL_GB, 0, 0)
assert _N_STAGES % MIX_NB == 0


def _decode_stage(t):
    u1 = t - _S_CONV
    u2 = t - _S_GATE
    c1 = lax.div(jnp.maximum(u1, 0), 3)
    c2 = lax.div(jnp.maximum(u2, 0), 3)
    kind = jnp.where(t < _S_CONV, K_P,
                     jnp.where(t < _S_GATE, K_CC + (u1 - 3 * c1),
                               jnp.where(t < _S_OUT, K_GA + (u2 - 3 * c2), K_O)))
    chunk = jnp.where(t < _S_CONV, t, jnp.where(t < _S_GATE, c1, jnp.where(t < _S_OUT, c2, t - _S_OUT)))
    col0 = jnp.int32(0)
    for k, base in enumerate(_KIND_COL0):
        if base:
            col0 = jnp.where(kind == k, base, col0)
    return kind, chunk, col0 + chunk * MIX_CH


def _mixer_kernel(x1p_hbm, x1s, x1pc, x1sc, shp, scp, g2pc, shs, scs, g2sc, nrm, grp, pscale, cw, cbias,
                  stp, stc, w_in, w_a, w_b, w_o,
                  x2p, x2s, pst, cst, nps, ncs,
                  wbuf, wsem, xbuf, xsem, h2, ap, v, m, zb, ybuf, sa, abf, e0, e1, qext, pcar, qcar,
                  *, tm, ts, tps, n_tiles):
    i = pl.program_id(0)
    s = pl.program_id(1)
    kind, chunk, _ = _decode_stage(s)
    slot = lax.rem(s, MIX_NB)
    cur = lax.rem(s, 2)
    slab = D_MODEL // MIX_DMA_SPLIT
    half = D_MODEL // 2
    prows = pl.ds(0, tm)
    srows = pl.ds(tm, ts)

    def wcopy(src, src_row, col, dst_slot, dst_row):
        return pltpu.make_async_copy(src.at[pl.ds(src_row, slab), pl.ds(col, MIX_CH)],
                                     wbuf.at[dst_slot, pl.ds(dst_row, slab), :], wsem.at[dst_slot])

    def wstart(t):
        k_t, _, col_t = _decode_stage(t)
        col_t = pl.multiple_of(col_t, MIX_CH)
        slot_t = lax.rem(t, MIX_NB)

        @pl.when(k_t < K_AB)
        def _():
            for k in range(MIX_DMA_SPLIT):
                wcopy(w_in, k * slab, col_t, slot_t, k * slab).start()

        @pl.when(k_t == K_AB)
        def _():
            for k in range(MIX_DMA_SPLIT):
                r0 = k * slab
                src, s0 = (w_a, r0) if r0 < half else (w_b, r0 - half)
                wcopy(src, s0, col_t, slot_t, r0).start()

        @pl.when(k_t == K_O)
        def _():
            for k in range(MIX_DMA_SPLIT):
                wcopy(w_o, k * slab, col_t, slot_t, k * slab).start()

    def wwait():
        for k in range(MIX_DMA_SPLIT):
            wcopy(w_in, k * slab, 0, slot, k * slab).wait()

    n_xchunks = tm // X_ROWS

    def xcopy(tile, r):
        return pltpu.make_async_copy(x1p_hbm.at[pl.ds(tile * tm + r * X_ROWS, X_ROWS), :],
                                     xbuf.at[r % X_BUFS], xsem.at[r % X_BUFS])

    def xprefetch(tile):
        for r in range(min(X_BUFS, n_xchunks)):
            xcopy(tile, r).start()

    def prologue():
        for r in range(n_xchunks):
            xcopy(i, r).wait()
            h2[pl.ds(r * X_ROWS, X_ROWS), :] = _modulate(xbuf[r % X_BUFS], nrm[...], scp[...], shp[...]).astype(BF16)
            if r + X_BUFS < n_xchunks:
                xcopy(i, r + X_BUFS).start()
        h2[srows, :] = _modulate(x1s[...], nrm[...], scs[...], shs[...]).astype(BF16)
        for r in range(POOL_STATE - 1):
            nps[r] = stp[r + 1]
        for r in range(CONV_STATE - 1):
            ncs[r] = stc[r + 1]

    def pool_stage(c):
        zbuf = zb.at[c % 2]
        for gl in range(MIX_CH // POOL_GROUP_DIM):
            g = c * (MIX_CH // POOL_GROUP_DIM) + gl
            w = POOL_WINDOWS[g]
            gcols = pl.ds(g * POOL_GROUP_DIM, POOL_GROUP_DIM)
            lcols = pl.ds(gl * POOL_GROUP_DIM, POOL_GROUP_DIM)
            pg = zbuf[prows, lcols]
            pad = jnp.zeros((SUBLANES, POOL_GROUP_DIM), F32)
            e0[pl.ds(SUBLANES, SUBLANES), :] = pad
            e1[pl.ds(SUBLANES, SUBLANES), :] = pad
            e0[pl.ds(CARRY_P, CARRY_P), :] = pcar[:, gcols]
            e0[pl.ds(2 * CARRY_P, tm), :] = pg
            src, dst = e0, e1
            d = 1
            while d < w:
                dst[pl.ds(CARRY_P, tm + CARRY_P), :] = (src[pl.ds(CARRY_P, tm + CARRY_P), :]
                                                        + src[pl.ds(CARRY_P - d, tm + CARRY_P), :])
                src, dst = dst, src
                d *= 2
            wsum = src[pl.ds(2 * CARRY_P, tm), :]
            pos = (i % tps) * tm + lax.broadcasted_iota(jnp.int32, (tm, 1), 0)
            cnt = jnp.minimum(pos + 1, w).astype(F32)
            abf[prows, :] = (wsum / cnt - pg).astype(BF16)
            tail = pg[tm - CARRY_P:, :]
            pcar[:, gcols] = tail
            pst[:, gcols] = tail
            ps = zbuf[srows, lcols]
            ssum = ps
            for r in range(POOL_STATE + 1 - w, POOL_STATE):
                ssum = ssum + stp[r, :, gcols]
            abf[srows, :] = (ssum / float(min(PAST_LEN + 1, w)) - ps).astype(BF16)
            nps[POOL_STATE - 1, :, gcols] = ps
            ag = _bdot(abf[...], grp[g].astype(BF16)) * pscale[:, gcols]
            ap[g] = ag.astype(BF16)

    def conv_stage(c):
        ccols = pl.ds(c * MIX_CH, MIX_CH)
        s_ch = _S_CONV + 3 * c + 1
        zbuf, zprev = zb.at[s_ch % 2], zb.at[(s_ch - 1) % 2]
        qp = zprev[prows, :] * zbuf[prows, :]
        qext[pl.ds(0, CARRY_Q), :] = qcar[:, ccols]
        qext[pl.ds(CARRY_Q, tm), :] = qp
        y = cbias[:, ccols] + qext[pl.ds(CARRY_Q - 2, tm), :] * cw[0:1, ccols]
        y = y + qext[pl.ds(CARRY_Q - 1, tm), :] * cw[1:2, ccols]
        tail = qp[tm - CARRY_Q:, :]
        qcar[:, ccols] = tail
        cst[:, ccols] = tail
        qs = zprev[srows, :] * zbuf[srows, :]
        ys = cbias[:, ccols] + stc[0, :, ccols] * cw[0:1, ccols]
        ys = ys + stc[1, :, ccols] * cw[1:2, ccols]
        ncs[CONV_STATE - 1, :, ccols] = qs
        ybuf[prows, :] = y + qp * cw[2:3, ccols]
        ybuf[srows, :] = ys + qs * cw[2:3, ccols]

    @pl.when(s == 0)
    def _():
        @pl.when(i == 0)
        def _():
            for t in range(MIX_NB - 1):
                wstart(jnp.int32(t))
            xprefetch(0)
            zb[...] = jnp.zeros(zb.shape, F32)

        @pl.when(i % tps == 0)
        def _():
            pcar[...] = jnp.zeros(pcar.shape, F32)
            qcar[...] = jnp.zeros(qcar.shape, F32)

        prologue()

    wwait()
    nxt = s + (MIX_NB - 1)

    @pl.when(nxt < _N_STAGES)
    def _():
        wstart(nxt)

    @pl.when(jnp.logical_and(nxt >= _N_STAGES, i < n_tiles - 1))
    def _():
        wstart(nxt - _N_STAGES)

    @pl.when(jnp.logical_and(s == _S_OUT + 1, i < n_tiles - 1))
    def _():
        xprefetch(i + 1)

    for par in range(2):
        @pl.when(jnp.logical_and(kind < K_AB, cur == par))
        def _(par=par):
            sa[...] = _sigmoid(zb[1 - par])
            zb[par] = _bdot(h2[...], wbuf[slot].astype(BF16))

    for c in range(POOL_WIDTH // MIX_CH):
        @pl.when(jnp.logical_and(kind == K_P, chunk == c))
        def _(c=c):
            pool_stage(c)

    for c in range(_N_CONV_CHUNKS):
        @pl.when(jnp.logical_and(kind == K_CH, chunk == c))
        def _(c=c):
            conv_stage(c)

    for c in range(_N_CONV_CHUNKS):
        @pl.when(jnp.logical_and(kind == K_CB, chunk == c))
        def _(c=c):
            v[c] = (zb[(_S_CONV + 3 * c + 2) % 2] * ybuf[...]).astype(BF16)

    @pl.when(kind == K_AB)
    def _():
        ua = None
        for g in range(len(POOL_WINDOWS)):
            part = _bdot(ap[g], wbuf[slot, pl.ds(g * POOL_GROUP_DIM, POOL_GROUP_DIM), :].astype(BF16))
            ua = part if ua is None else ua + part
        ub = None
        for k in range(_N_CONV_CHUNKS):
            part = _bdot(v[k], wbuf[slot, pl.ds(half + k * MIX_CH, MIX_CH), :].astype(BF16))
            ub = part if ub is None else ub + part
        m[chunk] = (sa[...] * ua + _sigmoid(zb[1 - cur]) * ub).astype(BF16)

    @pl.when(kind == K_O)
    def _():
        acc = None
        for k in range(_N_OUT_CHUNKS):
            part = _bdot(m[k], wbuf[slot, pl.ds(k * MIX_CH, MIX_CH), :].astype(BF16))
            acc = part if acc is None else acc + part
        x2p[...] = x1pc[...] + g2pc[...] * acc[:tm]
        x2s[...] = x1sc[...] + g2sc[...] * acc[tm:]


def _mixer(x1p, x1s, mod_p3, mod_s, nrm, grp, pscale, cw, cbias, stp_t, stc_t, w_in, w_a, w_b, w_o, *, tm, ts, tps):
    n_tiles = x1p.shape[0] // tm
    n_seq = n_tiles // tps
    n_s = x1s.shape[0]
    assert n_s == n_tiles * ts
    rows = tm + ts
    any_spec = pl.BlockSpec(memory_space=pl.ANY)

    def const(shape):
        return pl.BlockSpec(shape, lambda i, s: (0,) * len(shape))

    def oc(s):
        return jnp.maximum(s - _S_OUT, 0)

    g2_blk = (MOD_G2 * D_MODEL) // MIX_CH
    in_specs = [
        any_spec,
        pl.BlockSpec((ts, D_MODEL), lambda i, s: (i, 0)),
        pl.BlockSpec((tm, MIX_CH), lambda i, s: (i, oc(s))),
        pl.BlockSpec((ts, MIX_CH), lambda i, s: (i, oc(s))),
        pl.BlockSpec((None, 1, D_MODEL), lambda i, s: (i // tps, 0, MOD_SH2)),
        pl.BlockSpec((None, 1, D_MODEL), lambda i, s: (i // tps, 0, MOD_SC2)),
        pl.BlockSpec((None, 1, MIX_CH), lambda i, s: (i // tps, 0, g2_blk + oc(s))),
        pl.BlockSpec((ts, D_MODEL), lambda i, s: (i, MOD_SH2)),
        pl.BlockSpec((ts, D_MODEL), lambda i, s: (i, MOD_SC2)),
        pl.BlockSpec((ts, MIX_CH), lambda i, s: (i, g2_blk + oc(s))),
        const((1, D_MODEL)),
        const((len(POOL_WINDOWS), POOL_GROUP_DIM, POOL_GROUP_DIM)),
        const((1, POOL_WIDTH)),
        const((3, CONV_WIDTH)),
        const((1, CONV_WIDTH)),
        pl.BlockSpec((POOL_STATE, ts, POOL_WIDTH), lambda i, s: (0, i, 0)),
        pl.BlockSpec((CONV_STATE, ts, CONV_WIDTH), lambda i, s: (0, i, 0)),
        any_spec, any_spec, any_spec, any_spec,
    ]
    out_specs = [
        pl.BlockSpec((tm, MIX_CH), lambda i, s: (i, oc(s))),
        pl.BlockSpec((ts, MIX_CH), lambda i, s: (i, oc(s))),
        pl.BlockSpec((None, CARRY_P, POOL_WIDTH), lambda i, s: (i // tps, 0, 0)),
        pl.BlockSpec((None, CARRY_Q, CONV_WIDTH), lambda i, s: (i // tps, 0, 0)),
        pl.BlockSpec((POOL_STATE, ts, POOL_WIDTH), lambda i, s: (0, i, 0)),
        pl.BlockSpec((CONV_STATE, ts, CONV_WIDTH), lambda i, s: (0, i, 0)),
    ]
    out_shape = [
        jax.ShapeDtypeStruct(x1p.shape, F32),
        jax.ShapeDtypeStruct(x1s.shape, F32),
        jax.ShapeDtypeStruct((n_seq, CARRY_P, POOL_WIDTH), F32),
        jax.ShapeDtypeStruct((n_seq, CARRY_Q, CONV_WIDTH), F32),
        jax.ShapeDtypeStruct((POOL_STATE, n_s, POOL_WIDTH), F32),
        jax.ShapeDtypeStruct((CONV_STATE, n_s, CONV_WIDTH), F32),
    ]
    scratch = [
        pltpu.VMEM((MIX_NB, D_MODEL, MIX_CH), F32),
        pltpu.SemaphoreType.DMA((MIX_NB,)),
        pltpu.VMEM((X_BUFS, X_ROWS, D_MODEL), F32),
        pltpu.SemaphoreType.DMA((X_BUFS,)),
        pltpu.VMEM((rows, D_MODEL), BF16),
        pltpu.VMEM((len(POOL_WINDOWS), rows, POOL_GROUP_DIM), BF16),
        pltpu.VMEM((_N_CONV_CHUNKS, rows, MIX_CH), BF16),
        pltpu.VMEM((_N_OUT_CHUNKS, rows, MIX_CH), BF16),
        pltpu.VMEM((2, rows, MIX_CH), F32),
        pltpu.VMEM((rows, MIX_CH), F32),
        pltpu.VMEM((rows, MIX_CH), F32),
        pltpu.VMEM((rows, POOL_GROUP_DIM), BF16),
        pltpu.VMEM((tm + 2 * CARRY_P, POOL_GROUP_DIM), F32),
        pltpu.VMEM((tm + 2 * CARRY_P, POOL_GROUP_DIM), F32),
        pltpu.VMEM((tm + CARRY_Q, MIX_CH), F32),
        pltpu.VMEM((CARRY_P, POOL_WIDTH), F32),
        pltpu.VMEM((CARRY_Q, CONV_WIDTH), F32),
    ]
    return pl.pallas_call(
        functools.partial(_mixer_kernel, tm=tm, ts=ts, tps=tps, n_tiles=n_tiles),
        grid=(n_tiles, _N_STAGES),
        in_specs=in_specs,
        out_specs=out_specs,
        out_shape=out_shape,
        scratch_shapes=scratch,
        compiler_params=pltpu.CompilerParams(
            dimension_semantics=("arbitrary", "arbitrary"), vmem_limit_bytes=VMEM_LIMIT_BYTES),
        name="mixer",
    )(x1p, x1s, x1p, x1s, mod_p3, mod_p3, mod_p3, mod_s, mod_s, mod_s, nrm, grp, pscale, cw, cbias,
      stp_t, stc_t, w_in, w_a, w_b, w_o)


def kernel(x_prompt, x_sample, c_prompt, c_sample, state_pool, state_conv, w_ada, b_ada, norm1, ffn1_gate, ffn1_up, ffn1_down, norm2, w_in, pool_grp, pool_scale, w_branch_a, conv_w, conv_b, w_branch_b, w_o, norm3, ffn2_gate, ffn2_up, ffn2_down, norm_final):
    assert w_ada.shape[0] == 1, "single-layer step"
    batch, seq, d = x_prompt.shape
    dec_batch = x_sample.shape[0]
    assert d == D_MODEL and x_sample.shape[1] == 1

    tm = TM_PROMPT
    tps = seq // tm
    n_tiles = batch * tps
    ts = dec_batch // n_tiles
    assert seq % tm == 0 and dec_batch % n_tiles == 0 and ts % (2 * SUBLANES) == 0

    n_c = batch + dec_batch
    pad = (-n_c) % SUBLANES
    c_all = jnp.concatenate([c_sample, c_prompt, jnp.zeros((pad, d), F32)], axis=0)
    mod = _ada(c_all, w_ada[0], b_ada)
    mod_p3 = mod[dec_batch:n_c].reshape(batch, 1, N_MOD * d)

    row = lambda a: a.reshape(1, -1)
    xp = x_prompt.reshape(batch * seq, d)
    xs = x_sample.reshape(dec_batch, d)
    nf = row(norm_final)
    x1p, x1s = _ffn(xp, xs, mod_p3, mod, MOD_SH1, MOD_SC1, MOD_G1, row(norm1[0]), ffn1_gate[0], ffn1_up[0],
                    ffn1_down[0], nf, tm=tm, ts=ts, tps=tps, final_norm=False, name="ffn1")
    stp_t = jnp.transpose(state_pool[0], (1, 0, 2))
    stc_t = jnp.transpose(state_conv[0], (1, 0, 2))
    x2p, x2s, pst, cst, nps_t, ncs_t = _mixer(
        x1p, x1s, mod_p3, mod, row(norm2[0]), pool_grp[0], row(pool_scale[0]), conv_w[0], row(conv_b[0]),
        stp_t, stc_t, w_in[0], w_branch_a[0], w_branch_b[0], w_o[0], tm=tm, ts=ts, tps=tps)
    yp, ys = _ffn(x2p, x2s, mod_p3, mod, MOD_SH3, MOD_SC3, MOD_G3, row(norm3[0]), ffn2_gate[0], ffn2_up[0],
                  ffn2_down[0], nf, tm=tm, ts=ts, tps=tps, final_norm=True, name="ffn2")

    y_prompt = yp.reshape(batch, seq, d)
    y_sample = ys.reshape(dec_batch, 1, d)
    new_pool_prompt = pst[None, :, CARRY_P - POOL_STATE:, :]
    new_conv_prompt = cst[None, :, CARRY_Q - CONV_STATE:, :]
    new_pool_sample = jnp.transpose(nps_t, (1, 0, 2))[None]
    new_conv_sample = jnp.transpose(ncs_t, (1, 0, 2))[None]
    return (y_prompt, y_sample, new_pool_prompt, new_conv_prompt, new_pool_sample, new_conv_sample)
```

```python
import functools

import jax
import jax.numpy as jnp
from jax import lax
from jax.experimental import pallas as pl
from jax.experimental.pallas import tpu as pltpu

F32 = jnp.float32
BF16 = jnp.bfloat16

D_MODEL = 2048
D_FF = 5632
POOL_WIDTH = 1024
CONV_WIDTH = 1024
POOL_WINDOWS = (2, 4, 8, 16)
POOL_GROUP_DIM = 256
POOL_STATE = 15
CONV_STATE = 2
N_MOD = 9
EPS = 1e-6
PAST_LEN = 16384

V7X_VMEM_BYTES = 64 * 1024 * 1024
SUBLANES = 8
VMEM_LIMIT_BYTES = V7X_VMEM_BYTES - 2 * 1024 * 1024

TM_PROMPT = 1024

ADA_TN = 1024
FFN_TF = 512
MIX_CH = 512
MIX_NB = 3
X_ROWS = 128
FFN_ROWS = 128
FFN_DOWN_SPLIT = 4
X_BUFS = 4
MIX_DMA_SPLIT = 8
CARRY_P = 16
CARRY_Q = 8

COL_P, COL_CB, COL_CC, COL_CH, COL_GA, COL_GB = 0, 1024, 2048, 3072, 4096, 6144
MOD_SH1, MOD_SC1, MOD_G1, MOD_SH2, MOD_SC2, MOD_G2, MOD_SH3, MOD_SC3, MOD_G3 = range(9)


def _sigmoid(x):
    return 1.0 / (1.0 + jnp.exp(-x))


def _modulate(x, nrm, sc, sh):
    inv = lax.rsqrt(jnp.mean(x * x, axis=-1, keepdims=True) + EPS)
    return (x * inv) * nrm * (1.0 + sc) + sh


def _bdot(a, b):
    return jnp.dot(a, b, preferred_element_type=F32)


def _ada_kernel(c_ref, w_ref, b_ref, o_ref):
    c = c_ref[...]
    s = (c * _sigmoid(c)).astype(BF16)
    o_ref[...] = _bdot(s, w_ref[...].astype(BF16)) + b_ref[...]


def _ada(c_all, w_ada, b_ada):
    rows = c_all.shape[0]
    cols = w_ada.shape[1]
    return pl.pallas_call(
        _ada_kernel,
        grid=(cols // ADA_TN,),
        in_specs=[
            pl.BlockSpec((rows, D_MODEL), lambda j: (0, 0)),
            pl.BlockSpec((D_MODEL, ADA_TN), lambda j: (0, j)),
            pl.BlockSpec((1, ADA_TN), lambda j: (0, j)),
        ],
        out_specs=pl.BlockSpec((rows, ADA_TN), lambda j: (0, j)),
        out_shape=jax.ShapeDtypeStruct((rows, cols), F32),
        compiler_params=pltpu.CompilerParams(dimension_semantics=("arbitrary",)),
        name="ada_mod",
    )(c_all, w_ada, b_ada)


def _ffn_kernel(xp_hbm, xs_ref, shp, scp, gtp, shs, scs, gts, nrm_ref, wg_ref, wu_ref, wd_ref, nf_ref,
                op_ref, os_ref, h_ref, xbuf, xsem, *, n_f, n_tiles, final_norm):
    i = pl.program_id(0)
    f = pl.program_id(1)
    tm = op_ref.shape[0]
    ts = xs_ref.shape[0]
    n_xchunks = tm // X_ROWS

    def head(x, sc, sh):
        return _modulate(x, nrm_ref[...], sc, sh).astype(BF16)

    def xcopy(tile, r, slot):
        return pltpu.make_async_copy(xp_hbm.at[pl.ds(tile * tm + r * X_ROWS, X_ROWS), :],
                                     xbuf.at[slot], xsem.at[slot])

    def xprefetch(tile):
        for r in range(min(X_BUFS, n_xchunks)):
            xcopy(tile, r, r).start()

    @pl.when(f == 0)
    def _():
        @pl.when(i == 0)
        def _():
            xprefetch(0)

        @pl.loop(0, n_xchunks)
        def _(r):
            slot = lax.rem(r, X_BUFS)
            xcopy(i, r, slot).wait()
            rows = pl.ds(pl.multiple_of(r * X_ROWS, X_ROWS), X_ROWS)
            x = xbuf[slot]
            h_ref[rows, :] = head(x, scp[...], shp[...])
            op_ref[rows, :] = x

            @pl.when(r + X_BUFS < n_xchunks)
            def _():
                xcopy(i, r + X_BUFS, slot).start()

        xs = xs_ref[...]
        h_ref[pl.ds(tm, ts), :] = head(xs, scs[...], shs[...])
        os_ref[...] = xs

    @pl.when(jnp.logical_and(f == 1, i < n_tiles - 1))
    def _():
        xprefetch(i + 1)

    h = h_ref[...]
    g = _bdot(h, wg_ref[...].astype(BF16))
    u = _bdot(h, wu_ref[...].astype(BF16))
    a = ((g * _sigmoid(g)) * u).astype(BF16)
    cp = 0.5 * gtp[...]
    cs = 0.5 * gts[...]
    width = D_MODEL // FFN_DOWN_SPLIT
    for q in range(FFN_DOWN_SPLIT):
        y = _bdot(a, wd_ref[:, pl.ds(q * width, width)].astype(BF16))
        op_ref[:, pl.ds(q * width, width)] += cp[:, q * width:(q + 1) * width] * y[:tm]
        os_ref[:, pl.ds(q * width, width)] += cs[:, q * width:(q + 1) * width] * y[tm:]

    if final_norm:
        def norm(xo):
            inv = lax.rsqrt(jnp.mean(xo * xo, axis=-1, keepdims=True) + EPS)
            return (xo * inv) * nf_ref[...]

        @pl.when(f == n_f - 1)
        def _():
            @pl.loop(0, tm // FFN_ROWS)
            def _(r):
                rows = pl.ds(pl.multiple_of(r * FFN_ROWS, FFN_ROWS), FFN_ROWS)
                op_ref[rows, :] = norm(op_ref[rows, :])

            os_ref[...] = norm(os_ref[...])


def _ffn(xp, xs, mod_p3, mod_s, j_sh, j_sc, j_g, nrm, wg, wu, wd, nf, *, tm, ts, tps, final_norm, name):
    n_tiles = xp.shape[0] // tm
    assert xs.shape[0] == n_tiles * ts
    n_f = D_FF // FFN_TF
    assert n_f >= 2

    def pspec(j):
        return pl.BlockSpec((None, 1, D_MODEL), lambda i, f: (i // tps, 0, j))

    def sspec(j):
        return pl.BlockSpec((ts, D_MODEL), lambda i, f: (i, j))

    return pl.pallas_call(
        functools.partial(_ffn_kernel, n_f=n_f, n_tiles=n_tiles, final_norm=final_norm),
        grid=(n_tiles, n_f),
        in_specs=[
            pl.BlockSpec(memory_space=pl.ANY),
            pl.BlockSpec((ts, D_MODEL), lambda i, f: (i, 0)),
            pspec(j_sh), pspec(j_sc), pspec(j_g),
            sspec(j_sh), sspec(j_sc), sspec(j_g),
            pl.BlockSpec((1, D_MODEL), lambda i, f: (0, 0)),
            pl.BlockSpec((D_MODEL, FFN_TF), lambda i, f: (0, f)),
            pl.BlockSpec((D_MODEL, FFN_TF), lambda i, f: (0, f)),
            pl.BlockSpec((FFN_TF, D_MODEL), lambda i, f: (f, 0)),
            pl.BlockSpec((1, D_MODEL), lambda i, f: (0, 0)),
        ],
        out_specs=[pl.BlockSpec((tm, D_MODEL), lambda i, f: (i, 0)),
                   pl.BlockSpec((ts, D_MODEL), lambda i, f: (i, 0))],
        out_shape=[jax.ShapeDtypeStruct(xp.shape, F32), jax.ShapeDtypeStruct(xs.shape, F32)],
        scratch_shapes=[pltpu.VMEM((tm + ts, D_MODEL), BF16),
                        pltpu.VMEM((X_BUFS, X_ROWS, D_MODEL), F32),
                        pltpu.SemaphoreType.DMA((X_BUFS,))],
        compiler_params=pltpu.CompilerParams(
            dimension_semantics=("arbitrary", "arbitrary"), vmem_limit_bytes=VMEM_LIMIT_BYTES),
        name=name,
    )(xp, xs, mod_p3, mod_p3, mod_p3, mod_s, mod_s, mod_s, nrm, wg, wu, wd, nf)


K_P, K_CC, K_CH, K_CB, K_GA, K_GB, K_AB, K_O = range(8)
_N_CONV_CHUNKS = CONV_WIDTH // MIX_CH
_N_OUT_CHUNKS = D_MODEL // MIX_CH
_S_CONV = POOL_WIDTH // MIX_CH
_S_GATE = _S_CONV + 3 * _N_CONV_CHUNKS
_S_OUT = _S_GATE + 3 * _N_OUT_CHUNKS
_N_STAGES = _S_OUT + _N_OUT_CHUNKS
_KIND_COL0 = (COL_P, COL_CC, COL_CH, COL_CB, COL_GA, COL_GB, 0, 0)
assert _N_STAGES % MIX_NB == 0


def _decode_stage(t):
    u1 = t - _S_CONV
    u2 = t - _S_GATE
    c1 = lax.div(jnp.maximum(u1, 0), 3)
    c2 = lax.div(jnp.maximum(u2, 0), 3)
    kind = jnp.where(t < _S_CONV, K_P,
                     jnp.where(t < _S_GATE, K_CC + (u1 - 3 * c1),
                               jnp.where(t < _S_OUT, K_GA + (u2 - 3 * c2), K_O)))
    chunk = jnp.where(t < _S_CONV, t, jnp.where(t < _S_GATE, c1, jnp.where(t < _S_OUT, c2, t - _S_OUT)))
    col0 = jnp.int32(0)
    for k, base in enumerate(_KIND_COL0):
        if base:
            col0 = jnp.where(kind == k, base, col0)
    return kind, chunk, col0 + chunk * MIX_CH


def _mixer_kernel(x1p_hbm, x1s, x1pc, x1sc, shp, scp, g2pc, shs, scs, g2sc, nrm, grp, pscale, cw, cbias,
                  stp, stc, w_in, w_a, w_b, w_o,
                  x2p, x2s, pst, cst, nps, ncs,
                  wbuf, wsem, xbuf, xsem, h2, ap, v, m, zb, ybuf, sa, abf, e0, e1, qext, pcar, qcar,
                  *, tm, ts, tps, n_tiles):
    i = pl.program_id(0)
    s = pl.program_id(1)
    kind, chunk, _ = _decode_stage(s)
    slot = lax.rem(s, MIX_NB)
    cur = lax.rem(s, 2)
    slab = D_MODEL // MIX_DMA_SPLIT
    half = D_MODEL // 2
    prows = pl.ds(0, tm)
    srows = pl.ds(tm, ts)

    def wcopy(src, src_row, col, dst_slot, dst_row):
        return pltpu.make_async_copy(src.at[pl.ds(src_row, slab), pl.ds(col, MIX_CH)],
                                     wbuf.at[dst_slot, pl.ds(dst_row, slab), :], wsem.at[dst_slot])

    def wstart(t):
        k_t, _, col_t = _decode_stage(t)
        col_t = pl.multiple_of(col_t, MIX_CH)
        slot_t = lax.rem(t, MIX_NB)

        @pl.when(k_t < K_AB)
        def _():
            for k in range(MIX_DMA_SPLIT):
                wcopy(w_in, k * slab, col_t, slot_t, k * slab).start()

        @pl.when(k_t == K_AB)
        def _():
            for k in range(MIX_DMA_SPLIT):
                r0 = k * slab
                src, s0 = (w_a, r0) if r0 < half else (w_b, r0 - half)
                wcopy(src, s0, col_t, slot_t, r0).start()

        @pl.when(k_t == K_O)
        def _():
            for k in range(MIX_DMA_SPLIT):
                wcopy(w_o, k * slab, col_t, slot_t, k * slab).start()

    def wwait():
        for k in range(MIX_DMA_SPLIT):
            wcopy(w_in, k * slab, 0, slot, k * slab).wait()

    n_xchunks = tm // X_ROWS

    def xcopy(tile, r):
        return pltpu.make_async_copy(x1p_hbm.at[pl.ds(tile * tm + r * X_ROWS, X_ROWS), :],
                                     xbuf.at[r % X_BUFS], xsem.at[r % X_BUFS])

    def xprefetch(tile):
        for r in range(min(X_BUFS, n_xchunks)):
            xcopy(tile, r).start()

    def prologue():
        for r in range(n_xchunks):
            xcopy(i, r).wait()
            h2[pl.ds(r * X_ROWS, X_ROWS), :] = _modulate(xbuf[r % X_BUFS], nrm[...], scp[...], shp[...]).astype(BF16)
            if r + X_BUFS < n_xchunks:
                xcopy(i, r + X_BUFS).start()
        h2[srows, :] = _modulate(x1s[...], nrm[...], scs[...], shs[...]).astype(BF16)
        for r in range(POOL_STATE - 1):
            nps[r] = stp[r + 1]
        for r in range(CONV_STATE - 1):
            ncs[r] = stc[r + 1]

    def pool_stage(c):
        zbuf = zb.at[c % 2]
        for gl in range(MIX_CH // POOL_GROUP_DIM):
            g = c * (MIX_CH // POOL_GROUP_DIM) + gl
            w = POOL_WINDOWS[g]
            gcols = pl.ds(g * POOL_GROUP_DIM, POOL_GROUP_DIM)
            lcols = pl.ds(gl * POOL_GROUP_DIM, POOL_GROUP_DIM)
            pg = zbuf[prows, lcols]
            pad = jnp.zeros((SUBLANES, POOL_GROUP_DIM), F32)
            e0[pl.ds(SUBLANES, SUBLANES), :] = pad
            e1[pl.ds(SUBLANES, SUBLANES), :] = pad
            e0[pl.ds(CARRY_P, CARRY_P), :] = pcar[:, gcols]
            e0[pl.ds(2 * CARRY_P, tm), :] = pg
            src, dst = e0, e1
            d = 1
            while d < w:
                dst[pl.ds(CARRY_P, tm + CARRY_P), :] = (src[pl.ds(CARRY_P, tm + CARRY_P), :]
                                                        + src[pl.ds(CARRY_P - d, tm + CARRY_P), :])
                src, dst = dst, src
                d *= 2
            wsum = src[pl.ds(2 * CARRY_P, tm), :]
            pos = (i % tps) * tm + lax.broadcasted_iota(jnp.int32, (tm, 1), 0)
            cnt = jnp.minimum(pos + 1, w).astype(F32)
            abf[prows, :] = (wsum / cnt - pg).astype(BF16)
            tail = pg[tm - CARRY_P:, :]
            pcar[:, gcols] = tail
            pst[:, gcols] = tail
            ps = zbuf[srows, lcols]
            ssum = ps
            for r in range(POOL_STATE + 1 - w, POOL_STATE):
                ssum = ssum + stp[r, :, gcols]
            abf[srows, :] = (ssum / float(min(PAST_LEN + 1, w)) - ps).astype(BF16)
            nps[POOL_STATE - 1, :, gcols] = ps
            ag = _bdot(abf[...], grp[g].astype(BF16)) * pscale[:, gcols]
            ap[g] = ag.astype(BF16)

    def conv_stage(c):
        ccols = pl.ds(c * MIX_CH, MIX_CH)
        s_ch = _S_CONV + 3 * c + 1
        zbuf, zprev = zb.at[s_ch % 2], zb.at[(s_ch - 1) % 2]
        qp = zprev[prows, :] * zbuf[prows, :]
        qext[pl.ds(0, CARRY_Q), :] = qcar[:, ccols]
        qext[pl.ds(CARRY_Q, tm), :] = qp
        y = cbias[:, ccols] + qext[pl.ds(CARRY_Q - 2, tm), :] * cw[0:1, ccols]
        y = y + qext[pl.ds(CARRY_Q - 1, tm), :] * cw[1:2, ccols]
        tail = qp[tm - CARRY_Q:, :]
        qcar[:, ccols] = tail
        cst[:, ccols] = tail
        qs = zprev[srows, :] * zbuf[srows, :]
        ys = cbias[:, ccols] + stc[0, :, ccols] * cw[0:1, ccols]
        ys = ys + stc[1, :, ccols] * cw[1:2, ccols]
        ncs[CONV_STATE - 1, :, ccols] = qs
        ybuf[prows, :] = y + qp * cw[2:3, ccols]
        ybuf[srows, :] = ys + qs * cw[2:3, ccols]

    @pl.when(s == 0)
    def _():
        @pl.when(i == 0)
        def _():
            for t in range(MIX_NB - 1):
                wstart(jnp.int32(t))
            xprefetch(0)
            zb[...] = jnp.zeros(zb.shape, F32)

        @pl.when(i % tps == 0)
        def _():
            pcar[...] = jnp.zeros(pcar.shape, F32)
            qcar[...] = jnp.zeros(qcar.shape, F32)

        prologue()

    wwait()
    nxt = s + (MIX_NB - 1)

    @pl.when(nxt < _N_STAGES)
    def _():
        wstart(nxt)

    @pl.when(jnp.logical_and(nxt >= _N_STAGES, i < n_tiles - 1))
    def _():
        wstart(nxt - _N_STAGES)

    @pl.when(jnp.logical_and(s == _S_OUT + 1, i < n_tiles - 1))
    def _():
        xprefetch(i + 1)

    for par in range(2):
        @pl.when(jnp.logical_and(kind < K_AB, cur == par))
        def _(par=par):
            sa[...] = _sigmoid(zb[1 - par])
            zb[par] = _bdot(h2[...], wbuf[slot].astype(BF16))

    for c in range(POOL_WIDTH // MIX_CH):
        @pl.when(jnp.logical_and(kind == K_P, chunk == c))
        def _(c=c):
            pool_stage(c)

    for c in range(_N_CONV_CHUNKS):
        @pl.when(jnp.logical_and(kind == K_CH, chunk == c))
        def _(c=c):
            conv_stage(c)

    for c in range(_N_CONV_CHUNKS):
        @pl.when(jnp.logical_and(kind == K_CB, chunk == c))
        def _(c=c):
            v[c] = (zb[(_S_CONV + 3 * c + 2) % 2] * ybuf[...]).astype(BF16)

    @pl.when(kind == K_AB)
    def _():
        ua = None
        for g in range(len(POOL_WINDOWS)):
            part = _bdot(ap[g], wbuf[slot, pl.ds(g * POOL_GROUP_DIM, POOL_GROUP_DIM), :].astype(BF16))
            ua = part if ua is None else ua + part
        ub = None
        for k in range(_N_CONV_CHUNKS):
            part = _bdot(v[k], wbuf[slot, pl.ds(half + k * MIX_CH, MIX_CH), :].astype(BF16))
            ub = part if ub is None else ub + part
        m[chunk] = (sa[...] * ua + _sigmoid(zb[1 - cur]) * ub).astype(BF16)

    @pl.when(kind == K_O)
    def _():
        acc = None
        for k in range(_N_OUT_CHUNKS):
            part = _bdot(m[k], wbuf[slot, pl.ds(k * MIX_CH, MIX_CH), :].astype(BF16))
            acc = part if acc is None else acc + part
        x2p[...] = x1pc[...] + g2pc[...] * acc[:tm]
        x2s[...] = x1sc[...] + g2sc[...] * acc[tm:]


def _mixer(x1p, x1s, mod_p3, mod_s, nrm, grp, pscale, cw, cbias, stp_t, stc_t, w_in, w_a, w_b, w_o, *, tm, ts, tps):
    n_tiles = x1p.shape[0] // tm
    n_seq = n_tiles // tps
    n_s = x1s.shape[0]
    assert n_s == n_tiles * ts
    rows = tm + ts
    any_spec = pl.BlockSpec(memory_space=pl.ANY)

    def const(shape):
        return pl.BlockSpec(shape, lambda i, s: (0,) * len(shape))

    def oc(s):
        return jnp.maximum(s - _S_OUT, 0)

    g2_blk = (MOD_G2 * D_MODEL) // MIX_CH
    in_specs = [
        any_spec,
        pl.BlockSpec((ts, D_MODEL), lambda i, s: (i, 0)),
        pl.BlockSpec((tm, MIX_CH), lambda i, s: (i, oc(s))),
        pl.BlockSpec((ts, MIX_CH), lambda i, s: (i, oc(s))),
        pl.BlockSpec((None, 1, D_MODEL), lambda i, s: (i // tps, 0, MOD_SH2)),
        pl.BlockSpec((None, 1, D_MODEL), lambda i, s: (i // tps, 0, MOD_SC2)),
        pl.BlockSpec((None, 1, MIX_CH), lambda i, s: (i // tps, 0, g2_blk + oc(s))),
        pl.BlockSpec((ts, D_MODEL), lambda i, s: (i, MOD_SH2)),
        pl.BlockSpec((ts, D_MODEL), lambda i, s: (i, MOD_SC2)),
        pl.BlockSpec((ts, MIX_CH), lambda i, s: (i, g2_blk + oc(s))),
        const((1, D_MODEL)),
        const((len(POOL_WINDOWS), POOL_GROUP_DIM, POOL_GROUP_DIM)),
        const((1, POOL_WIDTH)),
        const((3, CONV_WIDTH)),
        const((1, CONV_WIDTH)),
        pl.BlockSpec((POOL_STATE, ts, POOL_WIDTH), lambda i, s: (0, i, 0)),
        pl.BlockSpec((CONV_STATE, ts, CONV_WIDTH), lambda i, s: (0, i, 0)),
        any_spec, any_spec, any_spec, any_spec,
    ]
    out_specs = [
        pl.BlockSpec((tm, MIX_CH), lambda i, s: (i, oc(s))),
        pl.BlockSpec((ts, MIX_CH), lambda i, s: (i, oc(s))),
        pl.BlockSpec((None, CARRY_P, POOL_WIDTH), lambda i, s: (i // tps, 0, 0)),
        pl.BlockSpec((None, CARRY_Q, CONV_WIDTH), lambda i, s: (i // tps, 0, 0)),
        pl.BlockSpec((POOL_STATE, ts, POOL_WIDTH), lambda i, s: (0, i, 0)),
        pl.BlockSpec((CONV_STATE, ts, CONV_WIDTH), lambda i, s: (0, i, 0)),
    ]
    out_shape = [
        jax.ShapeDtypeStruct(x1p.shape, F32),
        jax.ShapeDtypeStruct(x1s.shape, F32),
        jax.ShapeDtypeStruct((n_seq, CARRY_P, POOL_WIDTH), F32),
        jax.ShapeDtypeStruct((n_seq, CARRY_Q, CONV_WIDTH), F32),
        jax.ShapeDtypeStruct((POOL_STATE, n_s, POOL_WIDTH), F32),
        jax.ShapeDtypeStruct((CONV_STATE, n_s, CONV_WIDTH), F32),
    ]
    scratch = [
        pltpu.VMEM((MIX_NB, D_MODEL, MIX_CH), F32),
        pltpu.SemaphoreType.DMA((MIX_NB,)),
        pltpu.VMEM((X_BUFS, X_ROWS, D_MODEL), F32),
        pltpu.SemaphoreType.DMA((X_BUFS,)),
        pltpu.VMEM((rows, D_MODEL), BF16),
        pltpu.VMEM((len(POOL_WINDOWS), rows, POOL_GROUP_DIM), BF16),
        pltpu.VMEM((_N_CONV_CHUNKS, rows, MIX_CH), BF16),
        pltpu.VMEM((_N_OUT_CHUNKS, rows, MIX_CH), BF16),
        pltpu.VMEM((2, rows, MIX_CH), F32),
        pltpu.VMEM((rows, MIX_CH), F32),
        pltpu.VMEM((rows, MIX_CH), F32),
        pltpu.VMEM((rows, POOL_GROUP_DIM), BF16),
        pltpu.VMEM((tm + 2 * CARRY_P, POOL_GROUP_DIM), F32),
        pltpu.VMEM((tm + 2 * CARRY_P, POOL_GROUP_DIM), F32),
        pltpu.VMEM((tm + CARRY_Q, MIX_CH), F32),
        pltpu.VMEM((CARRY_P, POOL_WIDTH), F32),
        pltpu.VMEM((CARRY_Q, CONV_WIDTH), F32),
    ]
    return pl.pallas_call(
        functools.partial(_mixer_kernel, tm=tm, ts=ts, tps=tps, n_tiles=n_tiles),
        grid=(n_tiles, _N_STAGES),
        in_specs=in_specs,
        out_specs=out_specs,
        out_shape=out_shape,
        scratch_shapes=scratch,
        compiler_params=pltpu.CompilerParams(
            dimension_semantics=("arbitrary", "arbitrary"), vmem_limit_bytes=VMEM_LIMIT_BYTES),
        name="mixer",
    )(x1p, x1s, x1p, x1s, mod_p3, mod_p3, mod_p3, mod_s, mod_s, mod_s, nrm, grp, pscale, cw, cbias,
      stp_t, stc_t, w_in, w_a, w_b, w_o)


def kernel(x_prompt, x_sample, c_prompt, c_sample, state_pool, state_conv, w_ada, b_ada, norm1, ffn1_gate, ffn1_up, ffn1_down, norm2, w_in, pool_grp, pool_scale, w_branch_a, conv_w, conv_b, w_branch_b, w_o, norm3, ffn2_gate, ffn2_up, ffn2_down, norm_final):
    assert w_ada.shape[0] == 1, "single-layer step"
    batch, seq, d = x_prompt.shape
    dec_batch = x_sample.shape[0]
    assert d == D_MODEL and x_sample.shape[1] == 1

    tm = TM_PROMPT
    tps = seq // tm
    n_tiles = batch * tps
    ts = dec_batch // n_tiles
    assert seq % tm == 0 and dec_batch % n_tiles == 0 and ts % (2 * SUBLANES) == 0

    n_c = batch + dec_batch
    pad = (-n_c) % SUBLANES
    c_all = jnp.concatenate([c_sample, c_prompt, jnp.zeros((pad, d), F32)], axis=0)
    mod = _ada(c_all, w_ada[0], b_ada)
    mod_p3 = mod[dec_batch:n_c].reshape(batch, 1, N_MOD * d)

    row = lambda a: a.reshape(1, -1)
    xp = x_prompt.reshape(batch * seq, d)
    xs = x_sample.reshape(dec_batch, d)
    nf = row(norm_final)
    x1p, x1s = _ffn(xp, xs, mod_p3, mod, MOD_SH1, MOD_SC1, MOD_G1, row(norm1[0]), ffn1_gate[0], ffn1_up[0],
                    ffn1_down[0], nf, tm=tm, ts=ts, tps=tps, final_norm=False, name="ffn1")
    stp_t = jnp.transpose(state_pool[0], (1, 0, 2))
    stc_t = jnp.transpose(state_conv[0], (1, 0, 2))
    x2p, x2s, pst, cst, nps_t, ncs_t = _mixer(
        x1p, x1s, mod_p3, mod, row(norm2[0]), pool_grp[0], row(pool_scale[0]), conv_w[0], row(conv_b[0]),
        stp_t, stc_t, w_in[0], w_branch_a[0], w_branch_b[0], w_o[0], tm=tm, ts=ts, tps=tps)
    yp, ys = _ffn(x2p, x2s, mod_p3, mod, MOD_SH3, MOD_SC3, MOD_G3, row(norm3[0]), ffn2_gate[0], ffn2_up[0],
                  ffn2_down[0], nf, tm=tm, ts=ts, tps=tps, final_norm=True, name="ffn2")

    y_prompt = yp.reshape(batch, seq, d)
    y_sample = ys.reshape(dec_batch, 1, d)
    new_pool_prompt = pst[None, :, CARRY_P - POOL_STATE:, :]
    new_conv_prompt = cst[None, :, CARRY_Q - CONV_STATE:, :]
    new_pool_sample = jnp.transpose(nps_t, (1, 0, 2))[None]
    new_conv_sample = jnp.transpose(ncs_t, (1, 0, 2))[None]
    return (y_prompt, y_sample, new_pool_prompt, new_conv_prompt, new_pool_sample, new_conv_sample)
```

```python
import functools

import jax
import jax.numpy as jnp
from jax import lax
from jax.experimental import pallas as pl
from jax.experimental.pallas import tpu as pltpu

F32 = jnp.float32
BF16 = jnp.bfloat16

D_MODEL = 2048
D_FF = 5632
POOL_WIDTH = 1024
CONV_WIDTH = 1024
POOL_WINDOWS = (2, 4, 8, 16)
POOL_GROUP_DIM = 256
POOL_STATE = 15
CONV_STATE = 2
N_MOD = 9
EPS = 1e-6
PAST_LEN = 16384

V7X_VMEM_BYTES = 64 * 1024 * 1024
SUBLANES = 8
VMEM_LIMIT_BYTES = V7X_VMEM_BYTES - 2 * 1024 * 1024

TM_PROMPT = 1024

ADA_TN = 1024
FFN_TF = 256
MIX_CH = 512
MIX_NB = 3
X_ROWS = 128
FFN_ROWS = 128
X_BUFS = 4
MIX_DMA_SPLIT = 8
CARRY_P = 16
CARRY_Q = 8

COL_P, COL_CB, COL_CC, COL_CH, COL_GA, COL_GB = 0, 1024, 2048, 3072, 4096, 6144
MOD_SH1, MOD_SC1, MOD_G1, MOD_SH2, MOD_SC2, MOD_G2, MOD_SH3, MOD_SC3, MOD_G3 = range(9)


def _sigmoid(x):
    return 1.0 / (1.0 + jnp.exp(-x))


def _modulate(x, nrm, sc, sh):
    inv = lax.rsqrt(jnp.mean(x * x, axis=-1, keepdims=True) + EPS)
    return (x * inv) * nrm * (1.0 + sc) + sh


def _bdot(a, b):
    return jnp.dot(a, b, preferred_element_type=F32)


def _ada_kernel(c_ref, w_ref, b_ref, o_ref):
    c = c_ref[...]
    s = (c * _sigmoid(c)).astype(BF16)
    o_ref[...] = _bdot(s, w_ref[...].astype(BF16)) + b_ref[...]


def _ada(c_all, w_ada, b_ada):
    rows = c_all.shape[0]
    cols = w_ada.shape[1]
    return pl.pallas_call(
        _ada_kernel,
        grid=(cols // ADA_TN,),
        in_specs=[
            pl.BlockSpec((rows, D_MODEL), lambda j: (0, 0)),
            pl.BlockSpec((D_MODEL, ADA_TN), lambda j: (0, j)),
            pl.BlockSpec((1, ADA_TN), lambda j: (0, j)),
        ],
        out_specs=pl.BlockSpec((rows, ADA_TN), lambda j: (0, j)),
        out_shape=jax.ShapeDtypeStruct((rows, cols), F32),
        compiler_params=pltpu.CompilerParams(dimension_semantics=("arbitrary",)),
        name="ada_mod",
    )(c_all, w_ada, b_ada)


def _ffn_kernel(xp_ref, xs_ref, shp, scp, gtp, shs, scs, gts, nrm_ref, wg_ref, wu_ref, wd_ref, nf_ref,
                op_ref, os_ref, h_ref, inv_ref, *, n_f, final_norm):
    f = pl.program_id(1)
    tm = xp_ref.shape[0]
    ts = xs_ref.shape[0]

    def head(x, sc, sh):
        return _modulate(x, nrm_ref[...], sc, sh).astype(BF16)

    def tail(x, gt, acc):
        xo = x + 0.5 * gt * acc
        if final_norm:
            inv = lax.rsqrt(jnp.mean(xo * xo, axis=-1, keepdims=True) + EPS)
            xo = (xo * inv) * nf_ref[...]
        return xo

    @pl.when(f == 0)
    def _():
        @pl.loop(0, tm // FFN_ROWS)
        def _(r):
            rows = pl.ds(pl.multiple_of(r * FFN_ROWS, FFN_ROWS), FFN_ROWS)
            x = xp_ref[rows, :]
            inv_ref[rows, :] = lax.rsqrt(jnp.mean(x * x, axis=-1, keepdims=True) + EPS)

        @pl.loop(0, tm // FFN_ROWS)
        def _(r):
            rows = pl.ds(pl.multiple_of(r * FFN_ROWS, FFN_ROWS), FFN_ROWS)
            hx = (xp_ref[rows, :] * inv_ref[rows, :]) * nrm_ref[...] * (1.0 + scp[...]) + shp[...]
            h_ref[rows, :] = hx.astype(BF16)
            op_ref[rows, :] = jnp.zeros((FFN_ROWS, D_MODEL), F32)

        h_ref[pl.ds(tm, ts), :] = head(xs_ref[...], scs[...], shs[...])
        os_ref[...] = jnp.zeros((ts, D_MODEL), F32)

    h = h_ref[...]
    g = _bdot(h, wg_ref[...].astype(BF16))
    u = _bdot(h, wu_ref[...].astype(BF16))
    a = ((g * _sigmoid(g)) * u).astype(BF16)
    for half in range(2):
        hcols = pl.ds(half * (D_MODEL // 2), D_MODEL // 2)
        y = _bdot(a, wd_ref[:, hcols].astype(BF16))
        op_ref[:, hcols] += y[:tm]
        os_ref[:, hcols] += y[tm:]

    @pl.when(f == n_f - 1)
    def _():
        @pl.loop(0, tm // FFN_ROWS)
        def _(r):
            rows = pl.ds(pl.multiple_of(r * FFN_ROWS, FFN_ROWS), FFN_ROWS)
            op_ref[rows, :] = tail(xp_ref[rows, :], gtp[...], op_ref[rows, :])

        os_ref[...] = tail(xs_ref[...], gts[...], os_ref[...])


def _ffn(xp, xs, mod_p3, mod_s, j_sh, j_sc, j_g, nrm, wg, wu, wd, nf, *, tm, ts, tps, final_norm, name):
    n_tiles = xp.shape[0] // tm
    assert xs.shape[0] == n_tiles * ts
    n_f = D_FF // FFN_TF

    def pspec(j):
        return pl.BlockSpec((None, 1, D_MODEL), lambda i, f: (i // tps, 0, j))

    def sspec(j):
        return pl.BlockSpec((ts, D_MODEL), lambda i, f: (i, j))

    return pl.pallas_call(
        functools.partial(_ffn_kernel, n_f=n_f, final_norm=final_norm),
        grid=(n_tiles, n_f),
        in_specs=[
            pl.BlockSpec((tm, D_MODEL), lambda i, f: (i, 0)),
            pl.BlockSpec((ts, D_MODEL), lambda i, f: (i, 0)),
            pspec(j_sh), pspec(j_sc), pspec(j_g),
            sspec(j_sh), sspec(j_sc), sspec(j_g),
            pl.BlockSpec((1, D_MODEL), lambda i, f: (0, 0)),
            pl.BlockSpec((D_MODEL, FFN_TF), lambda i, f: (0, f)),
            pl.BlockSpec((D_MODEL, FFN_TF), lambda i, f: (0, f)),
            pl.BlockSpec((FFN_TF, D_MODEL), lambda i, f: (f, 0)),
            pl.BlockSpec((1, D_MODEL), lambda i, f: (0, 0)),
        ],
        out_specs=[pl.BlockSpec((tm, D_MODEL), lambda i, f: (i, 0)),
                   pl.BlockSpec((ts, D_MODEL), lambda i, f: (i, 0))],
        out_shape=[jax.ShapeDtypeStruct(xp.shape, F32), jax.ShapeDtypeStruct(xs.shape, F32)],
        scratch_shapes=[pltpu.VMEM((tm + ts, D_MODEL), BF16), pltpu.VMEM((tm, 1), F32)],
        compiler_params=pltpu.CompilerParams(
            dimension_semantics=("arbitrary", "arbitrary"), vmem_limit_bytes=VMEM_LIMIT_BYTES),
        name=name,
    )(xp, xs, mod_p3, mod_p3, mod_p3, mod_s, mod_s, mod_s, nrm, wg, wu, wd, nf)


K_P, K_CC, K_CH, K_CB, K_GA, K_GB, K_AB, K_O = range(8)
_N_CONV_CHUNKS = CONV_WIDTH // MIX_CH
_N_OUT_CHUNKS = D_MODEL // MIX_CH
_S_CONV = POOL_WIDTH // MIX_CH
_S_GATE = _S_CONV + 3 * _N_CONV_CHUNKS
_S_OUT = _S_GATE + 3 * _N_OUT_CHUNKS
_N_STAGES = _S_OUT + _N_OUT_CHUNKS
_KIND_COL0 = (COL_P, COL_CC, COL_CH, COL_CB, COL_GA, COL_GB, 0, 0)
assert _N_STAGES % MIX_NB == 0


def _decode_stage(t):
    u1 = t - _S_CONV
    u2 = t - _S_GATE
    c1 = lax.div(jnp.maximum(u1, 0), 3)
    c2 = lax.div(jnp.maximum(u2, 0), 3)
    kind = jnp.where(t < _S_CONV, K_P,
                     jnp.where(t < _S_GATE, K_CC + (u1 - 3 * c1),
                               jnp.where(t < _S_OUT, K_GA + (u2 - 3 * c2), K_O)))
    chunk = jnp.where(t < _S_CONV, t, jnp.where(t < _S_GATE, c1, jnp.where(t < _S_OUT, c2, t - _S_OUT)))
    col0 = jnp.int32(0)
    for k, base in enumerate(_KIND_COL0):
        if base:
            col0 = jnp.where(kind == k, base, col0)
    return kind, chunk, col0 + chunk * MIX_CH


def _mixer_kernel(x1p_hbm, x1s, x1pc, x1sc, shp, scp, g2pc, shs, scs, g2sc, nrm, grp, pscale, cw, cbias,
                  stp, stc, w_in, w_a, w_b, w_o,
                  x2p, x2s, pst, cst, nps, ncs,
                  wbuf, wsem, xbuf, xsem, h2, ap, v, m, zb, ybuf, sa, abf, e0, e1, qext, pcar, qcar,
                  *, tm, ts, tps, n_tiles):
    i = pl.program_id(0)
    s = pl.program_id(1)
    kind, chunk, _ = _decode_stage(s)
    slot = lax.rem(s, MIX_NB)
    slab = D_MODEL // MIX_DMA_SPLIT
    half = D_MODEL // 2
    prows = pl.ds(0, tm)
    srows = pl.ds(tm, ts)

    def wcopy(src, src_row, col, dst_slot, dst_row):
        return pltpu.make_async_copy(src.at[pl.ds(src_row, slab), pl.ds(col, MIX_CH)],
                                     wbuf.at[dst_slot, pl.ds(dst_row, slab), :], wsem.at[dst_slot])

    def wstart(t):
        k_t, _, col_t = _decode_stage(t)
        col_t = pl.multiple_of(col_t, MIX_CH)
        slot_t = lax.rem(t, MIX_NB)

        @pl.when(k_t < K_AB)
        def _():
            for k in range(MIX_DMA_SPLIT):
                wcopy(w_in, k * slab, col_t, slot_t, k * slab).start()

        @pl.when(k_t == K_AB)
        def _():
            for k in range(MIX_DMA_SPLIT):
                r0 = k * slab
                src, s0 = (w_a, r0) if r0 < half else (w_b, r0 - half)
                wcopy(src, s0, col_t, slot_t, r0).start()

        @pl.when(k_t == K_O)
        def _():
            for k in range(MIX_DMA_SPLIT):
                wcopy(w_o, k * slab, col_t, slot_t, k * slab).start()

    def wwait():
        for k in range(MIX_DMA_SPLIT):
            wcopy(w_in, k * slab, 0, slot, k * slab).wait()

    n_xchunks = tm // X_ROWS

    def xcopy(tile, r):
        return pltpu.make_async_copy(x1p_hbm.at[pl.ds(tile * tm + r * X_ROWS, X_ROWS), :],
                                     xbuf.at[r % X_BUFS], xsem.at[r % X_BUFS])

    def xprefetch(tile):
        for r in range(min(X_BUFS, n_xchunks)):
            xcopy(tile, r).start()

    def prologue():
        for r in range(n_xchunks):
            xcopy(i, r).wait()
            h2[pl.ds(r * X_ROWS, X_ROWS), :] = _modulate(xbuf[r % X_BUFS], nrm[...], scp[...], shp[...]).astype(BF16)
            if r + X_BUFS < n_xchunks:
                xcopy(i, r + X_BUFS).start()
        h2[srows, :] = _modulate(x1s[...], nrm[...], scs[...], shs[...]).astype(BF16)
        for r in range(POOL_STATE - 1):
            nps[r] = stp[r + 1]
        for r in range(CONV_STATE - 1):
            ncs[r] = stc[r + 1]

    def pool_stage(c):
        zbuf = zb.at[0]
        for gl in range(MIX_CH // POOL_GROUP_DIM):
            g = c * (MIX_CH // POOL_GROUP_DIM) + gl
            w = POOL_WINDOWS[g]
            gcols = pl.ds(g * POOL_GROUP_DIM, POOL_GROUP_DIM)
            lcols = pl.ds(gl * POOL_GROUP_DIM, POOL_GROUP_DIM)
            pg = zbuf[prows, lcols]
            pad = jnp.zeros((SUBLANES, POOL_GROUP_DIM), F32)
            e0[pl.ds(SUBLANES, SUBLANES), :] = pad
            e1[pl.ds(SUBLANES, SUBLANES), :] = pad
            e0[pl.ds(CARRY_P, CARRY_P), :] = pcar[:, gcols]
            e0[pl.ds(2 * CARRY_P, tm), :] = pg
            src, dst = e0, e1
            d = 1
            while d < w:
                dst[pl.ds(CARRY_P, tm + CARRY_P), :] = (src[pl.ds(CARRY_P, tm + CARRY_P), :]
                                                        + src[pl.ds(CARRY_P - d, tm + CARRY_P), :])
                src, dst = dst, src
                d *= 2
            wsum = src[pl.ds(2 * CARRY_P, tm), :]
            pos = (i % tps) * tm + lax.broadcasted_iota(jnp.int32, (tm, 1), 0)
            cnt = jnp.minimum(pos + 1, w).astype(F32)
            abf[prows, :] = (wsum / cnt - pg).astype(BF16)
            tail = pg[tm - CARRY_P:, :]
            pcar[:, gcols] = tail
            pst[:, gcols] = tail
            ps = zbuf[srows, lcols]
            ssum = ps
            for r in range(POOL_STATE + 1 - w, POOL_STATE):
                ssum = ssum + stp[r, :, gcols]
            abf[srows, :] = (ssum / float(min(PAST_LEN + 1, w)) - ps).astype(BF16)
            nps[POOL_STATE - 1, :, gcols] = ps
            ag = _bdot(abf[...], grp[g].astype(BF16)) * pscale[:, gcols]
            ap[g] = ag.astype(BF16)

    def conv_stage(c):
        ccols = pl.ds(c * MIX_CH, MIX_CH)
        zbuf, zprev = zb.at[0], zb.at[1]
        qp = zprev[prows, :] * zbuf[prows, :]
        qext[pl.ds(0, CARRY_Q), :] = qcar[:, ccols]
        qext[pl.ds(CARRY_Q, tm), :] = qp
        y = cbias[:, ccols] + qext[pl.ds(CARRY_Q - 2, tm), :] * cw[0:1, ccols]
        y = y + qext[pl.ds(CARRY_Q - 1, tm), :] * cw[1:2, ccols]
        tail = qp[tm - CARRY_Q:, :]
        qcar[:, ccols] = tail
        cst[:, ccols] = tail
        qs = zprev[srows, :] * zbuf[srows, :]
        ys = cbias[:, ccols] + stc[0, :, ccols] * cw[0:1, ccols]
        ys = ys + stc[1, :, ccols] * cw[1:2, ccols]
        ncs[CONV_STATE - 1, :, ccols] = qs
        ybuf[prows, :] = y + qp * cw[2:3, ccols]
        ybuf[srows, :] = ys + qs * cw[2:3, ccols]

    @pl.when(s == 0)
    def _():
        @pl.when(i == 0)
        def _():
            for t in range(MIX_NB - 1):
                wstart(jnp.int32(t))
            xprefetch(0)
            zb[...] = jnp.zeros(zb.shape, F32)

        @pl.when(i % tps == 0)
        def _():
            pcar[...] = jnp.zeros(pcar.shape, F32)
            qcar[...] = jnp.zeros(qcar.shape, F32)

        prologue()

    wwait()
    nxt = s + (MIX_NB - 1)

    @pl.when(nxt < _N_STAGES)
    def _():
        wstart(nxt)

    @pl.when(jnp.logical_and(nxt >= _N_STAGES, i < n_tiles - 1))
    def _():
        wstart(nxt - _N_STAGES)

    @pl.when(jnp.logical_and(s == _S_OUT + 1, i < n_tiles - 1))
    def _():
        xprefetch(i + 1)

    @pl.when(kind < K_AB)
    def _():
        zp = zb[0]
        sa[...] = _sigmoid(zp)
        zb[1] = zp
        zb[0] = _bdot(h2[...], wbuf[slot].astype(BF16))

    for c in range(POOL_WIDTH // MIX_CH):
        @pl.when(jnp.logical_and(kind == K_P, chunk == c))
        def _(c=c):
            pool_stage(c)

    for c in range(_N_CONV_CHUNKS):
        @pl.when(jnp.logical_and(kind == K_CH, chunk == c))
        def _(c=c):
            conv_stage(c)

    @pl.when(kind == K_CB)
    def _():
        v[chunk] = (zb[0] * ybuf[...]).astype(BF16)

    @pl.when(kind == K_AB)
    def _():
        ua = None
        for g in range(len(POOL_WINDOWS)):
            part = _bdot(ap[g], wbuf[slot, pl.ds(g * POOL_GROUP_DIM, POOL_GROUP_DIM), :].astype(BF16))
            ua = part if ua is None else ua + part
        ub = None
        for k in range(_N_CONV_CHUNKS):
            part = _bdot(v[k], wbuf[slot, pl.ds(half + k * MIX_CH, MIX_CH), :].astype(BF16))
            ub = part if ub is None else ub + part
        m[chunk] = (sa[...] * ua + _sigmoid(zb[0]) * ub).astype(BF16)

    @pl.when(kind == K_O)
    def _():
        acc = None
        for k in range(_N_OUT_CHUNKS):
            part = _bdot(m[k], wbuf[slot, pl.ds(k * MIX_CH, MIX_CH), :].astype(BF16))
            acc = part if acc is None else acc + part
        x2p[...] = x1pc[...] + g2pc[...] * acc[:tm]
        x2s[...] = x1sc[...] + g2sc[...] * acc[tm:]


def _mixer(x1p, x1s, mod_p3, mod_s, nrm, grp, pscale, cw, cbias, stp_t, stc_t, w_in, w_a, w_b, w_o, *, tm, ts, tps):
    n_tiles = x1p.shape[0] // tm
    n_seq = n_tiles // tps
    n_s = x1s.shape[0]
    assert n_s == n_tiles * ts
    rows = tm + ts
    any_spec = pl.BlockSpec(memory_space=pl.ANY)

    def const(shape):
        return pl.BlockSpec(shape, lambda i, s: (0,) * len(shape))

    def oc(s):
        return jnp.maximum(s - _S_OUT, 0)

    g2_blk = (MOD_G2 * D_MODEL) // MIX_CH
    in_specs = [
        any_spec,
        pl.BlockSpec((ts, D_MODEL), lambda i, s: (i, 0)),
        pl.BlockSpec((tm, MIX_CH), lambda i, s: (i, oc(s))),
        pl.BlockSpec((ts, MIX_CH), lambda i, s: (i, oc(s))),
        pl.BlockSpec((None, 1, D_MODEL), lambda i, s: (i // tps, 0, MOD_SH2)),
        pl.BlockSpec((None, 1, D_MODEL), lambda i, s: (i // tps, 0, MOD_SC2)),
        pl.BlockSpec((None, 1, MIX_CH), lambda i, s: (i // tps, 0, g2_blk + oc(s))),
        pl.BlockSpec((ts, D_MODEL), lambda i, s: (i, MOD_SH2)),
        pl.BlockSpec((ts, D_MODEL), lambda i, s: (i, MOD_SC2)),
        pl.BlockSpec((ts, MIX_CH), lambda i, s: (i, g2_blk + oc(s))),
        const((1, D_MODEL)),
        const((len(POOL_WINDOWS), POOL_GROUP_DIM, POOL_GROUP_DIM)),
        const((1, POOL_WIDTH)),
        const((3, CONV_WIDTH)),
        const((1, CONV_WIDTH)),
        pl.BlockSpec((POOL_STATE, ts, POOL_WIDTH), lambda i, s: (0, i, 0)),
        pl.BlockSpec((CONV_STATE, ts, CONV_WIDTH), lambda i, s: (0, i, 0)),
        any_spec, any_spec, any_spec, any_spec,
    ]
    out_specs = [
        pl.BlockSpec((tm, MIX_CH), lambda i, s: (i, oc(s))),
        pl.BlockSpec((ts, MIX_CH), lambda i, s: (i, oc(s))),
        pl.BlockSpec((None, CARRY_P, POOL_WIDTH), lambda i, s: (i // tps, 0, 0)),
        pl.BlockSpec((None, CARRY_Q, CONV_WIDTH), lambda i, s: (i // tps, 0, 0)),
        pl.BlockSpec((POOL_STATE, ts, POOL_WIDTH), lambda i, s: (0, i, 0)),
        pl.BlockSpec((CONV_STATE, ts, CONV_WIDTH), lambda i, s: (0, i, 0)),
    ]
    out_shape = [
        jax.ShapeDtypeStruct(x1p.shape, F32),
        jax.ShapeDtypeStruct(x1s.shape, F32),
        jax.ShapeDtypeStruct((n_seq, CARRY_P, POOL_WIDTH), F32),
        jax.ShapeDtypeStruct((n_seq, CARRY_Q, CONV_WIDTH), F32),
        jax.ShapeDtypeStruct((POOL_STATE, n_s, POOL_WIDTH), F32),
        jax.ShapeDtypeStruct((CONV_STATE, n_s, CONV_WIDTH), F32),
    ]
    scratch = [
        pltpu.VMEM((MIX_NB, D_MODEL, MIX_CH), F32),
        pltpu.SemaphoreType.DMA((MIX_NB,)),
        pltpu.VMEM((X_BUFS, X_ROWS, D_MODEL), F32),
        pltpu.SemaphoreType.DMA((X_BUFS,)),
        pltpu.VMEM((rows, D_MODEL), BF16),
        pltpu.VMEM((len(POOL_WINDOWS), rows, POOL_GROUP_DIM), BF16),
        pltpu.VMEM((_N_CONV_CHUNKS, rows, MIX_CH), BF16),
        pltpu.VMEM((_N_OUT_CHUNKS, rows, MIX_CH), BF16),
        pltpu.VMEM((2, rows, MIX_CH), F32),
        pltpu.VMEM((rows, MIX_CH), F32),
        pltpu.VMEM((rows, MIX_CH), F32),
        pltpu.VMEM((rows, POOL_GROUP_DIM), BF16),
        pltpu.VMEM((tm + 2 * CARRY_P, POOL_GROUP_DIM), F32),
        pltpu.VMEM((tm + 2 * CARRY_P, POOL_GROUP_DIM), F32),
        pltpu.VMEM((tm + CARRY_Q, MIX_CH), F32),
        pltpu.VMEM((CARRY_P, POOL_WIDTH), F32),
        pltpu.VMEM((CARRY_Q, CONV_WIDTH), F32),
    ]
    return pl.pallas_call(
        functools.partial(_mixer_kernel, tm=tm, ts=ts, tps=tps, n_tiles=n_tiles),
        grid=(n_tiles, _N_STAGES),
        in_specs=in_specs,
        out_specs=out_specs,
        out_shape=out_shape,
        scratch_shapes=scratch,
        compiler_params=pltpu.CompilerParams(
            dimension_semantics=("arbitrary", "arbitrary"), vmem_limit_bytes=VMEM_LIMIT_BYTES),
        name="mixer",
    )(x1p, x1s, x1p, x1s, mod_p3, mod_p3, mod_p3, mod_s, mod_s, mod_s, nrm, grp, pscale, cw, cbias,
      stp_t, stc_t, w_in, w_a, w_b, w_o)


def kernel(x_prompt, x_sample, c_prompt, c_sample, state_pool, state_conv, w_ada, b_ada, norm1, ffn1_gate, ffn1_up, ffn1_down, norm2, w_in, pool_grp, pool_scale, w_branch_a, conv_w, conv_b, w_branch_b, w_o, norm3, ffn2_gate, ffn2_up, ffn2_down, norm_final):
    assert w_ada.shape[0] == 1, "single-layer step"
    batch, seq, d = x_prompt.shape
    dec_batch = x_sample.shape[0]
    assert d == D_MODEL and x_sample.shape[1] == 1

    tm = TM_PROMPT
    tps = seq // tm
    n_tiles = batch * tps
    ts = dec_batch // n_tiles
    assert seq % tm == 0 and dec_batch % n_tiles == 0 and ts % (2 * SUBLANES) == 0

    n_c = batch + dec_batch
    pad = (-n_c) % SUBLANES
    c_all = jnp.concatenate([c_sample, c_prompt, jnp.zeros((pad, d), F32)], axis=0)
    mod = _ada(c_all, w_ada[0], b_ada)
    mod_p3 = mod[dec_batch:n_c].reshape(batch, 1, N_MOD * d)

    row = lambda a: a.reshape(1, -1)
    xp = x_prompt.reshape(batch * seq, d)
    xs = x_sample.reshape(dec_batch, d)
    nf = row(norm_final)
    x1p, x1s = _ffn(xp, xs, mod_p3, mod, MOD_SH1, MOD_SC1, MOD_G1, row(norm1[0]), ffn1_gate[0], ffn1_up[0],
                    ffn1_down[0], nf, tm=tm, ts=ts, tps=tps, final_norm=False, name="ffn1")
    stp_t = jnp.transpose(state_pool[0], (1, 0, 2))
    stc_t = jnp.transpose(state_conv[0], (1, 0, 2))
    x2p, x2s, pst, cst, nps_t, ncs_t = _mixer(
        x1p, x1s, mod_p3, mod, row(norm2[0]), pool_grp[0], row(pool_scale[0]), conv_w[0], row(conv_b[0]),
        stp_t, stc_t, w_in[0], w_branch_a[0], w_branch_b[0], w_o[0], tm=tm, ts=ts, tps=tps)
    yp, ys = _ffn(x2p, x2s, mod_p3, mod, MOD_SH3, MOD_SC3, MOD_G3, row(norm3[0]), ffn2_gate[0], ffn2_up[0],
                  ffn2_down[0], nf, tm=tm, ts=ts, tps=tps, final_norm=True, name="ffn2")

    y_prompt = yp.reshape(batch, seq, d)
    y_sample = ys.reshape(dec_batch, 1, d)
    new_pool_prompt = pst[None, :, CARRY_P - POOL_STATE:, :]
    new_conv_prompt = cst[None, :, CARRY_Q - CONV_STATE:, :]
    new_pool_sample = jnp.transpose(nps_t, (1, 0, 2))[None]
    new_conv_sample = jnp.transpose(ncs_t, (1, 0, 2))[None]
    return (y_prompt, y_sample, new_pool_prompt, new_conv_prompt, new_pool_sample, new_conv_sample)
```

```python
import functools

import jax
import jax.numpy as jnp
from jax import lax
from jax.experimental import pallas as pl
from jax.experimental.pallas import tpu as pltpu

F32 = jnp.float32
BF16 = jnp.bfloat16

D_MODEL = 2048
D_FF = 5632
POOL_WIDTH = 1024
CONV_WIDTH = 1024
POOL_WINDOWS = (2, 4, 8, 16)
POOL_GROUP_DIM = 256
POOL_STATE = 15
CONV_STATE = 2
N_MOD = 9
EPS = 1e-6
PAST_LEN = 16384

V7X_VMEM_BYTES = 64 * 1024 * 1024
SUBLANES = 8
VMEM_LIMIT_BYTES = V7X_VMEM_BYTES - 2 * 1024 * 1024

TM_PROMPT = 1024

ADA_TN = 1024
FFN_TF = 256
MIX_CH = 512
MIX_NB = 3
X_ROWS = 128
FFN_ROWS = 128
X_BUFS = 4
MIX_DMA_SPLIT = 8
CARRY_P = 16
CARRY_Q = 8

COL_P, COL_CB, COL_CC, COL_CH, COL_GA, COL_GB = 0, 1024, 2048, 3072, 4096, 6144
MOD_SH1, MOD_SC1, MOD_G1, MOD_SH2, MOD_SC2, MOD_G2, MOD_SH3, MOD_SC3, MOD_G3 = range(9)


def _sigmoid(x):
    return 1.0 / (1.0 + jnp.exp(-x))


def _modulate(x, nrm, sc, sh):
    inv = lax.rsqrt(jnp.mean(x * x, axis=-1, keepdims=True) + EPS)
    return (x * inv) * nrm * (1.0 + sc) + sh


def _bdot(a, b):
    return jnp.dot(a, b, preferred_element_type=F32)


def _ada_kernel(c_ref, w_ref, b_ref, o_ref):
    c = c_ref[...]
    s = (c * _sigmoid(c)).astype(BF16)
    o_ref[...] = _bdot(s, w_ref[...].astype(BF16)) + b_ref[...]


def _ada(c_all, w_ada, b_ada):
    rows = c_all.shape[0]
    cols = w_ada.shape[1]
    return pl.pallas_call(
        _ada_kernel,
        grid=(cols // ADA_TN,),
        in_specs=[
            pl.BlockSpec((rows, D_MODEL), lambda j: (0, 0)),
            pl.BlockSpec((D_MODEL, ADA_TN), lambda j: (0, j)),
            pl.BlockSpec((1, ADA_TN), lambda j: (0, j)),
        ],
        out_specs=pl.BlockSpec((rows, ADA_TN), lambda j: (0, j)),
        out_shape=jax.ShapeDtypeStruct((rows, cols), F32),
        compiler_params=pltpu.CompilerParams(dimension_semantics=("arbitrary",)),
        name="ada_mod",
    )(c_all, w_ada, b_ada)


def _ffn_kernel(xp_ref, xs_ref, shp, scp, gtp, shs, scs, gts, nrm_ref, wg_ref, wu_ref, wd_ref, nf_ref,
                op_ref, os_ref, h_ref, inv_ref, *, n_f, final_norm):
    f = pl.program_id(1)
    tm = xp_ref.shape[0]
    ts = xs_ref.shape[0]

    def head(x, sc, sh):
        return _modulate(x, nrm_ref[...], sc, sh).astype(BF16)

    def tail(x, gt, acc):
        xo = x + 0.5 * gt * acc
        if final_norm:
            inv = lax.rsqrt(jnp.mean(xo * xo, axis=-1, keepdims=True) + EPS)
            xo = (xo * inv) * nf_ref[...]
        return xo

    @pl.when(f == 0)
    def _():
        @pl.loop(0, tm // FFN_ROWS)
        def _(r):
            rows = pl.ds(pl.multiple_of(r * FFN_ROWS, FFN_ROWS), FFN_ROWS)
            x = xp_ref[rows, :]
            inv_ref[rows, :] = lax.rsqrt(jnp.mean(x * x, axis=-1, keepdims=True) + EPS)

        @pl.loop(0, tm // FFN_ROWS)
        def _(r):
            rows = pl.ds(pl.multiple_of(r * FFN_ROWS, FFN_ROWS), FFN_ROWS)
            hx = (xp_ref[rows, :] * inv_ref[rows, :]) * nrm_ref[...] * (1.0 + scp[...]) + shp[...]
            h_ref[rows, :] = hx.astype(BF16)
            op_ref[rows, :] = jnp.zeros((FFN_ROWS, D_MODEL), F32)

        h_ref[pl.ds(tm, ts), :] = head(xs_ref[...], scs[...], shs[...])
        os_ref[...] = jnp.zeros((ts, D_MODEL), F32)

    h = h_ref[...]
    g = _bdot(h, wg_ref[...].astype(BF16))
    u = _bdot(h, wu_ref[...].astype(BF16))
    a = ((g * _sigmoid(g)) * u).astype(BF16)
    for half in range(2):
        hcols = pl.ds(half * (D_MODEL // 2), D_MODEL // 2)
        y = _bdot(a, wd_ref[:, hcols].astype(BF16))
        op_ref[:, hcols] += y[:tm]
        os_ref[:, hcols] += y[tm:]

    @pl.when(f == n_f - 1)
    def _():
        @pl.loop(0, tm // FFN_ROWS)
        def _(r):
            rows = pl.ds(pl.multiple_of(r * FFN_ROWS, FFN_ROWS), FFN_ROWS)
            op_ref[rows, :] = tail(xp_ref[rows, :], gtp[...], op_ref[rows, :])

        os_ref[...] = tail(xs_ref[...], gts[...], os_ref[...])


def _ffn(xp, xs, mod_p3, mod_s, j_sh, j_sc, j_g, nrm, wg, wu, wd, nf, *, tm, ts, tps, final_norm, name):
    n_tiles = xp.shape[0] // tm
    assert xs.shape[0] == n_tiles * ts
    n_f = D_FF // FFN_TF

    def pspec(j):
        return pl.BlockSpec((None, 1, D_MODEL), lambda i, f: (i // tps, 0, j))

    def sspec(j):
        return pl.BlockSpec((ts, D_MODEL), lambda i, f: (i, j))

    return pl.pallas_call(
        functools.partial(_ffn_kernel, n_f=n_f, final_norm=final_norm),
        grid=(n_tiles, n_f),
        in_specs=[
            pl.BlockSpec((tm, D_MODEL), lambda i, f: (i, 0)),
            pl.BlockSpec((ts, D_MODEL), lambda i, f: (i, 0)),
            pspec(j_sh), pspec(j_sc), pspec(j_g),
            sspec(j_sh), sspec(j_sc), sspec(j_g),
            pl.BlockSpec((1, D_MODEL), lambda i, f: (0, 0)),
            pl.BlockSpec((D_MODEL, FFN_TF), lambda i, f: (0, f)),
            pl.BlockSpec((D_MODEL, FFN_TF), lambda i, f: (0, f)),
            pl.BlockSpec((FFN_TF, D_MODEL), lambda i, f: (f, 0)),
            pl.BlockSpec((1, D_MODEL), lambda i, f: (0, 0)),
        ],
        out_specs=[pl.BlockSpec((tm, D_MODEL), lambda i, f: (i, 0)),
                   pl.BlockSpec((ts, D_MODEL), lambda i, f: (i, 0))],
        out_shape=[jax.ShapeDtypeStruct(xp.shape, F32), jax.ShapeDtypeStruct(xs.shape, F32)],
        scratch_shapes=[pltpu.VMEM((tm + ts, D_MODEL), BF16), pltpu.VMEM((tm, 1), F32)],
        compiler_params=pltpu.CompilerParams(
            dimension_semantics=("arbitrary", "arbitrary"), vmem_limit_bytes=VMEM_LIMIT_BYTES),
        name=name,
    )(xp, xs, mod_p3, mod_p3, mod_p3, mod_s, mod_s, mod_s, nrm, wg, wu, wd, nf)


K_P, K_CC, K_CH, K_CB, K_GA, K_GB, K_AB, K_O = range(8)
_N_CONV_CHUNKS = CONV_WIDTH // MIX_CH
_N_OUT_CHUNKS = D_MODEL // MIX_CH
_S_CONV = POOL_WIDTH // MIX_CH
_S_GATE = _S_CONV + 3 * _N_CONV_CHUNKS
_S_OUT = _S_GATE + 3 * _N_OUT_CHUNKS
_N_STAGES = _S_OUT + _N_OUT_CHUNKS
_KIND_COL0 = (COL_P, COL_CC, COL_CH, COL_CB, COL_GA, COL_GB, 0, 0)
assert _N_STAGES % MIX_NB == 0


def _decode_stage(t):
    u1 = t - _S_CONV
    u2 = t - _S_GATE
    c1 = lax.div(jnp.maximum(u1, 0), 3)
    c2 = lax.div(jnp.maximum(u2, 0), 3)
    kind = jnp.where(t < _S_CONV, K_P,
                     jnp.where(t < _S_GATE, K_CC + (u1 - 3 * c1),
                               jnp.where(t < _S_OUT, K_GA + (u2 - 3 * c2), K_O)))
    chunk = jnp.where(t < _S_CONV, t, jnp.where(t < _S_GATE, c1, jnp.where(t < _S_OUT, c2, t - _S_OUT)))
    col0 = jnp.int32(0)
    for k, base in enumerate(_KIND_COL0):
        if base:
            col0 = jnp.where(kind == k, base, col0)
    return kind, chunk, col0 + chunk * MIX_CH


def _mixer_kernel(x1p_hbm, x1s, x1pc, x1sc, shp, scp, g2pc, shs, scs, g2sc, nrm, grp, pscale, cw, cbias,
                  stp, stc, w_in, w_a, w_b, w_o,
                  x2p, x2s, pst, cst, nps, ncs,
                  wbuf, wsem, xbuf, xsem, h2, ap, v, m, zb, ybuf, sa, abf, e0, e1, qext, pcar, qcar,
                  *, tm, ts, tps, n_tiles):
    i = pl.program_id(0)
    s = pl.program_id(1)
    kind, chunk, _ = _decode_stage(s)
    slot = lax.rem(s, MIX_NB)
    slab = D_MODEL // MIX_DMA_SPLIT
    half = D_MODEL // 2
    prows = pl.ds(0, tm)
    srows = pl.ds(tm, ts)

    def wcopy(src, src_row, col, dst_slot, dst_row):
        return pltpu.make_async_copy(src.at[pl.ds(src_row, slab), pl.ds(col, MIX_CH)],
                                     wbuf.at[dst_slot, pl.ds(dst_row, slab), :], wsem.at[dst_slot])

    def wstart(t):
        k_t, _, col_t = _decode_stage(t)
        col_t = pl.multiple_of(col_t, MIX_CH)
        slot_t = lax.rem(t, MIX_NB)

        @pl.when(k_t < K_AB)
        def _():
            for k in range(MIX_DMA_SPLIT):
                wcopy(w_in, k * slab, col_t, slot_t, k * slab).start()

        @pl.when(k_t == K_AB)
        def _():
            for k in range(MIX_DMA_SPLIT):
                r0 = k * slab
                src, s0 = (w_a, r0) if r0 < half else (w_b, r0 - half)
                wcopy(src, s0, col_t, slot_t, r0).start()

        @pl.when(k_t == K_O)
        def _():
            for k in range(MIX_DMA_SPLIT):
                wcopy(w_o, k * slab, col_t, slot_t, k * slab).start()

    def wwait():
        for k in range(MIX_DMA_SPLIT):
            wcopy(w_in, k * slab, 0, slot, k * slab).wait()

    n_xchunks = tm // X_ROWS

    def xcopy(tile, r):
        return pltpu.make_async_copy(x1p_hbm.at[pl.ds(tile * tm + r * X_ROWS, X_ROWS), :],
                                     xbuf.at[r % X_BUFS], xsem.at[r % X_BUFS])

    def xprefetch(tile):
        for r in range(min(X_BUFS, n_xchunks)):
            xcopy(tile, r).start()

    def prologue():
        for r in range(n_xchunks):
            xcopy(i, r).wait()
            h2[pl.ds(r * X_ROWS, X_ROWS), :] = _modulate(xbuf[r % X_BUFS], nrm[...], scp[...], shp[...]).astype(BF16)
            if r + X_BUFS < n_xchunks:
                xcopy(i, r + X_BUFS).start()
        h2[srows, :] = _modulate(x1s[...], nrm[...], scs[...], shs[...]).astype(BF16)
        for r in range(POOL_STATE - 1):
            nps[r] = stp[r + 1]
        for r in range(CONV_STATE - 1):
            ncs[r] = stc[r + 1]

    def pool_stage(c):
        zbuf = zb.at[0]
        for gl in range(MIX_CH // POOL_GROUP_DIM):
            g = c * (MIX_CH // POOL_GROUP_DIM) + gl
            w = POOL_WINDOWS[g]
            gcols = pl.ds(g * POOL_GROUP_DIM, POOL_GROUP_DIM)
            lcols = pl.ds(gl * POOL_GROUP_DIM, POOL_GROUP_DIM)
            pg = zbuf[prows, lcols]
            pad = jnp.zeros((SUBLANES, POOL_GROUP_DIM), F32)
            e0[pl.ds(SUBLANES, SUBLANES), :] = pad
            e1[pl.ds(SUBLANES, SUBLANES), :] = pad
            e0[pl.ds(CARRY_P, CARRY_P), :] = pcar[:, gcols]
            e0[pl.ds(2 * CARRY_P, tm), :] = pg
            src, dst = e0, e1
            d = 1
            while d < w:
                dst[pl.ds(CARRY_P, tm + CARRY_P), :] = (src[pl.ds(CARRY_P, tm + CARRY_P), :]
                                                        + src[pl.ds(CARRY_P - d, tm + CARRY_P), :])
                src, dst = dst, src
                d *= 2
            wsum = src[pl.ds(2 * CARRY_P, tm), :]
            pos = (i % tps) * tm + lax.broadcasted_iota(jnp.int32, (tm, 1), 0)
            cnt = jnp.minimum(pos + 1, w).astype(F32)
            abf[prows, :] = (wsum / cnt - pg).astype(BF16)
            tail = pg[tm - CARRY_P:, :]
            pcar[:, gcols] = tail
            pst[:, gcols] = tail
            ps = zbuf[srows, lcols]
            ssum = ps
            for r in range(POOL_STATE + 1 - w, POOL_STATE):
                ssum = ssum + stp[r, :, gcols]
            abf[srows, :] = (ssum / float(min(PAST_LEN + 1, w)) - ps).astype(BF16)
            nps[POOL_STATE - 1, :, gcols] = ps
            ag = _bdot(abf[...], grp[g].astype(BF16)) * pscale[:, gcols]
            ap[g] = ag.astype(BF16)

    def conv_stage(c):
        ccols = pl.ds(c * MIX_CH, MIX_CH)
        zbuf, zprev = zb.at[0], zb.at[1]
        qp = zprev[prows, :] * zbuf[prows, :]
        qext[pl.ds(0, CARRY_Q), :] = qcar[:, ccols]
        qext[pl.ds(CARRY_Q, tm), :] = qp
        y = cbias[:, ccols] + qext[pl.ds(CARRY_Q - 2, tm), :] * cw[0:1, ccols]
        y = y + qext[pl.ds(CARRY_Q - 1, tm), :] * cw[1:2, ccols]
        tail = qp[tm - CARRY_Q:, :]
        qcar[:, ccols] = tail
        cst[:, ccols] = tail
        qs = zprev[srows, :] * zbuf[srows, :]
        ys = cbias[:, ccols] + stc[0, :, ccols] * cw[0:1, ccols]
        ys = ys + stc[1, :, ccols] * cw[1:2, ccols]
        ncs[CONV_STATE - 1, :, ccols] = qs
        ybuf[prows, :] = y + qp * cw[2:3, ccols]
        ybuf[srows, :] = ys + qs * cw[2:3, ccols]

    @pl.when(s == 0)
    def _():
        @pl.when(i == 0)
        def _():
            for t in range(MIX_NB - 1):
                wstart(jnp.int32(t))
            xprefetch(0)
            zb[...] = jnp.zeros(zb.shape, F32)

        @pl.when(i % tps == 0)
        def _():
            pcar[...] = jnp.zeros(pcar.shape, F32)
            qcar[...] = jnp.zeros(qcar.shape, F32)

        prologue()

    wwait()
    nxt = s + (MIX_NB - 1)

    @pl.when(nxt < _N_STAGES)
    def _():
        wstart(nxt)

    @pl.when(jnp.logical_and(nxt >= _N_STAGES, i < n_tiles - 1))
    def _():
        wstart(nxt - _N_STAGES)

    @pl.when(jnp.logical_and(s == _S_OUT + 1, i < n_tiles - 1))
    def _():
        xprefetch(i + 1)

    @pl.when(kind < K_AB)
    def _():
        sa[...] = _sigmoid(zb[1])
        zb[0] = _bdot(h2[...], wbuf[slot].astype(BF16))

    @pl.when(jnp.logical_or(kind == K_CC, kind == K_GA))
    def _():
        zb[1] = zb[0]

    for c in range(POOL_WIDTH // MIX_CH):
        @pl.when(jnp.logical_and(kind == K_P, chunk == c))
        def _(c=c):
            pool_stage(c)

    for c in range(_N_CONV_CHUNKS):
        @pl.when(jnp.logical_and(kind == K_CH, chunk == c))
        def _(c=c):
            conv_stage(c)

    @pl.when(kind == K_CB)
    def _():
        v[chunk] = (zb[0] * ybuf[...]).astype(BF16)

    @pl.when(kind == K_AB)
    def _():
        ua = None
        for g in range(len(POOL_WINDOWS)):
            part = _bdot(ap[g], wbuf[slot, pl.ds(g * POOL_GROUP_DIM, POOL_GROUP_DIM), :].astype(BF16))
            ua = part if ua is None else ua + part
        ub = None
        for k in range(_N_CONV_CHUNKS):
            part = _bdot(v[k], wbuf[slot, pl.ds(half + k * MIX_CH, MIX_CH), :].astype(BF16))
            ub = part if ub is None else ub + part
        m[chunk] = (sa[...] * ua + _sigmoid(zb[0]) * ub).astype(BF16)

    @pl.when(kind == K_O)
    def _():
        acc = None
        for k in range(_N_OUT_CHUNKS):
            part = _bdot(m[k], wbuf[slot, pl.ds(k * MIX_CH, MIX_CH), :].astype(BF16))
            acc = part if acc is None else acc + part
        x2p[...] = x1pc[...] + g2pc[...] * acc[:tm]
        x2s[...] = x1sc[...] + g2sc[...] * acc[tm:]


def _mixer(x1p, x1s, mod_p3, mod_s, nrm, grp, pscale, cw, cbias, stp_t, stc_t, w_in, w_a, w_b, w_o, *, tm, ts, tps):
    n_tiles = x1p.shape[0] // tm
    n_seq = n_tiles // tps
    n_s = x1s.shape[0]
    assert n_s == n_tiles * ts
    rows = tm + ts
    any_spec = pl.BlockSpec(memory_space=pl.ANY)

    def const(shape):
        return pl.BlockSpec(shape, lambda i, s: (0,) * len(shape))

    def oc(s):
        return jnp.maximum(s - _S_OUT, 0)

    g2_blk = (MOD_G2 * D_MODEL) // MIX_CH
    in_specs = [
        any_spec,
        pl.BlockSpec((ts, D_MODEL), lambda i, s: (i, 0)),
        pl.BlockSpec((tm, MIX_CH), lambda i, s: (i, oc(s))),
        pl.BlockSpec((ts, MIX_CH), lambda i, s: (i, oc(s))),
        pl.BlockSpec((None, 1, D_MODEL), lambda i, s: (i // tps, 0, MOD_SH2)),
        pl.BlockSpec((None, 1, D_MODEL), lambda i, s: (i // tps, 0, MOD_SC2)),
        pl.BlockSpec((None, 1, MIX_CH), lambda i, s: (i // tps, 0, g2_blk + oc(s))),
        pl.BlockSpec((ts, D_MODEL), lambda i, s: (i, MOD_SH2)),
        pl.BlockSpec((ts, D_MODEL), lambda i, s: (i, MOD_SC2)),
        pl.BlockSpec((ts, MIX_CH), lambda i, s: (i, g2_blk + oc(s))),
        const((1, D_MODEL)),
        const((len(POOL_WINDOWS), POOL_GROUP_DIM, POOL_GROUP_DIM)),
        const((1, POOL_WIDTH)),
        const((3, CONV_WIDTH)),
        const((1, CONV_WIDTH)),
        pl.BlockSpec((POOL_STATE, ts, POOL_WIDTH), lambda i, s: (0, i, 0)),
        pl.BlockSpec((CONV_STATE, ts, CONV_WIDTH), lambda i, s: (0, i, 0)),
        any_spec, any_spec, any_spec, any_spec,
    ]
    out_specs = [
        pl.BlockSpec((tm, MIX_CH), lambda i, s: (i, oc(s))),
        pl.BlockSpec((ts, MIX_CH), lambda i, s: (i, oc(s))),
        pl.BlockSpec((None, CARRY_P, POOL_WIDTH), lambda i, s: (i // tps, 0, 0)),
        pl.BlockSpec((None, CARRY_Q, CONV_WIDTH), lambda i, s: (i // tps, 0, 0)),
        pl.BlockSpec((POOL_STATE, ts, POOL_WIDTH), lambda i, s: (0, i, 0)),
        pl.BlockSpec((CONV_STATE, ts, CONV_WIDTH), lambda i, s: (0, i, 0)),
    ]
    out_shape = [
        jax.ShapeDtypeStruct(x1p.shape, F32),
        jax.ShapeDtypeStruct(x1s.shape, F32),
        jax.ShapeDtypeStruct((n_seq, CARRY_P, POOL_WIDTH), F32),
        jax.ShapeDtypeStruct((n_seq, CARRY_Q, CONV_WIDTH), F32),
        jax.ShapeDtypeStruct((POOL_STATE, n_s, POOL_WIDTH), F32),
        jax.ShapeDtypeStruct((CONV_STATE, n_s, CONV_WIDTH), F32),
    ]
    scratch = [
        pltpu.VMEM((MIX_NB, D_MODEL, MIX_CH), F32),
        pltpu.SemaphoreType.DMA((MIX_NB,)),
        pltpu.VMEM((X_BUFS, X_ROWS, D_MODEL), F32),
        pltpu.SemaphoreType.DMA((X_BUFS,)),
        pltpu.VMEM((rows, D_MODEL), BF16),
        pltpu.VMEM((len(POOL_WINDOWS), rows, POOL_GROUP_DIM), BF16),
        pltpu.VMEM((_N_CONV_CHUNKS, rows, MIX_CH), BF16),
        pltpu.VMEM((_N_OUT_CHUNKS, rows, MIX_CH), BF16),
        pltpu.VMEM((2, rows, MIX_CH), F32),
        pltpu.VMEM((rows, MIX_CH), F32),
        pltpu.VMEM((rows, MIX_CH), F32),
        pltpu.VMEM((rows, POOL_GROUP_DIM), BF16),
        pltpu.VMEM((tm + 2 * CARRY_P, POOL_GROUP_DIM), F32),
        pltpu.VMEM((tm + 2 * CARRY_P, POOL_GROUP_DIM), F32),
        pltpu.VMEM((tm + CARRY_Q, MIX_CH), F32),
        pltpu.VMEM((CARRY_P, POOL_WIDTH), F32),
        pltpu.VMEM((CARRY_Q, CONV_WIDTH), F32),
    ]
    return pl.pallas_call(
        functools.partial(_mixer_kernel, tm=tm, ts=ts, tps=tps, n_tiles=n_tiles),
        grid=(n_tiles, _N_STAGES),
        in_specs=in_specs,
        out_specs=out_specs,
        out_shape=out_shape,
        scratch_shapes=scratch,
        compiler_params=pltpu.CompilerParams(
            dimension_semantics=("arbitrary", "arbitrary"), vmem_limit_bytes=VMEM_LIMIT_BYTES),
        name="mixer",
    )(x1p, x1s, x1p, x1s, mod_p3, mod_p3, mod_p3, mod_s, mod_s, mod_s, nrm, grp, pscale, cw, cbias,
      stp_t, stc_t, w_in, w_a, w_b, w_o)


def kernel(x_prompt, x_sample, c_prompt, c_sample, state_pool, state_conv, w_ada, b_ada, norm1, ffn1_gate, ffn1_up, ffn1_down, norm2, w_in, pool_grp, pool_scale, w_branch_a, conv_w, conv_b, w_branch_b, w_o, norm3, ffn2_gate, ffn2_up, ffn2_down, norm_final):
    assert w_ada.shape[0] == 1, "single-layer step"
    batch, seq, d = x_prompt.shape
    dec_batch = x_sample.shape[0]
    assert d == D_MODEL and x_sample.shape[1] == 1

    tm = TM_PROMPT
    tps = seq // tm
    n_tiles = batch * tps
    ts = dec_batch // n_tiles
    assert seq % tm == 0 and dec_batch % n_tiles == 0 and ts % (2 * SUBLANES) == 0

    n_c = batch + dec_batch
    pad = (-n_c) % SUBLANES
    c_all = jnp.concatenate([c_sample, c_prompt, jnp.zeros((pad, d), F32)], axis=0)
    mod = _ada(c_all, w_ada[0], b_ada)
    mod_p3 = mod[dec_batch:n_c].reshape(batch, 1, N_MOD * d)

    row = lambda a: a.reshape(1, -1)
    xp = x_prompt.reshape(batch * seq, d)
    xs = x_sample.reshape(dec_batch, d)
    nf = row(norm_final)
    x1p, x1s = _ffn(xp, xs, mod_p3, mod, MOD_SH1, MOD_SC1, MOD_G1, row(norm1[0]), ffn1_gate[0], ffn1_up[0],
                    ffn1_down[0], nf, tm=tm, ts=ts, tps=tps, final_norm=False, name="ffn1")
    stp_t = jnp.transpose(state_pool[0], (1, 0, 2))
    stc_t = jnp.transpose(state_conv[0], (1, 0, 2))
    x2p, x2s, pst, cst, nps_t, ncs_t = _mixer(
        x1p, x1s, mod_p3, mod, row(norm2[0]), pool_grp[0], row(pool_scale[0]), conv_w[0], row(conv_b[0]),
        stp_t, stc_t, w_in[0], w_branch_a[0], w_branch_b[0], w_o[0], tm=tm, ts=ts, tps=tps)
    yp, ys = _ffn(x2p, x2s, mod_p3, mod, MOD_SH3, MOD_SC3, MOD_G3, row(norm3[0]), ffn2_gate[0].astype(BF16),
                  ffn2_up[0].astype(BF16), ffn2_down[0].astype(BF16), nf, tm=tm, ts=ts, tps=tps, final_norm=True,
                  name="ffn2")

    y_prompt = yp.reshape(batch, seq, d)
    y_sample = ys.reshape(dec_batch, 1, d)
    new_pool_prompt = pst[None, :, CARRY_P - POOL_STATE:, :]
    new_conv_prompt = cst[None, :, CARRY_Q - CONV_STATE:, :]
    new_pool_sample = jnp.transpose(nps_t, (1, 0, 2))[None]
    new_conv_sample = jnp.transpose(ncs_t, (1, 0, 2))[None]
    return (y_prompt, y_sample, new_pool_prompt, new_conv_prompt, new_pool_sample, new_conv_sample)
```

```python
import functools

import jax
import jax.numpy as jnp
from jax import lax
from jax.experimental import pallas as pl
from jax.experimental.pallas import tpu as pltpu

F32 = jnp.float32
BF16 = jnp.bfloat16

D_MODEL = 2048
D_FF = 5632
POOL_WIDTH = 1024
CONV_WIDTH = 1024
POOL_WINDOWS = (2, 4, 8, 16)
POOL_GROUP_DIM = 256
POOL_STATE = 15
CONV_STATE = 2
N_MOD = 9
EPS = 1e-6
PAST_LEN = 16384

V7X_VMEM_BYTES = 64 * 1024 * 1024
SUBLANES = 8
VMEM_LIMIT_BYTES = V7X_VMEM_BYTES - 2 * 1024 * 1024

TM_PROMPT = 1024

ADA_TN = 1024
FFN_TF = 256
FFN_DMA_SPLIT = 4
MIX_CH = 512
MIX_NB = 3
X_ROWS = 128
FFN_ROWS = 128
X_BUFS = 4
MIX_DMA_SPLIT = 8
CARRY_P = 16
CARRY_Q = 8

COL_P, COL_CB, COL_CC, COL_CH, COL_GA, COL_GB = 0, 1024, 2048, 3072, 4096, 6144
MOD_SH1, MOD_SC1, MOD_G1, MOD_SH2, MOD_SC2, MOD_G2, MOD_SH3, MOD_SC3, MOD_G3 = range(9)


def _sigmoid(x):
    return 1.0 / (1.0 + jnp.exp(-x))


def _modulate(x, nrm, sc, sh):
    inv = lax.rsqrt(jnp.mean(x * x, axis=-1, keepdims=True) + EPS)
    return (x * inv) * nrm * (1.0 + sc) + sh


def _bdot(a, b):
    return jnp.dot(a, b, preferred_element_type=F32)


def _ada_kernel(c_ref, w_ref, b_ref, o_ref):
    c = c_ref[...]
    s = (c * _sigmoid(c)).astype(BF16)
    o_ref[...] = _bdot(s, w_ref[...].astype(BF16)) + b_ref[...]


def _ada(c_all, w_ada, b_ada):
    rows = c_all.shape[0]
    cols = w_ada.shape[1]
    return pl.pallas_call(
        _ada_kernel,
        grid=(cols // ADA_TN,),
        in_specs=[
            pl.BlockSpec((rows, D_MODEL), lambda j: (0, 0)),
            pl.BlockSpec((D_MODEL, ADA_TN), lambda j: (0, j)),
            pl.BlockSpec((1, ADA_TN), lambda j: (0, j)),
        ],
        out_specs=pl.BlockSpec((rows, ADA_TN), lambda j: (0, j)),
        out_shape=jax.ShapeDtypeStruct((rows, cols), F32),
        compiler_params=pltpu.CompilerParams(dimension_semantics=("arbitrary",)),
        name="ada_mod",
    )(c_all, w_ada, b_ada)


def _ffn_kernel(xp_ref, xs_ref, shp, scp, gtp, shs, scs, gts, nrm_ref, wg_hbm, wu_hbm, wd_hbm, nf_ref,
                op_ref, os_ref, h_ref, inv_ref, wgb, wub, wdb, wsem, *, n_f, n_tiles, final_norm):
    i = pl.program_id(0)
    tm = xp_ref.shape[0]
    ts = xs_ref.shape[0]

    def wcopies(f, slot):
        col = pl.multiple_of(f * FFN_TF, FFN_TF)
        rows_in = D_MODEL // FFN_DMA_SPLIT
        rows_dn = FFN_TF // FFN_DMA_SPLIT
        out = []
        for k in range(FFN_DMA_SPLIT):
            r_in = pl.ds(k * rows_in, rows_in)
            out.append(pltpu.make_async_copy(wg_hbm.at[r_in, pl.ds(col, FFN_TF)], wgb.at[slot, r_in, :],
                                             wsem.at[slot]))
            out.append(pltpu.make_async_copy(wu_hbm.at[r_in, pl.ds(col, FFN_TF)], wub.at[slot, r_in, :],
                                             wsem.at[slot]))
            out.append(pltpu.make_async_copy(wd_hbm.at[pl.ds(col + k * rows_dn, rows_dn), :],
                                             wdb.at[slot, pl.ds(k * rows_dn, rows_dn), :], wsem.at[slot]))
        return out

    def wstart(f, slot):
        for cp in wcopies(f, slot):
            cp.start()

    def wwait(f, slot):
        for cp in wcopies(f, slot):
            cp.wait()

    @pl.when(i == 0)
    def _():
        wstart(jnp.int32(0), 0)

    def head(x, sc, sh):
        return _modulate(x, nrm_ref[...], sc, sh).astype(BF16)

    def tail(x, gt, acc):
        xo = x + 0.5 * gt * acc
        if final_norm:
            inv = lax.rsqrt(jnp.mean(xo * xo, axis=-1, keepdims=True) + EPS)
            xo = (xo * inv) * nf_ref[...]
        return xo

    @pl.loop(0, tm // FFN_ROWS)
    def _(r):
        rows = pl.ds(pl.multiple_of(r * FFN_ROWS, FFN_ROWS), FFN_ROWS)
        x = xp_ref[rows, :]
        inv_ref[rows, :] = lax.rsqrt(jnp.mean(x * x, axis=-1, keepdims=True) + EPS)

    @pl.loop(0, tm // FFN_ROWS)
    def _(r):
        rows = pl.ds(pl.multiple_of(r * FFN_ROWS, FFN_ROWS), FFN_ROWS)
        hx = (xp_ref[rows, :] * inv_ref[rows, :]) * nrm_ref[...] * (1.0 + scp[...]) + shp[...]
        h_ref[rows, :] = hx.astype(BF16)
        op_ref[rows, :] = jnp.zeros((FFN_ROWS, D_MODEL), F32)

    h_ref[pl.ds(tm, ts), :] = head(xs_ref[...], scs[...], shs[...])
    os_ref[...] = jnp.zeros((ts, D_MODEL), F32)

    @pl.loop(0, n_f)
    def _(f):
        slot = lax.rem(f, 2)
        wwait(f, slot)

        @pl.when(f + 1 < n_f)
        def _():
            wstart(f + 1, 1 - slot)

        @pl.when(jnp.logical_and(f + 1 == n_f, i < n_tiles - 1))
        def _():
            wstart(jnp.int32(0), 1 - slot)

        h = h_ref[...]
        g = _bdot(h, wgb[slot].astype(BF16))
        u = _bdot(h, wub[slot].astype(BF16))
        a = ((g * _sigmoid(g)) * u).astype(BF16)
        for half in range(2):
            hcols = pl.ds(half * (D_MODEL // 2), D_MODEL // 2)
            y = _bdot(a, wdb[slot, :, hcols].astype(BF16))
            op_ref[:, hcols] += y[:tm]
            os_ref[:, hcols] += y[tm:]

    @pl.loop(0, tm // FFN_ROWS)
    def _(r):
        rows = pl.ds(pl.multiple_of(r * FFN_ROWS, FFN_ROWS), FFN_ROWS)
        op_ref[rows, :] = tail(xp_ref[rows, :], gtp[...], op_ref[rows, :])

    os_ref[...] = tail(xs_ref[...], gts[...], os_ref[...])


def _ffn(xp, xs, mod_p3, mod_s, j_sh, j_sc, j_g, nrm, wg, wu, wd, nf, *, tm, ts, tps, final_norm, name):
    n_tiles = xp.shape[0] // tm
    assert xs.shape[0] == n_tiles * ts
    n_f = D_FF // FFN_TF
    assert n_f % 2 == 0
    any_spec = pl.BlockSpec(memory_space=pl.ANY)

    def pspec(j):
        return pl.BlockSpec((None, 1, D_MODEL), lambda i: (i // tps, 0, j))

    def sspec(j):
        return pl.BlockSpec((ts, D_MODEL), lambda i: (i, j))

    return pl.pallas_call(
        functools.partial(_ffn_kernel, n_f=n_f, n_tiles=n_tiles, final_norm=final_norm),
        grid=(n_tiles,),
        in_specs=[
            pl.BlockSpec((tm, D_MODEL), lambda i: (i, 0)),
            pl.BlockSpec((ts, D_MODEL), lambda i: (i, 0)),
            pspec(j_sh), pspec(j_sc), pspec(j_g),
            sspec(j_sh), sspec(j_sc), sspec(j_g),
            pl.BlockSpec((1, D_MODEL), lambda i: (0, 0)),
            any_spec, any_spec, any_spec,
            pl.BlockSpec((1, D_MODEL), lambda i: (0, 0)),
        ],
        out_specs=[pl.BlockSpec((tm, D_MODEL), lambda i: (i, 0)),
                   pl.BlockSpec((ts, D_MODEL), lambda i: (i, 0))],
        out_shape=[jax.ShapeDtypeStruct(xp.shape, F32), jax.ShapeDtypeStruct(xs.shape, F32)],
        scratch_shapes=[pltpu.VMEM((tm + ts, D_MODEL), BF16), pltpu.VMEM((tm, 1), F32),
                        pltpu.VMEM((2, D_MODEL, FFN_TF), F32), pltpu.VMEM((2, D_MODEL, FFN_TF), F32),
                        pltpu.VMEM((2, FFN_TF, D_MODEL), F32), pltpu.SemaphoreType.DMA((2,))],
        compiler_params=pltpu.CompilerParams(
            dimension_semantics=("arbitrary",), vmem_limit_bytes=VMEM_LIMIT_BYTES),
        name=name,
    )(xp, xs, mod_p3, mod_p3, mod_p3, mod_s, mod_s, mod_s, nrm, wg, wu, wd, nf)


K_P, K_CC, K_CH, K_CB, K_GA, K_GB, K_AB, K_O = range(8)
_N_CONV_CHUNKS = CONV_WIDTH // MIX_CH
_N_OUT_CHUNKS = D_MODEL // MIX_CH
_S_CONV = POOL_WIDTH // MIX_CH
_S_GATE = _S_CONV + 3 * _N_CONV_CHUNKS
_S_OUT = _S_GATE + 3 * _N_OUT_CHUNKS
_N_STAGES = _S_OUT + _N_OUT_CHUNKS
_KIND_COL0 = (COL_P, COL_CC, COL_CH, COL_CB, COL_GA, COL_GB, 0, 0)
assert _N_STAGES % MIX_NB == 0


def _decode_stage(t):
    u1 = t - _S_CONV
    u2 = t - _S_GATE
    c1 = lax.div(jnp.maximum(u1, 0), 3)
    c2 = lax.div(jnp.maximum(u2, 0), 3)
    kind = jnp.where(t < _S_CONV, K_P,
                     jnp.where(t < _S_GATE, K_CC + (u1 - 3 * c1),
                               jnp.where(t < _S_OUT, K_GA + (u2 - 3 * c2), K_O)))
    chunk = jnp.where(t < _S_CONV, t, jnp.where(t < _S_GATE, c1, jnp.where(t < _S_OUT, c2, t - _S_OUT)))
    col0 = jnp.int32(0)
    for k, base in enumerate(_KIND_COL0):
        if base:
            col0 = jnp.where(kind == k, base, col0)
    return kind, chunk, col0 + chunk * MIX_CH


def _mixer_kernel(x1p_hbm, x1s, x1pc, x1sc, shp, scp, g2pc, shs, scs, g2sc, nrm, grp, pscale, cw, cbias,
                  stp, stc, w_in, w_a, w_b, w_o,
                  x2p, x2s, pst, cst, nps, ncs,
                  wbuf, wsem, xbuf, xsem, h2, ap, v, m, zb, ybuf, sa, abf, e0, e1, qext, pcar, qcar,
                  *, tm, ts, tps, n_tiles):
    i = pl.program_id(0)
    s = pl.program_id(1)
    kind, chunk, _ = _decode_stage(s)
    slot = lax.rem(s, MIX_NB)
    slab = D_MODEL // MIX_DMA_SPLIT
    half = D_MODEL // 2
    prows = pl.ds(0, tm)
    srows = pl.ds(tm, ts)

    def wcopy(src, src_row, col, dst_slot, dst_row):
        return pltpu.make_async_copy(src.at[pl.ds(src_row, slab), pl.ds(col, MIX_CH)],
                                     wbuf.at[dst_slot, pl.ds(dst_row, slab), :], wsem.at[dst_slot])

    def wstart(t):
        k_t, _, col_t = _decode_stage(t)
        col_t = pl.multiple_of(col_t, MIX_CH)
        slot_t = lax.rem(t, MIX_NB)

        @pl.when(k_t < K_AB)
        def _():
            for k in range(MIX_DMA_SPLIT):
                wcopy(w_in, k * slab, col_t, slot_t, k * slab).start()

        @pl.when(k_t == K_AB)
        def _():
            for k in range(MIX_DMA_SPLIT):
                r0 = k * slab
                src, s0 = (w_a, r0) if r0 < half else (w_b, r0 - half)
                wcopy(src, s0, col_t, slot_t, r0).start()

        @pl.when(k_t == K_O)
        def _():
            for k in range(MIX_DMA_SPLIT):
                wcopy(w_o, k * slab, col_t, slot_t, k * slab).start()

    def wwait():
        for k in range(MIX_DMA_SPLIT):
            wcopy(w_in, k * slab, 0, slot, k * slab).wait()

    n_xchunks = tm // X_ROWS

    def xcopy(tile, r):
        return pltpu.make_async_copy(x1p_hbm.at[pl.ds(tile * tm + r * X_ROWS, X_ROWS), :],
                                     xbuf.at[r % X_BUFS], xsem.at[r % X_BUFS])

    def xprefetch(tile):
        for r in range(min(X_BUFS, n_xchunks)):
            xcopy(tile, r).start()

    def prologue():
        for r in range(n_xchunks):
            xcopy(i, r).wait()
            h2[pl.ds(r * X_ROWS, X_ROWS), :] = _modulate(xbuf[r % X_BUFS], nrm[...], scp[...], shp[...]).astype(BF16)
            if r + X_BUFS < n_xchunks:
                xcopy(i, r + X_BUFS).start()
        h2[srows, :] = _modulate(x1s[...], nrm[...], scs[...], shs[...]).astype(BF16)
        for r in range(POOL_STATE - 1):
            nps[r] = stp[r + 1]
        for r in range(CONV_STATE - 1):
            ncs[r] = stc[r + 1]

    def pool_stage(c):
        zbuf = zb.at[0]
        for gl in range(MIX_CH // POOL_GROUP_DIM):
            g = c * (MIX_CH // POOL_GROUP_DIM) + gl
            w = POOL_WINDOWS[g]
            gcols = pl.ds(g * POOL_GROUP_DIM, POOL_GROUP_DIM)
            lcols = pl.ds(gl * POOL_GROUP_DIM, POOL_GROUP_DIM)
            pg = zbuf[prows, lcols]
            pad = jnp.zeros((SUBLANES, POOL_GROUP_DIM), F32)
            e0[pl.ds(SUBLANES, SUBLANES), :] = pad
            e1[pl.ds(SUBLANES, SUBLANES), :] = pad
            e0[pl.ds(CARRY_P, CARRY_P), :] = pcar[:, gcols]
            e0[pl.ds(2 * CARRY_P, tm), :] = pg
            src, dst = e0, e1
            d = 1
            while d < w:
                dst[pl.ds(CARRY_P, tm + CARRY_P), :] = (src[pl.ds(CARRY_P, tm + CARRY_P), :]
                                                        + src[pl.ds(CARRY_P - d, tm + CARRY_P), :])
                src, dst = dst, src
                d *= 2
            wsum = src[pl.ds(2 * CARRY_P, tm), :]
            pos = (i % tps) * tm + lax.broadcasted_iota(jnp.int32, (tm, 1), 0)
            cnt = jnp.minimum(pos + 1, w).astype(F32)
            abf[prows, :] = (wsum / cnt - pg).astype(BF16)
            tail = pg[tm - CARRY_P:, :]
            pcar[:, gcols] = tail
            pst[:, gcols] = tail
            ps = zbuf[srows, lcols]
            ssum = ps
            for r in range(POOL_STATE + 1 - w, POOL_STATE):
                ssum = ssum + stp[r, :, gcols]
            abf[srows, :] = (ssum / float(min(PAST_LEN + 1, w)) - ps).astype(BF16)
            nps[POOL_STATE - 1, :, gcols] = ps
            ag = _bdot(abf[...], grp[g].astype(BF16)) * pscale[:, gcols]
            ap[g] = ag.astype(BF16)

    def conv_stage(c):
        ccols = pl.ds(c * MIX_CH, MIX_CH)
        zbuf, zprev = zb.at[0], zb.at[1]
        qp = zprev[prows, :] * zbuf[prows, :]
        qext[pl.ds(0, CARRY_Q), :] = qcar[:, ccols]
        qext[pl.ds(CARRY_Q, tm), :] = qp
        y = cbias[:, ccols] + qext[pl.ds(CARRY_Q - 2, tm), :] * cw[0:1, ccols]
        y = y + qext[pl.ds(CARRY_Q - 1, tm), :] * cw[1:2, ccols]
        tail = qp[tm - CARRY_Q:, :]
        qcar[:, ccols] = tail
        cst[:, ccols] = tail
        qs = zprev[srows, :] * zbuf[srows, :]
        ys = cbias[:, ccols] + stc[0, :, ccols] * cw[0:1, ccols]
        ys = ys + stc[1, :, ccols] * cw[1:2, ccols]
        ncs[CONV_STATE - 1, :, ccols] = qs
        ybuf[prows, :] = y + qp * cw[2:3, ccols]
        ybuf[srows, :] = ys + qs * cw[2:3, ccols]

    @pl.when(s == 0)
    def _():
        @pl.when(i == 0)
        def _():
            for t in range(MIX_NB - 1):
                wstart(jnp.int32(t))
            xprefetch(0)
            zb[...] = jnp.zeros(zb.shape, F32)

        @pl.when(i % tps == 0)
        def _():
            pcar[...] = jnp.zeros(pcar.shape, F32)
            qcar[...] = jnp.zeros(qcar.shape, F32)

        prologue()

    wwait()
    nxt = s + (MIX_NB - 1)

    @pl.when(nxt < _N_STAGES)
    def _():
        wstart(nxt)

    @pl.when(jnp.logical_and(nxt >= _N_STAGES, i < n_tiles - 1))
    def _():
        wstart(nxt - _N_STAGES)

    @pl.when(jnp.logical_and(s == _S_OUT + 1, i < n_tiles - 1))
    def _():
        xprefetch(i + 1)

    @pl.when(kind < K_AB)
    def _():
        sa[...] = _sigmoid(zb[1])
        zb[0] = _bdot(h2[...], wbuf[slot].astype(BF16))

    @pl.when(jnp.logical_or(kind == K_CC, kind == K_GA))
    def _():
        zb[1] = zb[0]

    for c in range(POOL_WIDTH // MIX_CH):
        @pl.when(jnp.logical_and(kind == K_P, chunk == c))
        def _(c=c):
            pool_stage(c)

    for c in range(_N_CONV_CHUNKS):
        @pl.when(jnp.logical_and(kind == K_CH, chunk == c))
        def _(c=c):
            conv_stage(c)

    @pl.when(kind == K_CB)
    def _():
        v[chunk] = (zb[0] * ybuf[...]).astype(BF16)

    @pl.when(kind == K_AB)
    def _():
        ua = None
        for g in range(len(POOL_WINDOWS)):
            part = _bdot(ap[g], wbuf[slot, pl.ds(g * POOL_GROUP_DIM, POOL_GROUP_DIM), :].astype(BF16))
            ua = part if ua is None else ua + part
        ub = None
        for k in range(_N_CONV_CHUNKS):
            part = _bdot(v[k], wbuf[slot, pl.ds(half + k * MIX_CH, MIX_CH), :].astype(BF16))
            ub = part if ub is None else ub + part
        m[chunk] = (sa[...] * ua + _sigmoid(zb[0]) * ub).astype(BF16)

    @pl.when(kind == K_O)
    def _():
        acc = None
        for k in range(_N_OUT_CHUNKS):
            part = _bdot(m[k], wbuf[slot, pl.ds(k * MIX_CH, MIX_CH), :].astype(BF16))
            acc = part if acc is None else acc + part
        x2p[...] = x1pc[...] + g2pc[...] * acc[:tm]
        x2s[...] = x1sc[...] + g2sc[...] * acc[tm:]


def _mixer(x1p, x1s, mod_p3, mod_s, nrm, grp, pscale, cw, cbias, stp_t, stc_t, w_in, w_a, w_b, w_o, *, tm, ts, tps):
    n_tiles = x1p.shape[0] // tm
    n_seq = n_tiles // tps
    n_s = x1s.shape[0]
    assert n_s == n_tiles * ts
    rows = tm + ts
    any_spec = pl.BlockSpec(memory_space=pl.ANY)

    def const(shape):
        return pl.BlockSpec(shape, lambda i, s: (0,) * len(shape))

    def oc(s):
        return jnp.maximum(s - _S_OUT, 0)

    g2_blk = (MOD_G2 * D_MODEL) // MIX_CH
    in_specs = [
        any_spec,
        pl.BlockSpec((ts, D_MODEL), lambda i, s: (i, 0)),
        pl.BlockSpec((tm, MIX_CH), lambda i, s: (i, oc(s))),
        pl.BlockSpec((ts, MIX_CH), lambda i, s: (i, oc(s))),
        pl.BlockSpec((None, 1, D_MODEL), lambda i, s: (i // tps, 0, MOD_SH2)),
        pl.BlockSpec((None, 1, D_MODEL), lambda i, s: (i // tps, 0, MOD_SC2)),
        pl.BlockSpec((None, 1, MIX_CH), lambda i, s: (i // tps, 0, g2_blk + oc(s))),
        pl.BlockSpec((ts, D_MODEL), lambda i, s: (i, MOD_SH2)),
        pl.BlockSpec((ts, D_MODEL), lambda i, s: (i, MOD_SC2)),
        pl.BlockSpec((ts, MIX_CH), lambda i, s: (i, g2_blk + oc(s))),
        const((1, D_MODEL)),
        const((len(POOL_WINDOWS), POOL_GROUP_DIM, POOL_GROUP_DIM)),
        const((1, POOL_WIDTH)),
        const((3, CONV_WIDTH)),
        const((1, CONV_WIDTH)),
        pl.BlockSpec((POOL_STATE, ts, POOL_WIDTH), lambda i, s: (0, i, 0)),
        pl.BlockSpec((CONV_STATE, ts, CONV_WIDTH), lambda i, s: (0, i, 0)),
        any_spec, any_spec, any_spec, any_spec,
    ]
    out_specs = [
        pl.BlockSpec((tm, MIX_CH), lambda i, s: (i, oc(s))),
        pl.BlockSpec((ts, MIX_CH), lambda i, s: (i, oc(s))),
        pl.BlockSpec((None, CARRY_P, POOL_WIDTH), lambda i, s: (i // tps, 0, 0)),
        pl.BlockSpec((None, CARRY_Q, CONV_WIDTH), lambda i, s: (i // tps, 0, 0)),
        pl.BlockSpec((POOL_STATE, ts, POOL_WIDTH), lambda i, s: (0, i, 0)),
        pl.BlockSpec((CONV_STATE, ts, CONV_WIDTH), lambda i, s: (0, i, 0)),
    ]
    out_shape = [
        jax.ShapeDtypeStruct(x1p.shape, F32),
        jax.ShapeDtypeStruct(x1s.shape, F32),
        jax.ShapeDtypeStruct((n_seq, CARRY_P, POOL_WIDTH), F32),
        jax.ShapeDtypeStruct((n_seq, CARRY_Q, CONV_WIDTH), F32),
        jax.ShapeDtypeStruct((POOL_STATE, n_s, POOL_WIDTH), F32),
        jax.ShapeDtypeStruct((CONV_STATE, n_s, CONV_WIDTH), F32),
    ]
    scratch = [
        pltpu.VMEM((MIX_NB, D_MODEL, MIX_CH), F32),
        pltpu.SemaphoreType.DMA((MIX_NB,)),
        pltpu.VMEM((X_BUFS, X_ROWS, D_MODEL), F32),
        pltpu.SemaphoreType.DMA((X_BUFS,)),
        pltpu.VMEM((rows, D_MODEL), BF16),
        pltpu.VMEM((len(POOL_WINDOWS), rows, POOL_GROUP_DIM), BF16),
        pltpu.VMEM((_N_CONV_CHUNKS, rows, MIX_CH), BF16),
        pltpu.VMEM((_N_OUT_CHUNKS, rows, MIX_CH), BF16),
        pltpu.VMEM((2, rows, MIX_CH), F32),
        pltpu.VMEM((rows, MIX_CH), F32),
        pltpu.VMEM((rows, MIX_CH), F32),
        pltpu.VMEM((rows, POOL_GROUP_DIM), BF16),
        pltpu.VMEM((tm + 2 * CARRY_P, POOL_GROUP_DIM), F32),
        pltpu.VMEM((tm + 2 * CARRY_P, POOL_GROUP_DIM), F32),
        pltpu.VMEM((tm + CARRY_Q, MIX_CH), F32),
        pltpu.VMEM((CARRY_P, POOL_WIDTH), F32),
        pltpu.VMEM((CARRY_Q, CONV_WIDTH), F32),
    ]
    return pl.pallas_call(
        functools.partial(_mixer_kernel, tm=tm, ts=ts, tps=tps, n_tiles=n_tiles),
        grid=(n_tiles, _N_STAGES),
        in_specs=in_specs,
        out_specs=out_specs,
        out_shape=out_shape,
        scratch_shapes=scratch,
        compiler_params=pltpu.CompilerParams(
            dimension_semantics=("arbitrary", "arbitrary"), vmem_limit_bytes=VMEM_LIMIT_BYTES),
        name="mixer",
    )(x1p, x1s, x1p, x1s, mod_p3, mod_p3, mod_p3, mod_s, mod_s, mod_s, nrm, grp, pscale, cw, cbias,
      stp_t, stc_t, w_in, w_a, w_b, w_o)


def kernel(x_prompt, x_sample, c_prompt, c_sample, state_pool, state_conv, w_ada, b_ada, norm1, ffn1_gate, ffn1_up, ffn1_down, norm2, w_in, pool_grp, pool_scale, w_branch_a, conv_w, conv_b, w_branch_b, w_o, norm3, ffn2_gate, ffn2_up, ffn2_down, norm_final):
    assert w_ada.shape[0] == 1, "single-layer step"
    batch, seq, d = x_prompt.shape
    dec_batch = x_sample.shape[0]
    assert d == D_MODEL and x_sample.shape[1] == 1

    tm = TM_PROMPT
    tps = seq // tm
    n_tiles = batch * tps
    ts = dec_batch // n_tiles
    assert seq % tm == 0 and dec_batch % n_tiles == 0 and ts % (2 * SUBLANES) == 0

    n_c = batch + dec_batch
    pad = (-n_c) % SUBLANES
    c_all = jnp.concatenate([c_sample, c_prompt, jnp.zeros((pad, d), F32)], axis=0)
    mod = _ada(c_all, w_ada[0], b_ada)
    mod_p3 = mod[dec_batch:n_c].reshape(batch, 1, N_MOD * d)

    row = lambda a: a.reshape(1, -1)
    xp = x_prompt.reshape(batch * seq, d)
    xs = x_sample.reshape(dec_batch, d)
    nf = row(norm_final)
    x1p, x1s = _ffn(xp, xs, mod_p3, mod, MOD_SH1, MOD_SC1, MOD_G1, row(norm1[0]), ffn1_gate[0], ffn1_up[0],
                    ffn1_down[0], nf, tm=tm, ts=ts, tps=tps, final_norm=False, name="ffn1")
    stp_t = jnp.transpose(state_pool[0], (1, 0, 2))
    stc_t = jnp.transpose(state_conv[0], (1, 0, 2))
    x2p, x2s, pst, cst, nps_t, ncs_t = _mixer(
        x1p, x1s, mod_p3, mod, row(norm2[0]), pool_grp[0], row(pool_scale[0]), conv_w[0], row(conv_b[0]),
        stp_t, stc_t, w_in[0], w_branch_a[0], w_branch_b[0], w_o[0], tm=tm, ts=ts, tps=tps)
    yp, ys = _ffn(x2p, x2s, mod_p3, mod, MOD_SH3, MOD_SC3, MOD_G3, row(norm3[0]), ffn2_gate[0], ffn2_up[0],
                  ffn2_down[0], nf, tm=tm, ts=ts, tps=tps, final_norm=True, name="ffn2")

    y_prompt = yp.reshape(batch, seq, d)
    y_sample = ys.reshape(dec_batch, 1, d)
    new_pool_prompt = pst[None, :, CARRY_P - POOL_STATE:, :]
    new_conv_prompt = cst[None, :, CARRY_Q - CONV_STATE:, :]
    new_pool_sample = jnp.transpose(nps_t, (1, 0, 2))[None]
    new_conv_sample = jnp.transpose(ncs_t, (1, 0, 2))[None]
    return (y_prompt, y_sample, new_pool_prompt, new_conv_prompt, new_pool_sample, new_conv_sample)
```

```python
import functools

import jax
import jax.numpy as jnp
from jax import lax
from jax.experimental import pallas as pl
from jax.experimental.pallas import tpu as pltpu

F32 = jnp.float32
BF16 = jnp.bfloat16

D_MODEL = 2048
D_FF = 5632
POOL_WIDTH = 1024
CONV_WIDTH = 1024
POOL_WINDOWS = (2, 4, 8, 16)
POOL_GROUP_DIM = 256
POOL_STATE = 15
CONV_STATE = 2
N_MOD = 9
EPS = 1e-6
PAST_LEN = 16384

V7X_VMEM_BYTES = 64 * 1024 * 1024
SUBLANES = 8
VMEM_LIMIT_BYTES = V7X_VMEM_BYTES - 2 * 1024 * 1024

TM_PROMPT = 1024

ADA_TN = 1024
FFN_TF = 256
FFN_DMA_SPLIT = 4
MIX_CH = 512
MIX_HALF = 256
MIX_NB = 3
X_ROWS = 128
FFN_ROWS = 128
X_BUFS = 4
MIX_DMA_SPLIT = 8
CARRY_P = 16
CARRY_Q = 8

COL_P, COL_CB, COL_CC, COL_CH, COL_GA, COL_GB = 0, 1024, 2048, 3072, 4096, 6144
MOD_SH1, MOD_SC1, MOD_G1, MOD_SH2, MOD_SC2, MOD_G2, MOD_SH3, MOD_SC3, MOD_G3 = range(9)


def _sigmoid(x):
    return 1.0 / (1.0 + jnp.exp(-x))


def _modulate(x, nrm, sc, sh):
    inv = lax.rsqrt(jnp.mean(x * x, axis=-1, keepdims=True) + EPS)
    return (x * inv) * nrm * (1.0 + sc) + sh


def _bdot(a, b):
    return jnp.dot(a, b, preferred_element_type=F32)


def _ada_kernel(c_ref, w_ref, b_ref, o_ref):
    c = c_ref[...]
    s = (c * _sigmoid(c)).astype(BF16)
    o_ref[...] = _bdot(s, w_ref[...].astype(BF16)) + b_ref[...]


def _ada(c_all, w_ada, b_ada):
    rows = c_all.shape[0]
    cols = w_ada.shape[1]
    return pl.pallas_call(
        _ada_kernel,
        grid=(cols // ADA_TN,),
        in_specs=[
            pl.BlockSpec((rows, D_MODEL), lambda j: (0, 0)),
            pl.BlockSpec((D_MODEL, ADA_TN), lambda j: (0, j)),
            pl.BlockSpec((1, ADA_TN), lambda j: (0, j)),
        ],
        out_specs=pl.BlockSpec((rows, ADA_TN), lambda j: (0, j)),
        out_shape=jax.ShapeDtypeStruct((rows, cols), F32),
        compiler_params=pltpu.CompilerParams(dimension_semantics=("arbitrary",)),
        name="ada_mod",
    )(c_all, w_ada, b_ada)


def _ffn_kernel(xp_ref, xs_ref, shp, scp, gtp, shs, scs, gts, nrm_ref, wg_hbm, wu_hbm, wd_hbm, nf_ref,
                op_ref, os_ref, h_ref, inv_ref, wgb, wub, wdb, wsem, *, n_f, n_tiles, final_norm):
    i = pl.program_id(0)
    tm = xp_ref.shape[0]
    ts = xs_ref.shape[0]

    def wcopies(f, slot):
        col = pl.multiple_of(f * FFN_TF, FFN_TF)
        rows_in = D_MODEL // FFN_DMA_SPLIT
        rows_dn = FFN_TF // FFN_DMA_SPLIT
        out = []
        for k in range(FFN_DMA_SPLIT):
            r_in = pl.ds(k * rows_in, rows_in)
            out.append(pltpu.make_async_copy(wg_hbm.at[r_in, pl.ds(col, FFN_TF)], wgb.at[slot, r_in, :],
                                             wsem.at[slot]))
            out.append(pltpu.make_async_copy(wu_hbm.at[r_in, pl.ds(col, FFN_TF)], wub.at[slot, r_in, :],
                                             wsem.at[slot]))
            out.append(pltpu.make_async_copy(wd_hbm.at[pl.ds(col + k * rows_dn, rows_dn), :],
                                             wdb.at[slot, pl.ds(k * rows_dn, rows_dn), :], wsem.at[slot]))
        return out

    def wstart(f, slot):
        for cp in wcopies(f, slot):
            cp.start()

    def wwait(f, slot):
        for cp in wcopies(f, slot):
            cp.wait()

    @pl.when(i == 0)
    def _():
        wstart(jnp.int32(0), 0)

    def head(x, sc, sh):
        return _modulate(x, nrm_ref[...], sc, sh).astype(BF16)

    def tail(x, gt, acc):
        xo = x + 0.5 * gt * acc
        if final_norm:
            inv = lax.rsqrt(jnp.mean(xo * xo, axis=-1, keepdims=True) + EPS)
            xo = (xo * inv) * nf_ref[...]
        return xo

    @pl.loop(0, tm // FFN_ROWS)
    def _(r):
        rows = pl.ds(pl.multiple_of(r * FFN_ROWS, FFN_ROWS), FFN_ROWS)
        x = xp_ref[rows, :]
        inv_ref[rows, :] = lax.rsqrt(jnp.mean(x * x, axis=-1, keepdims=True) + EPS)

    @pl.loop(0, tm // FFN_ROWS)
    def _(r):
        rows = pl.ds(pl.multiple_of(r * FFN_ROWS, FFN_ROWS), FFN_ROWS)
        hx = (xp_ref[rows, :] * inv_ref[rows, :]) * nrm_ref[...] * (1.0 + scp[...]) + shp[...]
        h_ref[rows, :] = hx.astype(BF16)
        op_ref[rows, :] = jnp.zeros((FFN_ROWS, D_MODEL), F32)

    h_ref[pl.ds(tm, ts), :] = head(xs_ref[...], scs[...], shs[...])
    os_ref[...] = jnp.zeros((ts, D_MODEL), F32)

    @pl.loop(0, n_f)
    def _(f):
        slot = lax.rem(f, 2)
        wwait(f, slot)

        @pl.when(f + 1 < n_f)
        def _():
            wstart(f + 1, 1 - slot)

        @pl.when(jnp.logical_and(f + 1 == n_f, i < n_tiles - 1))
        def _():
            wstart(jnp.int32(0), 1 - slot)

        h = h_ref[...]
        g = _bdot(h, wgb[slot].astype(BF16))
        u = _bdot(h, wub[slot].astype(BF16))
        a = ((g * _sigmoid(g)) * u).astype(BF16)
        for half in range(2):
            hcols = pl.ds(half * (D_MODEL // 2), D_MODEL // 2)
            y = _bdot(a, wdb[slot, :, hcols].astype(BF16))
            op_ref[:, hcols] += y[:tm]
            os_ref[:, hcols] += y[tm:]

    @pl.loop(0, tm // FFN_ROWS)
    def _(r):
        rows = pl.ds(pl.multiple_of(r * FFN_ROWS, FFN_ROWS), FFN_ROWS)
        op_ref[rows, :] = tail(xp_ref[rows, :], gtp[...], op_ref[rows, :])

    os_ref[...] = tail(xs_ref[...], gts[...], os_ref[...])


def _ffn(xp, xs, mod_p3, mod_s, j_sh, j_sc, j_g, nrm, wg, wu, wd, nf, *, tm, ts, tps, final_norm, name):
    n_tiles = xp.shape[0] // tm
    assert xs.shape[0] == n_tiles * ts
    n_f = D_FF // FFN_TF
    assert n_f % 2 == 0
    any_spec = pl.BlockSpec(memory_space=pl.ANY)

    def pspec(j):
        return pl.BlockSpec((None, 1, D_MODEL), lambda i: (i // tps, 0, j))

    def sspec(j):
        return pl.BlockSpec((ts, D_MODEL), lambda i: (i, j))

    return pl.pallas_call(
        functools.partial(_ffn_kernel, n_f=n_f, n_tiles=n_tiles, final_norm=final_norm),
        grid=(n_tiles,),
        in_specs=[
            pl.BlockSpec((tm, D_MODEL), lambda i: (i, 0)),
            pl.BlockSpec((ts, D_MODEL), lambda i: (i, 0)),
            pspec(j_sh), pspec(j_sc), pspec(j_g),
            sspec(j_sh), sspec(j_sc), sspec(j_g),
            pl.BlockSpec((1, D_MODEL), lambda i: (0, 0)),
            any_spec, any_spec, any_spec,
            pl.BlockSpec((1, D_MODEL), lambda i: (0, 0)),
        ],
        out_specs=[pl.BlockSpec((tm, D_MODEL), lambda i: (i, 0)),
                   pl.BlockSpec((ts, D_MODEL), lambda i: (i, 0))],
        out_shape=[jax.ShapeDtypeStruct(xp.shape, F32), jax.ShapeDtypeStruct(xs.shape, F32)],
        scratch_shapes=[pltpu.VMEM((tm + ts, D_MODEL), BF16), pltpu.VMEM((tm, 1), F32),
                        pltpu.VMEM((2, D_MODEL, FFN_TF), F32), pltpu.VMEM((2, D_MODEL, FFN_TF), F32),
                        pltpu.VMEM((2, FFN_TF, D_MODEL), F32), pltpu.SemaphoreType.DMA((2,))],
        compiler_params=pltpu.CompilerParams(
            dimension_semantics=("arbitrary",), vmem_limit_bytes=VMEM_LIMIT_BYTES),
        name=name,
    )(xp, xs, mod_p3, mod_p3, mod_p3, mod_s, mod_s, mod_s, nrm, wg, wu, wd, nf)


K_P, K_CC, K_CH, K_CB, K_GA, K_GB, K_AB, K_O = range(8)
_N_CONV_CHUNKS = CONV_WIDTH // MIX_CH
_N_OUT_CHUNKS = D_MODEL // MIX_CH
_S_CONV = POOL_WIDTH // MIX_CH
_S_GATE = _S_CONV + 3 * _N_CONV_CHUNKS
_S_OUT = _S_GATE + 3 * _N_OUT_CHUNKS
_N_STAGES = _S_OUT + _N_OUT_CHUNKS
_KIND_COL0 = (COL_P, COL_CC, COL_CH, COL_CB, COL_GA, COL_GB, 0, 0)
assert _N_STAGES % MIX_NB == 0


def _decode_stage(t):
    u1 = t - _S_CONV
    u2 = t - _S_GATE
    c1 = lax.div(jnp.maximum(u1, 0), 3)
    c2 = lax.div(jnp.maximum(u2, 0), 3)
    kind = jnp.where(t < _S_CONV, K_P,
                     jnp.where(t < _S_GATE, K_CC + (u1 - 3 * c1),
                               jnp.where(t < _S_OUT, K_GA + (u2 - 3 * c2), K_O)))
    chunk = jnp.where(t < _S_CONV, t, jnp.where(t < _S_GATE, c1, jnp.where(t < _S_OUT, c2, t - _S_OUT)))
    col0 = jnp.int32(0)
    for k, base in enumerate(_KIND_COL0):
        if base:
            col0 = jnp.where(kind == k, base, col0)
    return kind, chunk, col0 + chunk * MIX_CH


def _mixer_kernel(x1p_hbm, x1s, x1pc, x1sc, shp, scp, g2pc, shs, scs, g2sc, nrm, grp, pscale, cw, cbias,
                  stp, stc, w_in, w_a, w_b, w_o,
                  x2p, x2s, pst, cst, nps, ncs,
                  wbuf, wsem, xbuf, xsem, h2, ap, v, m, zb, ybuf, sa, abf, e0, e1, qext, pcar, qcar,
                  *, tm, ts, tps, n_tiles):
    i = pl.program_id(0)
    oc = pl.program_id(1)
    slab = D_MODEL // MIX_DMA_SPLIT
    half = D_MODEL // 2
    prows = pl.ds(0, tm)
    srows = pl.ds(tm, ts)

    def wcopy(src, src_row, col, dst_slot, dst_row):
        return pltpu.make_async_copy(src.at[pl.ds(src_row, slab), pl.ds(col, MIX_CH)],
                                     wbuf.at[dst_slot, pl.ds(dst_row, slab), :], wsem.at[dst_slot])

    def wstart(t):
        k_t, _, col_t = _decode_stage(t)
        col_t = pl.multiple_of(col_t, MIX_CH)
        slot_t = lax.rem(t, MIX_NB)

        @pl.when(k_t < K_AB)
        def _():
            for k in range(MIX_DMA_SPLIT):
                wcopy(w_in, k * slab, col_t, slot_t, k * slab).start()

        @pl.when(k_t == K_AB)
        def _():
            for k in range(MIX_DMA_SPLIT):
                r0 = k * slab
                src, s0 = (w_a, r0) if r0 < half else (w_b, r0 - half)
                wcopy(src, s0, col_t, slot_t, r0).start()

        @pl.when(k_t == K_O)
        def _():
            for k in range(MIX_DMA_SPLIT):
                wcopy(w_o, k * slab, col_t, slot_t, k * slab).start()

    def wwait(slot):
        for k in range(MIX_DMA_SPLIT):
            wcopy(w_in, k * slab, 0, slot, k * slab).wait()

    n_xchunks = tm // X_ROWS

    def xcopy(tile, r):
        return pltpu.make_async_copy(x1p_hbm.at[pl.ds(tile * tm + r * X_ROWS, X_ROWS), :],
                                     xbuf.at[r % X_BUFS], xsem.at[r % X_BUFS])

    def xprefetch(tile):
        for r in range(min(X_BUFS, n_xchunks)):
            xcopy(tile, r).start()

    def prologue():
        for r in range(n_xchunks):
            xcopy(i, r).wait()
            h2[pl.ds(r * X_ROWS, X_ROWS), :] = _modulate(xbuf[r % X_BUFS], nrm[...], scp[...], shp[...]).astype(BF16)
            if r + X_BUFS < n_xchunks:
                xcopy(i, r + X_BUFS).start()
        h2[srows, :] = _modulate(x1s[...], nrm[...], scs[...], shs[...]).astype(BF16)
        for r in range(POOL_STATE - 1):
            nps[r] = stp[r + 1]
        for r in range(CONV_STATE - 1):
            ncs[r] = stc[r + 1]

    def pool_stage(c):
        zbuf = zb.at[0]
        for gl in range(MIX_CH // POOL_GROUP_DIM):
            g = c * (MIX_CH // POOL_GROUP_DIM) + gl
            w = POOL_WINDOWS[g]
            gcols = pl.ds(g * POOL_GROUP_DIM, POOL_GROUP_DIM)
            lcols = pl.ds(gl * POOL_GROUP_DIM, POOL_GROUP_DIM)
            pg = zbuf[prows, lcols]
            pad = jnp.zeros((SUBLANES, POOL_GROUP_DIM), F32)
            e0[pl.ds(SUBLANES, SUBLANES), :] = pad
            e1[pl.ds(SUBLANES, SUBLANES), :] = pad
            e0[pl.ds(CARRY_P, CARRY_P), :] = pcar[:, gcols]
            e0[pl.ds(2 * CARRY_P, tm), :] = pg
            src, dst = e0, e1
            d = 1
            while d < w:
                dst[pl.ds(CARRY_P, tm + CARRY_P), :] = (src[pl.ds(CARRY_P, tm + CARRY_P), :]
                                                        + src[pl.ds(CARRY_P - d, tm + CARRY_P), :])
                src, dst = dst, src
                d *= 2
            wsum = src[pl.ds(2 * CARRY_P, tm), :]
            pos = (i % tps) * tm + lax.broadcasted_iota(jnp.int32, (CARRY_P, 1), 0)
            cnt = jnp.minimum(pos + 1, w).astype(F32)
            abf[prows, :] = (wsum * (1.0 / w) - pg).astype(BF16)
            head = pl.ds(0, CARRY_P)
            abf[head, :] = (wsum[:CARRY_P, :] / cnt - pg[:CARRY_P, :]).astype(BF16)
            tail = pg[tm - CARRY_P:, :]
            pcar[:, gcols] = tail
            pst[:, gcols] = tail
            ps = zbuf[srows, lcols]
            ssum = ps
            for r in range(POOL_STATE + 1 - w, POOL_STATE):
                ssum = ssum + stp[r, :, gcols]
            abf[srows, :] = (ssum / float(min(PAST_LEN + 1, w)) - ps).astype(BF16)
            nps[POOL_STATE - 1, :, gcols] = ps
            ag = _bdot(abf[...], grp[g].astype(BF16)) * pscale[:, gcols]
            ap[g] = ag.astype(BF16)

    def conv_stage(c):
        ccols = pl.ds(c * MIX_CH, MIX_CH)
        zbuf, zprev = zb.at[0], zb.at[1]
        qp = zprev[prows, :] * zbuf[prows, :]
        qext[pl.ds(0, CARRY_Q), :] = qcar[:, ccols]
        qext[pl.ds(CARRY_Q, tm), :] = qp
        y = cbias[:, ccols] + qext[pl.ds(CARRY_Q - 2, tm), :] * cw[0:1, ccols]
        y = y + qext[pl.ds(CARRY_Q - 1, tm), :] * cw[1:2, ccols]
        tail = qp[tm - CARRY_Q:, :]
        qcar[:, ccols] = tail
        cst[:, ccols] = tail
        qs = zprev[srows, :] * zbuf[srows, :]
        ys = cbias[:, ccols] + stc[0, :, ccols] * cw[0:1, ccols]
        ys = ys + stc[1, :, ccols] * cw[1:2, ccols]
        ncs[CONV_STATE - 1, :, ccols] = qs
        ybuf[prows, :] = y + qp * cw[2:3, ccols]
        ybuf[srows, :] = ys + qs * cw[2:3, ccols]

    def main_stage(s):
        kind, chunk, _ = _decode_stage(s)
        slot = lax.rem(s, MIX_NB)
        wwait(slot)
        wstart(s + (MIX_NB - 1))

        @pl.when(kind < K_AB)
        def _():
            sa[...] = _sigmoid(zb[1])
            zb[0] = _bdot(h2[...], wbuf[slot].astype(BF16))

        @pl.when(jnp.logical_or(kind == K_CC, kind == K_GA))
        def _():
            zb[1] = zb[0]

        for c in range(POOL_WIDTH // MIX_CH):
            @pl.when(jnp.logical_and(kind == K_P, chunk == c))
            def _(c=c):
                pool_stage(c)

        for c in range(_N_CONV_CHUNKS):
            @pl.when(jnp.logical_and(kind == K_CH, chunk == c))
            def _(c=c):
                conv_stage(c)

        @pl.when(kind == K_CB)
        def _():
            v[chunk] = (zb[0] * ybuf[...]).astype(BF16)

        @pl.when(kind == K_AB)
        def _():
            sb = _sigmoid(zb[0])
            for hc in range(MIX_CH // MIX_HALF):
                hcols = pl.ds(hc * MIX_HALF, MIX_HALF)
                ua = None
                for g in range(len(POOL_WINDOWS)):
                    part = _bdot(ap[g], wbuf[slot, pl.ds(g * POOL_GROUP_DIM, POOL_GROUP_DIM), hcols].astype(BF16))
                    ua = part if ua is None else ua + part
                ub = None
                for k in range(_N_CONV_CHUNKS):
                    part = _bdot(v[k], wbuf[slot, pl.ds(half + k * MIX_CH, MIX_CH), hcols].astype(BF16))
                    ub = part if ub is None else ub + part
                sbh = sb[:, hc * MIX_HALF:(hc + 1) * MIX_HALF]
                m[chunk, :, hcols] = (sa[:, hcols] * ua + sbh * ub).astype(BF16)

    assert _S_OUT + MIX_NB - 2 < _N_STAGES

    @pl.when(oc == 0)
    def _():
        @pl.when(i == 0)
        def _():
            for t in range(MIX_NB - 1):
                wstart(jnp.int32(t))
            xprefetch(0)
            zb[...] = jnp.zeros(zb.shape, F32)

        @pl.when(i % tps == 0)
        def _():
            pcar[...] = jnp.zeros(pcar.shape, F32)
            qcar[...] = jnp.zeros(qcar.shape, F32)

        prologue()
        pl.loop(0, _S_OUT)(main_stage)

    s_out = _S_OUT + oc
    slot = lax.rem(s_out, MIX_NB)
    wwait(slot)
    nxt = s_out + (MIX_NB - 1)

    @pl.when(nxt < _N_STAGES)
    def _():
        wstart(nxt)

    @pl.when(jnp.logical_and(nxt >= _N_STAGES, i < n_tiles - 1))
    def _():
        wstart(nxt - _N_STAGES)

    @pl.when(jnp.logical_and(oc == 1, i < n_tiles - 1))
    def _():
        xprefetch(i + 1)

    for hc in range(MIX_CH // MIX_HALF):
        hcols = pl.ds(hc * MIX_HALF, MIX_HALF)
        acc = None
        for k in range(_N_OUT_CHUNKS):
            part = _bdot(m[k], wbuf[slot, pl.ds(k * MIX_CH, MIX_CH), hcols].astype(BF16))
            acc = part if acc is None else acc + part
        x2p[:, hcols] = x1pc[:, hcols] + g2pc[:, hcols] * acc[:tm]
        x2s[:, hcols] = x1sc[:, hcols] + g2sc[:, hcols] * acc[tm:]


def _mixer(x1p, x1s, mod_p3, mod_s, nrm, grp, pscale, cw, cbias, stp_t, stc_t, w_in, w_a, w_b, w_o, *, tm, ts, tps):
    n_tiles = x1p.shape[0] // tm
    n_seq = n_tiles // tps
    n_s = x1s.shape[0]
    assert n_s == n_tiles * ts
    rows = tm + ts
    any_spec = pl.BlockSpec(memory_space=pl.ANY)

    def const(shape):
        return pl.BlockSpec(shape, lambda i, s: (0,) * len(shape))

    def oc(s):
        return s

    g2_blk = (MOD_G2 * D_MODEL) // MIX_CH
    in_specs = [
        any_spec,
        pl.BlockSpec((ts, D_MODEL), lambda i, s: (i, 0)),
        pl.BlockSpec((tm, MIX_CH), lambda i, s: (i, oc(s))),
        pl.BlockSpec((ts, MIX_CH), lambda i, s: (i, oc(s))),
        pl.BlockSpec((None, 1, D_MODEL), lambda i, s: (i // tps, 0, MOD_SH2)),
        pl.BlockSpec((None, 1, D_MODEL), lambda i, s: (i // tps, 0, MOD_SC2)),
        pl.BlockSpec((None, 1, MIX_CH), lambda i, s: (i // tps, 0, g2_blk + oc(s))),
        pl.BlockSpec((ts, D_MODEL), lambda i, s: (i, MOD_SH2)),
        pl.BlockSpec((ts, D_MODEL), lambda i, s: (i, MOD_SC2)),
        pl.BlockSpec((ts, MIX_CH), lambda i, s: (i, g2_blk + oc(s))),
        const((1, D_MODEL)),
        const((len(POOL_WINDOWS), POOL_GROUP_DIM, POOL_GROUP_DIM)),
        const((1, POOL_WIDTH)),
        const((3, CONV_WIDTH)),
        const((1, CONV_WIDTH)),
        pl.BlockSpec((POOL_STATE, ts, POOL_WIDTH), lambda i, s: (0, i, 0)),
        pl.BlockSpec((CONV_STATE, ts, CONV_WIDTH), lambda i, s: (0, i, 0)),
        any_spec, any_spec, any_spec, any_spec,
    ]
    out_specs = [
        pl.BlockSpec((tm, MIX_CH), lambda i, s: (i, oc(s))),
        pl.BlockSpec((ts, MIX_CH), lambda i, s: (i, oc(s))),
        pl.BlockSpec((None, CARRY_P, POOL_WIDTH), lambda i, s: (i // tps, 0, 0)),
        pl.BlockSpec((None, CARRY_Q, CONV_WIDTH), lambda i, s: (i // tps, 0, 0)),
        pl.BlockSpec((POOL_STATE, ts, POOL_WIDTH), lambda i, s: (0, i, 0)),
        pl.BlockSpec((CONV_STATE, ts, CONV_WIDTH), lambda i, s: (0, i, 0)),
    ]
    out_shape = [
        jax.ShapeDtypeStruct(x1p.shape, F32),
        jax.ShapeDtypeStruct(x1s.shape, F32),
        jax.ShapeDtypeStruct((n_seq, CARRY_P, POOL_WIDTH), F32),
        jax.ShapeDtypeStruct((n_seq, CARRY_Q, CONV_WIDTH), F32),
        jax.ShapeDtypeStruct((POOL_STATE, n_s, POOL_WIDTH), F32),
        jax.ShapeDtypeStruct((CONV_STATE, n_s, CONV_WIDTH), F32),
    ]
    scratch = [
        pltpu.VMEM((MIX_NB, D_MODEL, MIX_CH), F32),
        pltpu.SemaphoreType.DMA((MIX_NB,)),
        pltpu.VMEM((X_BUFS, X_ROWS, D_MODEL), F32),
        pltpu.SemaphoreType.DMA((X_BUFS,)),
        pltpu.VMEM((rows, D_MODEL), BF16),
        pltpu.VMEM((len(POOL_WINDOWS), rows, POOL_GROUP_DIM), BF16),
        pltpu.VMEM((_N_CONV_CHUNKS, rows, MIX_CH), BF16),
        pltpu.VMEM((_N_OUT_CHUNKS, rows, MIX_CH), BF16),
        pltpu.VMEM((2, rows, MIX_CH), F32),
        pltpu.VMEM((rows, MIX_CH), F32),
        pltpu.VMEM((rows, MIX_CH), F32),
        pltpu.VMEM((rows, POOL_GROUP_DIM), BF16),
        pltpu.VMEM((tm + 2 * CARRY_P, POOL_GROUP_DIM), F32),
        pltpu.VMEM((tm + 2 * CARRY_P, POOL_GROUP_DIM), F32),
        pltpu.VMEM((tm + CARRY_Q, MIX_CH), F32),
        pltpu.VMEM((CARRY_P, POOL_WIDTH), F32),
        pltpu.VMEM((CARRY_Q, CONV_WIDTH), F32),
    ]
    return pl.pallas_call(
        functools.partial(_mixer_kernel, tm=tm, ts=ts, tps=tps, n_tiles=n_tiles),
        grid=(n_tiles, _N_OUT_CHUNKS),
        in_specs=in_specs,
        out_specs=out_specs,
        out_shape=out_shape,
        scratch_shapes=scratch,
        compiler_params=pltpu.CompilerParams(
            dimension_semantics=("arbitrary", "arbitrary"), vmem_limit_bytes=VMEM_LIMIT_BYTES),
        name="mixer",
    )(x1p, x1s, x1p, x1s, mod_p3, mod_p3, mod_p3, mod_s, mod_s, mod_s, nrm, grp, pscale, cw, cbias,
      stp_t, stc_t, w_in, w_a, w_b, w_o)


def kernel(x_prompt, x_sample, c_prompt, c_sample, state_pool, state_conv, w_ada, b_ada, norm1, ffn1_gate, ffn1_up, ffn1_down, norm2, w_in, pool_grp, pool_scale, w_branch_a, conv_w, conv_b, w_branch_b, w_o, norm3, ffn2_gate, ffn2_up, ffn2_down, norm_final):
    assert w_ada.shape[0] == 1, "single-layer step"
    batch, seq, d = x_prompt.shape
    dec_batch = x_sample.shape[0]
    assert d == D_MODEL and x_sample.shape[1] == 1

    tm = TM_PROMPT
    tps = seq // tm
    n_tiles = batch * tps
    ts = dec_batch // n_tiles
    assert seq % tm == 0 and dec_batch % n_tiles == 0 and ts % (2 * SUBLANES) == 0

    n_c = batch + dec_batch
    pad = (-n_c) % SUBLANES
    c_all = jnp.concatenate([c_sample, c_prompt, jnp.zeros((pad, d), F32)], axis=0)
    mod = _ada(c_all, w_ada[0], b_ada)
    mod_p3 = mod[dec_batch:n_c].reshape(batch, 1, N_MOD * d)

    row = lambda a: a.reshape(1, -1)
    xp = x_prompt.reshape(batch * seq, d)
    xs = x_sample.reshape(dec_batch, d)
    nf = row(norm_final)
    x1p, x1s = _ffn(xp, xs, mod_p3, mod, MOD_SH1, MOD_SC1, MOD_G1, row(norm1[0]), ffn1_gate[0], ffn1_up[0],
                    ffn1_down[0], nf, tm=tm, ts=ts, tps=tps, final_norm=False, name="ffn1")
    stp_t = jnp.transpose(state_pool[0], (1, 0, 2))
    stc_t = jnp.transpose(state_conv[0], (1, 0, 2))
    x2p, x2s, pst, cst, nps_t, ncs_t = _mixer(
        x1p, x1s, mod_p3, mod, row(norm2[0]), pool_grp[0], row(pool_scale[0]), conv_w[0], row(conv_b[0]),
        stp_t, stc_t, w_in[0], w_branch_a[0], w_branch_b[0], w_o[0], tm=tm, ts=ts, tps=tps)
    yp, ys = _ffn(x2p, x2s, mod_p3, mod, MOD_SH3, MOD_SC3, MOD_G3, row(norm3[0]), ffn2_gate[0], ffn2_up[0],
                  ffn2_down[0], nf, tm=tm, ts=ts, tps=tps, final_norm=True, name="ffn2")

    y_prompt = yp.reshape(batch, seq, d)
    y_sample = ys.reshape(dec_batch, 1, d)
    new_pool_prompt = pst[None, :, CARRY_P - POOL_STATE:, :]
    new_conv_prompt = cst[None, :, CARRY_Q - CONV_STATE:, :]
    new_pool_sample = jnp.transpose(nps_t, (1, 0, 2))[None]
    new_conv_sample = jnp.transpose(ncs_t, (1, 0, 2))[None]
    return (y_prompt, y_sample, new_pool_prompt, new_conv_prompt, new_pool_sample, new_conv_sample)
```

```python
import functools

import jax
import jax.numpy as jnp
from jax import lax
from jax.experimental import pallas as pl
from jax.experimental.pallas import tpu as pltpu

F32 = jnp.float32
BF16 = jnp.bfloat16

D_MODEL = 2048
D_FF = 5632
POOL_WIDTH = 1024
CONV_WIDTH = 1024
POOL_WINDOWS = (2, 4, 8, 16)
POOL_GROUP_DIM = 256
POOL_STATE = 15
CONV_STATE = 2
N_MOD = 9
EPS = 1e-6
PAST_LEN = 16384

V7X_VMEM_BYTES = 64 * 1024 * 1024
SUBLANES = 8
VMEM_LIMIT_BYTES = V7X_VMEM_BYTES - 2 * 1024 * 1024

TM_PROMPT = 1024

ADA_TN = 1024
FFN_TF = 256
FFN_DMA_SPLIT = 4
FFN_NB = 3
MIX_CH = 512
MIX_HALF = 256
MIX_NB = 3
X_ROWS = 128
FFN_ROWS = 128
X_BUFS = 4
MIX_DMA_SPLIT = 8
CARRY_P = 16
CARRY_Q = 8

COL_P, COL_CB, COL_CC, COL_CH, COL_GA, COL_GB = 0, 1024, 2048, 3072, 4096, 6144
MOD_SH1, MOD_SC1, MOD_G1, MOD_SH2, MOD_SC2, MOD_G2, MOD_SH3, MOD_SC3, MOD_G3 = range(9)


def _sigmoid(x):
    return 1.0 / (1.0 + jnp.exp(-x))


def _modulate(x, nrm, sc, sh):
    inv = lax.rsqrt(jnp.mean(x * x, axis=-1, keepdims=True) + EPS)
    return (x * inv) * nrm * (1.0 + sc) + sh


def _bdot(a, b):
    return jnp.dot(a, b, preferred_element_type=F32)


def _ada_kernel(c_ref, w_ref, b_ref, o_ref):
    c = c_ref[...]
    s = (c * _sigmoid(c)).astype(BF16)
    o_ref[...] = _bdot(s, w_ref[...].astype(BF16)) + b_ref[...]


def _ada(c_all, w_ada, b_ada):
    rows = c_all.shape[0]
    cols = w_ada.shape[1]
    return pl.pallas_call(
        _ada_kernel,
        grid=(cols // ADA_TN,),
        in_specs=[
            pl.BlockSpec((rows, D_MODEL), lambda j: (0, 0)),
            pl.BlockSpec((D_MODEL, ADA_TN), lambda j: (0, j)),
            pl.BlockSpec((1, ADA_TN), lambda j: (0, j)),
        ],
        out_specs=pl.BlockSpec((rows, ADA_TN), lambda j: (0, j)),
        out_shape=jax.ShapeDtypeStruct((rows, cols), F32),
        compiler_params=pltpu.CompilerParams(dimension_semantics=("arbitrary",)),
        name="ada_mod",
    )(c_all, w_ada, b_ada)


def _ffn_kernel(xp_ref, xs_ref, shp, scp, gtp, shs, scs, gts, nrm_ref, wg_hbm, wu_hbm, wd_hbm, nf_ref,
                op_ref, os_ref, h_ref, inv_ref, wgb, wub, wdb, wsem, *, n_f, n_tiles, final_norm):
    i = pl.program_id(0)
    tm = xp_ref.shape[0]
    ts = xs_ref.shape[0]

    def wcopies(f, slot):
        col = pl.multiple_of(f * FFN_TF, FFN_TF)
        rows_in = D_MODEL // FFN_DMA_SPLIT
        rows_dn = FFN_TF // FFN_DMA_SPLIT
        out = []
        for k in range(FFN_DMA_SPLIT):
            r_in = pl.ds(k * rows_in, rows_in)
            out.append(pltpu.make_async_copy(wg_hbm.at[r_in, pl.ds(col, FFN_TF)], wgb.at[slot, r_in, :],
                                             wsem.at[slot]))
            out.append(pltpu.make_async_copy(wu_hbm.at[r_in, pl.ds(col, FFN_TF)], wub.at[slot, r_in, :],
                                             wsem.at[slot]))
            out.append(pltpu.make_async_copy(wd_hbm.at[pl.ds(col + k * rows_dn, rows_dn), :],
                                             wdb.at[slot, pl.ds(k * rows_dn, rows_dn), :], wsem.at[slot]))
        return out

    def wstart(f, slot):
        for cp in wcopies(f, slot):
            cp.start()

    def wwait(f, slot):
        for cp in wcopies(f, slot):
            cp.wait()

    @pl.when(i == 0)
    def _():
        for t in range(FFN_NB - 1):
            wstart(jnp.int32(t), t)

    def head(x, sc, sh):
        return _modulate(x, nrm_ref[...], sc, sh).astype(BF16)

    def tail(x, gt, acc):
        xo = x + 0.5 * gt * acc
        if final_norm:
            inv = lax.rsqrt(jnp.mean(xo * xo, axis=-1, keepdims=True) + EPS)
            xo = (xo * inv) * nf_ref[...]
        return xo

    @pl.loop(0, tm // FFN_ROWS)
    def _(r):
        rows = pl.ds(pl.multiple_of(r * FFN_ROWS, FFN_ROWS), FFN_ROWS)
        x = xp_ref[rows, :]
        inv_ref[rows, :] = lax.rsqrt(jnp.mean(x * x, axis=-1, keepdims=True) + EPS)

    @pl.loop(0, tm // FFN_ROWS)
    def _(r):
        rows = pl.ds(pl.multiple_of(r * FFN_ROWS, FFN_ROWS), FFN_ROWS)
        hx = (xp_ref[rows, :] * inv_ref[rows, :]) * nrm_ref[...] * (1.0 + scp[...]) + shp[...]
        h_ref[rows, :] = hx.astype(BF16)
        op_ref[rows, :] = jnp.zeros((FFN_ROWS, D_MODEL), F32)

    h_ref[pl.ds(tm, ts), :] = head(xs_ref[...], scs[...], shs[...])
    os_ref[...] = jnp.zeros((ts, D_MODEL), F32)

    @pl.loop(0, n_f)
    def _(f):
        g_idx = i * n_f + f
        slot = lax.rem(g_idx, FFN_NB)
        wwait(f, slot)
        nxt = f + (FFN_NB - 1)
        nslot = lax.rem(g_idx + (FFN_NB - 1), FFN_NB)

        @pl.when(nxt < n_f)
        def _():
            wstart(nxt, nslot)

        @pl.when(jnp.logical_and(nxt >= n_f, i < n_tiles - 1))
        def _():
            wstart(nxt - n_f, nslot)

        h = h_ref[...]
        g = _bdot(h, wgb[slot].astype(BF16))
        u = _bdot(h, wub[slot].astype(BF16))
        a = ((g * _sigmoid(g)) * u).astype(BF16)
        for half in range(2):
            hcols = pl.ds(half * (D_MODEL // 2), D_MODEL // 2)
            y = _bdot(a, wdb[slot, :, hcols].astype(BF16))
            op_ref[:, hcols] += y[:tm]
            os_ref[:, hcols] += y[tm:]

    @pl.loop(0, tm // FFN_ROWS)
    def _(r):
        rows = pl.ds(pl.multiple_of(r * FFN_ROWS, FFN_ROWS), FFN_ROWS)
        op_ref[rows, :] = tail(xp_ref[rows, :], gtp[...], op_ref[rows, :])

    os_ref[...] = tail(xs_ref[...], gts[...], os_ref[...])


def _ffn(xp, xs, mod_p3, mod_s, j_sh, j_sc, j_g, nrm, wg, wu, wd, nf, *, tm, ts, tps, final_norm, name):
    n_tiles = xp.shape[0] // tm
    assert xs.shape[0] == n_tiles * ts
    n_f = D_FF // FFN_TF
    assert n_f >= FFN_NB
    any_spec = pl.BlockSpec(memory_space=pl.ANY)

    def pspec(j):
        return pl.BlockSpec((None, 1, D_MODEL), lambda i: (i // tps, 0, j))

    def sspec(j):
        return pl.BlockSpec((ts, D_MODEL), lambda i: (i, j))

    return pl.pallas_call(
        functools.partial(_ffn_kernel, n_f=n_f, n_tiles=n_tiles, final_norm=final_norm),
        grid=(n_tiles,),
        in_specs=[
            pl.BlockSpec((tm, D_MODEL), lambda i: (i, 0)),
            pl.BlockSpec((ts, D_MODEL), lambda i: (i, 0)),
            pspec(j_sh), pspec(j_sc), pspec(j_g),
            sspec(j_sh), sspec(j_sc), sspec(j_g),
            pl.BlockSpec((1, D_MODEL), lambda i: (0, 0)),
            any_spec, any_spec, any_spec,
            pl.BlockSpec((1, D_MODEL), lambda i: (0, 0)),
        ],
        out_specs=[pl.BlockSpec((tm, D_MODEL), lambda i: (i, 0)),
                   pl.BlockSpec((ts, D_MODEL), lambda i: (i, 0))],
        out_shape=[jax.ShapeDtypeStruct(xp.shape, F32), jax.ShapeDtypeStruct(xs.shape, F32)],
        scratch_shapes=[pltpu.VMEM((tm + ts, D_MODEL), BF16), pltpu.VMEM((tm, 1), F32),
                        pltpu.VMEM((FFN_NB, D_MODEL, FFN_TF), F32), pltpu.VMEM((FFN_NB, D_MODEL, FFN_TF), F32),
                        pltpu.VMEM((FFN_NB, FFN_TF, D_MODEL), F32), pltpu.SemaphoreType.DMA((FFN_NB,))],
        compiler_params=pltpu.CompilerParams(
            dimension_semantics=("arbitrary",), vmem_limit_bytes=VMEM_LIMIT_BYTES),
        name=name,
    )(xp, xs, mod_p3, mod_p3, mod_p3, mod_s, mod_s, mod_s, nrm, wg, wu, wd, nf)


K_P, K_CC, K_CH, K_CB, K_GA, K_GB, K_AB, K_O = range(8)
_N_CONV_CHUNKS = CONV_WIDTH // MIX_CH
_N_OUT_CHUNKS = D_MODEL // MIX_CH
_S_CONV = POOL_WIDTH // MIX_CH
_S_GATE = _S_CONV + 3 * _N_CONV_CHUNKS
_S_OUT = _S_GATE + 3 * _N_OUT_CHUNKS
_N_STAGES = _S_OUT + _N_OUT_CHUNKS
_KIND_COL0 = (COL_P, COL_CC, COL_CH, COL_CB, COL_GA, COL_GB, 0, 0)
assert _N_STAGES % MIX_NB == 0


def _decode_stage(t):
    u1 = t - _S_CONV
    u2 = t - _S_GATE
    c1 = lax.div(jnp.maximum(u1, 0), 3)
    c2 = lax.div(jnp.maximum(u2, 0), 3)
    kind = jnp.where(t < _S_CONV, K_P,
                     jnp.where(t < _S_GATE, K_CC + (u1 - 3 * c1),
                               jnp.where(t < _S_OUT, K_GA + (u2 - 3 * c2), K_O)))
    chunk = jnp.where(t < _S_CONV, t, jnp.where(t < _S_GATE, c1, jnp.where(t < _S_OUT, c2, t - _S_OUT)))
    col0 = jnp.int32(0)
    for k, base in enumerate(_KIND_COL0):
        if base:
            col0 = jnp.where(kind == k, base, col0)
    return kind, chunk, col0 + chunk * MIX_CH


def _mixer_kernel(x1p_hbm, x1s, x1pc, x1sc, shp, scp, g2pc, shs, scs, g2sc, nrm, grp, pscale, cw, cbias,
                  stp, stc, w_in, w_a, w_b, w_o,
                  x2p, x2s, pst, cst, nps, ncs,
                  wbuf, wsem, xbuf, xsem, h2, ap, v, m, zb, ybuf, sa, abf, e0, e1, qext, pcar, qcar,
                  *, tm, ts, tps, n_tiles):
    i = pl.program_id(0)
    oc = pl.program_id(1)
    slab = D_MODEL // MIX_DMA_SPLIT
    half = D_MODEL // 2
    prows = pl.ds(0, tm)
    srows = pl.ds(tm, ts)

    def wcopy(src, src_row, col, dst_slot, dst_row):
        return pltpu.make_async_copy(src.at[pl.ds(src_row, slab), pl.ds(col, MIX_CH)],
                                     wbuf.at[dst_slot, pl.ds(dst_row, slab), :], wsem.at[dst_slot])

    def wstart(t):
        k_t, _, col_t = _decode_stage(t)
        col_t = pl.multiple_of(col_t, MIX_CH)
        slot_t = lax.rem(t, MIX_NB)

        @pl.when(k_t < K_AB)
        def _():
            for k in range(MIX_DMA_SPLIT):
                wcopy(w_in, k * slab, col_t, slot_t, k * slab).start()

        @pl.when(k_t == K_AB)
        def _():
            for k in range(MIX_DMA_SPLIT):
                r0 = k * slab
                src, s0 = (w_a, r0) if r0 < half else (w_b, r0 - half)
                wcopy(src, s0, col_t, slot_t, r0).start()

        @pl.when(k_t == K_O)
        def _():
            for k in range(MIX_DMA_SPLIT):
                wcopy(w_o, k * slab, col_t, slot_t, k * slab).start()

    def wwait(slot):
        for k in range(MIX_DMA_SPLIT):
            wcopy(w_in, k * slab, 0, slot, k * slab).wait()

    n_xchunks = tm // X_ROWS

    def xcopy(tile, r):
        return pltpu.make_async_copy(x1p_hbm.at[pl.ds(tile * tm + r * X_ROWS, X_ROWS), :],
                                     xbuf.at[r % X_BUFS], xsem.at[r % X_BUFS])

    def xprefetch(tile):
        for r in range(min(X_BUFS, n_xchunks)):
            xcopy(tile, r).start()

    def prologue():
        for r in range(n_xchunks):
            xcopy(i, r).wait()
            h2[pl.ds(r * X_ROWS, X_ROWS), :] = _modulate(xbuf[r % X_BUFS], nrm[...], scp[...], shp[...]).astype(BF16)
            if r + X_BUFS < n_xchunks:
                xcopy(i, r + X_BUFS).start()
        h2[srows, :] = _modulate(x1s[...], nrm[...], scs[...], shs[...]).astype(BF16)
        for r in range(POOL_STATE - 1):
            nps[r] = stp[r + 1]
        for r in range(CONV_STATE - 1):
            ncs[r] = stc[r + 1]

    def pool_stage(c):
        zbuf = zb.at[0]
        for gl in range(MIX_CH // POOL_GROUP_DIM):
            g = c * (MIX_CH // POOL_GROUP_DIM) + gl
            w = POOL_WINDOWS[g]
            gcols = pl.ds(g * POOL_GROUP_DIM, POOL_GROUP_DIM)
            lcols = pl.ds(gl * POOL_GROUP_DIM, POOL_GROUP_DIM)
            pg = zbuf[prows, lcols]
            pad = jnp.zeros((SUBLANES, POOL_GROUP_DIM), F32)
            e0[pl.ds(SUBLANES, SUBLANES), :] = pad
            e1[pl.ds(SUBLANES, SUBLANES), :] = pad
            e0[pl.ds(CARRY_P, CARRY_P), :] = pcar[:, gcols]
            e0[pl.ds(2 * CARRY_P, tm), :] = pg
            src, dst = e0, e1
            d = 1
            while d < w:
                dst[pl.ds(CARRY_P, tm + CARRY_P), :] = (src[pl.ds(CARRY_P, tm + CARRY_P), :]
                                                        + src[pl.ds(CARRY_P - d, tm + CARRY_P), :])
                src, dst = dst, src
                d *= 2
            wsum = src[pl.ds(2 * CARRY_P, tm), :]
            pos = (i % tps) * tm + lax.broadcasted_iota(jnp.int32, (CARRY_P, 1), 0)
            cnt = jnp.minimum(pos + 1, w).astype(F32)
            abf[prows, :] = (wsum * (1.0 / w) - pg).astype(BF16)
            head = pl.ds(0, CARRY_P)
            abf[head, :] = (wsum[:CARRY_P, :] / cnt - pg[:CARRY_P, :]).astype(BF16)
            tail = pg[tm - CARRY_P:, :]
            pcar[:, gcols] = tail
            pst[:, gcols] = tail
            ps = zbuf[srows, lcols]
            ssum = ps
            for r in range(POOL_STATE + 1 - w, POOL_STATE):
                ssum = ssum + stp[r, :, gcols]
            abf[srows, :] = (ssum / float(min(PAST_LEN + 1, w)) - ps).astype(BF16)
            nps[POOL_STATE - 1, :, gcols] = ps
            ag = _bdot(abf[...], grp[g].astype(BF16)) * pscale[:, gcols]
            ap[g] = ag.astype(BF16)

    def conv_stage(c):
        ccols = pl.ds(c * MIX_CH, MIX_CH)
        zbuf, zprev = zb.at[0], zb.at[1]
        qp = zprev[prows, :] * zbuf[prows, :]
        qext[pl.ds(0, CARRY_Q), :] = qcar[:, ccols]
        qext[pl.ds(CARRY_Q, tm), :] = qp
        y = cbias[:, ccols] + qext[pl.ds(CARRY_Q - 2, tm), :] * cw[0:1, ccols]
        y = y + qext[pl.ds(CARRY_Q - 1, tm), :] * cw[1:2, ccols]
        tail = qp[tm - CARRY_Q:, :]
        qcar[:, ccols] = tail
        cst[:, ccols] = tail
        qs = zprev[srows, :] * zbuf[srows, :]
        ys = cbias[:, ccols] + stc[0, :, ccols] * cw[0:1, ccols]
        ys = ys + stc[1, :, ccols] * cw[1:2, ccols]
        ncs[CONV_STATE - 1, :, ccols] = qs
        ybuf[prows, :] = y + qp * cw[2:3, ccols]
        ybuf[srows, :] = ys + qs * cw[2:3, ccols]

    def main_stage(s):
        kind, chunk, _ = _decode_stage(s)
        slot = lax.rem(s, MIX_NB)
        wwait(slot)
        wstart(s + (MIX_NB - 1))

        @pl.when(kind < K_AB)
        def _():
            sa[...] = _sigmoid(zb[1])
            zb[0] = _bdot(h2[...], wbuf[slot].astype(BF16))

        @pl.when(jnp.logical_or(kind == K_CC, kind == K_GA))
        def _():
            zb[1] = zb[0]

        for c in range(POOL_WIDTH // MIX_CH):
            @pl.when(jnp.logical_and(kind == K_P, chunk == c))
            def _(c=c):
                pool_stage(c)

        for c in range(_N_CONV_CHUNKS):
            @pl.when(jnp.logical_and(kind == K_CH, chunk == c))
            def _(c=c):
                conv_stage(c)

        @pl.when(kind == K_CB)
        def _():
            v[chunk] = (zb[0] * ybuf[...]).astype(BF16)

        @pl.when(kind == K_AB)
        def _():
            sb = _sigmoid(zb[0])
            for hc in range(MIX_CH // MIX_HALF):
                hcols = pl.ds(hc * MIX_HALF, MIX_HALF)
                ua = None
                for g in range(len(POOL_WINDOWS)):
                    part = _bdot(ap[g], wbuf[slot, pl.ds(g * POOL_GROUP_DIM, POOL_GROUP_DIM), hcols].astype(BF16))
                    ua = part if ua is None else ua + part
                ub = None
                for k in range(_N_CONV_CHUNKS):
                    part = _bdot(v[k], wbuf[slot, pl.ds(half + k * MIX_CH, MIX_CH), hcols].astype(BF16))
                    ub = part if ub is None else ub + part
                sbh = sb[:, hc * MIX_HALF:(hc + 1) * MIX_HALF]
                m[chunk, :, hcols] = (sa[:, hcols] * ua + sbh * ub).astype(BF16)

    assert _S_OUT + MIX_NB - 2 < _N_STAGES

    @pl.when(oc == 0)
    def _():
        @pl.when(i == 0)
        def _():
            for t in range(MIX_NB - 1):
                wstart(jnp.int32(t))
            xprefetch(0)
            zb[...] = jnp.zeros(zb.shape, F32)

        @pl.when(i % tps == 0)
        def _():
            pcar[...] = jnp.zeros(pcar.shape, F32)
            qcar[...] = jnp.zeros(qcar.shape, F32)

        prologue()
        pl.loop(0, _S_OUT)(main_stage)

    s_out = _S_OUT + oc
    slot = lax.rem(s_out, MIX_NB)
    wwait(slot)
    nxt = s_out + (MIX_NB - 1)

    @pl.when(nxt < _N_STAGES)
    def _():
        wstart(nxt)

    @pl.when(jnp.logical_and(nxt >= _N_STAGES, i < n_tiles - 1))
    def _():
        wstart(nxt - _N_STAGES)

    @pl.when(jnp.logical_and(oc == 1, i < n_tiles - 1))
    def _():
        xprefetch(i + 1)

    for hc in range(MIX_CH // MIX_HALF):
        hcols = pl.ds(hc * MIX_HALF, MIX_HALF)
        acc = None
        for k in range(_N_OUT_CHUNKS):
            part = _bdot(m[k], wbuf[slot, pl.ds(k * MIX_CH, MIX_CH), hcols].astype(BF16))
            acc = part if acc is None else acc + part
        x2p[:, hcols] = x1pc[:, hcols] + g2pc[:, hcols] * acc[:tm]
        x2s[:, hcols] = x1sc[:, hcols] + g2sc[:, hcols] * acc[tm:]


def _mixer(x1p, x1s, mod_p3, mod_s, nrm, grp, pscale, cw, cbias, stp_t, stc_t, w_in, w_a, w_b, w_o, *, tm, ts, tps):
    n_tiles = x1p.shape[0] // tm
    n_seq = n_tiles // tps
    n_s = x1s.shape[0]
    assert n_s == n_tiles * ts
    rows = tm + ts
    any_spec = pl.BlockSpec(memory_space=pl.ANY)

    def const(shape):
        return pl.BlockSpec(shape, lambda i, s: (0,) * len(shape))

    def oc(s):
        return s

    g2_blk = (MOD_G2 * D_MODEL) // MIX_CH
    in_specs = [
        any_spec,
        pl.BlockSpec((ts, D_MODEL), lambda i, s: (i, 0)),
        pl.BlockSpec((tm, MIX_CH), lambda i, s: (i, oc(s))),
        pl.BlockSpec((ts, MIX_CH), lambda i, s: (i, oc(s))),
        pl.BlockSpec((None, 1, D_MODEL), lambda i, s: (i // tps, 0, MOD_SH2)),
        pl.BlockSpec((None, 1, D_MODEL), lambda i, s: (i // tps, 0, MOD_SC2)),
        pl.BlockSpec((None, 1, MIX_CH), lambda i, s: (i // tps, 0, g2_blk + oc(s))),
        pl.BlockSpec((ts, D_MODEL), lambda i, s: (i, MOD_SH2)),
        pl.BlockSpec((ts, D_MODEL), lambda i, s: (i, MOD_SC2)),
        pl.BlockSpec((ts, MIX_CH), lambda i, s: (i, g2_blk + oc(s))),
        const((1, D_MODEL)),
        const((len(POOL_WINDOWS), POOL_GROUP_DIM, POOL_GROUP_DIM)),
        const((1, POOL_WIDTH)),
        const((3, CONV_WIDTH)),
        const((1, CONV_WIDTH)),
        pl.BlockSpec((POOL_STATE, ts, POOL_WIDTH), lambda i, s: (0, i, 0)),
        pl.BlockSpec((CONV_STATE, ts, CONV_WIDTH), lambda i, s: (0, i, 0)),
        any_spec, any_spec, any_spec, any_spec,
    ]
    out_specs = [
        pl.BlockSpec((tm, MIX_CH), lambda i, s: (i, oc(s))),
        pl.BlockSpec((ts, MIX_CH), lambda i, s: (i, oc(s))),
        pl.BlockSpec((None, CARRY_P, POOL_WIDTH), lambda i, s: (i // tps, 0, 0)),
        pl.BlockSpec((None, CARRY_Q, CONV_WIDTH), lambda i, s: (i // tps, 0, 0)),
        pl.BlockSpec((POOL_STATE, ts, POOL_WIDTH), lambda i, s: (0, i, 0)),
        pl.BlockSpec((CONV_STATE, ts, CONV_WIDTH), lambda i, s: (0, i, 0)),
    ]
    out_shape = [
        jax.ShapeDtypeStruct(x1p.shape, F32),
        jax.ShapeDtypeStruct(x1s.shape, F32),
        jax.ShapeDtypeStruct((n_seq, CARRY_P, POOL_WIDTH), F32),
        jax.ShapeDtypeStruct((n_seq, CARRY_Q, CONV_WIDTH), F32),
        jax.ShapeDtypeStruct((POOL_STATE, n_s, POOL_WIDTH), F32),
        jax.ShapeDtypeStruct((CONV_STATE, n_s, CONV_WIDTH), F32),
    ]
    scratch = [
        pltpu.VMEM((MIX_NB, D_MODEL, MIX_CH), F32),
        pltpu.SemaphoreType.DMA((MIX_NB,)),
        pltpu.VMEM((X_BUFS, X_ROWS, D_MODEL), F32),
        pltpu.SemaphoreType.DMA((X_BUFS,)),
        pltpu.VMEM((rows, D_MODEL), BF16),
        pltpu.VMEM((len(POOL_WINDOWS), rows, POOL_GROUP_DIM), BF16),
        pltpu.VMEM((_N_CONV_CHUNKS, rows, MIX_CH), BF16),
        pltpu.VMEM((_N_OUT_CHUNKS, rows, MIX_CH), BF16),
        pltpu.VMEM((2, rows, MIX_CH), F32),
        pltpu.VMEM((rows, MIX_CH), F32),
        pltpu.VMEM((rows, MIX_CH), F32),
        pltpu.VMEM((rows, POOL_GROUP_DIM), BF16),
        pltpu.VMEM((tm + 2 * CARRY_P, POOL_GROUP_DIM), F32),
        pltpu.VMEM((tm + 2 * CARRY_P, POOL_GROUP_DIM), F32),
        pltpu.VMEM((tm + CARRY_Q, MIX_CH), F32),
        pltpu.VMEM((CARRY_P, POOL_WIDTH), F32),
        pltpu.VMEM((CARRY_Q, CONV_WIDTH), F32),
    ]
    return pl.pallas_call(
        functools.partial(_mixer_kernel, tm=tm, ts=ts, tps=tps, n_tiles=n_tiles),
        grid=(n_tiles, _N_OUT_CHUNKS),
        in_specs=in_specs,
        out_specs=out_specs,
        out_shape=out_shape,
        scratch_shapes=scratch,
        compiler_params=pltpu.CompilerParams(
            dimension_semantics=("arbitrary", "arbitrary"), vmem_limit_bytes=VMEM_LIMIT_BYTES),
        name="mixer",
    )(x1p, x1s, x1p, x1s, mod_p3, mod_p3, mod_p3, mod_s, mod_s, mod_s, nrm, grp, pscale, cw, cbias,
      stp_t, stc_t, w_in, w_a, w_b, w_o)


def kernel(x_prompt, x_sample, c_prompt, c_sample, state_pool, state_conv, w_ada, b_ada, norm1, ffn1_gate, ffn1_up, ffn1_down, norm2, w_in, pool_grp, pool_scale, w_branch_a, conv_w, conv_b, w_branch_b, w_o, norm3, ffn2_gate, ffn2_up, ffn2_down, norm_final):
    assert w_ada.shape[0] == 1, "single-layer step"
    batch, seq, d = x_prompt.shape
    dec_batch = x_sample.shape[0]
    assert d == D_MODEL and x_sample.shape[1] == 1

    tm = TM_PROMPT
    tps = seq // tm
    n_tiles = batch * tps
    ts = dec_batch // n_tiles
    assert seq % tm == 0 and dec_batch % n_tiles == 0 and ts % (2 * SUBLANES) == 0

    n_c = batch + dec_batch
    pad = (-n_c) % SUBLANES
    c_all = jnp.concatenate([c_sample, c_prompt, jnp.zeros((pad, d), F32)], axis=0)
    mod = _ada(c_all, w_ada[0], b_ada)
    mod_p3 = mod[dec_batch:n_c].reshape(batch, 1, N_MOD * d)

    row = lambda a: a.reshape(1, -1)
    xp = x_prompt.reshape(batch * seq, d)
    xs = x_sample.reshape(dec_batch, d)
    nf = row(norm_final)
    x1p, x1s = _ffn(xp, xs, mod_p3, mod, MOD_SH1, MOD_SC1, MOD_G1, row(norm1[0]), ffn1_gate[0], ffn1_up[0],
                    ffn1_down[0], nf, tm=tm, ts=ts, tps=tps, final_norm=False, name="ffn1")
    stp_t = jnp.transpose(state_pool[0], (1, 0, 2))
    stc_t = jnp.transpose(state_conv[0], (1, 0, 2))
    x2p, x2s, pst, cst, nps_t, ncs_t = _mixer(
        x1p, x1s, mod_p3, mod, row(norm2[0]), pool_grp[0], row(pool_scale[0]), conv_w[0], row(conv_b[0]),
        stp_t, stc_t, w_in[0], w_branch_a[0], w_branch_b[0], w_o[0], tm=tm, ts=ts, tps=tps)
    yp, ys = _ffn(x2p, x2s, mod_p3, mod, MOD_SH3, MOD_SC3, MOD_G3, row(norm3[0]), ffn2_gate[0], ffn2_up[0],
                  ffn2_down[0], nf, tm=tm, ts=ts, tps=tps, final_norm=True, name="ffn2")

    y_prompt = yp.reshape(batch, seq, d)
    y_sample = ys.reshape(dec_batch, 1, d)
    new_pool_prompt = pst[None, :, CARRY_P - POOL_STATE:, :]
    new_conv_prompt = cst[None, :, CARRY_Q - CONV_STATE:, :]
    new_pool_sample = jnp.transpose(nps_t, (1, 0, 2))[None]
    new_conv_sample = jnp.transpose(ncs_t, (1, 0, 2))[None]
    return (y_prompt, y_sample, new_pool_prompt, new_conv_prompt, new_pool_sample, new_conv_sample)
```

```python
import functools

import jax
import jax.numpy as jnp
from jax import lax
from jax.experimental import pallas as pl
from jax.experimental.pallas import tpu as pltpu

F32 = jnp.float32
BF16 = jnp.bfloat16

D_MODEL = 2048
D_FF = 5632
POOL_WIDTH = 1024
CONV_WIDTH = 1024
POOL_WINDOWS = (2, 4, 8, 16)
POOL_GROUP_DIM = 256
POOL_STATE = 15
CONV_STATE = 2
N_MOD = 9
EPS = 1e-6
PAST_LEN = 16384

V7X_VMEM_BYTES = 64 * 1024 * 1024
SUBLANES = 8
VMEM_LIMIT_BYTES = V7X_VMEM_BYTES - 2 * 1024 * 1024

TM_PROMPT = 1024

ADA_TN = 1024
FFN_TF = 256
FFN_DMA_SPLIT = 4
FFN_NB = 3
FFN_SIDE_ROWS = 64
FFN_XSLOTS = 3
FFN_RES_SPLIT = 8
MIX_CH = 512
MIX_HALF = 256
MIX_NB = 3
X_ROWS = 128
FFN_ROWS = 128
X_BUFS = 4
MIX_DMA_SPLIT = 8
CARRY_P = 16
CARRY_Q = 8

COL_P, COL_CB, COL_CC, COL_CH, COL_GA, COL_GB = 0, 1024, 2048, 3072, 4096, 6144
MOD_SH1, MOD_SC1, MOD_G1, MOD_SH2, MOD_SC2, MOD_G2, MOD_SH3, MOD_SC3, MOD_G3 = range(9)


def _sigmoid(x):
    return 1.0 / (1.0 + jnp.exp(-x))


def _modulate(x, nrm, sc, sh):
    inv = lax.rsqrt(jnp.mean(x * x, axis=-1, keepdims=True) + EPS)
    return (x * inv) * nrm * (1.0 + sc) + sh


def _bdot(a, b):
    return jnp.dot(a, b, preferred_element_type=F32)


def _ada_kernel(c_ref, w_ref, b_ref, o_ref):
    c = c_ref[...]
    s = (c * _sigmoid(c)).astype(BF16)
    o_ref[...] = _bdot(s, w_ref[...].astype(BF16)) + b_ref[...]


def _ada(c_all, w_ada, b_ada):
    rows = c_all.shape[0]
    cols = w_ada.shape[1]
    return pl.pallas_call(
        _ada_kernel,
        grid=(cols // ADA_TN,),
        in_specs=[
            pl.BlockSpec((rows, D_MODEL), lambda j: (0, 0)),
            pl.BlockSpec((D_MODEL, ADA_TN), lambda j: (0, j)),
            pl.BlockSpec((1, ADA_TN), lambda j: (0, j)),
        ],
        out_specs=pl.BlockSpec((rows, ADA_TN), lambda j: (0, j)),
        out_shape=jax.ShapeDtypeStruct((rows, cols), F32),
        compiler_params=pltpu.CompilerParams(dimension_semantics=("arbitrary",)),
        name="ada_mod",
    )(c_all, w_ada, b_ada)


def _ffn_kernel(xp_hbm, xs_ref, xsn_ref, shp, scp, gtp, shpn, scpn, shs, scs, gts, shsn, scsn, nrm_ref,
                wg_hbm, wu_hbm, wd_hbm, nf_ref, op_ref, os_ref,
                h_ref, xres, xch, wgb, wub, wdb, wsem, xsem, rsem, *, n_f, n_tiles, final_norm):
    i = pl.program_id(0)
    tm = op_ref.shape[0]
    ts = xs_ref.shape[0]
    n_side = tm // FFN_SIDE_ROWS
    nxt_tile = jnp.minimum(i + 1, n_tiles - 1)

    def wcopies(f, slot):
        col = pl.multiple_of(f * FFN_TF, FFN_TF)
        rows_in = D_MODEL // FFN_DMA_SPLIT
        rows_dn = FFN_TF // FFN_DMA_SPLIT
        out = []
        for k in range(FFN_DMA_SPLIT):
            r_in = pl.ds(k * rows_in, rows_in)
            out.append(pltpu.make_async_copy(wg_hbm.at[r_in, pl.ds(col, FFN_TF)], wgb.at[slot, r_in, :],
                                             wsem.at[slot]))
            out.append(pltpu.make_async_copy(wu_hbm.at[r_in, pl.ds(col, FFN_TF)], wub.at[slot, r_in, :],
                                             wsem.at[slot]))
            out.append(pltpu.make_async_copy(wd_hbm.at[pl.ds(col + k * rows_dn, rows_dn), :],
                                             wdb.at[slot, pl.ds(k * rows_dn, rows_dn), :], wsem.at[slot]))
        return out

    def wstart(f, slot):
        for cp in wcopies(f, slot):
            cp.start()

    def wwait(f, slot):
        for cp in wcopies(f, slot):
            cp.wait()

    def xcopy(tile, c):
        slot = lax.rem(c, FFN_XSLOTS)
        return pltpu.make_async_copy(xp_hbm.at[pl.ds(tile * tm + c * FFN_SIDE_ROWS, FFN_SIDE_ROWS), :],
                                     xch.at[slot], xsem.at[slot])

    def rcopies():
        piece = tm // FFN_RES_SPLIT
        return [pltpu.make_async_copy(xp_hbm.at[pl.ds(i * tm + k * piece, piece), :],
                                      xres.at[pl.ds(k * piece, piece), :], rsem.at[0])
                for k in range(FFN_RES_SPLIT)]

    def head(x, sc, sh):
        return _modulate(x, nrm_ref[...], sc, sh).astype(BF16)

    def tail(x, gt, acc):
        xo = x + 0.5 * gt * acc
        if final_norm:
            inv = lax.rsqrt(jnp.mean(xo * xo, axis=-1, keepdims=True) + EPS)
            xo = (xo * inv) * nf_ref[...]
        return xo

    @pl.when(i == 0)
    def _():
        for t in range(FFN_NB - 1):
            wstart(jnp.int32(t), t)
        for c in range(FFN_XSLOTS - 1):
            xcopy(0, jnp.int32(c)).start()

        @pl.loop(0, n_side)
        def _(c):
            xcopy(0, c).wait()

            @pl.when(c + (FFN_XSLOTS - 1) < n_side)
            def _():
                xcopy(0, c + (FFN_XSLOTS - 1)).start()

            rows = pl.ds(pl.multiple_of(c * FFN_SIDE_ROWS, FFN_SIDE_ROWS), FFN_SIDE_ROWS)
            h_ref[0, rows, :] = head(xch[lax.rem(c, FFN_XSLOTS)], scp[...], shp[...])

        h_ref[0, pl.ds(tm, ts), :] = head(xs_ref[...], scs[...], shs[...])

    @pl.loop(0, tm // FFN_ROWS)
    def _(r):
        rows = pl.ds(pl.multiple_of(r * FFN_ROWS, FFN_ROWS), FFN_ROWS)
        op_ref[rows, :] = jnp.zeros((FFN_ROWS, D_MODEL), F32)

    os_ref[...] = jnp.zeros((ts, D_MODEL), F32)
    for c in range(FFN_XSLOTS - 1):
        xcopy(nxt_tile, jnp.int32(c)).start()

    def chunk_loop(par):
        h_cur, h_nxt = h_ref.at[par], h_ref.at[1 - par]

        @pl.loop(0, n_f)
        def _(f):
            g_idx = i * n_f + f
            slot = lax.rem(g_idx, FFN_NB)
            wwait(f, slot)
            nxt = f + (FFN_NB - 1)
            nslot = lax.rem(g_idx + (FFN_NB - 1), FFN_NB)

            @pl.when(nxt < n_f)
            def _():
                wstart(nxt, nslot)

            @pl.when(jnp.logical_and(nxt >= n_f, i < n_tiles - 1))
            def _():
                wstart(nxt - n_f, nslot)

            @pl.when(f < n_side)
            def _():
                xcopy(nxt_tile, f).wait()

            @pl.when(f + (FFN_XSLOTS - 1) < n_side)
            def _():
                xcopy(nxt_tile, f + (FFN_XSLOTS - 1)).start()

            @pl.when(f == 1)
            def _():
                for cp in rcopies():
                    cp.start()

            h = h_cur[...]
            g = _bdot(h, wgb[slot].astype(BF16))
            u = _bdot(h, wub[slot].astype(BF16))
            a = ((g * _sigmoid(g)) * u).astype(BF16)
            c = jnp.minimum(f, n_side - 1)
            rows = pl.ds(pl.multiple_of(c * FFN_SIDE_ROWS, FFN_SIDE_ROWS), FFN_SIDE_ROWS)
            h_nxt[rows, :] = head(xch[lax.rem(c, FFN_XSLOTS)], scpn[...], shpn[...])
            h_nxt[pl.ds(tm, ts), :] = head(xsn_ref[...], scsn[...], shsn[...])
            for half in range(2):
                hcols = pl.ds(half * (D_MODEL // 2), D_MODEL // 2)
                y = _bdot(a, wdb[slot, :, hcols].astype(BF16))
                op_ref[:, hcols] += y[:tm]
                os_ref[:, hcols] += y[tm:]

    for par in range(2):
        @pl.when(lax.rem(i, 2) == par)
        def _(par=par):
            chunk_loop(par)

    for cp in rcopies():
        cp.wait()

    @pl.loop(0, tm // FFN_ROWS)
    def _(r):
        rows = pl.ds(pl.multiple_of(r * FFN_ROWS, FFN_ROWS), FFN_ROWS)
        op_ref[rows, :] = tail(xres[rows, :], gtp[...], op_ref[rows, :])

    os_ref[...] = tail(xs_ref[...], gts[...], os_ref[...])


def _ffn(xp, xs, mod_p3, mod_s, j_sh, j_sc, j_g, nrm, wg, wu, wd, nf, *, tm, ts, tps, final_norm, name):
    n_tiles = xp.shape[0] // tm
    assert xs.shape[0] == n_tiles * ts
    n_f = D_FF // FFN_TF
    assert n_f >= FFN_NB and n_f >= tm // FFN_SIDE_ROWS and n_f >= 2
    any_spec = pl.BlockSpec(memory_space=pl.ANY)

    def nxt(i):
        return jnp.minimum(i + 1, n_tiles - 1)

    def pspec(j, tile):
        return pl.BlockSpec((None, 1, D_MODEL), lambda i: (tile(i) // tps, 0, j))

    def sspec(j, tile):
        return pl.BlockSpec((ts, D_MODEL), lambda i: (tile(i), j))

    cur = lambda i: i
    return pl.pallas_call(
        functools.partial(_ffn_kernel, n_f=n_f, n_tiles=n_tiles, final_norm=final_norm),
        grid=(n_tiles,),
        in_specs=[
            any_spec,
            pl.BlockSpec((ts, D_MODEL), lambda i: (i, 0)),
            pl.BlockSpec((ts, D_MODEL), lambda i: (nxt(i), 0)),
            pspec(j_sh, cur), pspec(j_sc, cur), pspec(j_g, cur), pspec(j_sh, nxt), pspec(j_sc, nxt),
            sspec(j_sh, cur), sspec(j_sc, cur), sspec(j_g, cur), sspec(j_sh, nxt), sspec(j_sc, nxt),
            pl.BlockSpec((1, D_MODEL), lambda i: (0, 0)),
            any_spec, any_spec, any_spec,
            pl.BlockSpec((1, D_MODEL), lambda i: (0, 0)),
        ],
        out_specs=[pl.BlockSpec((tm, D_MODEL), lambda i: (i, 0)),
                   pl.BlockSpec((ts, D_MODEL), lambda i: (i, 0))],
        out_shape=[jax.ShapeDtypeStruct(xp.shape, F32), jax.ShapeDtypeStruct(xs.shape, F32)],
        scratch_shapes=[pltpu.VMEM((2, tm + ts, D_MODEL), BF16),
                        pltpu.VMEM((tm, D_MODEL), F32),
                        pltpu.VMEM((FFN_XSLOTS, FFN_SIDE_ROWS, D_MODEL), F32),
                        pltpu.VMEM((FFN_NB, D_MODEL, FFN_TF), F32), pltpu.VMEM((FFN_NB, D_MODEL, FFN_TF), F32),
                        pltpu.VMEM((FFN_NB, FFN_TF, D_MODEL), F32),
                        pltpu.SemaphoreType.DMA((FFN_NB,)), pltpu.SemaphoreType.DMA((FFN_XSLOTS,)),
                        pltpu.SemaphoreType.DMA((1,))],
        compiler_params=pltpu.CompilerParams(
            dimension_semantics=("arbitrary",), vmem_limit_bytes=VMEM_LIMIT_BYTES),
        name=name,
    )(xp, xs, xs, mod_p3, mod_p3, mod_p3, mod_p3, mod_p3, mod_s, mod_s, mod_s, mod_s, mod_s,
      nrm, wg, wu, wd, nf)


K_P, K_CC, K_CH, K_CB, K_GA, K_GB, K_AB, K_O = range(8)
_N_CONV_CHUNKS = CONV_WIDTH // MIX_CH
_N_OUT_CHUNKS = D_MODEL // MIX_CH
_S_CONV = POOL_WIDTH // MIX_CH
_S_GATE = _S_CONV + 3 * _N_CONV_CHUNKS
_S_OUT = _S_GATE + 3 * _N_OUT_CHUNKS
_N_STAGES = _S_OUT + _N_OUT_CHUNKS
_KIND_COL0 = (COL_P, COL_CC, COL_CH, COL_CB, COL_GA, COL_GB, 0, 0)
assert _N_STAGES % MIX_NB == 0


def _decode_stage(t):
    u1 = t - _S_CONV
    u2 = t - _S_GATE
    c1 = lax.div(jnp.maximum(u1, 0), 3)
    c2 = lax.div(jnp.maximum(u2, 0), 3)
    kind = jnp.where(t < _S_CONV, K_P,
                     jnp.where(t < _S_GATE, K_CC + (u1 - 3 * c1),
                               jnp.where(t < _S_OUT, K_GA + (u2 - 3 * c2), K_O)))
    chunk = jnp.where(t < _S_CONV, t, jnp.where(t < _S_GATE, c1, jnp.where(t < _S_OUT, c2, t - _S_OUT)))
    col0 = jnp.int32(0)
    for k, base in enumerate(_KIND_COL0):
        if base:
            col0 = jnp.where(kind == k, base, col0)
    return kind, chunk, col0 + chunk * MIX_CH


def _mixer_kernel(x1p_hbm, x1s, x1pc, x1sc, shp, scp, g2pc, shs, scs, g2sc, nrm, grp, pscale, cw, cbias,
                  stp, stc, w_in, w_a, w_b, w_o,
                  x2p, x2s, pst, cst, nps, ncs,
                  wbuf, wsem, xbuf, xsem, h2, ap, v, m, zb, ybuf, sa, abf, e0, e1, qext, pcar, qcar,
                  *, tm, ts, tps, n_tiles):
    i = pl.program_id(0)
    oc = pl.program_id(1)
    slab = D_MODEL // MIX_DMA_SPLIT
    half = D_MODEL // 2
    prows = pl.ds(0, tm)
    srows = pl.ds(tm, ts)

    def wcopy(src, src_row, col, dst_slot, dst_row):
        return pltpu.make_async_copy(src.at[pl.ds(src_row, slab), pl.ds(col, MIX_CH)],
                                     wbuf.at[dst_slot, pl.ds(dst_row, slab), :], wsem.at[dst_slot])

    def wstart(t):
        k_t, _, col_t = _decode_stage(t)
        col_t = pl.multiple_of(col_t, MIX_CH)
        slot_t = lax.rem(t, MIX_NB)

        @pl.when(k_t < K_AB)
        def _():
            for k in range(MIX_DMA_SPLIT):
                wcopy(w_in, k * slab, col_t, slot_t, k * slab).start()

        @pl.when(k_t == K_AB)
        def _():
            for k in range(MIX_DMA_SPLIT):
                r0 = k * slab
                src, s0 = (w_a, r0) if r0 < half else (w_b, r0 - half)
                wcopy(src, s0, col_t, slot_t, r0).start()

        @pl.when(k_t == K_O)
        def _():
            for k in range(MIX_DMA_SPLIT):
                wcopy(w_o, k * slab, col_t, slot_t, k * slab).start()

    def wwait(slot):
        for k in range(MIX_DMA_SPLIT):
            wcopy(w_in, k * slab, 0, slot, k * slab).wait()

    n_xchunks = tm // X_ROWS

    def xcopy(tile, r):
        return pltpu.make_async_copy(x1p_hbm.at[pl.ds(tile * tm + r * X_ROWS, X_ROWS), :],
                                     xbuf.at[r % X_BUFS], xsem.at[r % X_BUFS])

    def xprefetch(tile):
        for r in range(min(X_BUFS, n_xchunks)):
            xcopy(tile, r).start()

    def prologue():
        for r in range(n_xchunks):
            xcopy(i, r).wait()
            h2[pl.ds(r * X_ROWS, X_ROWS), :] = _modulate(xbuf[r % X_BUFS], nrm[...], scp[...], shp[...]).astype(BF16)
            if r + X_BUFS < n_xchunks:
                xcopy(i, r + X_BUFS).start()
        h2[srows, :] = _modulate(x1s[...], nrm[...], scs[...], shs[...]).astype(BF16)
        for r in range(POOL_STATE - 1):
            nps[r] = stp[r + 1]
        for r in range(CONV_STATE - 1):
            ncs[r] = stc[r + 1]

    def pool_stage(c):
        zbuf = zb.at[0]
        for gl in range(MIX_CH // POOL_GROUP_DIM):
            g = c * (MIX_CH // POOL_GROUP_DIM) + gl
            w = POOL_WINDOWS[g]
            gcols = pl.ds(g * POOL_GROUP_DIM, POOL_GROUP_DIM)
            lcols = pl.ds(gl * POOL_GROUP_DIM, POOL_GROUP_DIM)
            pg = zbuf[prows, lcols]
            pad = jnp.zeros((SUBLANES, POOL_GROUP_DIM), F32)
            e0[pl.ds(SUBLANES, SUBLANES), :] = pad
            e1[pl.ds(SUBLANES, SUBLANES), :] = pad
            e0[pl.ds(CARRY_P, CARRY_P), :] = pcar[:, gcols]
            e0[pl.ds(2 * CARRY_P, tm), :] = pg
            src, dst = e0, e1
            d = 1
            while d < w:
                dst[pl.ds(CARRY_P, tm + CARRY_P), :] = (src[pl.ds(CARRY_P, tm + CARRY_P), :]
                                                        + src[pl.ds(CARRY_P - d, tm + CARRY_P), :])
                src, dst = dst, src
                d *= 2
            wsum = src[pl.ds(2 * CARRY_P, tm), :]
            pos = (i % tps) * tm + lax.broadcasted_iota(jnp.int32, (CARRY_P, 1), 0)
            cnt = jnp.minimum(pos + 1, w).astype(F32)
            abf[prows, :] = (wsum * (1.0 / w) - pg).astype(BF16)
            head = pl.ds(0, CARRY_P)
            abf[head, :] = (wsum[:CARRY_P, :] / cnt - pg[:CARRY_P, :]).astype(BF16)
            tail = pg[tm - CARRY_P:, :]
            pcar[:, gcols] = tail
            pst[:, gcols] = tail
            ps = zbuf[srows, lcols]
            ssum = ps
            for r in range(POOL_STATE + 1 - w, POOL_STATE):
                ssum = ssum + stp[r, :, gcols]
            abf[srows, :] = (ssum / float(min(PAST_LEN + 1, w)) - ps).astype(BF16)
            nps[POOL_STATE - 1, :, gcols] = ps
            ag = _bdot(abf[...], grp[g].astype(BF16)) * pscale[:, gcols]
            ap[g] = ag.astype(BF16)

    def conv_stage(c):
        ccols = pl.ds(c * MIX_CH, MIX_CH)
        zbuf, zprev = zb.at[0], zb.at[1]
        qp = zprev[prows, :] * zbuf[prows, :]
        qext[pl.ds(0, CARRY_Q), :] = qcar[:, ccols]
        qext[pl.ds(CARRY_Q, tm), :] = qp
        y = cbias[:, ccols] + qext[pl.ds(CARRY_Q - 2, tm), :] * cw[0:1, ccols]
        y = y + qext[pl.ds(CARRY_Q - 1, tm), :] * cw[1:2, ccols]
        tail = qp[tm - CARRY_Q:, :]
        qcar[:, ccols] = tail
        cst[:, ccols] = tail
        qs = zprev[srows, :] * zbuf[srows, :]
        ys = cbias[:, ccols] + stc[0, :, ccols] * cw[0:1, ccols]
        ys = ys + stc[1, :, ccols] * cw[1:2, ccols]
        ncs[CONV_STATE - 1, :, ccols] = qs
        ybuf[prows, :] = y + qp * cw[2:3, ccols]
        ybuf[srows, :] = ys + qs * cw[2:3, ccols]

    def main_stage(s):
        kind, chunk, _ = _decode_stage(s)
        slot = lax.rem(s, MIX_NB)
        wwait(slot)
        wstart(s + (MIX_NB - 1))

        @pl.when(kind < K_AB)
        def _():
            sa[...] = _sigmoid(zb[1])
            zb[0] = _bdot(h2[...], wbuf[slot].astype(BF16))

        @pl.when(jnp.logical_or(kind == K_CC, kind == K_GA))
        def _():
            zb[1] = zb[0]

        for c in range(POOL_WIDTH // MIX_CH):
            @pl.when(jnp.logical_and(kind == K_P, chunk == c))
            def _(c=c):
                pool_stage(c)

        for c in range(_N_CONV_CHUNKS):
            @pl.when(jnp.logical_and(kind == K_CH, chunk == c))
            def _(c=c):
                conv_stage(c)

        @pl.when(kind == K_CB)
        def _():
            v[chunk] = (zb[0] * ybuf[...]).astype(BF16)

        @pl.when(kind == K_AB)
        def _():
            sb = _sigmoid(zb[0])
            for hc in range(MIX_CH // MIX_HALF):
                hcols = pl.ds(hc * MIX_HALF, MIX_HALF)
                ua = None
                for g in range(len(POOL_WINDOWS)):
                    part = _bdot(ap[g], wbuf[slot, pl.ds(g * POOL_GROUP_DIM, POOL_GROUP_DIM), hcols].astype(BF16))
                    ua = part if ua is None else ua + part
                ub = None
                for k in range(_N_CONV_CHUNKS):
                    part = _bdot(v[k], wbuf[slot, pl.ds(half + k * MIX_CH, MIX_CH), hcols].astype(BF16))
                    ub = part if ub is None else ub + part
                sbh = sb[:, hc * MIX_HALF:(hc + 1) * MIX_HALF]
                m[chunk, :, hcols] = (sa[:, hcols] * ua + sbh * ub).astype(BF16)

    assert _S_OUT + MIX_NB - 2 < _N_STAGES

    @pl.when(oc == 0)
    def _():
        @pl.when(i == 0)
        def _():
            for t in range(MIX_NB - 1):
                wstart(jnp.int32(t))
            xprefetch(0)
            zb[...] = jnp.zeros(zb.shape, F32)

        @pl.when(i % tps == 0)
        def _():
            pcar[...] = jnp.zeros(pcar.shape, F32)
            qcar[...] = jnp.zeros(qcar.shape, F32)

        prologue()
        pl.loop(0, _S_OUT)(main_stage)

    s_out = _S_OUT + oc
    slot = lax.rem(s_out, MIX_NB)
    wwait(slot)
    nxt = s_out + (MIX_NB - 1)

    @pl.when(nxt < _N_STAGES)
    def _():
        wstart(nxt)

    @pl.when(jnp.logical_and(nxt >= _N_STAGES, i < n_tiles - 1))
    def _():
        wstart(nxt - _N_STAGES)

    @pl.when(jnp.logical_and(oc == 1, i < n_tiles - 1))
    def _():
        xprefetch(i + 1)

    for hc in range(MIX_CH // MIX_HALF):
        hcols = pl.ds(hc * MIX_HALF, MIX_HALF)
        acc = None
        for k in range(_N_OUT_CHUNKS):
            part = _bdot(m[k], wbuf[slot, pl.ds(k * MIX_CH, MIX_CH), hcols].astype(BF16))
            acc = part if acc is None else acc + part
        x2p[:, hcols] = x1pc[:, hcols] + g2pc[:, hcols] * acc[:tm]
        x2s[:, hcols] = x1sc[:, hcols] + g2sc[:, hcols] * acc[tm:]


def _mixer(x1p, x1s, mod_p3, mod_s, nrm, grp, pscale, cw, cbias, stp_t, stc_t, w_in, w_a, w_b, w_o, *, tm, ts, tps):
    n_tiles = x1p.shape[0] // tm
    n_seq = n_tiles // tps
    n_s = x1s.shape[0]
    assert n_s == n_tiles * ts
    rows = tm + ts
    any_spec = pl.BlockSpec(memory_space=pl.ANY)

    def const(shape):
        return pl.BlockSpec(shape, lambda i, s: (0,) * len(shape))

    def oc(s):
        return s

    g2_blk = (MOD_G2 * D_MODEL) // MIX_CH
    in_specs = [
        any_spec,
        pl.BlockSpec((ts, D_MODEL), lambda i, s: (i, 0)),
        pl.BlockSpec((tm, MIX_CH), lambda i, s: (i, oc(s))),
        pl.BlockSpec((ts, MIX_CH), lambda i, s: (i, oc(s))),
        pl.BlockSpec((None, 1, D_MODEL), lambda i, s: (i // tps, 0, MOD_SH2)),
        pl.BlockSpec((None, 1, D_MODEL), lambda i, s: (i // tps, 0, MOD_SC2)),
        pl.BlockSpec((None, 1, MIX_CH), lambda i, s: (i // tps, 0, g2_blk + oc(s))),
        pl.BlockSpec((ts, D_MODEL), lambda i, s: (i, MOD_SH2)),
        pl.BlockSpec((ts, D_MODEL), lambda i, s: (i, MOD_SC2)),
        pl.BlockSpec((ts, MIX_CH), lambda i, s: (i, g2_blk + oc(s))),
        const((1, D_MODEL)),
        const((len(POOL_WINDOWS), POOL_GROUP_DIM, POOL_GROUP_DIM)),
        const((1, POOL_WIDTH)),
        const((3, CONV_WIDTH)),
        const((1, CONV_WIDTH)),
        pl.BlockSpec((POOL_STATE, ts, POOL_WIDTH), lambda i, s: (0, i, 0)),
        pl.BlockSpec((CONV_STATE, ts, CONV_WIDTH), lambda i, s: (0, i, 0)),
        any_spec, any_spec, any_spec, any_spec,
    ]
    out_specs = [
        pl.BlockSpec((tm, MIX_CH), lambda i, s: (i, oc(s))),
        pl.BlockSpec((ts, MIX_CH), lambda i, s: (i, oc(s))),
        pl.BlockSpec((None, CARRY_P, POOL_WIDTH), lambda i, s: (i // tps, 0, 0)),
        pl.BlockSpec((None, CARRY_Q, CONV_WIDTH), lambda i, s: (i // tps, 0, 0)),
        pl.BlockSpec((POOL_STATE, ts, POOL_WIDTH), lambda i, s: (0, i, 0)),
        pl.BlockSpec((CONV_STATE, ts, CONV_WIDTH), lambda i, s: (0, i, 0)),
    ]
    out_shape = [
        jax.ShapeDtypeStruct(x1p.shape, F32),
        jax.ShapeDtypeStruct(x1s.shape, F32),
        jax.ShapeDtypeStruct((n_seq, CARRY_P, POOL_WIDTH), F32),
        jax.ShapeDtypeStruct((n_seq, CARRY_Q, CONV_WIDTH), F32),
        jax.ShapeDtypeStruct((POOL_STATE, n_s, POOL_WIDTH), F32),
        jax.ShapeDtypeStruct((CONV_STATE, n_s, CONV_WIDTH), F32),
    ]
    scratch = [
        pltpu.VMEM((MIX_NB, D_MODEL, MIX_CH), F32),
        pltpu.SemaphoreType.DMA((MIX_NB,)),
        pltpu.VMEM((X_BUFS, X_ROWS, D_MODEL), F32),
        pltpu.SemaphoreType.DMA((X_BUFS,)),
        pltpu.VMEM((rows, D_MODEL), BF16),
        pltpu.VMEM((len(POOL_WINDOWS), rows, POOL_GROUP_DIM), BF16),
        pltpu.VMEM((_N_CONV_CHUNKS, rows, MIX_CH), BF16),
        pltpu.VMEM((_N_OUT_CHUNKS, rows, MIX_CH), BF16),
        pltpu.VMEM((2, rows, MIX_CH), F32),
        pltpu.VMEM((rows, MIX_CH), F32),
        pltpu.VMEM((rows, MIX_CH), F32),
        pltpu.VMEM((rows, POOL_GROUP_DIM), BF16),
        pltpu.VMEM((tm + 2 * CARRY_P, POOL_GROUP_DIM), F32),
        pltpu.VMEM((tm + 2 * CARRY_P, POOL_GROUP_DIM), F32),
        pltpu.VMEM((tm + CARRY_Q, MIX_CH), F32),
        pltpu.VMEM((CARRY_P, POOL_WIDTH), F32),
        pltpu.VMEM((CARRY_Q, CONV_WIDTH), F32),
    ]
    return pl.pallas_call(
        functools.partial(_mixer_kernel, tm=tm, ts=ts, tps=tps, n_tiles=n_tiles),
        grid=(n_tiles, _N_OUT_CHUNKS),
        in_specs=in_specs,
        out_specs=out_specs,
        out_shape=out_shape,
        scratch_shapes=scratch,
        compiler_params=pltpu.CompilerParams(
            dimension_semantics=("arbitrary", "arbitrary"), vmem_limit_bytes=VMEM_LIMIT_BYTES),
        name="mixer",
    )(x1p, x1s, x1p, x1s, mod_p3, mod_p3, mod_p3, mod_s, mod_s, mod_s, nrm, grp, pscale, cw, cbias,
      stp_t, stc_t, w_in, w_a, w_b, w_o)


def kernel(x_prompt, x_sample, c_prompt, c_sample, state_pool, state_conv, w_ada, b_ada, norm1, ffn1_gate, ffn1_up, ffn1_down, norm2, w_in, pool_grp, pool_scale, w_branch_a, conv_w, conv_b, w_branch_b, w_o, norm3, ffn2_gate, ffn2_up, ffn2_down, norm_final):
    assert w_ada.shape[0] == 1, "single-layer step"
    batch, seq, d = x_prompt.shape
    dec_batch = x_sample.shape[0]
    assert d == D_MODEL and x_sample.shape[1] == 1

    tm = TM_PROMPT
    tps = seq // tm
    n_tiles = batch * tps
    ts = dec_batch // n_tiles
    assert seq % tm == 0 and dec_batch % n_tiles == 0 and ts % (2 * SUBLANES) == 0

    n_c = batch + dec_batch
    pad = (-n_c) % SUBLANES
    c_all = jnp.concatenate([c_sample, c_prompt, jnp.zeros((pad, d), F32)], axis=0)
    mod = _ada(c_all, w_ada[0], b_ada)
    mod_p3 = mod[dec_batch:n_c].reshape(batch, 1, N_MOD * d)

    row = lambda a: a.reshape(1, -1)
    xp = x_prompt.reshape(batch * seq, d)
    xs = x_sample.reshape(dec_batch, d)
    nf = row(norm_final)
    x1p, x1s = _ffn(xp, xs, mod_p3, mod, MOD_SH1, MOD_SC1, MOD_G1, row(norm1[0]), ffn1_gate[0], ffn1_up[0],
                    ffn1_down[0], nf, tm=tm, ts=ts, tps=tps, final_norm=False, name="ffn1")
    stp_t = jnp.transpose(state_pool[0], (1, 0, 2))
    stc_t = jnp.transpose(state_conv[0], (1, 0, 2))
    x2p, x2s, pst, cst, nps_t, ncs_t = _mixer(
        x1p, x1s, mod_p3, mod, row(norm2[0]), pool_grp[0], row(pool_scale[0]), conv_w[0], row(conv_b[0]),
        stp_t, stc_t, w_in[0], w_branch_a[0], w_branch_b[0], w_o[0], tm=tm, ts=ts, tps=tps)
    yp, ys = _ffn(x2p, x2s, mod_p3, mod, MOD_SH3, MOD_SC3, MOD_G3, row(norm3[0]), ffn2_gate[0], ffn2_up[0],
                  ffn2_down[0], nf, tm=tm, ts=ts, tps=tps, final_norm=True, name="ffn2")

    y_prompt = yp.reshape(batch, seq, d)
    y_sample = ys.reshape(dec_batch, 1, d)
    new_pool_prompt = pst[None, :, CARRY_P - POOL_STATE:, :]
    new_conv_prompt = cst[None, :, CARRY_Q - CONV_STATE:, :]
    new_pool_sample = jnp.transpose(nps_t, (1, 0, 2))[None]
    new_conv_sample = jnp.transpose(ncs_t, (1, 0, 2))[None]
    return (y_prompt, y_sample, new_pool_prompt, new_conv_prompt, new_pool_sample, new_conv_sample)
```

```python
import functools

import jax
import jax.numpy as jnp
from jax import lax
from jax.experimental import pallas as pl
from jax.experimental.pallas import tpu as pltpu

F32 = jnp.float32
BF16 = jnp.bfloat16

D_MODEL = 2048
D_FF = 5632
POOL_WIDTH = 1024
CONV_WIDTH = 1024
POOL_WINDOWS = (2, 4, 8, 16)
POOL_GROUP_DIM = 256
POOL_STATE = 15
CONV_STATE = 2
N_MOD = 9
EPS = 1e-6
PAST_LEN = 16384

V7X_VMEM_BYTES = 64 * 1024 * 1024
SUBLANES = 8
VMEM_LIMIT_BYTES = V7X_VMEM_BYTES - 2 * 1024 * 1024

TM_PROMPT = 1024

ADA_TN = 1024
FFN_TF = 256
FFN_DMA_SPLIT = 4
FFN_NB = 3
MIX_CH = 512
MIX_HALF = 256
MIX_NB = 3
X_ROWS = 128
FFN_ROWS = 128
X_BUFS = 4
MIX_DMA_SPLIT = 8
CARRY_P = 16
CARRY_Q = 8

COL_P, COL_CB, COL_CC, COL_CH, COL_GA, COL_GB = 0, 1024, 2048, 3072, 4096, 6144
MOD_SH1, MOD_SC1, MOD_G1, MOD_SH2, MOD_SC2, MOD_G2, MOD_SH3, MOD_SC3, MOD_G3 = range(9)


def _sigmoid(x):
    return 1.0 / (1.0 + jnp.exp(-x))


def _modulate(x, nrm, sc, sh):
    inv = lax.rsqrt(jnp.mean(x * x, axis=-1, keepdims=True) + EPS)
    return (x * inv) * nrm * (1.0 + sc) + sh


def _bdot(a, b):
    return jnp.dot(a, b, preferred_element_type=F32)


def _ada_kernel(c_ref, w_ref, b_ref, o_ref):
    c = c_ref[...]
    s = (c * _sigmoid(c)).astype(BF16)
    o_ref[...] = _bdot(s, w_ref[...].astype(BF16)) + b_ref[...]


def _ada(c_all, w_ada, b_ada):
    rows = c_all.shape[0]
    cols = w_ada.shape[1]
    return pl.pallas_call(
        _ada_kernel,
        grid=(cols // ADA_TN,),
        in_specs=[
            pl.BlockSpec((rows, D_MODEL), lambda j: (0, 0)),
            pl.BlockSpec((D_MODEL, ADA_TN), lambda j: (0, j)),
            pl.BlockSpec((1, ADA_TN), lambda j: (0, j)),
        ],
        out_specs=pl.BlockSpec((rows, ADA_TN), lambda j: (0, j)),
        out_shape=jax.ShapeDtypeStruct((rows, cols), F32),
        compiler_params=pltpu.CompilerParams(dimension_semantics=("arbitrary",)),
        name="ada_mod",
    )(c_all, w_ada, b_ada)


def _ffn_kernel(*refs, n_f, n_tiles, final_norm, inv_given, emit_inv):
    refs = list(refs)
    xp_ref, xs_ref = refs[:2]
    del refs[:2]
    inv_in = refs.pop(0) if inv_given else None
    shp, scp, gtp, shs, scs, gts, nrm_ref, wg_hbm, wu_hbm, wd_hbm, nf_ref, op_ref, os_ref = refs[:13]
    del refs[:13]
    inv_out = refs.pop(0) if emit_inv else None
    h_ref, inv_ref, wgb, wub, wdb, wsem = refs
    if inv_given:
        inv_ref = inv_in
    i = pl.program_id(0)
    tm = xp_ref.shape[0]
    ts = xs_ref.shape[0]

    def wcopies(f, slot):
        col = pl.multiple_of(f * FFN_TF, FFN_TF)
        rows_in = D_MODEL // FFN_DMA_SPLIT
        rows_dn = FFN_TF // FFN_DMA_SPLIT
        out = []
        for k in range(FFN_DMA_SPLIT):
            r_in = pl.ds(k * rows_in, rows_in)
            out.append(pltpu.make_async_copy(wg_hbm.at[r_in, pl.ds(col, FFN_TF)], wgb.at[slot, r_in, :],
                                             wsem.at[slot]))
            out.append(pltpu.make_async_copy(wu_hbm.at[r_in, pl.ds(col, FFN_TF)], wub.at[slot, r_in, :],
                                             wsem.at[slot]))
            out.append(pltpu.make_async_copy(wd_hbm.at[pl.ds(col + k * rows_dn, rows_dn), :],
                                             wdb.at[slot, pl.ds(k * rows_dn, rows_dn), :], wsem.at[slot]))
        return out

    def wstart(f, slot):
        for cp in wcopies(f, slot):
            cp.start()

    def wwait(f, slot):
        for cp in wcopies(f, slot):
            cp.wait()

    @pl.when(i == 0)
    def _():
        for t in range(FFN_NB - 1):
            wstart(jnp.int32(t), t)

    def head(x, sc, sh):
        return _modulate(x, nrm_ref[...], sc, sh).astype(BF16)

    def tail(x, gt, acc):
        xo = x + 0.5 * gt * acc
        if final_norm:
            inv = lax.rsqrt(jnp.mean(xo * xo, axis=-1, keepdims=True) + EPS)
            xo = (xo * inv) * nf_ref[...]
        return xo

    if not inv_given:
        @pl.loop(0, tm // FFN_ROWS)
        def _(r):
            rows = pl.ds(pl.multiple_of(r * FFN_ROWS, FFN_ROWS), FFN_ROWS)
            x = xp_ref[rows, :]
            inv_ref[rows, :] = lax.rsqrt(jnp.mean(x * x, axis=-1, keepdims=True) + EPS)

    @pl.loop(0, tm // FFN_ROWS)
    def _(r):
        rows = pl.ds(pl.multiple_of(r * FFN_ROWS, FFN_ROWS), FFN_ROWS)
        hx = (xp_ref[rows, :] * inv_ref[rows, :]) * nrm_ref[...] * (1.0 + scp[...]) + shp[...]
        h_ref[rows, :] = hx.astype(BF16)
        op_ref[rows, :] = jnp.zeros((FFN_ROWS, D_MODEL), F32)

    h_ref[pl.ds(tm, ts), :] = head(xs_ref[...], scs[...], shs[...])
    os_ref[...] = jnp.zeros((ts, D_MODEL), F32)

    @pl.loop(0, n_f)
    def _(f):
        g_idx = i * n_f + f
        slot = lax.rem(g_idx, FFN_NB)
        wwait(f, slot)
        nxt = f + (FFN_NB - 1)
        nslot = lax.rem(g_idx + (FFN_NB - 1), FFN_NB)

        @pl.when(nxt < n_f)
        def _():
            wstart(nxt, nslot)

        @pl.when(jnp.logical_and(nxt >= n_f, i < n_tiles - 1))
        def _():
            wstart(nxt - n_f, nslot)

        h = h_ref[...]
        g = _bdot(h, wgb[slot].astype(BF16))
        u = _bdot(h, wub[slot].astype(BF16))
        a = ((g * _sigmoid(g)) * u).astype(BF16)
        for half in range(2):
            hcols = pl.ds(half * (D_MODEL // 2), D_MODEL // 2)
            y = _bdot(a, wdb[slot, :, hcols].astype(BF16))
            op_ref[:, hcols] += y[:tm]
            os_ref[:, hcols] += y[tm:]

    @pl.loop(0, tm // FFN_ROWS)
    def _(r):
        rows = pl.ds(pl.multiple_of(r * FFN_ROWS, FFN_ROWS), FFN_ROWS)
        xo = tail(xp_ref[rows, :], gtp[...], op_ref[rows, :])
        op_ref[rows, :] = xo
        if emit_inv:
            inv_out[rows, :] = lax.rsqrt(jnp.mean(xo * xo, axis=-1, keepdims=True) + EPS)

    os_ref[...] = tail(xs_ref[...], gts[...], os_ref[...])


def _ffn(xp, xs, mod_p3, mod_s, j_sh, j_sc, j_g, nrm, wg, wu, wd, nf, *, tm, ts, tps, final_norm, name,
         inv_in=None, emit_inv=False):
    n_tiles = xp.shape[0] // tm
    assert xs.shape[0] == n_tiles * ts
    n_f = D_FF // FFN_TF
    assert n_f >= FFN_NB
    any_spec = pl.BlockSpec(memory_space=pl.ANY)

    def pspec(j):
        return pl.BlockSpec((None, 1, D_MODEL), lambda i: (i // tps, 0, j))

    def sspec(j):
        return pl.BlockSpec((ts, D_MODEL), lambda i: (i, j))

    assert not (emit_inv and final_norm)
    inv_spec = pl.BlockSpec((tm, 1), lambda i: (i, 0))
    inv_shape = jax.ShapeDtypeStruct((xp.shape[0], 1), F32)
    inv_given = inv_in is not None
    return pl.pallas_call(
        functools.partial(_ffn_kernel, n_f=n_f, n_tiles=n_tiles, final_norm=final_norm, inv_given=inv_given,
                          emit_inv=emit_inv),
        grid=(n_tiles,),
        in_specs=[
            pl.BlockSpec((tm, D_MODEL), lambda i: (i, 0)),
            pl.BlockSpec((ts, D_MODEL), lambda i: (i, 0)),
            *([inv_spec] if inv_given else []),
            pspec(j_sh), pspec(j_sc), pspec(j_g),
            sspec(j_sh), sspec(j_sc), sspec(j_g),
            pl.BlockSpec((1, D_MODEL), lambda i: (0, 0)),
            any_spec, any_spec, any_spec,
            pl.BlockSpec((1, D_MODEL), lambda i: (0, 0)),
        ],
        out_specs=[pl.BlockSpec((tm, D_MODEL), lambda i: (i, 0)),
                   pl.BlockSpec((ts, D_MODEL), lambda i: (i, 0)),
                   *([inv_spec] if emit_inv else [])],
        out_shape=[jax.ShapeDtypeStruct(xp.shape, F32), jax.ShapeDtypeStruct(xs.shape, F32),
                   *([inv_shape] if emit_inv else [])],
        scratch_shapes=[pltpu.VMEM((tm + ts, D_MODEL), BF16), pltpu.VMEM((tm, 1), F32),
                        pltpu.VMEM((FFN_NB, D_MODEL, FFN_TF), F32), pltpu.VMEM((FFN_NB, D_MODEL, FFN_TF), F32),
                        pltpu.VMEM((FFN_NB, FFN_TF, D_MODEL), F32), pltpu.SemaphoreType.DMA((FFN_NB,))],
        compiler_params=pltpu.CompilerParams(
            dimension_semantics=("arbitrary",), vmem_limit_bytes=VMEM_LIMIT_BYTES),
        name=name,
    )(xp, xs, *([inv_in] if inv_given else []), mod_p3, mod_p3, mod_p3, mod_s, mod_s, mod_s, nrm, wg, wu, wd, nf)


K_P, K_CC, K_CH, K_CB, K_GA, K_GB, K_AB, K_O = range(8)
_N_CONV_CHUNKS = CONV_WIDTH // MIX_CH
_N_OUT_CHUNKS = D_MODEL // MIX_CH
_S_CONV = POOL_WIDTH // MIX_CH
_S_GATE = _S_CONV + 3 * _N_CONV_CHUNKS
_S_OUT = _S_GATE + 3 * _N_OUT_CHUNKS
_N_STAGES = _S_OUT + _N_OUT_CHUNKS
_KIND_COL0 = (COL_P, COL_CC, COL_CH, COL_CB, COL_GA, COL_GB, 0, 0)
assert _N_STAGES % MIX_NB == 0


def _decode_stage(t):
    u1 = t - _S_CONV
    u2 = t - _S_GATE
    c1 = lax.div(jnp.maximum(u1, 0), 3)
    c2 = lax.div(jnp.maximum(u2, 0), 3)
    kind = jnp.where(t < _S_CONV, K_P,
                     jnp.where(t < _S_GATE, K_CC + (u1 - 3 * c1),
                               jnp.where(t < _S_OUT, K_GA + (u2 - 3 * c2), K_O)))
    chunk = jnp.where(t < _S_CONV, t, jnp.where(t < _S_GATE, c1, jnp.where(t < _S_OUT, c2, t - _S_OUT)))
    col0 = jnp.int32(0)
    for k, base in enumerate(_KIND_COL0):
        if base:
            col0 = jnp.where(kind == k, base, col0)
    return kind, chunk, col0 + chunk * MIX_CH


def _mixer_kernel(x1p_hbm, x1s, inv1, x1pc, x1sc, shp, scp, g2pc, shs, scs, g2sc, nrm, grp, pscale, cw, cbias,
                  stp, stc, w_in, w_a, w_b, w_o,
                  x2p, x2s, pst, cst, nps, ncs, inv2,
                  wbuf, wsem, xbuf, xsem, h2, ap, v, m, zb, ybuf, sa, abf, e0, e1, qext, pcar, qcar, ssq,
                  *, tm, ts, tps, n_tiles):
    i = pl.program_id(0)
    oc = pl.program_id(1)
    slab = D_MODEL // MIX_DMA_SPLIT
    half = D_MODEL // 2
    prows = pl.ds(0, tm)
    srows = pl.ds(tm, ts)

    def wcopy(src, src_row, col, dst_slot, dst_row):
        return pltpu.make_async_copy(src.at[pl.ds(src_row, slab), pl.ds(col, MIX_CH)],
                                     wbuf.at[dst_slot, pl.ds(dst_row, slab), :], wsem.at[dst_slot])

    def wstart(t):
        k_t, _, col_t = _decode_stage(t)
        col_t = pl.multiple_of(col_t, MIX_CH)
        slot_t = lax.rem(t, MIX_NB)

        @pl.when(k_t < K_AB)
        def _():
            for k in range(MIX_DMA_SPLIT):
                wcopy(w_in, k * slab, col_t, slot_t, k * slab).start()

        @pl.when(k_t == K_AB)
        def _():
            for k in range(MIX_DMA_SPLIT):
                r0 = k * slab
                src, s0 = (w_a, r0) if r0 < half else (w_b, r0 - half)
                wcopy(src, s0, col_t, slot_t, r0).start()

        @pl.when(k_t == K_O)
        def _():
            for k in range(MIX_DMA_SPLIT):
                wcopy(w_o, k * slab, col_t, slot_t, k * slab).start()

    def wwait(slot):
        for k in range(MIX_DMA_SPLIT):
            wcopy(w_in, k * slab, 0, slot, k * slab).wait()

    n_xchunks = tm // X_ROWS

    def xcopy(tile, r):
        return pltpu.make_async_copy(x1p_hbm.at[pl.ds(tile * tm + r * X_ROWS, X_ROWS), :],
                                     xbuf.at[r % X_BUFS], xsem.at[r % X_BUFS])

    def xprefetch(tile):
        for r in range(min(X_BUFS, n_xchunks)):
            xcopy(tile, r).start()

    def prologue():
        for r in range(n_xchunks):
            xcopy(i, r).wait()
            rows = pl.ds(r * X_ROWS, X_ROWS)
            hx = (xbuf[r % X_BUFS] * inv1[rows, :]) * nrm[...] * (1.0 + scp[...]) + shp[...]
            h2[rows, :] = hx.astype(BF16)
            if r + X_BUFS < n_xchunks:
                xcopy(i, r + X_BUFS).start()
        h2[srows, :] = _modulate(x1s[...], nrm[...], scs[...], shs[...]).astype(BF16)
        for r in range(POOL_STATE - 1):
            nps[r] = stp[r + 1]
        for r in range(CONV_STATE - 1):
            ncs[r] = stc[r + 1]

    def pool_stage(c):
        zbuf = zb.at[0]
        for gl in range(MIX_CH // POOL_GROUP_DIM):
            g = c * (MIX_CH // POOL_GROUP_DIM) + gl
            w = POOL_WINDOWS[g]
            gcols = pl.ds(g * POOL_GROUP_DIM, POOL_GROUP_DIM)
            lcols = pl.ds(gl * POOL_GROUP_DIM, POOL_GROUP_DIM)
            pg = zbuf[prows, lcols]
            pad = jnp.zeros((SUBLANES, POOL_GROUP_DIM), F32)
            e0[pl.ds(SUBLANES, SUBLANES), :] = pad
            e1[pl.ds(SUBLANES, SUBLANES), :] = pad
            e0[pl.ds(CARRY_P, CARRY_P), :] = pcar[:, gcols]
            e0[pl.ds(2 * CARRY_P, tm), :] = pg
            src, dst = e0, e1
            d = 1
            while d < w:
                dst[pl.ds(CARRY_P, tm + CARRY_P), :] = (src[pl.ds(CARRY_P, tm + CARRY_P), :]
                                                        + src[pl.ds(CARRY_P - d, tm + CARRY_P), :])
                src, dst = dst, src
                d *= 2
            wsum = src[pl.ds(2 * CARRY_P, tm), :]
            pos = (i % tps) * tm + lax.broadcasted_iota(jnp.int32, (CARRY_P, 1), 0)
            cnt = jnp.minimum(pos + 1, w).astype(F32)
            abf[prows, :] = (wsum * (1.0 / w) - pg).astype(BF16)
            head = pl.ds(0, CARRY_P)
            abf[head, :] = (wsum[:CARRY_P, :] / cnt - pg[:CARRY_P, :]).astype(BF16)
            tail = pg[tm - CARRY_P:, :]
            pcar[:, gcols] = tail
            pst[:, gcols] = tail
            ps = zbuf[srows, lcols]
            ssum = ps
            for r in range(POOL_STATE + 1 - w, POOL_STATE):
                ssum = ssum + stp[r, :, gcols]
            abf[srows, :] = (ssum / float(min(PAST_LEN + 1, w)) - ps).astype(BF16)
            nps[POOL_STATE - 1, :, gcols] = ps
            ag = _bdot(abf[...], grp[g].astype(BF16)) * pscale[:, gcols]
            ap[g] = ag.astype(BF16)

    def conv_stage(c):
        ccols = pl.ds(c * MIX_CH, MIX_CH)
        zbuf, zprev = zb.at[0], zb.at[1]
        qp = zprev[prows, :] * zbuf[prows, :]
        qext[pl.ds(0, CARRY_Q), :] = qcar[:, ccols]
        qext[pl.ds(CARRY_Q, tm), :] = qp
        y = cbias[:, ccols] + qext[pl.ds(CARRY_Q - 2, tm), :] * cw[0:1, ccols]
        y = y + qext[pl.ds(CARRY_Q - 1, tm), :] * cw[1:2, ccols]
        tail = qp[tm - CARRY_Q:, :]
        qcar[:, ccols] = tail
        cst[:, ccols] = tail
        qs = zprev[srows, :] * zbuf[srows, :]
        ys = cbias[:, ccols] + stc[0, :, ccols] * cw[0:1, ccols]
        ys = ys + stc[1, :, ccols] * cw[1:2, ccols]
        ncs[CONV_STATE - 1, :, ccols] = qs
        ybuf[prows, :] = y + qp * cw[2:3, ccols]
        ybuf[srows, :] = ys + qs * cw[2:3, ccols]

    def main_stage(s):
        kind, chunk, _ = _decode_stage(s)
        slot = lax.rem(s, MIX_NB)
        wwait(slot)
        wstart(s + (MIX_NB - 1))

        @pl.when(kind < K_AB)
        def _():
            sa[...] = _sigmoid(zb[1])
            zb[0] = _bdot(h2[...], wbuf[slot].astype(BF16))

        @pl.when(jnp.logical_or(kind == K_CC, kind == K_GA))
        def _():
            zb[1] = zb[0]

        for c in range(POOL_WIDTH // MIX_CH):
            @pl.when(jnp.logical_and(kind == K_P, chunk == c))
            def _(c=c):
                pool_stage(c)

        for c in range(_N_CONV_CHUNKS):
            @pl.when(jnp.logical_and(kind == K_CH, chunk == c))
            def _(c=c):
                conv_stage(c)

        @pl.when(kind == K_CB)
        def _():
            v[chunk] = (zb[0] * ybuf[...]).astype(BF16)

        @pl.when(kind == K_AB)
        def _():
            sb = _sigmoid(zb[0])
            for hc in range(MIX_CH // MIX_HALF):
                hcols = pl.ds(hc * MIX_HALF, MIX_HALF)
                ua = None
                for g in range(len(POOL_WINDOWS)):
                    part = _bdot(ap[g], wbuf[slot, pl.ds(g * POOL_GROUP_DIM, POOL_GROUP_DIM), hcols].astype(BF16))
                    ua = part if ua is None else ua + part
                ub = None
                for k in range(_N_CONV_CHUNKS):
                    part = _bdot(v[k], wbuf[slot, pl.ds(half + k * MIX_CH, MIX_CH), hcols].astype(BF16))
                    ub = part if ub is None else ub + part
                sbh = sb[:, hc * MIX_HALF:(hc + 1) * MIX_HALF]
                m[chunk, :, hcols] = (sa[:, hcols] * ua + sbh * ub).astype(BF16)

    assert _S_OUT + MIX_NB - 2 < _N_STAGES

    @pl.when(oc == 0)
    def _():
        @pl.when(i == 0)
        def _():
            for t in range(MIX_NB - 1):
                wstart(jnp.int32(t))
            xprefetch(0)
            zb[...] = jnp.zeros(zb.shape, F32)

        @pl.when(i % tps == 0)
        def _():
            pcar[...] = jnp.zeros(pcar.shape, F32)
            qcar[...] = jnp.zeros(qcar.shape, F32)

        prologue()
        pl.loop(0, _S_OUT)(main_stage)

    s_out = _S_OUT + oc
    slot = lax.rem(s_out, MIX_NB)
    wwait(slot)
    nxt = s_out + (MIX_NB - 1)

    @pl.when(nxt < _N_STAGES)
    def _():
        wstart(nxt)

    @pl.when(jnp.logical_and(nxt >= _N_STAGES, i < n_tiles - 1))
    def _():
        wstart(nxt - _N_STAGES)

    @pl.when(jnp.logical_and(oc == 1, i < n_tiles - 1))
    def _():
        xprefetch(i + 1)

    for hc in range(MIX_CH // MIX_HALF):
        hcols = pl.ds(hc * MIX_HALF, MIX_HALF)
        acc = None
        for k in range(_N_OUT_CHUNKS):
            part = _bdot(m[k], wbuf[slot, pl.ds(k * MIX_CH, MIX_CH), hcols].astype(BF16))
            acc = part if acc is None else acc + part
        xo = x1pc[:, hcols] + g2pc[:, hcols] * acc[:tm]
        x2p[:, hcols] = xo
        x2s[:, hcols] = x1sc[:, hcols] + g2sc[:, hcols] * acc[tm:]
        part = jnp.sum(xo * xo, axis=-1, keepdims=True)
        if hc == 0:
            @pl.when(oc == 0)
            def _():
                ssq[...] = part

            @pl.when(oc > 0)
            def _():
                ssq[...] += part
        else:
            ssq[...] += part

    @pl.when(oc == _N_OUT_CHUNKS - 1)
    def _():
        inv2[...] = lax.rsqrt(ssq[...] / D_MODEL + EPS)


def _mixer(x1p, x1s, inv1, mod_p3, mod_s, nrm, grp, pscale, cw, cbias, stp_t, stc_t, w_in, w_a, w_b, w_o, *, tm, ts,
           tps):
    n_tiles = x1p.shape[0] // tm
    n_seq = n_tiles // tps
    n_s = x1s.shape[0]
    assert n_s == n_tiles * ts
    rows = tm + ts
    any_spec = pl.BlockSpec(memory_space=pl.ANY)

    def const(shape):
        return pl.BlockSpec(shape, lambda i, s: (0,) * len(shape))

    def oc(s):
        return s

    g2_blk = (MOD_G2 * D_MODEL) // MIX_CH
    in_specs = [
        any_spec,
        pl.BlockSpec((ts, D_MODEL), lambda i, s: (i, 0)),
        pl.BlockSpec((tm, 1), lambda i, s: (i, 0)),
        pl.BlockSpec((tm, MIX_CH), lambda i, s: (i, oc(s))),
        pl.BlockSpec((ts, MIX_CH), lambda i, s: (i, oc(s))),
        pl.BlockSpec((None, 1, D_MODEL), lambda i, s: (i // tps, 0, MOD_SH2)),
        pl.BlockSpec((None, 1, D_MODEL), lambda i, s: (i // tps, 0, MOD_SC2)),
        pl.BlockSpec((None, 1, MIX_CH), lambda i, s: (i // tps, 0, g2_blk + oc(s))),
        pl.BlockSpec((ts, D_MODEL), lambda i, s: (i, MOD_SH2)),
        pl.BlockSpec((ts, D_MODEL), lambda i, s: (i, MOD_SC2)),
        pl.BlockSpec((ts, MIX_CH), lambda i, s: (i, g2_blk + oc(s))),
        const((1, D_MODEL)),
        const((len(POOL_WINDOWS), POOL_GROUP_DIM, POOL_GROUP_DIM)),
        const((1, POOL_WIDTH)),
        const((3, CONV_WIDTH)),
        const((1, CONV_WIDTH)),
        pl.BlockSpec((POOL_STATE, ts, POOL_WIDTH), lambda i, s: (0, i, 0)),
        pl.BlockSpec((CONV_STATE, ts, CONV_WIDTH), lambda i, s: (0, i, 0)),
        any_spec, any_spec, any_spec, any_spec,
    ]
    out_specs = [
        pl.BlockSpec((tm, MIX_CH), lambda i, s: (i, oc(s))),
        pl.BlockSpec((ts, MIX_CH), lambda i, s: (i, oc(s))),
        pl.BlockSpec((None, CARRY_P, POOL_WIDTH), lambda i, s: (i // tps, 0, 0)),
        pl.BlockSpec((None, CARRY_Q, CONV_WIDTH), lambda i, s: (i // tps, 0, 0)),
        pl.BlockSpec((POOL_STATE, ts, POOL_WIDTH), lambda i, s: (0, i, 0)),
        pl.BlockSpec((CONV_STATE, ts, CONV_WIDTH), lambda i, s: (0, i, 0)),
        pl.BlockSpec((tm, 1), lambda i, s: (i, 0)),
    ]
    out_shape = [
        jax.ShapeDtypeStruct(x1p.shape, F32),
        jax.ShapeDtypeStruct(x1s.shape, F32),
        jax.ShapeDtypeStruct((n_seq, CARRY_P, POOL_WIDTH), F32),
        jax.ShapeDtypeStruct((n_seq, CARRY_Q, CONV_WIDTH), F32),
        jax.ShapeDtypeStruct((POOL_STATE, n_s, POOL_WIDTH), F32),
        jax.ShapeDtypeStruct((CONV_STATE, n_s, CONV_WIDTH), F32),
        jax.ShapeDtypeStruct((x1p.shape[0], 1), F32),
    ]
    scratch = [
        pltpu.VMEM((MIX_NB, D_MODEL, MIX_CH), F32),
        pltpu.SemaphoreType.DMA((MIX_NB,)),
        pltpu.VMEM((X_BUFS, X_ROWS, D_MODEL), F32),
        pltpu.SemaphoreType.DMA((X_BUFS,)),
        pltpu.VMEM((rows, D_MODEL), BF16),
        pltpu.VMEM((len(POOL_WINDOWS), rows, POOL_GROUP_DIM), BF16),
        pltpu.VMEM((_N_CONV_CHUNKS, rows, MIX_CH), BF16),
        pltpu.VMEM((_N_OUT_CHUNKS, rows, MIX_CH), BF16),
        pltpu.VMEM((2, rows, MIX_CH), F32),
        pltpu.VMEM((rows, MIX_CH), F32),
        pltpu.VMEM((rows, MIX_CH), F32),
        pltpu.VMEM((rows, POOL_GROUP_DIM), BF16),
        pltpu.VMEM((tm + 2 * CARRY_P, POOL_GROUP_DIM), F32),
        pltpu.VMEM((tm + 2 * CARRY_P, POOL_GROUP_DIM), F32),
        pltpu.VMEM((tm + CARRY_Q, MIX_CH), F32),
        pltpu.VMEM((CARRY_P, POOL_WIDTH), F32),
        pltpu.VMEM((CARRY_Q, CONV_WIDTH), F32),
        pltpu.VMEM((tm, 1), F32),
    ]
    return pl.pallas_call(
        functools.partial(_mixer_kernel, tm=tm, ts=ts, tps=tps, n_tiles=n_tiles),
        grid=(n_tiles, _N_OUT_CHUNKS),
        in_specs=in_specs,
        out_specs=out_specs,
        out_shape=out_shape,
        scratch_shapes=scratch,
        compiler_params=pltpu.CompilerParams(
            dimension_semantics=("arbitrary", "arbitrary"), vmem_limit_bytes=VMEM_LIMIT_BYTES),
        name="mixer",
    )(x1p, x1s, inv1, x1p, x1s, mod_p3, mod_p3, mod_p3, mod_s, mod_s, mod_s, nrm, grp, pscale, cw, cbias,
      stp_t, stc_t, w_in, w_a, w_b, w_o)


def kernel(x_prompt, x_sample, c_prompt, c_sample, state_pool, state_conv, w_ada, b_ada, norm1, ffn1_gate, ffn1_up, ffn1_down, norm2, w_in, pool_grp, pool_scale, w_branch_a, conv_w, conv_b, w_branch_b, w_o, norm3, ffn2_gate, ffn2_up, ffn2_down, norm_final):
    assert w_ada.shape[0] == 1, "single-layer step"
    batch, seq, d = x_prompt.shape
    dec_batch = x_sample.shape[0]
    assert d == D_MODEL and x_sample.shape[1] == 1

    tm = TM_PROMPT
    tps = seq // tm
    n_tiles = batch * tps
    ts = dec_batch // n_tiles
    assert seq % tm == 0 and dec_batch % n_tiles == 0 and ts % (2 * SUBLANES) == 0

    n_c = batch + dec_batch
    pad = (-n_c) % SUBLANES
    c_all = jnp.concatenate([c_sample, c_prompt, jnp.zeros((pad, d), F32)], axis=0)
    mod = _ada(c_all, w_ada[0], b_ada)
    mod_p3 = mod[dec_batch:n_c].reshape(batch, 1, N_MOD * d)

    row = lambda a: a.reshape(1, -1)
    xp = x_prompt.reshape(batch * seq, d)
    xs = jnp.transpose(x_sample, (1, 0, 2))[0]
    nf = row(norm_final)
    x1p, x1s, inv1 = _ffn(xp, xs, mod_p3, mod, MOD_SH1, MOD_SC1, MOD_G1, row(norm1[0]), ffn1_gate[0], ffn1_up[0],
                          ffn1_down[0], nf, tm=tm, ts=ts, tps=tps, final_norm=False, name="ffn1", emit_inv=True)
    stp_t = jnp.transpose(state_pool[0], (1, 0, 2))
    stc_t = jnp.transpose(state_conv[0], (1, 0, 2))
    x2p, x2s, pst, cst, nps_t, ncs_t, inv2 = _mixer(
        x1p, x1s, inv1, mod_p3, mod, row(norm2[0]), pool_grp[0], row(pool_scale[0]), conv_w[0], row(conv_b[0]),
        stp_t, stc_t, w_in[0], w_branch_a[0], w_branch_b[0], w_o[0], tm=tm, ts=ts, tps=tps)
    yp, ys = _ffn(x2p, x2s, mod_p3, mod, MOD_SH3, MOD_SC3, MOD_G3, row(norm3[0]), ffn2_gate[0], ffn2_up[0],
                  ffn2_down[0], nf, tm=tm, ts=ts, tps=tps, final_norm=True, name="ffn2", inv_in=inv2)

    y_prompt = yp.reshape(batch, seq, d)
    y_sample = ys.reshape(dec_batch, 1, d)
    new_pool_prompt = pst[None, :, CARRY_P - POOL_STATE:, :]
    new_conv_prompt = cst[None, :, CARRY_Q - CONV_STATE:, :]
    new_pool_sample = jnp.transpose(nps_t, (1, 0, 2))[None]
    new_conv_sample = jnp.transpose(ncs_t, (1, 0, 2))[None]
    return (y_prompt, y_sample, new_pool_prompt, new_conv_prompt, new_pool_sample, new_conv_sample)
```

```python
import functools

import jax
import jax.numpy as jnp
from jax import lax
from jax.experimental import pallas as pl
from jax.experimental.pallas import tpu as pltpu

F32 = jnp.float32
BF16 = jnp.bfloat16

D_MODEL = 2048
D_FF = 5632
POOL_WIDTH = 1024
CONV_WIDTH = 1024
POOL_WINDOWS = (2, 4, 8, 16)
POOL_GROUP_DIM = 256
POOL_STATE = 15
CONV_STATE = 2
N_MOD = 9
EPS = 1e-6
PAST_LEN = 16384

V7X_VMEM_BYTES = 64 * 1024 * 1024
SUBLANES = 8
VMEM_LIMIT_BYTES = V7X_VMEM_BYTES - 2 * 1024 * 1024

TM_PROMPT = 1024

ADA_TN = 2048
FFN_TF = 256
FFN_DMA_SPLIT = 4
FFN_NB = 3
MIX_CH = 512
MIX_HALF = 256
MIX_NB = 3
X_ROWS = 128
FFN_ROWS = 128
X_BUFS = 4
MIX_DMA_SPLIT = 8
CARRY_P = 16
CARRY_Q = 8

COL_P, COL_CB, COL_CC, COL_CH, COL_GA, COL_GB = 0, 1024, 2048, 3072, 4096, 6144
MOD_SH1, MOD_SC1, MOD_G1, MOD_SH2, MOD_SC2, MOD_G2, MOD_SH3, MOD_SC3, MOD_G3 = range(9)


def _sigmoid(x):
    return 1.0 / (1.0 + jnp.exp(-x))


def _modulate(x, nrm, sc, sh):
    inv = lax.rsqrt(jnp.mean(x * x, axis=-1, keepdims=True) + EPS)
    return (x * inv) * nrm * (1.0 + sc) + sh


def _bdot(a, b):
    return jnp.dot(a, b, preferred_element_type=F32)


def _ada_kernel(c_ref, w_ref, b_ref, o_ref):
    c = c_ref[...]
    s = (c * _sigmoid(c)).astype(BF16)
    o_ref[...] = _bdot(s, w_ref[...].astype(BF16)) + b_ref[...]


def _ada(c_all, w_ada, b_ada):
    rows = c_all.shape[0]
    cols = w_ada.shape[1]
    return pl.pallas_call(
        _ada_kernel,
        grid=(cols // ADA_TN,),
        in_specs=[
            pl.BlockSpec((rows, D_MODEL), lambda j: (0, 0)),
            pl.BlockSpec((D_MODEL, ADA_TN), lambda j: (0, j)),
            pl.BlockSpec((1, ADA_TN), lambda j: (0, j)),
        ],
        out_specs=pl.BlockSpec((rows, ADA_TN), lambda j: (0, j)),
        out_shape=jax.ShapeDtypeStruct((rows, cols), F32),
        compiler_params=pltpu.CompilerParams(dimension_semantics=("arbitrary",)),
        name="ada_mod",
    )(c_all, w_ada, b_ada)


def _ffn_kernel(xp_ref, xs_ref, shp, scp, gtp, shs, scs, gts, nrm_ref, wg_hbm, wu_hbm, wd_hbm, nf_ref,
                op_ref, os_ref, h_ref, inv_ref, wgb, wub, wdb, wsem, *, n_f, n_tiles, final_norm):
    i = pl.program_id(0)
    tm = xp_ref.shape[0]
    ts = xs_ref.shape[0]

    def wcopies(f, slot):
        col = pl.multiple_of(f * FFN_TF, FFN_TF)
        rows_in = D_MODEL // FFN_DMA_SPLIT
        rows_dn = FFN_TF // FFN_DMA_SPLIT
        out = []
        for k in range(FFN_DMA_SPLIT):
            r_in = pl.ds(k * rows_in, rows_in)
            out.append(pltpu.make_async_copy(wg_hbm.at[r_in, pl.ds(col, FFN_TF)], wgb.at[slot, r_in, :],
                                             wsem.at[slot]))
            out.append(pltpu.make_async_copy(wu_hbm.at[r_in, pl.ds(col, FFN_TF)], wub.at[slot, r_in, :],
                                             wsem.at[slot]))
            out.append(pltpu.make_async_copy(wd_hbm.at[pl.ds(col + k * rows_dn, rows_dn), :],
                                             wdb.at[slot, pl.ds(k * rows_dn, rows_dn), :], wsem.at[slot]))
        return out

    def wstart(f, slot):
        for cp in wcopies(f, slot):
            cp.start()

    def wwait(f, slot):
        for cp in wcopies(f, slot):
            cp.wait()

    @pl.when(i == 0)
    def _():
        for t in range(FFN_NB - 1):
            wstart(jnp.int32(t), t)

    def head(x, sc, sh):
        return _modulate(x, nrm_ref[...], sc, sh).astype(BF16)

    def tail(x, gt, acc):
        xo = x + 0.5 * gt * acc
        if final_norm:
            inv = lax.rsqrt(jnp.mean(xo * xo, axis=-1, keepdims=True) + EPS)
            xo = (xo * inv) * nf_ref[...]
        return xo

    @pl.loop(0, tm // FFN_ROWS)
    def _(r):
        rows = pl.ds(pl.multiple_of(r * FFN_ROWS, FFN_ROWS), FFN_ROWS)
        x = xp_ref[rows, :]
        inv_ref[rows, :] = lax.rsqrt(jnp.mean(x * x, axis=-1, keepdims=True) + EPS)

    @pl.loop(0, tm // FFN_ROWS)
    def _(r):
        rows = pl.ds(pl.multiple_of(r * FFN_ROWS, FFN_ROWS), FFN_ROWS)
        hx = (xp_ref[rows, :] * inv_ref[rows, :]) * nrm_ref[...] * (1.0 + scp[...]) + shp[...]
        h_ref[rows, :] = hx.astype(BF16)
        op_ref[rows, :] = jnp.zeros((FFN_ROWS, D_MODEL), F32)

    h_ref[pl.ds(tm, ts), :] = head(xs_ref[...], scs[...], shs[...])
    os_ref[...] = jnp.zeros((ts, D_MODEL), F32)

    @pl.loop(0, n_f)
    def _(f):
        g_idx = i * n_f + f
        slot = lax.rem(g_idx, FFN_NB)
        wwait(f, slot)
        nxt = f + (FFN_NB - 1)
        nslot = lax.rem(g_idx + (FFN_NB - 1), FFN_NB)

        @pl.when(nxt < n_f)
        def _():
            wstart(nxt, nslot)

        @pl.when(jnp.logical_and(nxt >= n_f, i < n_tiles - 1))
        def _():
            wstart(nxt - n_f, nslot)

        h = h_ref[...]
        g = _bdot(h, wgb[slot].astype(BF16))
        u = _bdot(h, wub[slot].astype(BF16))
        a = ((g * _sigmoid(g)) * u).astype(BF16)
        for half in range(2):
            hcols = pl.ds(half * (D_MODEL // 2), D_MODEL // 2)
            y = _bdot(a, wdb[slot, :, hcols].astype(BF16))
            op_ref[:, hcols] += y[:tm]
            os_ref[:, hcols] += y[tm:]

    @pl.loop(0, tm // FFN_ROWS)
    def _(r):
        rows = pl.ds(pl.multiple_of(r * FFN_ROWS, FFN_ROWS), FFN_ROWS)
        op_ref[rows, :] = tail(xp_ref[rows, :], gtp[...], op_ref[rows, :])

    os_ref[...] = tail(xs_ref[...], gts[...], os_ref[...])


def _ffn(xp, xs, mod_p3, mod_s, j_sh, j_sc, j_g, nrm, wg, wu, wd, nf, *, tm, ts, tps, final_norm, name):
    n_tiles = xp.shape[0] // tm
    assert xs.shape[0] == n_tiles * ts
    n_f = D_FF // FFN_TF
    assert n_f >= FFN_NB
    any_spec = pl.BlockSpec(memory_space=pl.ANY)

    def pspec(j):
        return pl.BlockSpec((None, 1, D_MODEL), lambda i: (i // tps, 0, j))

    def sspec(j):
        return pl.BlockSpec((ts, D_MODEL), lambda i: (i, j))

    return pl.pallas_call(
        functools.partial(_ffn_kernel, n_f=n_f, n_tiles=n_tiles, final_norm=final_norm),
        grid=(n_tiles,),
        in_specs=[
            pl.BlockSpec((tm, D_MODEL), lambda i: (i, 0)),
            pl.BlockSpec((ts, D_MODEL), lambda i: (i, 0)),
            pspec(j_sh), pspec(j_sc), pspec(j_g),
            sspec(j_sh), sspec(j_sc), sspec(j_g),
            pl.BlockSpec((1, D_MODEL), lambda i: (0, 0)),
            any_spec, any_spec, any_spec,
            pl.BlockSpec((1, D_MODEL), lambda i: (0, 0)),
        ],
        out_specs=[pl.BlockSpec((tm, D_MODEL), lambda i: (i, 0)),
                   pl.BlockSpec((ts, D_MODEL), lambda i: (i, 0))],
        out_shape=[jax.ShapeDtypeStruct(xp.shape, F32), jax.ShapeDtypeStruct(xs.shape, F32)],
        scratch_shapes=[pltpu.VMEM((tm + ts, D_MODEL), BF16), pltpu.VMEM((tm, 1), F32),
                        pltpu.VMEM((FFN_NB, D_MODEL, FFN_TF), F32), pltpu.VMEM((FFN_NB, D_MODEL, FFN_TF), F32),
                        pltpu.VMEM((FFN_NB, FFN_TF, D_MODEL), F32), pltpu.SemaphoreType.DMA((FFN_NB,))],
        compiler_params=pltpu.CompilerParams(
            dimension_semantics=("arbitrary",), vmem_limit_bytes=VMEM_LIMIT_BYTES),
        name=name,
    )(xp, xs, mod_p3, mod_p3, mod_p3, mod_s, mod_s, mod_s, nrm, wg, wu, wd, nf)


K_P, K_CC, K_CH, K_CB, K_GA, K_GB, K_AB, K_O = range(8)
_N_CONV_CHUNKS = CONV_WIDTH // MIX_CH
_N_OUT_CHUNKS = D_MODEL // MIX_CH
_S_CONV = POOL_WIDTH // MIX_CH
_S_GATE = _S_CONV + 3 * _N_CONV_CHUNKS
_S_OUT = _S_GATE + 3 * _N_OUT_CHUNKS
_N_STAGES = _S_OUT + _N_OUT_CHUNKS
_KIND_COL0 = (COL_P, COL_CC, COL_CH, COL_CB, COL_GA, COL_GB, 0, 0)
assert _N_STAGES % MIX_NB == 0


def _decode_stage(t):
    u1 = t - _S_CONV
    u2 = t - _S_GATE
    c1 = lax.div(jnp.maximum(u1, 0), 3)
    c2 = lax.div(jnp.maximum(u2, 0), 3)
    kind = jnp.where(t < _S_CONV, K_P,
                     jnp.where(t < _S_GATE, K_CC + (u1 - 3 * c1),
                               jnp.where(t < _S_OUT, K_GA + (u2 - 3 * c2), K_O)))
    chunk = jnp.where(t < _S_CONV, t, jnp.where(t < _S_GATE, c1, jnp.where(t < _S_OUT, c2, t - _S_OUT)))
    col0 = jnp.int32(0)
    for k, base in enumerate(_KIND_COL0):
        if base:
            col0 = jnp.where(kind == k, base, col0)
    return kind, chunk, col0 + chunk * MIX_CH


def _mixer_kernel(x1p_hbm, x1s, x1pc, x1sc, shp, scp, g2pc, shs, scs, g2sc, nrm, grp, pscale, cw, cbias,
                  stp, stc, w_in, w_a, w_b, w_o,
                  x2p, x2s, pst, cst, nps, ncs,
                  wbuf, wsem, xbuf, xsem, h2, ap, v, m, zb, ybuf, sa, abf, e0, e1, qext, pcar, qcar,
                  *, tm, ts, tps, n_tiles):
    i = pl.program_id(0)
    oc = pl.program_id(1)
    slab = D_MODEL // MIX_DMA_SPLIT
    half = D_MODEL // 2
    prows = pl.ds(0, tm)
    srows = pl.ds(tm, ts)

    def wcopy(src, src_row, col, dst_slot, dst_row):
        return pltpu.make_async_copy(src.at[pl.ds(src_row, slab), pl.ds(col, MIX_CH)],
                                     wbuf.at[dst_slot, pl.ds(dst_row, slab), :], wsem.at[dst_slot])

    def wstart(t):
        k_t, _, col_t = _decode_stage(t)
        col_t = pl.multiple_of(col_t, MIX_CH)
        slot_t = lax.rem(t, MIX_NB)

        @pl.when(k_t < K_AB)
        def _():
            for k in range(MIX_DMA_SPLIT):
                wcopy(w_in, k * slab, col_t, slot_t, k * slab).start()

        @pl.when(k_t == K_AB)
        def _():
            for k in range(MIX_DMA_SPLIT):
                r0 = k * slab
                src, s0 = (w_a, r0) if r0 < half else (w_b, r0 - half)
                wcopy(src, s0, col_t, slot_t, r0).start()

        @pl.when(k_t == K_O)
        def _():
            for k in range(MIX_DMA_SPLIT):
                wcopy(w_o, k * slab, col_t, slot_t, k * slab).start()

    def wwait(slot):
        for k in range(MIX_DMA_SPLIT):
            wcopy(w_in, k * slab, 0, slot, k * slab).wait()

    n_xchunks = tm // X_ROWS

    def xcopy(tile, r):
        return pltpu.make_async_copy(x1p_hbm.at[pl.ds(tile * tm + r * X_ROWS, X_ROWS), :],
                                     xbuf.at[r % X_BUFS], xsem.at[r % X_BUFS])

    def xprefetch(tile):
        for r in range(min(X_BUFS, n_xchunks)):
            xcopy(tile, r).start()

    def prologue():
        for r in range(n_xchunks):
            xcopy(i, r).wait()
            h2[pl.ds(r * X_ROWS, X_ROWS), :] = _modulate(xbuf[r % X_BUFS], nrm[...], scp[...], shp[...]).astype(BF16)
            if r + X_BUFS < n_xchunks:
                xcopy(i, r + X_BUFS).start()
        h2[srows, :] = _modulate(x1s[...], nrm[...], scs[...], shs[...]).astype(BF16)
        for r in range(POOL_STATE - 1):
            nps[r] = stp[r + 1]
        for r in range(CONV_STATE - 1):
            ncs[r] = stc[r + 1]

    def pool_stage(c):
        zbuf = zb.at[0]
        for gl in range(MIX_CH // POOL_GROUP_DIM):
            g = c * (MIX_CH // POOL_GROUP_DIM) + gl
            w = POOL_WINDOWS[g]
            gcols = pl.ds(g * POOL_GROUP_DIM, POOL_GROUP_DIM)
            lcols = pl.ds(gl * POOL_GROUP_DIM, POOL_GROUP_DIM)
            pg = zbuf[prows, lcols]
            pad = jnp.zeros((SUBLANES, POOL_GROUP_DIM), F32)
            e0[pl.ds(SUBLANES, SUBLANES), :] = pad
            e1[pl.ds(SUBLANES, SUBLANES), :] = pad
            e0[pl.ds(CARRY_P, CARRY_P), :] = pcar[:, gcols]
            e0[pl.ds(2 * CARRY_P, tm), :] = pg
            src, dst = e0, e1
            d = 1
            while d < w:
                dst[pl.ds(CARRY_P, tm + CARRY_P), :] = (src[pl.ds(CARRY_P, tm + CARRY_P), :]
                                                        + src[pl.ds(CARRY_P - d, tm + CARRY_P), :])
                src, dst = dst, src
                d *= 2
            wsum = src[pl.ds(2 * CARRY_P, tm), :]
            pos = (i % tps) * tm + lax.broadcasted_iota(jnp.int32, (CARRY_P, 1), 0)
            cnt = jnp.minimum(pos + 1, w).astype(F32)
            abf[prows, :] = (wsum * (1.0 / w) - pg).astype(BF16)
            head = pl.ds(0, CARRY_P)
            abf[head, :] = (wsum[:CARRY_P, :] / cnt - pg[:CARRY_P, :]).astype(BF16)
            tail = pg[tm - CARRY_P:, :]
            pcar[:, gcols] = tail
            pst[:, gcols] = tail
            ps = zbuf[srows, lcols]
            ssum = ps
            for r in range(POOL_STATE + 1 - w, POOL_STATE):
                ssum = ssum + stp[r, :, gcols]
            abf[srows, :] = (ssum / float(min(PAST_LEN + 1, w)) - ps).astype(BF16)
            nps[POOL_STATE - 1, :, gcols] = ps
            ag = _bdot(abf[...], grp[g].astype(BF16)) * pscale[:, gcols]
            ap[g] = ag.astype(BF16)

    def conv_stage(c):
        ccols = pl.ds(c * MIX_CH, MIX_CH)
        zbuf, zprev = zb.at[0], zb.at[1]
        qp = zprev[prows, :] * zbuf[prows, :]
        qext[pl.ds(0, CARRY_Q), :] = qcar[:, ccols]
        qext[pl.ds(CARRY_Q, tm), :] = qp
        y = cbias[:, ccols] + qext[pl.ds(CARRY_Q - 2, tm), :] * cw[0:1, ccols]
        y = y + qext[pl.ds(CARRY_Q - 1, tm), :] * cw[1:2, ccols]
        tail = qp[tm - CARRY_Q:, :]
        qcar[:, ccols] = tail
        cst[:, ccols] = tail
        qs = zprev[srows, :] * zbuf[srows, :]
        ys = cbias[:, ccols] + stc[0, :, ccols] * cw[0:1, ccols]
        ys = ys + stc[1, :, ccols] * cw[1:2, ccols]
        ncs[CONV_STATE - 1, :, ccols] = qs
        ybuf[prows, :] = y + qp * cw[2:3, ccols]
        ybuf[srows, :] = ys + qs * cw[2:3, ccols]

    def main_stage(s):
        kind, chunk, _ = _decode_stage(s)
        slot = lax.rem(s, MIX_NB)
        wwait(slot)
        wstart(s + (MIX_NB - 1))

        keep = jnp.logical_or(kind == K_CC, kind == K_GA)

        @pl.when(keep)
        def _():
            zb[1] = _bdot(h2[...], wbuf[slot].astype(BF16))

        @pl.when(jnp.logical_and(kind < K_AB, jnp.logical_not(keep)))
        def _():
            sa[...] = _sigmoid(zb[1])
            zb[0] = _bdot(h2[...], wbuf[slot].astype(BF16))

        for c in range(POOL_WIDTH // MIX_CH):
            @pl.when(jnp.logical_and(kind == K_P, chunk == c))
            def _(c=c):
                pool_stage(c)

        for c in range(_N_CONV_CHUNKS):
            @pl.when(jnp.logical_and(kind == K_CH, chunk == c))
            def _(c=c):
                conv_stage(c)

        @pl.when(kind == K_CB)
        def _():
            v[chunk] = (zb[0] * ybuf[...]).astype(BF16)

        @pl.when(kind == K_AB)
        def _():
            sb = _sigmoid(zb[0])
            for hc in range(MIX_CH // MIX_HALF):
                hcols = pl.ds(hc * MIX_HALF, MIX_HALF)
                ua = None
                for g in range(len(POOL_WINDOWS)):
                    part = _bdot(ap[g], wbuf[slot, pl.ds(g * POOL_GROUP_DIM, POOL_GROUP_DIM), hcols].astype(BF16))
                    ua = part if ua is None else ua + part
                ub = None
                for k in range(_N_CONV_CHUNKS):
                    part = _bdot(v[k], wbuf[slot, pl.ds(half + k * MIX_CH, MIX_CH), hcols].astype(BF16))
                    ub = part if ub is None else ub + part
                sbh = sb[:, hc * MIX_HALF:(hc + 1) * MIX_HALF]
                m[chunk, :, hcols] = (sa[:, hcols] * ua + sbh * ub).astype(BF16)

    assert _S_OUT + MIX_NB - 2 < _N_STAGES

    @pl.when(oc == 0)
    def _():
        @pl.when(i == 0)
        def _():
            for t in range(MIX_NB - 1):
                wstart(jnp.int32(t))
            xprefetch(0)
            zb[...] = jnp.zeros(zb.shape, F32)

        @pl.when(i % tps == 0)
        def _():
            pcar[...] = jnp.zeros(pcar.shape, F32)
            qcar[...] = jnp.zeros(qcar.shape, F32)

        prologue()
        pl.loop(0, _S_OUT)(main_stage)

    s_out = _S_OUT + oc
    slot = lax.rem(s_out, MIX_NB)
    wwait(slot)
    nxt = s_out + (MIX_NB - 1)

    @pl.when(nxt < _N_STAGES)
    def _():
        wstart(nxt)

    @pl.when(jnp.logical_and(nxt >= _N_STAGES, i < n_tiles - 1))
    def _():
        wstart(nxt - _N_STAGES)

    @pl.when(jnp.logical_and(oc == 1, i < n_tiles - 1))
    def _():
        xprefetch(i + 1)

    for hc in range(MIX_CH // MIX_HALF):
        hcols = pl.ds(hc * MIX_HALF, MIX_HALF)
        acc = None
        for k in range(_N_OUT_CHUNKS):
            part = _bdot(m[k], wbuf[slot, pl.ds(k * MIX_CH, MIX_CH), hcols].astype(BF16))
            acc = part if acc is None else acc + part
        x2p[:, hcols] = x1pc[:, hcols] + g2pc[:, hcols] * acc[:tm]
        x2s[:, hcols] = x1sc[:, hcols] + g2sc[:, hcols] * acc[tm:]


def _mixer(x1p, x1s, mod_p3, mod_s, nrm, grp, pscale, cw, cbias, stp_t, stc_t, w_in, w_a, w_b, w_o, *, tm, ts, tps):
    n_tiles = x1p.shape[0] // tm
    n_seq = n_tiles // tps
    n_s = x1s.shape[0]
    assert n_s == n_tiles * ts
    rows = tm + ts
    any_spec = pl.BlockSpec(memory_space=pl.ANY)

    def const(shape):
        return pl.BlockSpec(shape, lambda i, s: (0,) * len(shape))

    def oc(s):
        return s

    g2_blk = (MOD_G2 * D_MODEL) // MIX_CH
    in_specs = [
        any_spec,
        pl.BlockSpec((ts, D_MODEL), lambda i, s: (i, 0)),
        pl.BlockSpec((tm, MIX_CH), lambda i, s: (i, oc(s))),
        pl.BlockSpec((ts, MIX_CH), lambda i, s: (i, oc(s))),
        pl.BlockSpec((None, 1, D_MODEL), lambda i, s: (i // tps, 0, MOD_SH2)),
        pl.BlockSpec((None, 1, D_MODEL), lambda i, s: (i // tps, 0, MOD_SC2)),
        pl.BlockSpec((None, 1, MIX_CH), lambda i, s: (i // tps, 0, g2_blk + oc(s))),
        pl.BlockSpec((ts, D_MODEL), lambda i, s: (i, MOD_SH2)),
        pl.BlockSpec((ts, D_MODEL), lambda i, s: (i, MOD_SC2)),
        pl.BlockSpec((ts, MIX_CH), lambda i, s: (i, g2_blk + oc(s))),
        const((1, D_MODEL)),
        const((len(POOL_WINDOWS), POOL_GROUP_DIM, POOL_GROUP_DIM)),
        const((1, POOL_WIDTH)),
        const((3, CONV_WIDTH)),
        const((1, CONV_WIDTH)),
        pl.BlockSpec((POOL_STATE, ts, POOL_WIDTH), lambda i, s: (0, i, 0)),
        pl.BlockSpec((CONV_STATE, ts, CONV_WIDTH), lambda i, s: (0, i, 0)),
        any_spec, any_spec, any_spec, any_spec,
    ]
    out_specs = [
        pl.BlockSpec((tm, MIX_CH), lambda i, s: (i, oc(s))),
        pl.BlockSpec((ts, MIX_CH), lambda i, s: (i, oc(s))),
        pl.BlockSpec((None, CARRY_P, POOL_WIDTH), lambda i, s: (i // tps, 0, 0)),
        pl.BlockSpec((None, CARRY_Q, CONV_WIDTH), lambda i, s: (i // tps, 0, 0)),
        pl.BlockSpec((POOL_STATE, ts, POOL_WIDTH), lambda i, s: (0, i, 0)),
        pl.BlockSpec((CONV_STATE, ts, CONV_WIDTH), lambda i, s: (0, i, 0)),
    ]
    out_shape = [
        jax.ShapeDtypeStruct(x1p.shape, F32),
        jax.ShapeDtypeStruct(x1s.shape, F32),
        jax.ShapeDtypeStruct((n_seq, CARRY_P, POOL_WIDTH), F32),
        jax.ShapeDtypeStruct((n_seq, CARRY_Q, CONV_WIDTH), F32),
        jax.ShapeDtypeStruct((POOL_STATE, n_s, POOL_WIDTH), F32),
        jax.ShapeDtypeStruct((CONV_STATE, n_s, CONV_WIDTH), F32),
    ]
    scratch = [
        pltpu.VMEM((MIX_NB, D_MODEL, MIX_CH), F32),
        pltpu.SemaphoreType.DMA((MIX_NB,)),
        pltpu.VMEM((X_BUFS, X_ROWS, D_MODEL), F32),
        pltpu.SemaphoreType.DMA((X_BUFS,)),
        pltpu.VMEM((rows, D_MODEL), BF16),
        pltpu.VMEM((len(POOL_WINDOWS), rows, POOL_GROUP_DIM), BF16),
        pltpu.VMEM((_N_CONV_CHUNKS, rows, MIX_CH), BF16),
        pltpu.VMEM((_N_OUT_CHUNKS, rows, MIX_CH), BF16),
        pltpu.VMEM((2, rows, MIX_CH), F32),
        pltpu.VMEM((rows, MIX_CH), F32),
        pltpu.VMEM((rows, MIX_CH), F32),
        pltpu.VMEM((rows, POOL_GROUP_DIM), BF16),
        pltpu.VMEM((tm + 2 * CARRY_P, POOL_GROUP_DIM), F32),
        pltpu.VMEM((tm + 2 * CARRY_P, POOL_GROUP_DIM), F32),
        pltpu.VMEM((tm + CARRY_Q, MIX_CH), F32),
        pltpu.VMEM((CARRY_P, POOL_WIDTH), F32),
        pltpu.VMEM((CARRY_Q, CONV_WIDTH), F32),
    ]
    return pl.pallas_call(
        functools.partial(_mixer_kernel, tm=tm, ts=ts, tps=tps, n_tiles=n_tiles),
        grid=(n_tiles, _N_OUT_CHUNKS),
        in_specs=in_specs,
        out_specs=out_specs,
        out_shape=out_shape,
        scratch_shapes=scratch,
        compiler_params=pltpu.CompilerParams(
            dimension_semantics=("arbitrary", "arbitrary"), vmem_limit_bytes=VMEM_LIMIT_BYTES),
        name="mixer",
    )(x1p, x1s, x1p, x1s, mod_p3, mod_p3, mod_p3, mod_s, mod_s, mod_s, nrm, grp, pscale, cw, cbias,
      stp_t, stc_t, w_in, w_a, w_b, w_o)


def kernel(x_prompt, x_sample, c_prompt, c_sample, state_pool, state_conv, w_ada, b_ada, norm1, ffn1_gate, ffn1_up, ffn1_down, norm2, w_in, pool_grp, pool_scale, w_branch_a, conv_w, conv_b, w_branch_b, w_o, norm3, ffn2_gate, ffn2_up, ffn2_down, norm_final):
    assert w_ada.shape[0] == 1, "single-layer step"
    batch, seq, d = x_prompt.shape
    dec_batch = x_sample.shape[0]
    assert d == D_MODEL and x_sample.shape[1] == 1

    tm = TM_PROMPT
    tps = seq // tm
    n_tiles = batch * tps
    ts = dec_batch // n_tiles
    assert seq % tm == 0 and dec_batch % n_tiles == 0 and ts % (2 * SUBLANES) == 0

    n_c = batch + dec_batch
    pad = (-n_c) % SUBLANES
    c_all = jnp.concatenate([c_sample, c_prompt, jnp.zeros((pad, d), F32)], axis=0)
    mod = _ada(c_all, w_ada[0], b_ada)
    mod_p3 = mod[dec_batch:n_c].reshape(batch, 1, N_MOD * d)

    row = lambda a: a.reshape(1, -1)
    xp = x_prompt.reshape(batch * seq, d)
    xs = x_sample.reshape(dec_batch, d)
    nf = row(norm_final)
    x1p, x1s = _ffn(xp, xs, mod_p3, mod, MOD_SH1, MOD_SC1, MOD_G1, row(norm1[0]), ffn1_gate[0], ffn1_up[0],
                    ffn1_down[0], nf, tm=tm, ts=ts, tps=tps, final_norm=False, name="ffn1")
    stp_t = jnp.transpose(state_pool[0], (1, 0, 2))
    stc_t = jnp.transpose(state_conv[0], (1, 0, 2))
    x2p, x2s, pst, cst, nps_t, ncs_t = _mixer(
        x1p, x1s, mod_p3, mod, row(norm2[0]), pool_grp[0], row(pool_scale[0]), conv_w[0], row(conv_b[0]),
        stp_t, stc_t, w_in[0], w_branch_a[0], w_branch_b[0], w_o[0], tm=tm, ts=ts, tps=tps)
    yp, ys = _ffn(x2p, x2s, mod_p3, mod, MOD_SH3, MOD_SC3, MOD_G3, row(norm3[0]), ffn2_gate[0], ffn2_up[0],
                  ffn2_down[0], nf, tm=tm, ts=ts, tps=tps, final_norm=True, name="ffn2")

    y_prompt = yp.reshape(batch, seq, d)
    y_sample = ys.reshape(dec_batch, 1, d)
    new_pool_prompt = pst[None, :, CARRY_P - POOL_STATE:, :]
    new_conv_prompt = cst[None, :, CARRY_Q - CONV_STATE:, :]
    new_pool_sample = jnp.transpose(nps_t, (1, 0, 2))[None]
    new_conv_sample = jnp.transpose(ncs_t, (1, 0, 2))[None]
    return (y_prompt, y_sample, new_pool_prompt, new_conv_prompt, new_pool_sample, new_conv_sample)
```

```python
import functools

import jax
import jax.numpy as jnp
from jax import lax
from jax.experimental import pallas as pl
from jax.experimental.pallas import tpu as pltpu

F32 = jnp.float32
BF16 = jnp.bfloat16

D_MODEL = 2048
D_FF = 5632
POOL_WIDTH = 1024
CONV_WIDTH = 1024
POOL_WINDOWS = (2, 4, 8, 16)
POOL_GROUP_DIM = 256
POOL_STATE = 15
CONV_STATE = 2
N_MOD = 9
EPS = 1e-6
PAST_LEN = 16384

V7X_VMEM_BYTES = 64 * 1024 * 1024
SUBLANES = 8
VMEM_LIMIT_BYTES = V7X_VMEM_BYTES - 2 * 1024 * 1024

TM_PROMPT = 1024

ADA_TN = 1024
FFN_TF = 256
FFN_DMA_SPLIT = 4
FFN_NB = 3
MIX_CH = 512
MIX_HALF = 256
MIX_NB = 3
X_ROWS = 128
FFN_ROWS = 128
X_BUFS = 4
MIX_DMA_SPLIT = 8
CARRY_P = 16
CARRY_Q = 8

COL_P, COL_CB, COL_CC, COL_CH, COL_GA, COL_GB = 0, 1024, 2048, 3072, 4096, 6144
MOD_SH1, MOD_SC1, MOD_G1, MOD_SH2, MOD_SC2, MOD_G2, MOD_SH3, MOD_SC3, MOD_G3 = range(9)


def _sigmoid(x):
    return 1.0 / (1.0 + jnp.exp(-x))


def _modulate(x, nrm, sc, sh):
    inv = lax.rsqrt(jnp.mean(x * x, axis=-1, keepdims=True) + EPS)
    return (x * inv) * nrm * (1.0 + sc) + sh


def _bdot(a, b):
    return jnp.dot(a, b, preferred_element_type=F32)


def _ada_kernel(c_ref, w_ref, b_ref, o_ref):
    c = c_ref[...]
    s = (c * _sigmoid(c)).astype(BF16)
    o_ref[...] = _bdot(s, w_ref[...].astype(BF16)) + b_ref[...]


def _ada(c_all, w_ada, b_ada):
    rows = c_all.shape[0]
    cols = w_ada.shape[1]
    return pl.pallas_call(
        _ada_kernel,
        grid=(cols // ADA_TN,),
        in_specs=[
            pl.BlockSpec((rows, D_MODEL), lambda j: (0, 0)),
            pl.BlockSpec((D_MODEL, ADA_TN), lambda j: (0, j)),
            pl.BlockSpec((1, ADA_TN), lambda j: (0, j)),
        ],
        out_specs=pl.BlockSpec((rows, ADA_TN), lambda j: (0, j)),
        out_shape=jax.ShapeDtypeStruct((rows, cols), F32),
        compiler_params=pltpu.CompilerParams(dimension_semantics=("arbitrary",)),
        name="ada_mod",
    )(c_all, w_ada, b_ada)


def _ffn_kernel(xp_ref, xs_ref, shp, scp, gtp, shs, scs, gts, nrm_ref, wg_hbm, wu_hbm, wd_hbm, nf_ref,
                op_ref, os_ref, h_ref, inv_ref, wgb, wub, wdb, wsem, *, n_f, n_tiles, final_norm):
    i = pl.program_id(0)
    tm = xp_ref.shape[0]
    ts = xs_ref.shape[0]

    def wcopies(f, slot):
        col = pl.multiple_of(f * FFN_TF, FFN_TF)
        rows_in = D_MODEL // FFN_DMA_SPLIT
        rows_dn = FFN_TF // FFN_DMA_SPLIT
        out = []
        for k in range(FFN_DMA_SPLIT):
            r_in = pl.ds(k * rows_in, rows_in)
            out.append(pltpu.make_async_copy(wg_hbm.at[r_in, pl.ds(col, FFN_TF)], wgb.at[slot, r_in, :],
                                             wsem.at[slot]))
            out.append(pltpu.make_async_copy(wu_hbm.at[r_in, pl.ds(col, FFN_TF)], wub.at[slot, r_in, :],
                                             wsem.at[slot]))
            out.append(pltpu.make_async_copy(wd_hbm.at[pl.ds(col + k * rows_dn, rows_dn), :],
                                             wdb.at[slot, pl.ds(k * rows_dn, rows_dn), :], wsem.at[slot]))
        return out

    def wstart(f, slot):
        for cp in wcopies(f, slot):
            cp.start()

    def wwait(f, slot):
        for cp in wcopies(f, slot):
            cp.wait()

    @pl.when(i == 0)
    def _():
        for t in range(FFN_NB - 1):
            wstart(jnp.int32(t), t)

    def head(x, sc, sh):
        return _modulate(x, nrm_ref[...], sc, sh).astype(BF16)

    def tail(x, gt, acc):
        xo = x + 0.5 * gt * acc
        if final_norm:
            inv = lax.rsqrt(jnp.mean(xo * xo, axis=-1, keepdims=True) + EPS)
            xo = (xo * inv) * nf_ref[...]
        return xo

    @pl.loop(0, tm // FFN_ROWS)
    def _(r):
        rows = pl.ds(pl.multiple_of(r * FFN_ROWS, FFN_ROWS), FFN_ROWS)
        x = xp_ref[rows, :]
        inv_ref[rows, :] = lax.rsqrt(jnp.mean(x * x, axis=-1, keepdims=True) + EPS)

    @pl.loop(0, tm // FFN_ROWS)
    def _(r):
        rows = pl.ds(pl.multiple_of(r * FFN_ROWS, FFN_ROWS), FFN_ROWS)
        hx = (xp_ref[rows, :] * inv_ref[rows, :]) * nrm_ref[...] * (1.0 + scp[...]) + shp[...]
        h_ref[rows, :] = hx.astype(BF16)
        op_ref[rows, :] = jnp.zeros((FFN_ROWS, D_MODEL), F32)

    h_ref[pl.ds(tm, ts), :] = head(xs_ref[...], scs[...], shs[...])
    os_ref[...] = jnp.zeros((ts, D_MODEL), F32)

    @pl.loop(0, n_f)
    def _(f):
        g_idx = i * n_f + f
        slot = lax.rem(g_idx, FFN_NB)
        wwait(f, slot)
        nxt = f + (FFN_NB - 1)
        nslot = lax.rem(g_idx + (FFN_NB - 1), FFN_NB)

        @pl.when(nxt < n_f)
        def _():
            wstart(nxt, nslot)

        @pl.when(jnp.logical_and(nxt >= n_f, i < n_tiles - 1))
        def _():
            wstart(nxt - n_f, nslot)

        h = h_ref[...]
        g = _bdot(h, wgb[slot].astype(BF16))
        u = _bdot(h, wub[slot].astype(BF16))
        a = ((g * _sigmoid(g)) * u).astype(BF16)
        for half in range(2):
            hcols = pl.ds(half * (D_MODEL // 2), D_MODEL // 2)
            y = _bdot(a, wdb[slot, :, hcols].astype(BF16))
            op_ref[:, hcols] += y[:tm]
            os_ref[:, hcols] += y[tm:]

    @pl.loop(0, tm // FFN_ROWS)
    def _(r):
        rows = pl.ds(pl.multiple_of(r * FFN_ROWS, FFN_ROWS), FFN_ROWS)
        op_ref[rows, :] = tail(xp_ref[rows, :], gtp[...], op_ref[rows, :])

    os_ref[...] = tail(xs_ref[...], gts[...], os_ref[...])


def _ffn(xp, xs, mod_p3, mod_s, j_sh, j_sc, j_g, nrm, wg, wu, wd, nf, *, tm, ts, tps, final_norm, name):
    n_tiles = xp.shape[0] // tm
    assert xs.shape[0] == n_tiles * ts
    n_f = D_FF // FFN_TF
    assert n_f >= FFN_NB
    any_spec = pl.BlockSpec(memory_space=pl.ANY)

    def pspec(j):
        return pl.BlockSpec((None, 1, D_MODEL), lambda i: (i // tps, 0, j))

    def sspec(j):
        return pl.BlockSpec((ts, D_MODEL), lambda i: (i, j))

    return pl.pallas_call(
        functools.partial(_ffn_kernel, n_f=n_f, n_tiles=n_tiles, final_norm=final_norm),
        grid=(n_tiles,),
        in_specs=[
            pl.BlockSpec((tm, D_MODEL), lambda i: (i, 0)),
            pl.BlockSpec((ts, D_MODEL), lambda i: (i, 0)),
            pspec(j_sh), pspec(j_sc), pspec(j_g),
            sspec(j_sh), sspec(j_sc), sspec(j_g),
            pl.BlockSpec((1, D_MODEL), lambda i: (0, 0)),
            any_spec, any_spec, any_spec,
            pl.BlockSpec((1, D_MODEL), lambda i: (0, 0)),
        ],
        out_specs=[pl.BlockSpec((tm, D_MODEL), lambda i: (i, 0)),
                   pl.BlockSpec((ts, D_MODEL), lambda i: (i, 0))],
        out_shape=[jax.ShapeDtypeStruct(xp.shape, F32), jax.ShapeDtypeStruct(xs.shape, F32)],
        scratch_shapes=[pltpu.VMEM((tm + ts, D_MODEL), BF16), pltpu.VMEM((tm, 1), F32),
                        pltpu.VMEM((FFN_NB, D_MODEL, FFN_TF), F32), pltpu.VMEM((FFN_NB, D_MODEL, FFN_TF), F32),
                        pltpu.VMEM((FFN_NB, FFN_TF, D_MODEL), F32), pltpu.SemaphoreType.DMA((FFN_NB,))],
        compiler_params=pltpu.CompilerParams(
            dimension_semantics=("arbitrary",), vmem_limit_bytes=VMEM_LIMIT_BYTES),
        name=name,
    )(xp, xs, mod_p3, mod_p3, mod_p3, mod_s, mod_s, mod_s, nrm, wg, wu, wd, nf)


K_P, K_CC, K_CH, K_CB, K_GA, K_GB, K_AB, K_O = range(8)
_N_CONV_CHUNKS = CONV_WIDTH // MIX_CH
_N_OUT_CHUNKS = D_MODEL // MIX_CH
_S_CONV = POOL_WIDTH // MIX_CH
_S_GATE = _S_CONV + 3 * _N_CONV_CHUNKS
_S_OUT = _S_GATE + 3 * _N_OUT_CHUNKS
_N_STAGES = _S_OUT + _N_OUT_CHUNKS
_KIND_COL0 = (COL_P, COL_CC, COL_CH, COL_CB, COL_GA, COL_GB, 0, 0)
assert _N_STAGES % MIX_NB == 0


def _decode_stage(t):
    u1 = t - _S_CONV
    u2 = t - _S_GATE
    c1 = lax.div(jnp.maximum(u1, 0), 3)
    c2 = lax.div(jnp.maximum(u2, 0), 3)
    kind = jnp.where(t < _S_CONV, K_P,
                     jnp.where(t < _S_GATE, K_CC + (u1 - 3 * c1),
                               jnp.where(t < _S_OUT, K_GA + (u2 - 3 * c2), K_O)))
    chunk = jnp.where(t < _S_CONV, t, jnp.where(t < _S_GATE, c1, jnp.where(t < _S_OUT, c2, t - _S_OUT)))
    col0 = jnp.int32(0)
    for k, base in enumerate(_KIND_COL0):
        if base:
            col0 = jnp.where(kind == k, base, col0)
    return kind, chunk, col0 + chunk * MIX_CH


def _mixer_kernel(x1p_hbm, x1s, x1pc, x1sc, shp, scp, g2pc, shs, scs, g2sc, nrm, grp, pscale, cw, cbias,
                  stp, stc, w_in, w_a, w_b, w_o,
                  x2p, x2s, pst, cst, nps, ncs,
                  wbuf, wsem, xbuf, xsem, h2, ap, v, m, zb, ybuf, sa, abf, e0, e1, qext, pcar, qcar,
                  *, tm, ts, tps, n_tiles):
    i = pl.program_id(0)
    oc = pl.program_id(1)
    slab = D_MODEL // MIX_DMA_SPLIT
    half = D_MODEL // 2
    prows = pl.ds(0, tm)
    srows = pl.ds(tm, ts)

    def wcopy(src, src_row, col, dst_slot, dst_row):
        return pltpu.make_async_copy(src.at[pl.ds(src_row, slab), pl.ds(col, MIX_CH)],
                                     wbuf.at[dst_slot, pl.ds(dst_row, slab), :], wsem.at[dst_slot])

    def wstart(t):
        k_t, _, col_t = _decode_stage(t)
        col_t = pl.multiple_of(col_t, MIX_CH)
        slot_t = lax.rem(t, MIX_NB)

        @pl.when(k_t < K_AB)
        def _():
            for k in range(MIX_DMA_SPLIT):
                wcopy(w_in, k * slab, col_t, slot_t, k * slab).start()

        @pl.when(k_t == K_AB)
        def _():
            for k in range(MIX_DMA_SPLIT):
                r0 = k * slab
                src, s0 = (w_a, r0) if r0 < half else (w_b, r0 - half)
                wcopy(src, s0, col_t, slot_t, r0).start()

        @pl.when(k_t == K_O)
        def _():
            for k in range(MIX_DMA_SPLIT):
                wcopy(w_o, k * slab, col_t, slot_t, k * slab).start()

    def wwait(slot):
        for k in range(MIX_DMA_SPLIT):
            wcopy(w_in, k * slab, 0, slot, k * slab).wait()

    n_xchunks = tm // X_ROWS

    def xcopy(tile, r):
        return pltpu.make_async_copy(x1p_hbm.at[pl.ds(tile * tm + r * X_ROWS, X_ROWS), :],
                                     xbuf.at[r % X_BUFS], xsem.at[r % X_BUFS])

    def xprefetch(tile):
        for r in range(min(X_BUFS, n_xchunks)):
            xcopy(tile, r).start()

    def prologue():
        for r in range(n_xchunks):
            xcopy(i, r).wait()
            h2[pl.ds(r * X_ROWS, X_ROWS), :] = _modulate(xbuf[r % X_BUFS], nrm[...], scp[...], shp[...]).astype(BF16)
            if r + X_BUFS < n_xchunks:
                xcopy(i, r + X_BUFS).start()
        h2[srows, :] = _modulate(x1s[...], nrm[...], scs[...], shs[...]).astype(BF16)
        for r in range(POOL_STATE - 1):
            nps[r] = stp[r + 1]
        for r in range(CONV_STATE - 1):
            ncs[r] = stc[r + 1]

    def pool_stage(c):
        zbuf = zb.at[0]
        for gl in range(MIX_CH // POOL_GROUP_DIM):
            g = c * (MIX_CH // POOL_GROUP_DIM) + gl
            w = POOL_WINDOWS[g]
            gcols = pl.ds(g * POOL_GROUP_DIM, POOL_GROUP_DIM)
            lcols = pl.ds(gl * POOL_GROUP_DIM, POOL_GROUP_DIM)
            pg = zbuf[prows, lcols]
            pad = jnp.zeros((SUBLANES, POOL_GROUP_DIM), F32)
            e0[pl.ds(SUBLANES, SUBLANES), :] = pad
            e1[pl.ds(SUBLANES, SUBLANES), :] = pad
            e0[pl.ds(CARRY_P, CARRY_P), :] = pcar[:, gcols]
            e0[pl.ds(2 * CARRY_P, tm), :] = pg
            src, dst = e0, e1
            d = 1
            while d < w:
                dst[pl.ds(CARRY_P, tm + CARRY_P), :] = (src[pl.ds(CARRY_P, tm + CARRY_P), :]
                                                        + src[pl.ds(CARRY_P - d, tm + CARRY_P), :])
                src, dst = dst, src
                d *= 2
            wsum = src[pl.ds(2 * CARRY_P, tm), :]
            pos = (i % tps) * tm + lax.broadcasted_iota(jnp.int32, (CARRY_P, 1), 0)
            cnt = jnp.minimum(pos + 1, w).astype(F32)
            abf[prows, :] = (wsum * (1.0 / w) - pg).astype(BF16)
            head = pl.ds(0, CARRY_P)
            abf[head, :] = (wsum[:CARRY_P, :] / cnt - pg[:CARRY_P, :]).astype(BF16)
            tail = pg[tm - CARRY_P:, :]
            pcar[:, gcols] = tail
            pst[:, gcols] = tail
            ps = zbuf[srows, lcols]
            ssum = ps
            for r in range(POOL_STATE + 1 - w, POOL_STATE):
                ssum = ssum + stp[r, :, gcols]
            abf[srows, :] = (ssum / float(min(PAST_LEN + 1, w)) - ps).astype(BF16)
            nps[POOL_STATE - 1, :, gcols] = ps
            ag = _bdot(abf[...], grp[g].astype(BF16)) * pscale[:, gcols]
            ap[g] = ag.astype(BF16)

    def conv_stage(c):
        ccols = pl.ds(c * MIX_CH, MIX_CH)
        zbuf, zprev = zb.at[0], zb.at[1]
        qp = zprev[prows, :] * zbuf[prows, :]
        qext[pl.ds(0, CARRY_Q), :] = qcar[:, ccols]
        qext[pl.ds(CARRY_Q, tm), :] = qp
        y = cbias[:, ccols] + qext[pl.ds(CARRY_Q - 2, tm), :] * cw[0:1, ccols]
        y = y + qext[pl.ds(CARRY_Q - 1, tm), :] * cw[1:2, ccols]
        tail = qp[tm - CARRY_Q:, :]
        qcar[:, ccols] = tail
        cst[:, ccols] = tail
        qs = zprev[srows, :] * zbuf[srows, :]
        ys = cbias[:, ccols] + stc[0, :, ccols] * cw[0:1, ccols]
        ys = ys + stc[1, :, ccols] * cw[1:2, ccols]
        ncs[CONV_STATE - 1, :, ccols] = qs
        ybuf[prows, :] = y + qp * cw[2:3, ccols]
        ybuf[srows, :] = ys + qs * cw[2:3, ccols]

    def main_stage(s):
        kind, chunk, _ = _decode_stage(s)
        slot = lax.rem(s, MIX_NB)
        wwait(slot)
        wstart(s + (MIX_NB - 1))

        keep = jnp.logical_or(kind == K_CC, kind == K_GA)

        @pl.when(keep)
        def _():
            zb[1] = _bdot(h2[...], wbuf[slot].astype(BF16))

        @pl.when(kind == K_GB)
        def _():
            sa[...] = _sigmoid(zb[1])
            zb[0] = _bdot(h2[...], wbuf[slot].astype(BF16))

        @pl.when(jnp.logical_and(kind < K_GA, jnp.logical_not(keep)))
        def _():
            zb[0] = _bdot(h2[...], wbuf[slot].astype(BF16))

        for c in range(POOL_WIDTH // MIX_CH):
            @pl.when(jnp.logical_and(kind == K_P, chunk == c))
            def _(c=c):
                pool_stage(c)

        for c in range(_N_CONV_CHUNKS):
            @pl.when(jnp.logical_and(kind == K_CH, chunk == c))
            def _(c=c):
                conv_stage(c)

        @pl.when(kind == K_CB)
        def _():
            v[chunk] = (zb[0] * ybuf[...]).astype(BF16)

        @pl.when(kind == K_AB)
        def _():
            sb = _sigmoid(zb[0])
            for hc in range(MIX_CH // MIX_HALF):
                hcols = pl.ds(hc * MIX_HALF, MIX_HALF)
                ua = None
                for g in range(len(POOL_WINDOWS)):
                    part = _bdot(ap[g], wbuf[slot, pl.ds(g * POOL_GROUP_DIM, POOL_GROUP_DIM), hcols].astype(BF16))
                    ua = part if ua is None else ua + part
                ub = None
                for k in range(_N_CONV_CHUNKS):
                    part = _bdot(v[k], wbuf[slot, pl.ds(half + k * MIX_CH, MIX_CH), hcols].astype(BF16))
                    ub = part if ub is None else ub + part
                sbh = sb[:, hc * MIX_HALF:(hc + 1) * MIX_HALF]
                m[chunk, :, hcols] = (sa[:, hcols] * ua + sbh * ub).astype(BF16)

    assert _S_OUT + MIX_NB - 2 < _N_STAGES

    @pl.when(oc == 0)
    def _():
        @pl.when(i == 0)
        def _():
            for t in range(MIX_NB - 1):
                wstart(jnp.int32(t))
            xprefetch(0)
            zb[...] = jnp.zeros(zb.shape, F32)

        @pl.when(i % tps == 0)
        def _():
            pcar[...] = jnp.zeros(pcar.shape, F32)
            qcar[...] = jnp.zeros(qcar.shape, F32)

        prologue()
        pl.loop(0, _S_OUT)(main_stage)

    s_out = _S_OUT + oc
    slot = lax.rem(s_out, MIX_NB)
    wwait(slot)
    nxt = s_out + (MIX_NB - 1)

    @pl.when(nxt < _N_STAGES)
    def _():
        wstart(nxt)

    @pl.when(jnp.logical_and(nxt >= _N_STAGES, i < n_tiles - 1))
    def _():
        wstart(nxt - _N_STAGES)

    @pl.when(jnp.logical_and(oc == 1, i < n_tiles - 1))
    def _():
        xprefetch(i + 1)

    for hc in range(MIX_CH // MIX_HALF):
        hcols = pl.ds(hc * MIX_HALF, MIX_HALF)
        acc = None
        for k in range(_N_OUT_CHUNKS):
            part = _bdot(m[k], wbuf[slot, pl.ds(k * MIX_CH, MIX_CH), hcols].astype(BF16))
            acc = part if acc is None else acc + part
        x2p[:, hcols] = x1pc[:, hcols] + g2pc[:, hcols] * acc[:tm]
        x2s[:, hcols] = x1sc[:, hcols] + g2sc[:, hcols] * acc[tm:]


def _mixer(x1p, x1s, mod_p3, mod_s, nrm, grp, pscale, cw, cbias, stp_t, stc_t, w_in, w_a, w_b, w_o, *, tm, ts, tps):
    n_tiles = x1p.shape[0] // tm
    n_seq = n_tiles // tps
    n_s = x1s.shape[0]
    assert n_s == n_tiles * ts
    rows = tm + ts
    any_spec = pl.BlockSpec(memory_space=pl.ANY)

    def const(shape):
        return pl.BlockSpec(shape, lambda i, s: (0,) * len(shape))

    def oc(s):
        return s

    g2_blk = (MOD_G2 * D_MODEL) // MIX_CH
    in_specs = [
        any_spec,
        pl.BlockSpec((ts, D_MODEL), lambda i, s: (i, 0)),
        pl.BlockSpec((tm, MIX_CH), lambda i, s: (i, oc(s))),
        pl.BlockSpec((ts, MIX_CH), lambda i, s: (i, oc(s))),
        pl.BlockSpec((None, 1, D_MODEL), lambda i, s: (i // tps, 0, MOD_SH2)),
        pl.BlockSpec((None, 1, D_MODEL), lambda i, s: (i // tps, 0, MOD_SC2)),
        pl.BlockSpec((None, 1, MIX_CH), lambda i, s: (i // tps, 0, g2_blk + oc(s))),
        pl.BlockSpec((ts, D_MODEL), lambda i, s: (i, MOD_SH2)),
        pl.BlockSpec((ts, D_MODEL), lambda i, s: (i, MOD_SC2)),
        pl.BlockSpec((ts, MIX_CH), lambda i, s: (i, g2_blk + oc(s))),
        const((1, D_MODEL)),
        const((len(POOL_WINDOWS), POOL_GROUP_DIM, POOL_GROUP_DIM)),
        const((1, POOL_WIDTH)),
        const((3, CONV_WIDTH)),
        const((1, CONV_WIDTH)),
        pl.BlockSpec((POOL_STATE, ts, POOL_WIDTH), lambda i, s: (0, i, 0)),
        pl.BlockSpec((CONV_STATE, ts, CONV_WIDTH), lambda i, s: (0, i, 0)),
        any_spec, any_spec, any_spec, any_spec,
    ]
    out_specs = [
        pl.BlockSpec((tm, MIX_CH), lambda i, s: (i, oc(s))),
        pl.BlockSpec((ts, MIX_CH), lambda i, s: (i, oc(s))),
        pl.BlockSpec((None, CARRY_P, POOL_WIDTH), lambda i, s: (i // tps, 0, 0)),
        pl.BlockSpec((None, CARRY_Q, CONV_WIDTH), lambda i, s: (i // tps, 0, 0)),
        pl.BlockSpec((POOL_STATE, ts, POOL_WIDTH), lambda i, s: (0, i, 0)),
        pl.BlockSpec((CONV_STATE, ts, CONV_WIDTH), lambda i, s: (0, i, 0)),
    ]
    out_shape = [
        jax.ShapeDtypeStruct(x1p.shape, F32),
        jax.ShapeDtypeStruct(x1s.shape, F32),
        jax.ShapeDtypeStruct((n_seq, CARRY_P, POOL_WIDTH), F32),
        jax.ShapeDtypeStruct((n_seq, CARRY_Q, CONV_WIDTH), F32),
        jax.ShapeDtypeStruct((POOL_STATE, n_s, POOL_WIDTH), F32),
        jax.ShapeDtypeStruct((CONV_STATE, n_s, CONV_WIDTH), F32),
    ]
    scratch = [
        pltpu.VMEM((MIX_NB, D_MODEL, MIX_CH), F32),
        pltpu.SemaphoreType.DMA((MIX_NB,)),
        pltpu.VMEM((X_BUFS, X_ROWS, D_MODEL), F32),
        pltpu.SemaphoreType.DMA((X_BUFS,)),
        pltpu.VMEM((rows, D_MODEL), BF16),
        pltpu.VMEM((len(POOL_WINDOWS), rows, POOL_GROUP_DIM), BF16),
        pltpu.VMEM((_N_CONV_CHUNKS, rows, MIX_CH), BF16),
        pltpu.VMEM((_N_OUT_CHUNKS, rows, MIX_CH), BF16),
        pltpu.VMEM((2, rows, MIX_CH), F32),
        pltpu.VMEM((rows, MIX_CH), F32),
        pltpu.VMEM((rows, MIX_CH), F32),
        pltpu.VMEM((rows, POOL_GROUP_DIM), BF16),
        pltpu.VMEM((tm + 2 * CARRY_P, POOL_GROUP_DIM), F32),
        pltpu.VMEM((tm + 2 * CARRY_P, POOL_GROUP_DIM), F32),
        pltpu.VMEM((tm + CARRY_Q, MIX_CH), F32),
        pltpu.VMEM((CARRY_P, POOL_WIDTH), F32),
        pltpu.VMEM((CARRY_Q, CONV_WIDTH), F32),
    ]
    return pl.pallas_call(
        functools.partial(_mixer_kernel, tm=tm, ts=ts, tps=tps, n_tiles=n_tiles),
        grid=(n_tiles, _N_OUT_CHUNKS),
        in_specs=in_specs,
        out_specs=out_specs,
        out_shape=out_shape,
        scratch_shapes=scratch,
        compiler_params=pltpu.CompilerParams(
            dimension_semantics=("arbitrary", "arbitrary"), vmem_limit_bytes=VMEM_LIMIT_BYTES),
        name="mixer",
    )(x1p, x1s, x1p, x1s, mod_p3, mod_p3, mod_p3, mod_s, mod_s, mod_s, nrm, grp, pscale, cw, cbias,
      stp_t, stc_t, w_in, w_a, w_b, w_o)


def kernel(x_prompt, x_sample, c_prompt, c_sample, state_pool, state_conv, w_ada, b_ada, norm1, ffn1_gate, ffn1_up, ffn1_down, norm2, w_in, pool_grp, pool_scale, w_branch_a, conv_w, conv_b, w_branch_b, w_o, norm3, ffn2_gate, ffn2_up, ffn2_down, norm_final):
    assert w_ada.shape[0] == 1, "single-layer step"
    batch, seq, d = x_prompt.shape
    dec_batch = x_sample.shape[0]
    assert d == D_MODEL and x_sample.shape[1] == 1

    tm = TM_PROMPT
    tps = seq // tm
    n_tiles = batch * tps
    ts = dec_batch // n_tiles
    assert seq % tm == 0 and dec_batch % n_tiles == 0 and ts % (2 * SUBLANES) == 0

    n_c = batch + dec_batch
    pad = (-n_c) % SUBLANES
    c_all = jnp.concatenate([c_sample, c_prompt, jnp.zeros((pad, d), F32)], axis=0)
    mod = _ada(c_all, w_ada[0], b_ada)
    mod_p3 = mod[dec_batch:n_c].reshape(batch, 1, N_MOD * d)

    row = lambda a: a.reshape(1, -1)
    xp = x_prompt.reshape(batch * seq, d)
    xs = x_sample.reshape(dec_batch, d)
    nf = row(norm_final)
    x1p, x1s = _ffn(xp, xs, mod_p3, mod, MOD_SH1, MOD_SC1, MOD_G1, row(norm1[0]), ffn1_gate[0], ffn1_up[0],
                    ffn1_down[0], nf, tm=tm, ts=ts, tps=tps, final_norm=False, name="ffn1")
    stp_t = jnp.transpose(state_pool[0], (1, 0, 2))
    stc_t = jnp.transpose(state_conv[0], (1, 0, 2))
    x2p, x2s, pst, cst, nps_t, ncs_t = _mixer(
        x1p, x1s, mod_p3, mod, row(norm2[0]), pool_grp[0], row(pool_scale[0]), conv_w[0], row(conv_b[0]),
        stp_t, stc_t, w_in[0], w_branch_a[0], w_branch_b[0], w_o[0], tm=tm, ts=ts, tps=tps)
    yp, ys = _ffn(x2p, x2s, mod_p3, mod, MOD_SH3, MOD_SC3, MOD_G3, row(norm3[0]), ffn2_gate[0], ffn2_up[0],
                  ffn2_down[0], nf, tm=tm, ts=ts, tps=tps, final_norm=True, name="ffn2")

    y_prompt = yp.reshape(batch, seq, d)
    y_sample = ys.reshape(dec_batch, 1, d)
    new_pool_prompt = pst[None, :, CARRY_P - POOL_STATE:, :]
    new_conv_prompt = cst[None, :, CARRY_Q - CONV_STATE:, :]
    new_pool_sample = jnp.transpose(nps_t, (1, 0, 2))[None]
    new_conv_sample = jnp.transpose(ncs_t, (1, 0, 2))[None]
    return (y_prompt, y_sample, new_pool_prompt, new_conv_prompt, new_pool_sample, new_conv_sample)
```

```python
import functools

import jax
import jax.numpy as jnp
from jax import lax
from jax.experimental import pallas as pl
from jax.experimental.pallas import tpu as pltpu

F32 = jnp.float32
BF16 = jnp.bfloat16

D_MODEL = 2048
D_FF = 5632
POOL_WIDTH = 1024
CONV_WIDTH = 1024
POOL_WINDOWS = (2, 4, 8, 16)
POOL_GROUP_DIM = 256
POOL_STATE = 15
CONV_STATE = 2
N_MOD = 9
EPS = 1e-6
PAST_LEN = 16384

V7X_VMEM_BYTES = 64 * 1024 * 1024
SUBLANES = 8
VMEM_LIMIT_BYTES = V7X_VMEM_BYTES - 2 * 1024 * 1024

TM_PROMPT = 1024

ADA_TN = 1024
FFN_TF = 256
FFN_DMA_SPLIT = 1
FFN_NB = 3
MIX_CH = 512
MIX_HALF = 256
MIX_NB = 3
X_ROWS = 128
FFN_ROWS = 128
X_BUFS = 4
MIX_DMA_SPLIT = 8
CARRY_P = 16
CARRY_Q = 8

COL_P, COL_CB, COL_CC, COL_CH, COL_GA, COL_GB = 0, 1024, 2048, 3072, 4096, 6144
MOD_SH1, MOD_SC1, MOD_G1, MOD_SH2, MOD_SC2, MOD_G2, MOD_SH3, MOD_SC3, MOD_G3 = range(9)


def _sigmoid(x):
    return 1.0 / (1.0 + jnp.exp(-x))


def _modulate(x, nrm, sc, sh):
    inv = lax.rsqrt(jnp.mean(x * x, axis=-1, keepdims=True) + EPS)
    return (x * inv) * nrm * (1.0 + sc) + sh


def _bdot(a, b):
    return jnp.dot(a, b, preferred_element_type=F32)


def _ada_kernel(c_ref, w_ref, b_ref, o_ref):
    c = c_ref[...]
    s = (c * _sigmoid(c)).astype(BF16)
    o_ref[...] = _bdot(s, w_ref[...].astype(BF16)) + b_ref[...]


def _ada(c_all, w_ada, b_ada):
    rows = c_all.shape[0]
    cols = w_ada.shape[1]
    return pl.pallas_call(
        _ada_kernel,
        grid=(cols // ADA_TN,),
        in_specs=[
            pl.BlockSpec((rows, D_MODEL), lambda j: (0, 0)),
            pl.BlockSpec((D_MODEL, ADA_TN), lambda j: (0, j)),
            pl.BlockSpec((1, ADA_TN), lambda j: (0, j)),
        ],
        out_specs=pl.BlockSpec((rows, ADA_TN), lambda j: (0, j)),
        out_shape=jax.ShapeDtypeStruct((rows, cols), F32),
        compiler_params=pltpu.CompilerParams(dimension_semantics=("arbitrary",)),
        name="ada_mod",
    )(c_all, w_ada, b_ada)


def _ffn_kernel(xp_ref, xs_ref, shp, scp, gtp, shs, scs, gts, nrm_ref, wg_hbm, wu_hbm, wd_hbm, nf_ref,
                op_ref, os_ref, h_ref, inv_ref, wgb, wub, wdb, wsem, *, n_f, n_tiles, final_norm):
    i = pl.program_id(0)
    tm = xp_ref.shape[0]
    ts = xs_ref.shape[0]

    def wcopies(f, slot):
        col = pl.multiple_of(f * FFN_TF, FFN_TF)
        rows_in = D_MODEL // FFN_DMA_SPLIT
        rows_dn = FFN_TF // FFN_DMA_SPLIT
        out = []
        for k in range(FFN_DMA_SPLIT):
            r_in = pl.ds(k * rows_in, rows_in)
            out.append(pltpu.make_async_copy(wg_hbm.at[r_in, pl.ds(col, FFN_TF)], wgb.at[slot, r_in, :],
                                             wsem.at[slot]))
            out.append(pltpu.make_async_copy(wu_hbm.at[r_in, pl.ds(col, FFN_TF)], wub.at[slot, r_in, :],
                                             wsem.at[slot]))
            out.append(pltpu.make_async_copy(wd_hbm.at[pl.ds(col + k * rows_dn, rows_dn), :],
                                             wdb.at[slot, pl.ds(k * rows_dn, rows_dn), :], wsem.at[slot]))
        return out

    def wstart(f, slot):
        for cp in wcopies(f, slot):
            cp.start()

    def wwait(f, slot):
        for cp in wcopies(f, slot):
            cp.wait()

    @pl.when(i == 0)
    def _():
        for t in range(FFN_NB - 1):
            wstart(jnp.int32(t), t)

    def head(x, sc, sh):
        return _modulate(x, nrm_ref[...], sc, sh).astype(BF16)

    def tail(x, gt, acc):
        xo = x + 0.5 * gt * acc
        if final_norm:
            inv = lax.rsqrt(jnp.mean(xo * xo, axis=-1, keepdims=True) + EPS)
            xo = (xo * inv) * nf_ref[...]
        return xo

    @pl.loop(0, tm // FFN_ROWS)
    def _(r):
        rows = pl.ds(pl.multiple_of(r * FFN_ROWS, FFN_ROWS), FFN_ROWS)
        x = xp_ref[rows, :]
        inv_ref[rows, :] = lax.rsqrt(jnp.mean(x * x, axis=-1, keepdims=True) + EPS)

    @pl.loop(0, tm // FFN_ROWS)
    def _(r):
        rows = pl.ds(pl.multiple_of(r * FFN_ROWS, FFN_ROWS), FFN_ROWS)
        hx = (xp_ref[rows, :] * inv_ref[rows, :]) * nrm_ref[...] * (1.0 + scp[...]) + shp[...]
        h_ref[rows, :] = hx.astype(BF16)
        op_ref[rows, :] = jnp.zeros((FFN_ROWS, D_MODEL), F32)

    h_ref[pl.ds(tm, ts), :] = head(xs_ref[...], scs[...], shs[...])
    os_ref[...] = jnp.zeros((ts, D_MODEL), F32)

    @pl.loop(0, n_f)
    def _(f):
        g_idx = i * n_f + f
        slot = lax.rem(g_idx, FFN_NB)
        wwait(f, slot)
        nxt = f + (FFN_NB - 1)
        nslot = lax.rem(g_idx + (FFN_NB - 1), FFN_NB)

        @pl.when(nxt < n_f)
        def _():
            wstart(nxt, nslot)

        @pl.when(jnp.logical_and(nxt >= n_f, i < n_tiles - 1))
        def _():
            wstart(nxt - n_f, nslot)

        h = h_ref[...]
        g = _bdot(h, wgb[slot].astype(BF16))
        u = _bdot(h, wub[slot].astype(BF16))
        a = ((g * _sigmoid(g)) * u).astype(BF16)
        for half in range(2):
            hcols = pl.ds(half * (D_MODEL // 2), D_MODEL // 2)
            y = _bdot(a, wdb[slot, :, hcols].astype(BF16))
            op_ref[:, hcols] += y[:tm]
            os_ref[:, hcols] += y[tm:]

    @pl.loop(0, tm // FFN_ROWS)
    def _(r):
        rows = pl.ds(pl.multiple_of(r * FFN_ROWS, FFN_ROWS), FFN_ROWS)
        op_ref[rows, :] = tail(xp_ref[rows, :], gtp[...], op_ref[rows, :])

    os_ref[...] = tail(xs_ref[...], gts[...], os_ref[...])


def _ffn(xp, xs, mod_p3, mod_s, j_sh, j_sc, j_g, nrm, wg, wu, wd, nf, *, tm, ts, tps, final_norm, name):
    n_tiles = xp.shape[0] // tm
    assert xs.shape[0] == n_tiles * ts
    n_f = D_FF // FFN_TF
    assert n_f >= FFN_NB
    any_spec = pl.BlockSpec(memory_space=pl.ANY)

    def pspec(j):
        return pl.BlockSpec((None, 1, D_MODEL), lambda i: (i // tps, 0, j))

    def sspec(j):
        return pl.BlockSpec((ts, D_MODEL), lambda i: (i, j))

    return pl.pallas_call(
        functools.partial(_ffn_kernel, n_f=n_f, n_tiles=n_tiles, final_norm=final_norm),
        grid=(n_tiles,),
        in_specs=[
            pl.BlockSpec((tm, D_MODEL), lambda i: (i, 0)),
            pl.BlockSpec((ts, D_MODEL), lambda i: (i, 0)),
            pspec(j_sh), pspec(j_sc), pspec(j_g),
            sspec(j_sh), sspec(j_sc), sspec(j_g),
            pl.BlockSpec((1, D_MODEL), lambda i: (0, 0)),
            any_spec, any_spec, any_spec,
            pl.BlockSpec((1, D_MODEL), lambda i: (0, 0)),
        ],
        out_specs=[pl.BlockSpec((tm, D_MODEL), lambda i: (i, 0)),
                   pl.BlockSpec((ts, D_MODEL), lambda i: (i, 0))],
        out_shape=[jax.ShapeDtypeStruct(xp.shape, F32), jax.ShapeDtypeStruct(xs.shape, F32)],
        scratch_shapes=[pltpu.VMEM((tm + ts, D_MODEL), BF16), pltpu.VMEM((tm, 1), F32),
                        pltpu.VMEM((FFN_NB, D_MODEL, FFN_TF), F32), pltpu.VMEM((FFN_NB, D_MODEL, FFN_TF), F32),
                        pltpu.VMEM((FFN_NB, FFN_TF, D_MODEL), F32), pltpu.SemaphoreType.DMA((FFN_NB,))],
        compiler_params=pltpu.CompilerParams(
            dimension_semantics=("arbitrary",), vmem_limit_bytes=VMEM_LIMIT_BYTES),
        name=name,
    )(xp, xs, mod_p3, mod_p3, mod_p3, mod_s, mod_s, mod_s, nrm, wg, wu, wd, nf)


K_P, K_CC, K_CH, K_CB, K_GA, K_GB, K_AB, K_O = range(8)
_N_CONV_CHUNKS = CONV_WIDTH // MIX_CH
_N_OUT_CHUNKS = D_MODEL // MIX_CH
_S_CONV = POOL_WIDTH // MIX_CH
_S_GATE = _S_CONV + 3 * _N_CONV_CHUNKS
_S_OUT = _S_GATE + 3 * _N_OUT_CHUNKS
_N_STAGES = _S_OUT + _N_OUT_CHUNKS
_KIND_COL0 = (COL_P, COL_CC, COL_CH, COL_CB, COL_GA, COL_GB, 0, 0)
assert _N_STAGES % MIX_NB == 0


def _decode_stage(t):
    u1 = t - _S_CONV
    u2 = t - _S_GATE
    c1 = lax.div(jnp.maximum(u1, 0), 3)
    c2 = lax.div(jnp.maximum(u2, 0), 3)
    kind = jnp.where(t < _S_CONV, K_P,
                     jnp.where(t < _S_GATE, K_CC + (u1 - 3 * c1),
                               jnp.where(t < _S_OUT, K_GA + (u2 - 3 * c2), K_O)))
    chunk = jnp.where(t < _S_CONV, t, jnp.where(t < _S_GATE, c1, jnp.where(t < _S_OUT, c2, t - _S_OUT)))
    col0 = jnp.int32(0)
    for k, base in enumerate(_KIND_COL0):
        if base:
            col0 = jnp.where(kind == k, base, col0)
    return kind, chunk, col0 + chunk * MIX_CH


def _mixer_kernel(x1p_hbm, x1s, x1pc, x1sc, shp, scp, g2pc, shs, scs, g2sc, nrm, grp, pscale, cw, cbias,
                  stp, stc, w_in, w_a, w_b, w_o,
                  x2p, x2s, pst, cst, nps, ncs,
                  wbuf, wsem, xbuf, xsem, h2, ap, v, m, zb, ybuf, sa, abf, e0, e1, qext, pcar, qcar,
                  *, tm, ts, tps, n_tiles):
    i = pl.program_id(0)
    oc = pl.program_id(1)
    slab = D_MODEL // MIX_DMA_SPLIT
    half = D_MODEL // 2
    prows = pl.ds(0, tm)
    srows = pl.ds(tm, ts)

    def wcopy(src, src_row, col, dst_slot, dst_row):
        return pltpu.make_async_copy(src.at[pl.ds(src_row, slab), pl.ds(col, MIX_CH)],
                                     wbuf.at[dst_slot, pl.ds(dst_row, slab), :], wsem.at[dst_slot])

    def wstart(t):
        k_t, _, col_t = _decode_stage(t)
        col_t = pl.multiple_of(col_t, MIX_CH)
        slot_t = lax.rem(t, MIX_NB)

        @pl.when(k_t < K_AB)
        def _():
            for k in range(MIX_DMA_SPLIT):
                wcopy(w_in, k * slab, col_t, slot_t, k * slab).start()

        @pl.when(k_t == K_AB)
        def _():
            for k in range(MIX_DMA_SPLIT):
                r0 = k * slab
                src, s0 = (w_a, r0) if r0 < half else (w_b, r0 - half)
                wcopy(src, s0, col_t, slot_t, r0).start()

        @pl.when(k_t == K_O)
        def _():
            for k in range(MIX_DMA_SPLIT):
                wcopy(w_o, k * slab, col_t, slot_t, k * slab).start()

    def wwait(slot):
        for k in range(MIX_DMA_SPLIT):
            wcopy(w_in, k * slab, 0, slot, k * slab).wait()

    n_xchunks = tm // X_ROWS

    def xcopy(tile, r):
        return pltpu.make_async_copy(x1p_hbm.at[pl.ds(tile * tm + r * X_ROWS, X_ROWS), :],
                                     xbuf.at[r % X_BUFS], xsem.at[r % X_BUFS])

    def xprefetch(tile):
        for r in range(min(X_BUFS, n_xchunks)):
            xcopy(tile, r).start()

    def prologue():
        for r in range(n_xchunks):
            xcopy(i, r).wait()
            h2[pl.ds(r * X_ROWS, X_ROWS), :] = _modulate(xbuf[r % X_BUFS], nrm[...], scp[...], shp[...]).astype(BF16)
            if r + X_BUFS < n_xchunks:
                xcopy(i, r + X_BUFS).start()
        h2[srows, :] = _modulate(x1s[...], nrm[...], scs[...], shs[...]).astype(BF16)
        for r in range(POOL_STATE - 1):
            nps[r] = stp[r + 1]
        for r in range(CONV_STATE - 1):
            ncs[r] = stc[r + 1]

    def pool_stage(c):
        zbuf = zb.at[0]
        for gl in range(MIX_CH // POOL_GROUP_DIM):
            g = c * (MIX_CH // POOL_GROUP_DIM) + gl
            w = POOL_WINDOWS[g]
            gcols = pl.ds(g * POOL_GROUP_DIM, POOL_GROUP_DIM)
            lcols = pl.ds(gl * POOL_GROUP_DIM, POOL_GROUP_DIM)
            pg = zbuf[prows, lcols]
            pad = jnp.zeros((SUBLANES, POOL_GROUP_DIM), F32)
            e0[pl.ds(SUBLANES, SUBLANES), :] = pad
            e1[pl.ds(SUBLANES, SUBLANES), :] = pad
            e0[pl.ds(CARRY_P, CARRY_P), :] = pcar[:, gcols]
            e0[pl.ds(2 * CARRY_P, tm), :] = pg
            src, dst = e0, e1
            d = 1
            while d < w:
                dst[pl.ds(CARRY_P, tm + CARRY_P), :] = (src[pl.ds(CARRY_P, tm + CARRY_P), :]
                                                        + src[pl.ds(CARRY_P - d, tm + CARRY_P), :])
                src, dst = dst, src
                d *= 2
            wsum = src[pl.ds(2 * CARRY_P, tm), :]
            pos = (i % tps) * tm + lax.broadcasted_iota(jnp.int32, (CARRY_P, 1), 0)
            cnt = jnp.minimum(pos + 1, w).astype(F32)
            abf[prows, :] = (wsum * (1.0 / w) - pg).astype(BF16)
            head = pl.ds(0, CARRY_P)
            abf[head, :] = (wsum[:CARRY_P, :] / cnt - pg[:CARRY_P, :]).astype(BF16)
            tail = pg[tm - CARRY_P:, :]
            pcar[:, gcols] = tail
            pst[:, gcols] = tail
            ps = zbuf[srows, lcols]
            ssum = ps
            for r in range(POOL_STATE + 1 - w, POOL_STATE):
                ssum = ssum + stp[r, :, gcols]
            abf[srows, :] = (ssum / float(min(PAST_LEN + 1, w)) - ps).astype(BF16)
            nps[POOL_STATE - 1, :, gcols] = ps
            ag = _bdot(abf[...], grp[g].astype(BF16)) * pscale[:, gcols]
            ap[g] = ag.astype(BF16)

    def conv_stage(c):
        ccols = pl.ds(c * MIX_CH, MIX_CH)
        zbuf, zprev = zb.at[0], zb.at[1]
        qp = zprev[prows, :] * zbuf[prows, :]
        qext[pl.ds(0, CARRY_Q), :] = qcar[:, ccols]
        qext[pl.ds(CARRY_Q, tm), :] = qp
        y = cbias[:, ccols] + qext[pl.ds(CARRY_Q - 2, tm), :] * cw[0:1, ccols]
        y = y + qext[pl.ds(CARRY_Q - 1, tm), :] * cw[1:2, ccols]
        tail = qp[tm - CARRY_Q:, :]
        qcar[:, ccols] = tail
        cst[:, ccols] = tail
        qs = zprev[srows, :] * zbuf[srows, :]
        ys = cbias[:, ccols] + stc[0, :, ccols] * cw[0:1, ccols]
        ys = ys + stc[1, :, ccols] * cw[1:2, ccols]
        ncs[CONV_STATE - 1, :, ccols] = qs
        ybuf[prows, :] = y + qp * cw[2:3, ccols]
        ybuf[srows, :] = ys + qs * cw[2:3, ccols]

    def main_stage(s):
        kind, chunk, _ = _decode_stage(s)
        slot = lax.rem(s, MIX_NB)
        wwait(slot)
        wstart(s + (MIX_NB - 1))

        keep = jnp.logical_or(kind == K_CC, kind == K_GA)

        @pl.when(keep)
        def _():
            zb[1] = _bdot(h2[...], wbuf[slot].astype(BF16))

        @pl.when(kind == K_GB)
        def _():
            sa[...] = _sigmoid(zb[1])
            zb[0] = _bdot(h2[...], wbuf[slot].astype(BF16))

        @pl.when(jnp.logical_and(kind < K_GA, jnp.logical_not(keep)))
        def _():
            zb[0] = _bdot(h2[...], wbuf[slot].astype(BF16))

        for c in range(POOL_WIDTH // MIX_CH):
            @pl.when(jnp.logical_and(kind == K_P, chunk == c))
            def _(c=c):
                pool_stage(c)

        for c in range(_N_CONV_CHUNKS):
            @pl.when(jnp.logical_and(kind == K_CH, chunk == c))
            def _(c=c):
                conv_stage(c)

        @pl.when(kind == K_CB)
        def _():
            v[chunk] = (zb[0] * ybuf[...]).astype(BF16)

        @pl.when(kind == K_AB)
        def _():
            sb = _sigmoid(zb[0])
            for hc in range(MIX_CH // MIX_HALF):
                hcols = pl.ds(hc * MIX_HALF, MIX_HALF)
                ua = None
                for g in range(len(POOL_WINDOWS)):
                    part = _bdot(ap[g], wbuf[slot, pl.ds(g * POOL_GROUP_DIM, POOL_GROUP_DIM), hcols].astype(BF16))
                    ua = part if ua is None else ua + part
                ub = None
                for k in range(_N_CONV_CHUNKS):
                    part = _bdot(v[k], wbuf[slot, pl.ds(half + k * MIX_CH, MIX_CH), hcols].astype(BF16))
                    ub = part if ub is None else ub + part
                sbh = sb[:, hc * MIX_HALF:(hc + 1) * MIX_HALF]
                m[chunk, :, hcols] = (sa[:, hcols] * ua + sbh * ub).astype(BF16)

    assert _S_OUT + MIX_NB - 2 < _N_STAGES

    @pl.when(oc == 0)
    def _():
        @pl.when(i == 0)
        def _():
            for t in range(MIX_NB - 1):
                wstart(jnp.int32(t))
            xprefetch(0)
            zb[...] = jnp.zeros(zb.shape, F32)

        @pl.when(i % tps == 0)
        def _():
            pcar[...] = jnp.zeros(pcar.shape, F32)
            qcar[...] = jnp.zeros(qcar.shape, F32)

        prologue()
        pl.loop(0, _S_OUT)(main_stage)

    s_out = _S_OUT + oc
    slot = lax.rem(s_out, MIX_NB)
    wwait(slot)
    nxt = s_out + (MIX_NB - 1)

    @pl.when(nxt < _N_STAGES)
    def _():
        wstart(nxt)

    @pl.when(jnp.logical_and(nxt >= _N_STAGES, i < n_tiles - 1))
    def _():
        wstart(nxt - _N_STAGES)

    @pl.when(jnp.logical_and(oc == 1, i < n_tiles - 1))
    def _():
        xprefetch(i + 1)

    for hc in range(MIX_CH // MIX_HALF):
        hcols = pl.ds(hc * MIX_HALF, MIX_HALF)
        acc = None
        for k in range(_N_OUT_CHUNKS):
            part = _bdot(m[k], wbuf[slot, pl.ds(k * MIX_CH, MIX_CH), hcols].astype(BF16))
            acc = part if acc is None else acc + part
        x2p[:, hcols] = x1pc[:, hcols] + g2pc[:, hcols] * acc[:tm]
        x2s[:, hcols] = x1sc[:, hcols] + g2sc[:, hcols] * acc[tm:]


def _mixer(x1p, x1s, mod_p3, mod_s, nrm, grp, pscale, cw, cbias, stp_t, stc_t, w_in, w_a, w_b, w_o, *, tm, ts, tps):
    n_tiles = x1p.shape[0] // tm
    n_seq = n_tiles // tps
    n_s = x1s.shape[0]
    assert n_s == n_tiles * ts
    rows = tm + ts
    any_spec = pl.BlockSpec(memory_space=pl.ANY)

    def const(shape):
        return pl.BlockSpec(shape, lambda i, s: (0,) * len(shape))

    def oc(s):
        return s

    g2_blk = (MOD_G2 * D_MODEL) // MIX_CH
    in_specs = [
        any_spec,
        pl.BlockSpec((ts, D_MODEL), lambda i, s: (i, 0)),
        pl.BlockSpec((tm, MIX_CH), lambda i, s: (i, oc(s))),
        pl.BlockSpec((ts, MIX_CH), lambda i, s: (i, oc(s))),
        pl.BlockSpec((None, 1, D_MODEL), lambda i, s: (i // tps, 0, MOD_SH2)),
        pl.BlockSpec((None, 1, D_MODEL), lambda i, s: (i // tps, 0, MOD_SC2)),
        pl.BlockSpec((None, 1, MIX_CH), lambda i, s: (i // tps, 0, g2_blk + oc(s))),
        pl.BlockSpec((ts, D_MODEL), lambda i, s: (i, MOD_SH2)),
        pl.BlockSpec((ts, D_MODEL), lambda i, s: (i, MOD_SC2)),
        pl.BlockSpec((ts, MIX_CH), lambda i, s: (i, g2_blk + oc(s))),
        const((1, D_MODEL)),
        const((len(POOL_WINDOWS), POOL_GROUP_DIM, POOL_GROUP_DIM)),
        const((1, POOL_WIDTH)),
        const((3, CONV_WIDTH)),
        const((1, CONV_WIDTH)),
        pl.BlockSpec((POOL_STATE, ts, POOL_WIDTH), lambda i, s: (0, i, 0)),
        pl.BlockSpec((CONV_STATE, ts, CONV_WIDTH), lambda i, s: (0, i, 0)),
        any_spec, any_spec, any_spec, any_spec,
    ]
    out_specs = [
        pl.BlockSpec((tm, MIX_CH), lambda i, s: (i, oc(s))),
        pl.BlockSpec((ts, MIX_CH), lambda i, s: (i, oc(s))),
        pl.BlockSpec((None, CARRY_P, POOL_WIDTH), lambda i, s: (i // tps, 0, 0)),
        pl.BlockSpec((None, CARRY_Q, CONV_WIDTH), lambda i, s: (i // tps, 0, 0)),
        pl.BlockSpec((POOL_STATE, ts, POOL_WIDTH), lambda i, s: (0, i, 0)),
        pl.BlockSpec((CONV_STATE, ts, CONV_WIDTH), lambda i, s: (0, i, 0)),
    ]
    out_shape = [
        jax.ShapeDtypeStruct(x1p.shape, F32),
        jax.ShapeDtypeStruct(x1s.shape, F32),
        jax.ShapeDtypeStruct((n_seq, CARRY_P, POOL_WIDTH), F32),
        jax.ShapeDtypeStruct((n_seq, CARRY_Q, CONV_WIDTH), F32),
        jax.ShapeDtypeStruct((POOL_STATE, n_s, POOL_WIDTH), F32),
        jax.ShapeDtypeStruct((CONV_STATE, n_s, CONV_WIDTH), F32),
    ]
    scratch = [
        pltpu.VMEM((MIX_NB, D_MODEL, MIX_CH), F32),
        pltpu.SemaphoreType.DMA((MIX_NB,)),
        pltpu.VMEM((X_BUFS, X_ROWS, D_MODEL), F32),
        pltpu.SemaphoreType.DMA((X_BUFS,)),
        pltpu.VMEM((rows, D_MODEL), BF16),
        pltpu.VMEM((len(POOL_WINDOWS), rows, POOL_GROUP_DIM), BF16),
        pltpu.VMEM((_N_CONV_CHUNKS, rows, MIX_CH), BF16),
        pltpu.VMEM((_N_OUT_CHUNKS, rows, MIX_CH), BF16),
        pltpu.VMEM((2, rows, MIX_CH), F32),
        pltpu.VMEM((rows, MIX_CH), F32),
        pltpu.VMEM((rows, MIX_CH), F32),
        pltpu.VMEM((rows, POOL_GROUP_DIM), BF16),
        pltpu.VMEM((tm + 2 * CARRY_P, POOL_GROUP_DIM), F32),
        pltpu.VMEM((tm + 2 * CARRY_P, POOL_GROUP_DIM), F32),
        pltpu.VMEM((tm + CARRY_Q, MIX_CH), F32),
        pltpu.VMEM((CARRY_P, POOL_WIDTH), F32),
        pltpu.VMEM((CARRY_Q, CONV_WIDTH), F32),
    ]
    return pl.pallas_call(
        functools.partial(_mixer_kernel, tm=tm, ts=ts, tps=tps, n_tiles=n_tiles),
        grid=(n_tiles, _N_OUT_CHUNKS),
        in_specs=in_specs,
        out_specs=out_specs,
        out_shape=out_shape,
        scratch_shapes=scratch,
        compiler_params=pltpu.CompilerParams(
            dimension_semantics=("arbitrary", "arbitrary"), vmem_limit_bytes=VMEM_LIMIT_BYTES),
        name="mixer",
    )(x1p, x1s, x1p, x1s, mod_p3, mod_p3, mod_p3, mod_s, mod_s, mod_s, nrm, grp, pscale, cw, cbias,
      stp_t, stc_t, w_in, w_a, w_b, w_o)


def kernel(x_prompt, x_sample, c_prompt, c_sample, state_pool, state_conv, w_ada, b_ada, norm1, ffn1_gate, ffn1_up, ffn1_down, norm2, w_in, pool_grp, pool_scale, w_branch_a, conv_w, conv_b, w_branch_b, w_o, norm3, ffn2_gate, ffn2_up, ffn2_down, norm_final):
    assert w_ada.shape[0] == 1, "single-layer step"
    batch, seq, d = x_prompt.shape
    dec_batch = x_sample.shape[0]
    assert d == D_MODEL and x_sample.shape[1] == 1

    tm = TM_PROMPT
    tps = seq // tm
    n_tiles = batch * tps
    ts = dec_batch // n_tiles
    assert seq % tm == 0 and dec_batch % n_tiles == 0 and ts % (2 * SUBLANES) == 0

    n_c = batch + dec_batch
    pad = (-n_c) % SUBLANES
    c_all = jnp.concatenate([c_sample, c_prompt, jnp.zeros((pad, d), F32)], axis=0)
    mod = _ada(c_all, w_ada[0], b_ada)
    mod_p3 = mod[dec_batch:n_c].reshape(batch, 1, N_MOD * d)

    row = lambda a: a.reshape(1, -1)
    xp = x_prompt.reshape(batch * seq, d)
    xs = x_sample.reshape(dec_batch, d)
    nf = row(norm_final)
    x1p, x1s = _ffn(xp, xs, mod_p3, mod, MOD_SH1, MOD_SC1, MOD_G1, row(norm1[0]), ffn1_gate[0], ffn1_up[0],
                    ffn1_down[0], nf, tm=tm, ts=ts, tps=tps, final_norm=False, name="ffn1")
    stp_t = jnp.transpose(state_pool[0], (1, 0, 2))
    stc_t = jnp.transpose(state_conv[0], (1, 0, 2))
    x2p, x2s, pst, cst, nps_t, ncs_t = _mixer(
        x1p, x1s, mod_p3, mod, row(norm2[0]), pool_grp[0], row(pool_scale[0]), conv_w[0], row(conv_b[0]),
        stp_t, stc_t, w_in[0], w_branch_a[0], w_branch_b[0], w_o[0], tm=tm, ts=ts, tps=tps)
    yp, ys = _ffn(x2p, x2s, mod_p3, mod, MOD_SH3, MOD_SC3, MOD_G3, row(norm3[0]), ffn2_gate[0], ffn2_up[0],
                  ffn2_down[0], nf, tm=tm, ts=ts, tps=tps, final_norm=True, name="ffn2")

    y_prompt = yp.reshape(batch, seq, d)
    y_sample = ys.reshape(dec_batch, 1, d)
    new_pool_prompt = pst[None, :, CARRY_P - POOL_STATE:, :]
    new_conv_prompt = cst[None, :, CARRY_Q - CONV_STATE:, :]
    new_pool_sample = jnp.transpose(nps_t, (1, 0, 2))[None]
    new_conv_sample = jnp.transpose(ncs_t, (1, 0, 2))[None]
    return (y_prompt, y_sample, new_pool_prompt, new_conv_prompt, new_pool_sample, new_conv_sample)
```

```python
import functools

import jax
import jax.numpy as jnp
from jax import lax
from jax.experimental import pallas as pl
from jax.experimental.pallas import tpu as pltpu

F32 = jnp.float32
BF16 = jnp.bfloat16

D_MODEL = 2048
D_FF = 5632
POOL_WIDTH = 1024
CONV_WIDTH = 1024
POOL_WINDOWS = (2, 4, 8, 16)
POOL_GROUP_DIM = 256
POOL_STATE = 15
CONV_STATE = 2
N_MOD = 9
EPS = 1e-6
PAST_LEN = 16384

V7X_VMEM_BYTES = 64 * 1024 * 1024
SUBLANES = 8
VMEM_LIMIT_BYTES = V7X_VMEM_BYTES - 2 * 1024 * 1024

TM_PROMPT = 1024

ADA_TN = 1024
FFN_TF = 256
FFN_DMA_SPLIT = 1
FFN_NB = 3
MIX_CH = 512
MIX_HALF = 256
MIX_NB = 3
X_ROWS = 128
FFN_ROWS = 128
X_BUFS = 4
MIX_DMA_SPLIT = 2
CARRY_P = 16
CARRY_Q = 8

COL_P, COL_CB, COL_CC, COL_CH, COL_GA, COL_GB = 0, 1024, 2048, 3072, 4096, 6144
MOD_SH1, MOD_SC1, MOD_G1, MOD_SH2, MOD_SC2, MOD_G2, MOD_SH3, MOD_SC3, MOD_G3 = range(9)


def _sigmoid(x):
    return 1.0 / (1.0 + jnp.exp(-x))


def _modulate(x, nrm, sc, sh):
    inv = lax.rsqrt(jnp.mean(x * x, axis=-1, keepdims=True) + EPS)
    return (x * inv) * nrm * (1.0 + sc) + sh


def _bdot(a, b):
    return jnp.dot(a, b, preferred_element_type=F32)


def _ada_kernel(c_ref, w_ref, b_ref, o_ref):
    c = c_ref[...]
    s = (c * _sigmoid(c)).astype(BF16)
    o_ref[...] = _bdot(s, w_ref[...].astype(BF16)) + b_ref[...]


def _ada(c_all, w_ada, b_ada):
    rows = c_all.shape[0]
    cols = w_ada.shape[1]
    return pl.pallas_call(
        _ada_kernel,
        grid=(cols // ADA_TN,),
        in_specs=[
            pl.BlockSpec((rows, D_MODEL), lambda j: (0, 0)),
            pl.BlockSpec((D_MODEL, ADA_TN), lambda j: (0, j)),
            pl.BlockSpec((1, ADA_TN), lambda j: (0, j)),
        ],
        out_specs=pl.BlockSpec((rows, ADA_TN), lambda j: (0, j)),
        out_shape=jax.ShapeDtypeStruct((rows, cols), F32),
        compiler_params=pltpu.CompilerParams(dimension_semantics=("arbitrary",)),
        name="ada_mod",
    )(c_all, w_ada, b_ada)


def _ffn_kernel(xp_ref, xs_ref, shp, scp, gtp, shs, scs, gts, nrm_ref, wg_hbm, wu_hbm, wd_hbm, nf_ref,
                op_ref, os_ref, h_ref, inv_ref, wgb, wub, wdb, wsem, *, n_f, n_tiles, final_norm):
    i = pl.program_id(0)
    tm = xp_ref.shape[0]
    ts = xs_ref.shape[0]

    def wcopies(f, slot):
        col = pl.multiple_of(f * FFN_TF, FFN_TF)
        rows_in = D_MODEL // FFN_DMA_SPLIT
        rows_dn = FFN_TF // FFN_DMA_SPLIT
        out = []
        for k in range(FFN_DMA_SPLIT):
            r_in = pl.ds(k * rows_in, rows_in)
            out.append(pltpu.make_async_copy(wg_hbm.at[r_in, pl.ds(col, FFN_TF)], wgb.at[slot, r_in, :],
                                             wsem.at[slot]))
            out.append(pltpu.make_async_copy(wu_hbm.at[r_in, pl.ds(col, FFN_TF)], wub.at[slot, r_in, :],
                                             wsem.at[slot]))
            out.append(pltpu.make_async_copy(wd_hbm.at[pl.ds(col + k * rows_dn, rows_dn), :],
                                             wdb.at[slot, pl.ds(k * rows_dn, rows_dn), :], wsem.at[slot]))
        return out

    def wstart(f, slot):
        for cp in wcopies(f, slot):
            cp.start()

    def wwait(f, slot):
        for cp in wcopies(f, slot):
            cp.wait()

    @pl.when(i == 0)
    def _():
        for t in range(FFN_NB - 1):
            wstart(jnp.int32(t), t)

    def head(x, sc, sh):
        return _modulate(x, nrm_ref[...], sc, sh).astype(BF16)

    def tail(x, gt, acc):
        xo = x + 0.5 * gt * acc
        if final_norm:
            inv = lax.rsqrt(jnp.mean(xo * xo, axis=-1, keepdims=True) + EPS)
            xo = (xo * inv) * nf_ref[...]
        return xo

    @pl.loop(0, tm // FFN_ROWS)
    def _(r):
        rows = pl.ds(pl.multiple_of(r * FFN_ROWS, FFN_ROWS), FFN_ROWS)
        x = xp_ref[rows, :]
        inv_ref[rows, :] = lax.rsqrt(jnp.mean(x * x, axis=-1, keepdims=True) + EPS)

    @pl.loop(0, tm // FFN_ROWS)
    def _(r):
        rows = pl.ds(pl.multiple_of(r * FFN_ROWS, FFN_ROWS), FFN_ROWS)
        hx = (xp_ref[rows, :] * inv_ref[rows, :]) * nrm_ref[...] * (1.0 + scp[...]) + shp[...]
        h_ref[rows, :] = hx.astype(BF16)
        op_ref[rows, :] = jnp.zeros((FFN_ROWS, D_MODEL), F32)

    h_ref[pl.ds(tm, ts), :] = head(xs_ref[...], scs[...], shs[...])
    os_ref[...] = jnp.zeros((ts, D_MODEL), F32)

    @pl.loop(0, n_f)
    def _(f):
        g_idx = i * n_f + f
        slot = lax.rem(g_idx, FFN_NB)
        wwait(f, slot)
        nxt = f + (FFN_NB - 1)
        nslot = lax.rem(g_idx + (FFN_NB - 1), FFN_NB)

        @pl.when(nxt < n_f)
        def _():
            wstart(nxt, nslot)

        @pl.when(jnp.logical_and(nxt >= n_f, i < n_tiles - 1))
        def _():
            wstart(nxt - n_f, nslot)

        h = h_ref[...]
        g = _bdot(h, wgb[slot].astype(BF16))
        u = _bdot(h, wub[slot].astype(BF16))
        a = ((g * _sigmoid(g)) * u).astype(BF16)
        for half in range(2):
            hcols = pl.ds(half * (D_MODEL // 2), D_MODEL // 2)
            y = _bdot(a, wdb[slot, :, hcols].astype(BF16))
            op_ref[:, hcols] += y[:tm]
            os_ref[:, hcols] += y[tm:]

    @pl.loop(0, tm // FFN_ROWS)
    def _(r):
        rows = pl.ds(pl.multiple_of(r * FFN_ROWS, FFN_ROWS), FFN_ROWS)
        op_ref[rows, :] = tail(xp_ref[rows, :], gtp[...], op_ref[rows, :])

    os_ref[...] = tail(xs_ref[...], gts[...], os_ref[...])


def _ffn(xp, xs, mod_p3, mod_s, j_sh, j_sc, j_g, nrm, wg, wu, wd, nf, *, tm, ts, tps, final_norm, name):
    n_tiles = xp.shape[0] // tm
    assert xs.shape[0] == n_tiles * ts
    n_f = D_FF // FFN_TF
    assert n_f >= FFN_NB
    any_spec = pl.BlockSpec(memory_space=pl.ANY)

    def pspec(j):
        return pl.BlockSpec((None, 1, D_MODEL), lambda i: (i // tps, 0, j))

    def sspec(j):
        return pl.BlockSpec((ts, D_MODEL), lambda i: (i, j))

    return pl.pallas_call(
        functools.partial(_ffn_kernel, n_f=n_f, n_tiles=n_tiles, final_norm=final_norm),
        grid=(n_tiles,),
        in_specs=[
            pl.BlockSpec((tm, D_MODEL), lambda i: (i, 0)),
            pl.BlockSpec((ts, D_MODEL), lambda i: (i, 0)),
            pspec(j_sh), pspec(j_sc), pspec(j_g),
            sspec(j_sh), sspec(j_sc), sspec(j_g),
            pl.BlockSpec((1, D_MODEL), lambda i: (0, 0)),
            any_spec, any_spec, any_spec,
            pl.BlockSpec((1, D_MODEL), lambda i: (0, 0)),
        ],
        out_specs=[pl.BlockSpec((tm, D_MODEL), lambda i: (i, 0)),
                   pl.BlockSpec((ts, D_MODEL), lambda i: (i, 0))],
        out_shape=[jax.ShapeDtypeStruct(xp.shape, F32), jax.ShapeDtypeStruct(xs.shape, F32)],
        scratch_shapes=[pltpu.VMEM((tm + ts, D_MODEL), BF16), pltpu.VMEM((tm, 1), F32),
                        pltpu.VMEM((FFN_NB, D_MODEL, FFN_TF), F32), pltpu.VMEM((FFN_NB, D_MODEL, FFN_TF), F32),
                        pltpu.VMEM((FFN_NB, FFN_TF, D_MODEL), F32), pltpu.SemaphoreType.DMA((FFN_NB,))],
        compiler_params=pltpu.CompilerParams(
            dimension_semantics=("arbitrary",), vmem_limit_bytes=VMEM_LIMIT_BYTES),
        name=name,
    )(xp, xs, mod_p3, mod_p3, mod_p3, mod_s, mod_s, mod_s, nrm, wg, wu, wd, nf)


K_P, K_CC, K_CH, K_CB, K_GA, K_GB, K_AB, K_O = range(8)
_N_CONV_CHUNKS = CONV_WIDTH // MIX_CH
_N_OUT_CHUNKS = D_MODEL // MIX_CH
_S_CONV = POOL_WIDTH // MIX_CH
_S_GATE = _S_CONV + 3 * _N_CONV_CHUNKS
_S_OUT = _S_GATE + 3 * _N_OUT_CHUNKS
_N_STAGES = _S_OUT + _N_OUT_CHUNKS
_KIND_COL0 = (COL_P, COL_CC, COL_CH, COL_CB, COL_GA, COL_GB, 0, 0)
assert _N_STAGES % MIX_NB == 0


def _decode_stage(t):
    u1 = t - _S_CONV
    u2 = t - _S_GATE
    c1 = lax.div(jnp.maximum(u1, 0), 3)
    c2 = lax.div(jnp.maximum(u2, 0), 3)
    kind = jnp.where(t < _S_CONV, K_P,
                     jnp.where(t < _S_GATE, K_CC + (u1 - 3 * c1),
                               jnp.where(t < _S_OUT, K_GA + (u2 - 3 * c2), K_O)))
    chunk = jnp.where(t < _S_CONV, t, jnp.where(t < _S_GATE, c1, jnp.where(t < _S_OUT, c2, t - _S_OUT)))
    col0 = jnp.int32(0)
    for k, base in enumerate(_KIND_COL0):
        if base:
            col0 = jnp.where(kind == k, base, col0)
    return kind, chunk, col0 + chunk * MIX_CH


def _mixer_kernel(x1p_hbm, x1s, x1pc, x1sc, shp, scp, g2pc, shs, scs, g2sc, nrm, grp, pscale, cw, cbias,
                  stp, stc, w_in, w_a, w_b, w_o,
                  x2p, x2s, pst, cst, nps, ncs,
                  wbuf, wsem, xbuf, xsem, h2, ap, v, m, zb, ybuf, sa, abf, e0, e1, qext, pcar, qcar,
                  *, tm, ts, tps, n_tiles):
    i = pl.program_id(0)
    oc = pl.program_id(1)
    slab = D_MODEL // MIX_DMA_SPLIT
    half = D_MODEL // 2
    prows = pl.ds(0, tm)
    srows = pl.ds(tm, ts)

    def wcopy(src, src_row, col, dst_slot, dst_row):
        return pltpu.make_async_copy(src.at[pl.ds(src_row, slab), pl.ds(col, MIX_CH)],
                                     wbuf.at[dst_slot, pl.ds(dst_row, slab), :], wsem.at[dst_slot])

    def wstart(t):
        k_t, _, col_t = _decode_stage(t)
        col_t = pl.multiple_of(col_t, MIX_CH)
        slot_t = lax.rem(t, MIX_NB)

        @pl.when(k_t < K_AB)
        def _():
            for k in range(MIX_DMA_SPLIT):
                wcopy(w_in, k * slab, col_t, slot_t, k * slab).start()

        @pl.when(k_t == K_AB)
        def _():
            for k in range(MIX_DMA_SPLIT):
                r0 = k * slab
                src, s0 = (w_a, r0) if r0 < half else (w_b, r0 - half)
                wcopy(src, s0, col_t, slot_t, r0).start()

        @pl.when(k_t == K_O)
        def _():
            for k in range(MIX_DMA_SPLIT):
                wcopy(w_o, k * slab, col_t, slot_t, k * slab).start()

    def wwait(slot):
        for k in range(MIX_DMA_SPLIT):
            wcopy(w_in, k * slab, 0, slot, k * slab).wait()

    n_xchunks = tm // X_ROWS

    def xcopy(tile, r):
        return pltpu.make_async_copy(x1p_hbm.at[pl.ds(tile * tm + r * X_ROWS, X_ROWS), :],
                                     xbuf.at[r % X_BUFS], xsem.at[r % X_BUFS])

    def xprefetch(tile):
        for r in range(min(X_BUFS, n_xchunks)):
            xcopy(tile, r).start()

    def prologue():
        for r in range(n_xchunks):
            xcopy(i, r).wait()
            h2[pl.ds(r * X_ROWS, X_ROWS), :] = _modulate(xbuf[r % X_BUFS], nrm[...], scp[...], shp[...]).astype(BF16)
            if r + X_BUFS < n_xchunks:
                xcopy(i, r + X_BUFS).start()
        h2[srows, :] = _modulate(x1s[...], nrm[...], scs[...], shs[...]).astype(BF16)
        for r in range(POOL_STATE - 1):
            nps[r] = stp[r + 1]
        for r in range(CONV_STATE - 1):
            ncs[r] = stc[r + 1]

    def pool_stage(c):
        zbuf = zb.at[0]
        for gl in range(MIX_CH // POOL_GROUP_DIM):
            g = c * (MIX_CH // POOL_GROUP_DIM) + gl
            w = POOL_WINDOWS[g]
            gcols = pl.ds(g * POOL_GROUP_DIM, POOL_GROUP_DIM)
            lcols = pl.ds(gl * POOL_GROUP_DIM, POOL_GROUP_DIM)
            pg = zbuf[prows, lcols]
            pad = jnp.zeros((SUBLANES, POOL_GROUP_DIM), F32)
            e0[pl.ds(SUBLANES, SUBLANES), :] = pad
            e1[pl.ds(SUBLANES, SUBLANES), :] = pad
            e0[pl.ds(CARRY_P, CARRY_P), :] = pcar[:, gcols]
            e0[pl.ds(2 * CARRY_P, tm), :] = pg
            src, dst = e0, e1
            d = 1
            while d < w:
                dst[pl.ds(CARRY_P, tm + CARRY_P), :] = (src[pl.ds(CARRY_P, tm + CARRY_P), :]
                                                        + src[pl.ds(CARRY_P - d, tm + CARRY_P), :])
                src, dst = dst, src
                d *= 2
            wsum = src[pl.ds(2 * CARRY_P, tm), :]
            pos = (i % tps) * tm + lax.broadcasted_iota(jnp.int32, (CARRY_P, 1), 0)
            cnt = jnp.minimum(pos + 1, w).astype(F32)
            abf[prows, :] = (wsum * (1.0 / w) - pg).astype(BF16)
            head = pl.ds(0, CARRY_P)
            abf[head, :] = (wsum[:CARRY_P, :] / cnt - pg[:CARRY_P, :]).astype(BF16)
            tail = pg[tm - CARRY_P:, :]
            pcar[:, gcols] = tail
            pst[:, gcols] = tail
            ps = zbuf[srows, lcols]
            ssum = ps
            for r in range(POOL_STATE + 1 - w, POOL_STATE):
                ssum = ssum + stp[r, :, gcols]
            abf[srows, :] = (ssum / float(min(PAST_LEN + 1, w)) - ps).astype(BF16)
            nps[POOL_STATE - 1, :, gcols] = ps
            ag = _bdot(abf[...], grp[g].astype(BF16)) * pscale[:, gcols]
            ap[g] = ag.astype(BF16)

    def conv_stage(c):
        ccols = pl.ds(c * MIX_CH, MIX_CH)
        zbuf, zprev = zb.at[0], zb.at[1]
        qp = zprev[prows, :] * zbuf[prows, :]
        qext[pl.ds(0, CARRY_Q), :] = qcar[:, ccols]
        qext[pl.ds(CARRY_Q, tm), :] = qp
        y = cbias[:, ccols] + qext[pl.ds(CARRY_Q - 2, tm), :] * cw[0:1, ccols]
        y = y + qext[pl.ds(CARRY_Q - 1, tm), :] * cw[1:2, ccols]
        tail = qp[tm - CARRY_Q:, :]
        qcar[:, ccols] = tail
        cst[:, ccols] = tail
        qs = zprev[srows, :] * zbuf[srows, :]
        ys = cbias[:, ccols] + stc[0, :, ccols] * cw[0:1, ccols]
        ys = ys + stc[1, :, ccols] * cw[1:2, ccols]
        ncs[CONV_STATE - 1, :, ccols] = qs
        ybuf[prows, :] = y + qp * cw[2:3, ccols]
        ybuf[srows, :] = ys + qs * cw[2:3, ccols]

    def main_stage(s):
        kind, chunk, _ = _decode_stage(s)
        slot = lax.rem(s, MIX_NB)
        wwait(slot)
        wstart(s + (MIX_NB - 1))

        keep = jnp.logical_or(kind == K_CC, kind == K_GA)

        @pl.when(keep)
        def _():
            zb[1] = _bdot(h2[...], wbuf[slot].astype(BF16))

        @pl.when(kind == K_GB)
        def _():
            sa[...] = _sigmoid(zb[1])
            zb[0] = _bdot(h2[...], wbuf[slot].astype(BF16))

        @pl.when(jnp.logical_and(kind < K_GA, jnp.logical_not(keep)))
        def _():
            zb[0] = _bdot(h2[...], wbuf[slot].astype(BF16))

        for c in range(POOL_WIDTH // MIX_CH):
            @pl.when(jnp.logical_and(kind == K_P, chunk == c))
            def _(c=c):
                pool_stage(c)

        for c in range(_N_CONV_CHUNKS):
            @pl.when(jnp.logical_and(kind == K_CH, chunk == c))
            def _(c=c):
                conv_stage(c)

        @pl.when(kind == K_CB)
        def _():
            v[chunk] = (zb[0] * ybuf[...]).astype(BF16)

        @pl.when(kind == K_AB)
        def _():
            sb = _sigmoid(zb[0])
            for hc in range(MIX_CH // MIX_HALF):
                hcols = pl.ds(hc * MIX_HALF, MIX_HALF)
                ua = None
                for g in range(len(POOL_WINDOWS)):
                    part = _bdot(ap[g], wbuf[slot, pl.ds(g * POOL_GROUP_DIM, POOL_GROUP_DIM), hcols].astype(BF16))
                    ua = part if ua is None else ua + part
                ub = None
                for k in range(_N_CONV_CHUNKS):
                    part = _bdot(v[k], wbuf[slot, pl.ds(half + k * MIX_CH, MIX_CH), hcols].astype(BF16))
                    ub = part if ub is None else ub + part
                sbh = sb[:, hc * MIX_HALF:(hc + 1) * MIX_HALF]
                m[chunk, :, hcols] = (sa[:, hcols] * ua + sbh * ub).astype(BF16)

    assert _S_OUT + MIX_NB - 2 < _N_STAGES

    @pl.when(oc == 0)
    def _():
        @pl.when(i == 0)
        def _():
            for t in range(MIX_NB - 1):
                wstart(jnp.int32(t))
            xprefetch(0)
            zb[...] = jnp.zeros(zb.shape, F32)

        @pl.when(i % tps == 0)
        def _():
            pcar[...] = jnp.zeros(pcar.shape, F32)
            qcar[...] = jnp.zeros(qcar.shape, F32)

        prologue()
        pl.loop(0, _S_OUT)(main_stage)

    s_out = _S_OUT + oc
    slot = lax.rem(s_out, MIX_NB)
    wwait(slot)
    nxt = s_out + (MIX_NB - 1)

    @pl.when(nxt < _N_STAGES)
    def _():
        wstart(nxt)

    @pl.when(jnp.logical_and(nxt >= _N_STAGES, i < n_tiles - 1))
    def _():
        wstart(nxt - _N_STAGES)

    @pl.when(jnp.logical_and(oc == 1, i < n_tiles - 1))
    def _():
        xprefetch(i + 1)

    for hc in range(MIX_CH // MIX_HALF):
        hcols = pl.ds(hc * MIX_HALF, MIX_HALF)
        acc = None
        for k in range(_N_OUT_CHUNKS):
            part = _bdot(m[k], wbuf[slot, pl.ds(k * MIX_CH, MIX_CH), hcols].astype(BF16))
            acc = part if acc is None else acc + part
        x2p[:, hcols] = x1pc[:, hcols] + g2pc[:, hcols] * acc[:tm]
        x2s[:, hcols] = x1sc[:, hcols] + g2sc[:, hcols] * acc[tm:]


def _mixer(x1p, x1s, mod_p3, mod_s, nrm, grp, pscale, cw, cbias, stp_t, stc_t, w_in, w_a, w_b, w_o, *, tm, ts, tps):
    n_tiles = x1p.shape[0] // tm
    n_seq = n_tiles // tps
    n_s = x1s.shape[0]
    assert n_s == n_tiles * ts
    rows = tm + ts
    any_spec = pl.BlockSpec(memory_space=pl.ANY)

    def const(shape):
        return pl.BlockSpec(shape, lambda i, s: (0,) * len(shape))

    def oc(s):
        return s

    g2_blk = (MOD_G2 * D_MODEL) // MIX_CH
    in_specs = [
        any_spec,
        pl.BlockSpec((ts, D_MODEL), lambda i, s: (i, 0)),
        pl.BlockSpec((tm, MIX_CH), lambda i, s: (i, oc(s))),
        pl.BlockSpec((ts, MIX_CH), lambda i, s: (i, oc(s))),
        pl.BlockSpec((None, 1, D_MODEL), lambda i, s: (i // tps, 0, MOD_SH2)),
        pl.BlockSpec((None, 1, D_MODEL), lambda i, s: (i // tps, 0, MOD_SC2)),
        pl.BlockSpec((None, 1, MIX_CH), lambda i, s: (i // tps, 0, g2_blk + oc(s))),
        pl.BlockSpec((ts, D_MODEL), lambda i, s: (i, MOD_SH2)),
        pl.BlockSpec((ts, D_MODEL), lambda i, s: (i, MOD_SC2)),
        pl.BlockSpec((ts, MIX_CH), lambda i, s: (i, g2_blk + oc(s))),
        const((1, D_MODEL)),
        const((len(POOL_WINDOWS), POOL_GROUP_DIM, POOL_GROUP_DIM)),
        const((1, POOL_WIDTH)),
        const((3, CONV_WIDTH)),
        const((1, CONV_WIDTH)),
        pl.BlockSpec((POOL_STATE, ts, POOL_WIDTH), lambda i, s: (0, i, 0)),
        pl.BlockSpec((CONV_STATE, ts, CONV_WIDTH), lambda i, s: (0, i, 0)),
        any_spec, any_spec, any_spec, any_spec,
    ]
    out_specs = [
        pl.BlockSpec((tm, MIX_CH), lambda i, s: (i, oc(s))),
        pl.BlockSpec((ts, MIX_CH), lambda i, s: (i, oc(s))),
        pl.BlockSpec((None, CARRY_P, POOL_WIDTH), lambda i, s: (i // tps, 0, 0)),
        pl.BlockSpec((None, CARRY_Q, CONV_WIDTH), lambda i, s: (i // tps, 0, 0)),
        pl.BlockSpec((POOL_STATE, ts, POOL_WIDTH), lambda i, s: (0, i, 0)),
        pl.BlockSpec((CONV_STATE, ts, CONV_WIDTH), lambda i, s: (0, i, 0)),
    ]
    out_shape = [
        jax.ShapeDtypeStruct(x1p.shape, F32),
        jax.ShapeDtypeStruct(x1s.shape, F32),
        jax.ShapeDtypeStruct((n_seq, CARRY_P, POOL_WIDTH), F32),
        jax.ShapeDtypeStruct((n_seq, CARRY_Q, CONV_WIDTH), F32),
        jax.ShapeDtypeStruct((POOL_STATE, n_s, POOL_WIDTH), F32),
        jax.ShapeDtypeStruct((CONV_STATE, n_s, CONV_WIDTH), F32),
    ]
    scratch = [
        pltpu.VMEM((MIX_NB, D_MODEL, MIX_CH), F32),
        pltpu.SemaphoreType.DMA((MIX_NB,)),
        pltpu.VMEM((X_BUFS, X_ROWS, D_MODEL), F32),
        pltpu.SemaphoreType.DMA((X_BUFS,)),
        pltpu.VMEM((rows, D_MODEL), BF16),
        pltpu.VMEM((len(POOL_WINDOWS), rows, POOL_GROUP_DIM), BF16),
        pltpu.VMEM((_N_CONV_CHUNKS, rows, MIX_CH), BF16),
        pltpu.VMEM((_N_OUT_CHUNKS, rows, MIX_CH), BF16),
        pltpu.VMEM((2, rows, MIX_CH), F32),
        pltpu.VMEM((rows, MIX_CH), F32),
        pltpu.VMEM((rows, MIX_CH), F32),
        pltpu.VMEM((rows, POOL_GROUP_DIM), BF16),
        pltpu.VMEM((tm + 2 * CARRY_P, POOL_GROUP_DIM), F32),
        pltpu.VMEM((tm + 2 * CARRY_P, POOL_GROUP_DIM), F32),
        pltpu.VMEM((tm + CARRY_Q, MIX_CH), F32),
        pltpu.VMEM((CARRY_P, POOL_WIDTH), F32),
        pltpu.VMEM((CARRY_Q, CONV_WIDTH), F32),
    ]
    return pl.pallas_call(
        functools.partial(_mixer_kernel, tm=tm, ts=ts, tps=tps, n_tiles=n_tiles),
        grid=(n_tiles, _N_OUT_CHUNKS),
        in_specs=in_specs,
        out_specs=out_specs,
        out_shape=out_shape,
        scratch_shapes=scratch,
        compiler_params=pltpu.CompilerParams(
            dimension_semantics=("arbitrary", "arbitrary"), vmem_limit_bytes=VMEM_LIMIT_BYTES),
        name="mixer",
    )(x1p, x1s, x1p, x1s, mod_p3, mod_p3, mod_p3, mod_s, mod_s, mod_s, nrm, grp, pscale, cw, cbias,
      stp_t, stc_t, w_in, w_a, w_b, w_o)


def kernel(x_prompt, x_sample, c_prompt, c_sample, state_pool, state_conv, w_ada, b_ada, norm1, ffn1_gate, ffn1_up, ffn1_down, norm2, w_in, pool_grp, pool_scale, w_branch_a, conv_w, conv_b, w_branch_b, w_o, norm3, ffn2_gate, ffn2_up, ffn2_down, norm_final):
    assert w_ada.shape[0] == 1, "single-layer step"
    batch, seq, d = x_prompt.shape
    dec_batch = x_sample.shape[0]
    assert d == D_MODEL and x_sample.shape[1] == 1

    tm = TM_PROMPT
    tps = seq // tm
    n_tiles = batch * tps
    ts = dec_batch // n_tiles
    assert seq % tm == 0 and dec_batch % n_tiles == 0 and ts % (2 * SUBLANES) == 0

    n_c = batch + dec_batch
    pad = (-n_c) % SUBLANES
    c_all = jnp.concatenate([c_sample, c_prompt, jnp.zeros((pad, d), F32)], axis=0)
    mod = _ada(c_all, w_ada[0], b_ada)
    mod_p3 = mod[dec_batch:n_c].reshape(batch, 1, N_MOD * d)

    row = lambda a: a.reshape(1, -1)
    xp = x_prompt.reshape(batch * seq, d)
    xs = x_sample.reshape(dec_batch, d)
    nf = row(norm_final)
    x1p, x1s = _ffn(xp, xs, mod_p3, mod, MOD_SH1, MOD_SC1, MOD_G1, row(norm1[0]), ffn1_gate[0], ffn1_up[0],
                    ffn1_down[0], nf, tm=tm, ts=ts, tps=tps, final_norm=False, name="ffn1")
    stp_t = jnp.transpose(state_pool[0], (1, 0, 2))
    stc_t = jnp.transpose(state_conv[0], (1, 0, 2))
    x2p, x2s, pst, cst, nps_t, ncs_t = _mixer(
        x1p, x1s, mod_p3, mod, row(norm2[0]), pool_grp[0], row(pool_scale[0]), conv_w[0], row(conv_b[0]),
        stp_t, stc_t, w_in[0], w_branch_a[0], w_branch_b[0], w_o[0], tm=tm, ts=ts, tps=tps)
    yp, ys = _ffn(x2p, x2s, mod_p3, mod, MOD_SH3, MOD_SC3, MOD_G3, row(norm3[0]), ffn2_gate[0], ffn2_up[0],
                  ffn2_down[0], nf, tm=tm, ts=ts, tps=tps, final_norm=True, name="ffn2")

    y_prompt = yp.reshape(batch, seq, d)
    y_sample = ys.reshape(dec_batch, 1, d)
    new_pool_prompt = pst[None, :, CARRY_P - POOL_STATE:, :]
    new_conv_prompt = cst[None, :, CARRY_Q - CONV_STATE:, :]
    new_pool_sample = jnp.transpose(nps_t, (1, 0, 2))[None]
    new_conv_sample = jnp.transpose(ncs_t, (1, 0, 2))[None]
    return (y_prompt, y_sample, new_pool_prompt, new_conv_prompt, new_pool_sample, new_conv_sample)
```

```python
import functools

import jax
import jax.numpy as jnp
from jax import lax
from jax.experimental import pallas as pl
from jax.experimental.pallas import tpu as pltpu

F32 = jnp.float32
BF16 = jnp.bfloat16

D_MODEL = 2048
D_FF = 5632
POOL_WIDTH = 1024
CONV_WIDTH = 1024
POOL_WINDOWS = (2, 4, 8, 16)
POOL_GROUP_DIM = 256
POOL_STATE = 15
CONV_STATE = 2
N_MOD = 9
EPS = 1e-6
PAST_LEN = 16384

V7X_VMEM_BYTES = 64 * 1024 * 1024
SUBLANES = 8
VMEM_LIMIT_BYTES = V7X_VMEM_BYTES - 2 * 1024 * 1024

TM_PROMPT = 1024

ADA_TN = 512
ADA_NB = 4
FFN_TF = 256
FFN_DMA_SPLIT = 1
FFN_NB = 3
MIX_CH = 512
MIX_HALF = 256
MIX_NB = 3
X_ROWS = 128
FFN_ROWS = 128
X_BUFS = 4
MIX_DMA_SPLIT = 2
CARRY_P = 16
CARRY_Q = 8

COL_P, COL_CB, COL_CC, COL_CH, COL_GA, COL_GB = 0, 1024, 2048, 3072, 4096, 6144
MOD_SH1, MOD_SC1, MOD_G1, MOD_SH2, MOD_SC2, MOD_G2, MOD_SH3, MOD_SC3, MOD_G3 = range(9)


def _sigmoid(x):
    return 1.0 / (1.0 + jnp.exp(-x))


def _modulate(x, nrm, sc, sh):
    inv = lax.rsqrt(jnp.mean(x * x, axis=-1, keepdims=True) + EPS)
    return (x * inv) * nrm * (1.0 + sc) + sh


def _bdot(a, b):
    return jnp.dot(a, b, preferred_element_type=F32)


def _ada_kernel(c_ref, b_ref, w_hbm, o_hbm, s_ref, wbuf, obuf, wsem, osem, *, n_chunks):
    c = c_ref[...]
    s_ref[...] = (c * _sigmoid(c)).astype(BF16)

    def wcopy(j, slot):
        return pltpu.make_async_copy(w_hbm.at[:, pl.ds(pl.multiple_of(j * ADA_TN, ADA_TN), ADA_TN)],
                                     wbuf.at[slot], wsem.at[slot])

    def ocopy(j, slot):
        return pltpu.make_async_copy(obuf.at[slot],
                                     o_hbm.at[:, pl.ds(pl.multiple_of(j * ADA_TN, ADA_TN), ADA_TN)], osem.at[slot])

    for t in range(ADA_NB - 1):
        wcopy(jnp.int32(t), t).start()

    @pl.loop(0, n_chunks)
    def _(j):
        slot = lax.rem(j, ADA_NB)
        oslot = lax.rem(j, 2)
        wcopy(j, slot).wait()

        @pl.when(j + (ADA_NB - 1) < n_chunks)
        def _():
            wcopy(j + (ADA_NB - 1), lax.rem(j + (ADA_NB - 1), ADA_NB)).start()

        @pl.when(j >= 2)
        def _():
            ocopy(j - 2, oslot).wait()

        obuf[oslot] = _bdot(s_ref[...], wbuf[slot].astype(BF16)) + b_ref[j]
        ocopy(j, oslot).start()

    for t in range(2):
        j_last = n_chunks - 2 + t
        ocopy(jnp.int32(j_last), j_last % 2).wait()


def _ada(c_all, w_ada, b_ada):
    rows = c_all.shape[0]
    cols = w_ada.shape[1]
    n_chunks = cols // ADA_TN
    assert n_chunks >= ADA_NB and n_chunks >= 2
    return pl.pallas_call(
        functools.partial(_ada_kernel, n_chunks=n_chunks),
        grid=(1,),
        in_specs=[
            pl.BlockSpec((rows, D_MODEL), lambda g: (0, 0)),
            pl.BlockSpec((n_chunks, 1, ADA_TN), lambda g: (0, 0, 0)),
            pl.BlockSpec(memory_space=pl.ANY),
        ],
        out_specs=pl.BlockSpec(memory_space=pl.ANY),
        out_shape=jax.ShapeDtypeStruct((rows, cols), F32),
        scratch_shapes=[pltpu.VMEM((rows, D_MODEL), BF16),
                        pltpu.VMEM((ADA_NB, D_MODEL, ADA_TN), F32),
                        pltpu.VMEM((2, rows, ADA_TN), F32),
                        pltpu.SemaphoreType.DMA((ADA_NB,)), pltpu.SemaphoreType.DMA((2,))],
        compiler_params=pltpu.CompilerParams(dimension_semantics=("arbitrary",)),
        name="ada_mod",
    )(c_all, b_ada.reshape(n_chunks, 1, ADA_TN), w_ada)


def _ffn_kernel(xp_ref, xs_ref, shp, scp, gtp, shs, scs, gts, nrm_ref, wg_hbm, wu_hbm, wd_hbm, nf_ref,
                op_ref, os_ref, h_ref, inv_ref, wgb, wub, wdb, wsem, *, n_f, n_tiles, final_norm):
    i = pl.program_id(0)
    tm = xp_ref.shape[0]
    ts = xs_ref.shape[0]

    def wcopies(f, slot):
        col = pl.multiple_of(f * FFN_TF, FFN_TF)
        rows_in = D_MODEL // FFN_DMA_SPLIT
        rows_dn = FFN_TF // FFN_DMA_SPLIT
        out = []
        for k in range(FFN_DMA_SPLIT):
            r_in = pl.ds(k * rows_in, rows_in)
            out.append(pltpu.make_async_copy(wg_hbm.at[r_in, pl.ds(col, FFN_TF)], wgb.at[slot, r_in, :],
                                             wsem.at[slot]))
            out.append(pltpu.make_async_copy(wu_hbm.at[r_in, pl.ds(col, FFN_TF)], wub.at[slot, r_in, :],
                                             wsem.at[slot]))
            out.append(pltpu.make_async_copy(wd_hbm.at[pl.ds(col + k * rows_dn, rows_dn), :],
                                             wdb.at[slot, pl.ds(k * rows_dn, rows_dn), :], wsem.at[slot]))
        return out

    def wstart(f, slot):
        for cp in wcopies(f, slot):
            cp.start()

    def wwait(f, slot):
        for cp in wcopies(f, slot):
            cp.wait()

    @pl.when(i == 0)
    def _():
        for t in range(FFN_NB - 1):
            wstart(jnp.int32(t), t)

    def head(x, sc, sh):
        return _modulate(x, nrm_ref[...], sc, sh).astype(BF16)

    def tail(x, gt, acc):
        xo = x + 0.5 * gt * acc
        if final_norm:
            inv = lax.rsqrt(jnp.mean(xo * xo, axis=-1, keepdims=True) + EPS)
            xo = (xo * inv) * nf_ref[...]
        return xo

    @pl.loop(0, tm // FFN_ROWS)
    def _(r):
        rows = pl.ds(pl.multiple_of(r * FFN_ROWS, FFN_ROWS), FFN_ROWS)
        x = xp_ref[rows, :]
        inv_ref[rows, :] = lax.rsqrt(jnp.mean(x * x, axis=-1, keepdims=True) + EPS)

    @pl.loop(0, tm // FFN_ROWS)
    def _(r):
        rows = pl.ds(pl.multiple_of(r * FFN_ROWS, FFN_ROWS), FFN_ROWS)
        hx = (xp_ref[rows, :] * inv_ref[rows, :]) * nrm_ref[...] * (1.0 + scp[...]) + shp[...]
        h_ref[rows, :] = hx.astype(BF16)
        op_ref[rows, :] = jnp.zeros((FFN_ROWS, D_MODEL), F32)

    h_ref[pl.ds(tm, ts), :] = head(xs_ref[...], scs[...], shs[...])
    os_ref[...] = jnp.zeros((ts, D_MODEL), F32)

    @pl.loop(0, n_f)
    def _(f):
        g_idx = i * n_f + f
        slot = lax.rem(g_idx, FFN_NB)
        wwait(f, slot)
        nxt = f + (FFN_NB - 1)
        nslot = lax.rem(g_idx + (FFN_NB - 1), FFN_NB)

        @pl.when(nxt < n_f)
        def _():
            wstart(nxt, nslot)

        @pl.when(jnp.logical_and(nxt >= n_f, i < n_tiles - 1))
        def _():
            wstart(nxt - n_f, nslot)

        h = h_ref[...]
        g = _bdot(h, wgb[slot].astype(BF16))
        u = _bdot(h, wub[slot].astype(BF16))
        a = ((g * _sigmoid(g)) * u).astype(BF16)
        for half in range(2):
            hcols = pl.ds(half * (D_MODEL // 2), D_MODEL // 2)
            y = _bdot(a, wdb[slot, :, hcols].astype(BF16))
            op_ref[:, hcols] += y[:tm]
            os_ref[:, hcols] += y[tm:]

    @pl.loop(0, tm // FFN_ROWS)
    def _(r):
        rows = pl.ds(pl.multiple_of(r * FFN_ROWS, FFN_ROWS), FFN_ROWS)
        op_ref[rows, :] = tail(xp_ref[rows, :], gtp[...], op_ref[rows, :])

    os_ref[...] = tail(xs_ref[...], gts[...], os_ref[...])


def _ffn(xp, xs, mod_p3, mod_s, j_sh, j_sc, j_g, nrm, wg, wu, wd, nf, *, tm, ts, tps, final_norm, name):
    n_tiles = xp.shape[0] // tm
    assert xs.shape[0] == n_tiles * ts
    n_f = D_FF // FFN_TF
    assert n_f >= FFN_NB
    any_spec = pl.BlockSpec(memory_space=pl.ANY)

    def pspec(j):
        return pl.BlockSpec((None, 1, D_MODEL), lambda i: (i // tps, 0, j))

    def sspec(j):
        return pl.BlockSpec((ts, D_MODEL), lambda i: (i, j))

    return pl.pallas_call(
        functools.partial(_ffn_kernel, n_f=n_f, n_tiles=n_tiles, final_norm=final_norm),
        grid=(n_tiles,),
        in_specs=[
            pl.BlockSpec((tm, D_MODEL), lambda i: (i, 0)),
            pl.BlockSpec((ts, D_MODEL), lambda i: (i, 0)),
            pspec(j_sh), pspec(j_sc), pspec(j_g),
            sspec(j_sh), sspec(j_sc), sspec(j_g),
            pl.BlockSpec((1, D_MODEL), lambda i: (0, 0)),
            any_spec, any_spec, any_spec,
            pl.BlockSpec((1, D_MODEL), lambda i: (0, 0)),
        ],
        out_specs=[pl.BlockSpec((tm, D_MODEL), lambda i: (i, 0)),
                   pl.BlockSpec((ts, D_MODEL), lambda i: (i, 0))],
        out_shape=[jax.ShapeDtypeStruct(xp.shape, F32), jax.ShapeDtypeStruct(xs.shape, F32)],
        scratch_shapes=[pltpu.VMEM((tm + ts, D_MODEL), BF16), pltpu.VMEM((tm, 1), F32),
                        pltpu.VMEM((FFN_NB, D_MODEL, FFN_TF), F32), pltpu.VMEM((FFN_NB, D_MODEL, FFN_TF), F32),
                        pltpu.VMEM((FFN_NB, FFN_TF, D_MODEL), F32), pltpu.SemaphoreType.DMA((FFN_NB,))],
        compiler_params=pltpu.CompilerParams(
            dimension_semantics=("arbitrary",), vmem_limit_bytes=VMEM_LIMIT_BYTES),
        name=name,
    )(xp, xs, mod_p3, mod_p3, mod_p3, mod_s, mod_s, mod_s, nrm, wg, wu, wd, nf)


K_P, K_CC, K_CH, K_CB, K_GA, K_GB, K_AB, K_O = range(8)
_N_CONV_CHUNKS = CONV_WIDTH // MIX_CH
_N_OUT_CHUNKS = D_MODEL // MIX_CH
_S_CONV = POOL_WIDTH // MIX_CH
_S_GATE = _S_CONV + 3 * _N_CONV_CHUNKS
_S_OUT = _S_GATE + 3 * _N_OUT_CHUNKS
_N_STAGES = _S_OUT + _N_OUT_CHUNKS
_KIND_COL0 = (COL_P, COL_CC, COL_CH, COL_CB, COL_GA, COL_GB, 0, 0)
assert _N_STAGES % MIX_NB == 0


def _decode_stage(t):
    u1 = t - _S_CONV
    u2 = t - _S_GATE
    c1 = lax.div(jnp.maximum(u1, 0), 3)
    c2 = lax.div(jnp.maximum(u2, 0), 3)
    kind = jnp.where(t < _S_CONV, K_P,
                     jnp.where(t < _S_GATE, K_CC + (u1 - 3 * c1),
                               jnp.where(t < _S_OUT, K_GA + (u2 - 3 * c2), K_O)))
    chunk = jnp.where(t < _S_CONV, t, jnp.where(t < _S_GATE, c1, jnp.where(t < _S_OUT, c2, t - _S_OUT)))
    col0 = jnp.int32(0)
    for k, base in enumerate(_KIND_COL0):
        if base:
            col0 = jnp.where(kind == k, base, col0)
    return kind, chunk, col0 + chunk * MIX_CH


def _mixer_kernel(x1p_hbm, x1s, x1pc, x1sc, shp, scp, g2pc, shs, scs, g2sc, nrm, grp, pscale, cw, cbias,
                  stp, stc, w_in, w_a, w_b, w_o,
                  x2p, x2s, pst, cst, nps, ncs,
                  wbuf, wsem, xbuf, xsem, h2, ap, v, m, zb, ybuf, sa, abf, e0, e1, qext, pcar, qcar,
                  *, tm, ts, tps, n_tiles):
    i = pl.program_id(0)
    oc = pl.program_id(1)
    slab = D_MODEL // MIX_DMA_SPLIT
    half = D_MODEL // 2
    prows = pl.ds(0, tm)
    srows = pl.ds(tm, ts)

    def wcopy(src, src_row, col, dst_slot, dst_row):
        return pltpu.make_async_copy(src.at[pl.ds(src_row, slab), pl.ds(col, MIX_CH)],
                                     wbuf.at[dst_slot, pl.ds(dst_row, slab), :], wsem.at[dst_slot])

    def wstart(t):
        k_t, _, col_t = _decode_stage(t)
        col_t = pl.multiple_of(col_t, MIX_CH)
        slot_t = lax.rem(t, MIX_NB)

        @pl.when(k_t < K_AB)
        def _():
            for k in range(MIX_DMA_SPLIT):
                wcopy(w_in, k * slab, col_t, slot_t, k * slab).start()

        @pl.when(k_t == K_AB)
        def _():
            for k in range(MIX_DMA_SPLIT):
                r0 = k * slab
                src, s0 = (w_a, r0) if r0 < half else (w_b, r0 - half)
                wcopy(src, s0, col_t, slot_t, r0).start()

        @pl.when(k_t == K_O)
        def _():
            for k in range(MIX_DMA_SPLIT):
                wcopy(w_o, k * slab, col_t, slot_t, k * slab).start()

    def wwait(slot):
        for k in range(MIX_DMA_SPLIT):
            wcopy(w_in, k * slab, 0, slot, k * slab).wait()

    n_xchunks = tm // X_ROWS

    def xcopy(tile, r):
        return pltpu.make_async_copy(x1p_hbm.at[pl.ds(tile * tm + r * X_ROWS, X_ROWS), :],
                                     xbuf.at[r % X_BUFS], xsem.at[r % X_BUFS])

    def xprefetch(tile):
        for r in range(min(X_BUFS, n_xchunks)):
            xcopy(tile, r).start()

    def prologue():
        for r in range(n_xchunks):
            xcopy(i, r).wait()
            h2[pl.ds(r * X_ROWS, X_ROWS), :] = _modulate(xbuf[r % X_BUFS], nrm[...], scp[...], shp[...]).astype(BF16)
            if r + X_BUFS < n_xchunks:
                xcopy(i, r + X_BUFS).start()
        h2[srows, :] = _modulate(x1s[...], nrm[...], scs[...], shs[...]).astype(BF16)
        for r in range(POOL_STATE - 1):
            nps[r] = stp[r + 1]
        for r in range(CONV_STATE - 1):
            ncs[r] = stc[r + 1]

    def pool_stage(c):
        zbuf = zb.at[0]
        for gl in range(MIX_CH // POOL_GROUP_DIM):
            g = c * (MIX_CH // POOL_GROUP_DIM) + gl
            w = POOL_WINDOWS[g]
            gcols = pl.ds(g * POOL_GROUP_DIM, POOL_GROUP_DIM)
            lcols = pl.ds(gl * POOL_GROUP_DIM, POOL_GROUP_DIM)
            pg = zbuf[prows, lcols]
            pad = jnp.zeros((SUBLANES, POOL_GROUP_DIM), F32)
            e0[pl.ds(SUBLANES, SUBLANES), :] = pad
            e1[pl.ds(SUBLANES, SUBLANES), :] = pad
            e0[pl.ds(CARRY_P, CARRY_P), :] = pcar[:, gcols]
            e0[pl.ds(2 * CARRY_P, tm), :] = pg
            src, dst = e0, e1
            d = 1
            while d < w:
                dst[pl.ds(CARRY_P, tm + CARRY_P), :] = (src[pl.ds(CARRY_P, tm + CARRY_P), :]
                                                        + src[pl.ds(CARRY_P - d, tm + CARRY_P), :])
                src, dst = dst, src
                d *= 2
            wsum = src[pl.ds(2 * CARRY_P, tm), :]
            pos = (i % tps) * tm + lax.broadcasted_iota(jnp.int32, (CARRY_P, 1), 0)
            cnt = jnp.minimum(pos + 1, w).astype(F32)
            abf[prows, :] = (wsum * (1.0 / w) - pg).astype(BF16)
            head = pl.ds(0, CARRY_P)
            abf[head, :] = (wsum[:CARRY_P, :] / cnt - pg[:CARRY_P, :]).astype(BF16)
            tail = pg[tm - CARRY_P:, :]
            pcar[:, gcols] = tail
            pst[:, gcols] = tail
            ps = zbuf[srows, lcols]
            ssum = ps
            for r in range(POOL_STATE + 1 - w, POOL_STATE):
                ssum = ssum + stp[r, :, gcols]
            abf[srows, :] = (ssum / float(min(PAST_LEN + 1, w)) - ps).astype(BF16)
            nps[POOL_STATE - 1, :, gcols] = ps
            ag = _bdot(abf[...], grp[g].astype(BF16)) * pscale[:, gcols]
            ap[g] = ag.astype(BF16)

    def conv_stage(c):
        ccols = pl.ds(c * MIX_CH, MIX_CH)
        zbuf, zprev = zb.at[0], zb.at[1]
        qp = zprev[prows, :] * zbuf[prows, :]
        qext[pl.ds(0, CARRY_Q), :] = qcar[:, ccols]
        qext[pl.ds(CARRY_Q, tm), :] = qp
        y = cbias[:, ccols] + qext[pl.ds(CARRY_Q - 2, tm), :] * cw[0:1, ccols]
        y = y + qext[pl.ds(CARRY_Q - 1, tm), :] * cw[1:2, ccols]
        tail = qp[tm - CARRY_Q:, :]
        qcar[:, ccols] = tail
        cst[:, ccols] = tail
        qs = zprev[srows, :] * zbuf[srows, :]
        ys = cbias[:, ccols] + stc[0, :, ccols] * cw[0:1, ccols]
        ys = ys + stc[1, :, ccols] * cw[1:2, ccols]
        ncs[CONV_STATE - 1, :, ccols] = qs
        ybuf[prows, :] = y + qp * cw[2:3, ccols]
        ybuf[srows, :] = ys + qs * cw[2:3, ccols]

    def main_stage(s):
        kind, chunk, _ = _decode_stage(s)
        slot = lax.rem(s, MIX_NB)
        wwait(slot)
        wstart(s + (MIX_NB - 1))

        keep = jnp.logical_or(kind == K_CC, kind == K_GA)

        @pl.when(keep)
        def _():
            zb[1] = _bdot(h2[...], wbuf[slot].astype(BF16))

        @pl.when(kind == K_GB)
        def _():
            sa[...] = _sigmoid(zb[1])
            zb[0] = _bdot(h2[...], wbuf[slot].astype(BF16))

        @pl.when(jnp.logical_and(kind < K_GA, jnp.logical_not(keep)))
        def _():
            zb[0] = _bdot(h2[...], wbuf[slot].astype(BF16))

        for c in range(POOL_WIDTH // MIX_CH):
            @pl.when(jnp.logical_and(kind == K_P, chunk == c))
            def _(c=c):
                pool_stage(c)

        for c in range(_N_CONV_CHUNKS):
            @pl.when(jnp.logical_and(kind == K_CH, chunk == c))
            def _(c=c):
                conv_stage(c)

        @pl.when(kind == K_CB)
        def _():
            v[chunk] = (zb[0] * ybuf[...]).astype(BF16)

        @pl.when(kind == K_AB)
        def _():
            sb = _sigmoid(zb[0])
            for hc in range(MIX_CH // MIX_HALF):
                hcols = pl.ds(hc * MIX_HALF, MIX_HALF)
                ua = None
                for g in range(len(POOL_WINDOWS)):
                    part = _bdot(ap[g], wbuf[slot, pl.ds(g * POOL_GROUP_DIM, POOL_GROUP_DIM), hcols].astype(BF16))
                    ua = part if ua is None else ua + part
                ub = None
                for k in range(_N_CONV_CHUNKS):
                    part = _bdot(v[k], wbuf[slot, pl.ds(half + k * MIX_CH, MIX_CH), hcols].astype(BF16))
                    ub = part if ub is None else ub + part
                sbh = sb[:, hc * MIX_HALF:(hc + 1) * MIX_HALF]
                m[chunk, :, hcols] = (sa[:, hcols] * ua + sbh * ub).astype(BF16)

    assert _S_OUT + MIX_NB - 2 < _N_STAGES

    @pl.when(oc == 0)
    def _():
        @pl.when(i == 0)
        def _():
            for t in range(MIX_NB - 1):
                wstart(jnp.int32(t))
            xprefetch(0)
            zb[...] = jnp.zeros(zb.shape, F32)

        @pl.when(i % tps == 0)
        def _():
            pcar[...] = jnp.zeros(pcar.shape, F32)
            qcar[...] = jnp.zeros(qcar.shape, F32)

        prologue()
        pl.loop(0, _S_OUT)(main_stage)

    s_out = _S_OUT + oc
    slot = lax.rem(s_out, MIX_NB)
    wwait(slot)
    nxt = s_out + (MIX_NB - 1)

    @pl.when(nxt < _N_STAGES)
    def _():
        wstart(nxt)

    @pl.when(jnp.logical_and(nxt >= _N_STAGES, i < n_tiles - 1))
    def _():
        wstart(nxt - _N_STAGES)

    @pl.when(jnp.logical_and(oc == 1, i < n_tiles - 1))
    def _():
        xprefetch(i + 1)

    for hc in range(MIX_CH // MIX_HALF):
        hcols = pl.ds(hc * MIX_HALF, MIX_HALF)
        acc = None
        for k in range(_N_OUT_CHUNKS):
            part = _bdot(m[k], wbuf[slot, pl.ds(k * MIX_CH, MIX_CH), hcols].astype(BF16))
            acc = part if acc is None else acc + part
        x2p[:, hcols] = x1pc[:, hcols] + g2pc[:, hcols] * acc[:tm]
        x2s[:, hcols] = x1sc[:, hcols] + g2sc[:, hcols] * acc[tm:]


def _mixer(x1p, x1s, mod_p3, mod_s, nrm, grp, pscale, cw, cbias, stp_t, stc_t, w_in, w_a, w_b, w_o, *, tm, ts, tps):
    n_tiles = x1p.shape[0] // tm
    n_seq = n_tiles // tps
    n_s = x1s.shape[0]
    assert n_s == n_tiles * ts
    rows = tm + ts
    any_spec = pl.BlockSpec(memory_space=pl.ANY)

    def const(shape):
        return pl.BlockSpec(shape, lambda i, s: (0,) * len(shape))

    def oc(s):
        return s

    g2_blk = (MOD_G2 * D_MODEL) // MIX_CH
    in_specs = [
        any_spec,
        pl.BlockSpec((ts, D_MODEL), lambda i, s: (i, 0)),
        pl.BlockSpec((tm, MIX_CH), lambda i, s: (i, oc(s))),
        pl.BlockSpec((ts, MIX_CH), lambda i, s: (i, oc(s))),
        pl.BlockSpec((None, 1, D_MODEL), lambda i, s: (i // tps, 0, MOD_SH2)),
        pl.BlockSpec((None, 1, D_MODEL), lambda i, s: (i // tps, 0, MOD_SC2)),
        pl.BlockSpec((None, 1, MIX_CH), lambda i, s: (i // tps, 0, g2_blk + oc(s))),
        pl.BlockSpec((ts, D_MODEL), lambda i, s: (i, MOD_SH2)),
        pl.BlockSpec((ts, D_MODEL), lambda i, s: (i, MOD_SC2)),
        pl.BlockSpec((ts, MIX_CH), lambda i, s: (i, g2_blk + oc(s))),
        const((1, D_MODEL)),
        const((len(POOL_WINDOWS), POOL_GROUP_DIM, POOL_GROUP_DIM)),
        const((1, POOL_WIDTH)),
        const((3, CONV_WIDTH)),
        const((1, CONV_WIDTH)),
        pl.BlockSpec((POOL_STATE, ts, POOL_WIDTH), lambda i, s: (0, i, 0)),
        pl.BlockSpec((CONV_STATE, ts, CONV_WIDTH), lambda i, s: (0, i, 0)),
        any_spec, any_spec, any_spec, any_spec,
    ]
    out_specs = [
        pl.BlockSpec((tm, MIX_CH), lambda i, s: (i, oc(s))),
        pl.BlockSpec((ts, MIX_CH), lambda i, s: (i, oc(s))),
        pl.BlockSpec((None, CARRY_P, POOL_WIDTH), lambda i, s: (i // tps, 0, 0)),
        pl.BlockSpec((None, CARRY_Q, CONV_WIDTH), lambda i, s: (i // tps, 0, 0)),
        pl.BlockSpec((POOL_STATE, ts, POOL_WIDTH), lambda i, s: (0, i, 0)),
        pl.BlockSpec((CONV_STATE, ts, CONV_WIDTH), lambda i, s: (0, i, 0)),
    ]
    out_shape = [
        jax.ShapeDtypeStruct(x1p.shape, F32),
        jax.ShapeDtypeStruct(x1s.shape, F32),
        jax.ShapeDtypeStruct((n_seq, CARRY_P, POOL_WIDTH), F32),
        jax.ShapeDtypeStruct((n_seq, CARRY_Q, CONV_WIDTH), F32),
        jax.ShapeDtypeStruct((POOL_STATE, n_s, POOL_WIDTH), F32),
        jax.ShapeDtypeStruct((CONV_STATE, n_s, CONV_WIDTH), F32),
    ]
    scratch = [
        pltpu.VMEM((MIX_NB, D_MODEL, MIX_CH), F32),
        pltpu.SemaphoreType.DMA((MIX_NB,)),
        pltpu.VMEM((X_BUFS, X_ROWS, D_MODEL), F32),
        pltpu.SemaphoreType.DMA((X_BUFS,)),
        pltpu.VMEM((rows, D_MODEL), BF16),
        pltpu.VMEM((len(POOL_WINDOWS), rows, POOL_GROUP_DIM), BF16),
        pltpu.VMEM((_N_CONV_CHUNKS, rows, MIX_CH), BF16),
        pltpu.VMEM((_N_OUT_CHUNKS, rows, MIX_CH), BF16),
        pltpu.VMEM((2, rows, MIX_CH), F32),
        pltpu.VMEM((rows, MIX_CH), F32),
        pltpu.VMEM((rows, MIX_CH), F32),
        pltpu.VMEM((rows, POOL_GROUP_DIM), BF16),
        pltpu.VMEM((tm + 2 * CARRY_P, POOL_GROUP_DIM), F32),
        pltpu.VMEM((tm + 2 * CARRY_P, POOL_GROUP_DIM), F32),
        pltpu.VMEM((tm + CARRY_Q, MIX_CH), F32),
        pltpu.VMEM((CARRY_P, POOL_WIDTH), F32),
        pltpu.VMEM((CARRY_Q, CONV_WIDTH), F32),
    ]
    return pl.pallas_call(
        functools.partial(_mixer_kernel, tm=tm, ts=ts, tps=tps, n_tiles=n_tiles),
        grid=(n_tiles, _N_OUT_CHUNKS),
        in_specs=in_specs,
        out_specs=out_specs,
        out_shape=out_shape,
        scratch_shapes=scratch,
        compiler_params=pltpu.CompilerParams(
            dimension_semantics=("arbitrary", "arbitrary"), vmem_limit_bytes=VMEM_LIMIT_BYTES),
        name="mixer",
    )(x1p, x1s, x1p, x1s, mod_p3, mod_p3, mod_p3, mod_s, mod_s, mod_s, nrm, grp, pscale, cw, cbias,
      stp_t, stc_t, w_in, w_a, w_b, w_o)


def kernel(x_prompt, x_sample, c_prompt, c_sample, state_pool, state_conv, w_ada, b_ada, norm1, ffn1_gate, ffn1_up, ffn1_down, norm2, w_in, pool_grp, pool_scale, w_branch_a, conv_w, conv_b, w_branch_b, w_o, norm3, ffn2_gate, ffn2_up, ffn2_down, norm_final):
    assert w_ada.shape[0] == 1, "single-layer step"
    batch, seq, d = x_prompt.shape
    dec_batch = x_sample.shape[0]
    assert d == D_MODEL and x_sample.shape[1] == 1

    tm = TM_PROMPT
    tps = seq // tm
    n_tiles = batch * tps
    ts = dec_batch // n_tiles
    assert seq % tm == 0 and dec_batch % n_tiles == 0 and ts % (2 * SUBLANES) == 0

    n_c = batch + dec_batch
    pad = (-n_c) % SUBLANES
    c_all = jnp.concatenate([c_sample, c_prompt, jnp.zeros((pad, d), F32)], axis=0)
    mod = _ada(c_all, w_ada[0], b_ada)
    mod_p3 = mod[dec_batch:n_c].reshape(batch, 1, N_MOD * d)

    row = lambda a: a.reshape(1, -1)
    xp = x_prompt.reshape(batch * seq, d)
    xs = x_sample.reshape(dec_batch, d)
    nf = row(norm_final)
    x1p, x1s = _ffn(xp, xs, mod_p3, mod, MOD_SH1, MOD_SC1, MOD_G1, row(norm1[0]), ffn1_gate[0], ffn1_up[0],
                    ffn1_down[0], nf, tm=tm, ts=ts, tps=tps, final_norm=False, name="ffn1")
    stp_t = jnp.transpose(state_pool[0], (1, 0, 2))
    stc_t = jnp.transpose(state_conv[0], (1, 0, 2))
    x2p, x2s, pst, cst, nps_t, ncs_t = _mixer(
        x1p, x1s, mod_p3, mod, row(norm2[0]), pool_grp[0], row(pool_scale[0]), conv_w[0], row(conv_b[0]),
        stp_t, stc_t, w_in[0], w_branch_a[0], w_branch_b[0], w_o[0], tm=tm, ts=ts, tps=tps)
    yp, ys = _ffn(x2p, x2s, mod_p3, mod, MOD_SH3, MOD_SC3, MOD_G3, row(norm3[0]), ffn2_gate[0], ffn2_up[0],
                  ffn2_down[0], nf, tm=tm, ts=ts, tps=tps, final_norm=True, name="ffn2")

    y_prompt = yp.reshape(batch, seq, d)
    y_sample = ys.reshape(dec_batch, 1, d)
    new_pool_prompt = pst[None, :, CARRY_P - POOL_STATE:, :]
    new_conv_prompt = cst[None, :, CARRY_Q - CONV_STATE:, :]
    new_pool_sample = jnp.transpose(nps_t, (1, 0, 2))[None]
    new_conv_sample = jnp.transpose(ncs_t, (1, 0, 2))[None]
    return (y_prompt, y_sample, new_pool_prompt, new_conv_prompt, new_pool_sample, new_conv_sample)
```

```python
import functools

import jax
import jax.numpy as jnp
from jax import lax
from jax.experimental import pallas as pl
from jax.experimental.pallas import tpu as pltpu

F32 = jnp.float32
BF16 = jnp.bfloat16

D_MODEL = 2048
D_FF = 5632
POOL_WIDTH = 1024
CONV_WIDTH = 1024
POOL_WINDOWS = (2, 4, 8, 16)
POOL_GROUP_DIM = 256
POOL_STATE = 15
CONV_STATE = 2
N_MOD = 9
EPS = 1e-6
PAST_LEN = 16384

V7X_VMEM_BYTES = 64 * 1024 * 1024
SUBLANES = 8
VMEM_LIMIT_BYTES = V7X_VMEM_BYTES - 2 * 1024 * 1024

TM_PROMPT = 1024

ADA_TN = 1024
ADA_NB = 3
FFN_TF = 256
FFN_DMA_SPLIT = 1
FFN_NB = 3
MIX_CH = 512
MIX_HALF = 256
MIX_NB = 3
X_ROWS = 128
FFN_ROWS = 128
X_BUFS = 4
MIX_DMA_SPLIT = 2
CARRY_P = 16
CARRY_Q = 8

COL_P, COL_CB, COL_CC, COL_CH, COL_GA, COL_GB = 0, 1024, 2048, 3072, 4096, 6144
MOD_SH1, MOD_SC1, MOD_G1, MOD_SH2, MOD_SC2, MOD_G2, MOD_SH3, MOD_SC3, MOD_G3 = range(9)


def _sigmoid(x):
    return 1.0 / (1.0 + jnp.exp(-x))


def _modulate(x, nrm, sc, sh):
    inv = lax.rsqrt(jnp.mean(x * x, axis=-1, keepdims=True) + EPS)
    return (x * inv) * nrm * (1.0 + sc) + sh


def _bdot(a, b):
    return jnp.dot(a, b, preferred_element_type=F32)


def _ada_kernel(c_ref, b_ref, w_hbm, o_hbm, s_ref, wbuf, obuf, wsem, osem, *, n_chunks):
    c = c_ref[...]
    s_ref[...] = (c * _sigmoid(c)).astype(BF16)

    def wcopy(j, slot):
        return pltpu.make_async_copy(w_hbm.at[:, pl.ds(pl.multiple_of(j * ADA_TN, ADA_TN), ADA_TN)],
                                     wbuf.at[slot], wsem.at[slot])

    def ocopy(j, slot):
        return pltpu.make_async_copy(obuf.at[slot],
                                     o_hbm.at[:, pl.ds(pl.multiple_of(j * ADA_TN, ADA_TN), ADA_TN)], osem.at[slot])

    for t in range(ADA_NB - 1):
        wcopy(jnp.int32(t), t).start()

    @pl.loop(0, n_chunks)
    def _(j):
        slot = lax.rem(j, ADA_NB)
        oslot = lax.rem(j, 2)
        wcopy(j, slot).wait()

        @pl.when(j + (ADA_NB - 1) < n_chunks)
        def _():
            wcopy(j + (ADA_NB - 1), lax.rem(j + (ADA_NB - 1), ADA_NB)).start()

        @pl.when(j >= 2)
        def _():
            ocopy(j - 2, oslot).wait()

        obuf[oslot] = _bdot(s_ref[...], wbuf[slot].astype(BF16)) + b_ref[j]
        ocopy(j, oslot).start()

    for t in range(2):
        j_last = n_chunks - 2 + t
        ocopy(jnp.int32(j_last), j_last % 2).wait()


def _ada(c_all, w_ada, b_ada):
    rows = c_all.shape[0]
    cols = w_ada.shape[1]
    n_chunks = cols // ADA_TN
    assert n_chunks >= ADA_NB and n_chunks >= 2
    return pl.pallas_call(
        functools.partial(_ada_kernel, n_chunks=n_chunks),
        grid=(1,),
        in_specs=[
            pl.BlockSpec((rows, D_MODEL), lambda g: (0, 0)),
            pl.BlockSpec((n_chunks, 1, ADA_TN), lambda g: (0, 0, 0)),
            pl.BlockSpec(memory_space=pl.ANY),
        ],
        out_specs=pl.BlockSpec(memory_space=pl.ANY),
        out_shape=jax.ShapeDtypeStruct((rows, cols), F32),
        scratch_shapes=[pltpu.VMEM((rows, D_MODEL), BF16),
                        pltpu.VMEM((ADA_NB, D_MODEL, ADA_TN), F32),
                        pltpu.VMEM((2, rows, ADA_TN), F32),
                        pltpu.SemaphoreType.DMA((ADA_NB,)), pltpu.SemaphoreType.DMA((2,))],
        compiler_params=pltpu.CompilerParams(dimension_semantics=("arbitrary",)),
        name="ada_mod",
    )(c_all, b_ada.reshape(n_chunks, 1, ADA_TN), w_ada)


def _ffn_kernel(xp_ref, xs_ref, shp, scp, gtp, shs, scs, gts, nrm_ref, wg_hbm, wu_hbm, wd_hbm, nf_ref,
                op_ref, os_ref, h_ref, inv_ref, wgb, wub, wdb, wsem, *, n_f, n_tiles, final_norm):
    i = pl.program_id(0)
    tm = xp_ref.shape[0]
    ts = xs_ref.shape[0]

    def wcopies(f, slot):
        col = pl.multiple_of(f * FFN_TF, FFN_TF)
        rows_in = D_MODEL // FFN_DMA_SPLIT
        rows_dn = FFN_TF // FFN_DMA_SPLIT
        out = []
        for k in range(FFN_DMA_SPLIT):
            r_in = pl.ds(k * rows_in, rows_in)
            out.append(pltpu.make_async_copy(wg_hbm.at[r_in, pl.ds(col, FFN_TF)], wgb.at[slot, r_in, :],
                                             wsem.at[slot]))
            out.append(pltpu.make_async_copy(wu_hbm.at[r_in, pl.ds(col, FFN_TF)], wub.at[slot, r_in, :],
                                             wsem.at[slot]))
            out.append(pltpu.make_async_copy(wd_hbm.at[pl.ds(col + k * rows_dn, rows_dn), :],
                                             wdb.at[slot, pl.ds(k * rows_dn, rows_dn), :], wsem.at[slot]))
        return out

    def wstart(f, slot):
        for cp in wcopies(f, slot):
            cp.start()

    def wwait(f, slot):
        for cp in wcopies(f, slot):
            cp.wait()

    @pl.when(i == 0)
    def _():
        for t in range(FFN_NB - 1):
            wstart(jnp.int32(t), t)

    def head(x, sc, sh):
        return _modulate(x, nrm_ref[...], sc, sh).astype(BF16)

    def tail(x, gt, acc):
        xo = x + 0.5 * gt * acc
        if final_norm:
            inv = lax.rsqrt(jnp.mean(xo * xo, axis=-1, keepdims=True) + EPS)
            xo = (xo * inv) * nf_ref[...]
        return xo

    @pl.loop(0, tm // FFN_ROWS)
    def _(r):
        rows = pl.ds(pl.multiple_of(r * FFN_ROWS, FFN_ROWS), FFN_ROWS)
        x = xp_ref[rows, :]
        inv_ref[rows, :] = lax.rsqrt(jnp.mean(x * x, axis=-1, keepdims=True) + EPS)

    @pl.loop(0, tm // FFN_ROWS)
    def _(r):
        rows = pl.ds(pl.multiple_of(r * FFN_ROWS, FFN_ROWS), FFN_ROWS)
        hx = (xp_ref[rows, :] * inv_ref[rows, :]) * nrm_ref[...] * (1.0 + scp[...]) + shp[...]
        h_ref[rows, :] = hx.astype(BF16)
        op_ref[rows, :] = jnp.zeros((FFN_ROWS, D_MODEL), F32)

    h_ref[pl.ds(tm, ts), :] = head(xs_ref[...], scs[...], shs[...])
    os_ref[...] = jnp.zeros((ts, D_MODEL), F32)

    @pl.loop(0, n_f)
    def _(f):
        g_idx = i * n_f + f
        slot = lax.rem(g_idx, FFN_NB)
        wwait(f, slot)
        nxt = f + (FFN_NB - 1)
        nslot = lax.rem(g_idx + (FFN_NB - 1), FFN_NB)

        @pl.when(jnp.logical_or(nxt < n_f, i < n_tiles - 1))
        def _():
            wstart(jnp.where(nxt < n_f, nxt, nxt - n_f), nslot)

        h = h_ref[...]
        g = _bdot(h, wgb[slot].astype(BF16))
        u = _bdot(h, wub[slot].astype(BF16))
        a = ((g * _sigmoid(g)) * u).astype(BF16)
        for half in range(2):
            hcols = pl.ds(half * (D_MODEL // 2), D_MODEL // 2)
            y = _bdot(a, wdb[slot, :, hcols].astype(BF16))
            op_ref[:, hcols] += y[:tm]
            os_ref[:, hcols] += y[tm:]

    @pl.loop(0, tm // FFN_ROWS)
    def _(r):
        rows = pl.ds(pl.multiple_of(r * FFN_ROWS, FFN_ROWS), FFN_ROWS)
        op_ref[rows, :] = tail(xp_ref[rows, :], gtp[...], op_ref[rows, :])

    os_ref[...] = tail(xs_ref[...], gts[...], os_ref[...])


def _ffn(xp, xs, mod_p3, mod_s, j_sh, j_sc, j_g, nrm, wg, wu, wd, nf, *, tm, ts, tps, final_norm, name):
    n_tiles = xp.shape[0] // tm
    assert xs.shape[0] == n_tiles * ts
    n_f = D_FF // FFN_TF
    assert n_f >= FFN_NB
    any_spec = pl.BlockSpec(memory_space=pl.ANY)

    def pspec(j):
        return pl.BlockSpec((None, 1, D_MODEL), lambda i: (i // tps, 0, j))

    def sspec(j):
        return pl.BlockSpec((ts, D_MODEL), lambda i: (i, j))

    return pl.pallas_call(
        functools.partial(_ffn_kernel, n_f=n_f, n_tiles=n_tiles, final_norm=final_norm),
        grid=(n_tiles,),
        in_specs=[
            pl.BlockSpec((tm, D_MODEL), lambda i: (i, 0)),
            pl.BlockSpec((ts, D_MODEL), lambda i: (i, 0)),
            pspec(j_sh), pspec(j_sc), pspec(j_g),
            sspec(j_sh), sspec(j_sc), sspec(j_g),
            pl.BlockSpec((1, D_MODEL), lambda i: (0, 0)),
            any_spec, any_spec, any_spec,
            pl.BlockSpec((1, D_MODEL), lambda i: (0, 0)),
        ],
        out_specs=[pl.BlockSpec((tm, D_MODEL), lambda i: (i, 0)),
                   pl.BlockSpec((ts, D_MODEL), lambda i: (i, 0))],
        out_shape=[jax.ShapeDtypeStruct(xp.shape, F32), jax.ShapeDtypeStruct(xs.shape, F32)],
        scratch_shapes=[pltpu.VMEM((tm + ts, D_MODEL), BF16), pltpu.VMEM((tm, 1), F32),
                        pltpu.VMEM((FFN_NB, D_MODEL, FFN_TF), F32), pltpu.VMEM((FFN_NB, D_MODEL, FFN_TF), F32),
                        pltpu.VMEM((FFN_NB, FFN_TF, D_MODEL), F32), pltpu.SemaphoreType.DMA((FFN_NB,))],
        compiler_params=pltpu.CompilerParams(
            dimension_semantics=("arbitrary",), vmem_limit_bytes=VMEM_LIMIT_BYTES),
        name=name,
    )(xp, xs, mod_p3, mod_p3, mod_p3, mod_s, mod_s, mod_s, nrm, wg, wu, wd, nf)


K_P, K_CC, K_CH, K_CB, K_GA, K_GB, K_AB, K_O = range(8)
_N_CONV_CHUNKS = CONV_WIDTH // MIX_CH
_N_OUT_CHUNKS = D_MODEL // MIX_CH
_S_CONV = POOL_WIDTH // MIX_CH
_S_GATE = _S_CONV + 3 * _N_CONV_CHUNKS
_S_OUT = _S_GATE + 3 * _N_OUT_CHUNKS
_N_STAGES = _S_OUT + _N_OUT_CHUNKS
_KIND_COL0 = (COL_P, COL_CC, COL_CH, COL_CB, COL_GA, COL_GB, 0, 0)
assert _N_STAGES % MIX_NB == 0


def _decode_stage(t):
    u1 = t - _S_CONV
    u2 = t - _S_GATE
    c1 = lax.div(jnp.maximum(u1, 0), 3)
    c2 = lax.div(jnp.maximum(u2, 0), 3)
    kind = jnp.where(t < _S_CONV, K_P,
                     jnp.where(t < _S_GATE, K_CC + (u1 - 3 * c1),
                               jnp.where(t < _S_OUT, K_GA + (u2 - 3 * c2), K_O)))
    chunk = jnp.where(t < _S_CONV, t, jnp.where(t < _S_GATE, c1, jnp.where(t < _S_OUT, c2, t - _S_OUT)))
    col0 = jnp.int32(0)
    for k, base in enumerate(_KIND_COL0):
        if base:
            col0 = jnp.where(kind == k, base, col0)
    return kind, chunk, col0 + chunk * MIX_CH


def _mixer_kernel(x1p_hbm, x1s, x1pc, x1sc, shp, scp, g2pc, shs, scs, g2sc, nrm, grp, pscale, cw, cbias,
                  stp, stc, w_in, w_a, w_b, w_o,
                  x2p, x2s, pst, cst, nps, ncs,
                  wbuf, wsem, xbuf, xsem, h2, ap, v, m, zb, ybuf, sa, abf, e0, e1, qext, pcar, qcar,
                  *, tm, ts, tps, n_tiles):
    i = pl.program_id(0)
    oc = pl.program_id(1)
    slab = D_MODEL // MIX_DMA_SPLIT
    half = D_MODEL // 2
    prows = pl.ds(0, tm)
    srows = pl.ds(tm, ts)

    def wcopy(src, src_row, col, dst_slot, dst_row):
        return pltpu.make_async_copy(src.at[pl.ds(src_row, slab), pl.ds(col, MIX_CH)],
                                     wbuf.at[dst_slot, pl.ds(dst_row, slab), :], wsem.at[dst_slot])

    def wstart(t):
        k_t, _, col_t = _decode_stage(t)
        col_t = pl.multiple_of(col_t, MIX_CH)
        slot_t = lax.rem(t, MIX_NB)

        @pl.when(k_t < K_AB)
        def _():
            for k in range(MIX_DMA_SPLIT):
                wcopy(w_in, k * slab, col_t, slot_t, k * slab).start()

        @pl.when(k_t == K_AB)
        def _():
            for k in range(MIX_DMA_SPLIT):
                r0 = k * slab
                src, s0 = (w_a, r0) if r0 < half else (w_b, r0 - half)
                wcopy(src, s0, col_t, slot_t, r0).start()

        @pl.when(k_t == K_O)
        def _():
            for k in range(MIX_DMA_SPLIT):
                wcopy(w_o, k * slab, col_t, slot_t, k * slab).start()

    def wwait(slot):
        for k in range(MIX_DMA_SPLIT):
            wcopy(w_in, k * slab, 0, slot, k * slab).wait()

    n_xchunks = tm // X_ROWS

    def xcopy(tile, r):
        return pltpu.make_async_copy(x1p_hbm.at[pl.ds(tile * tm + r * X_ROWS, X_ROWS), :],
                                     xbuf.at[r % X_BUFS], xsem.at[r % X_BUFS])

    def xprefetch(tile):
        for r in range(min(X_BUFS, n_xchunks)):
            xcopy(tile, r).start()

    def prologue():
        for r in range(n_xchunks):
            xcopy(i, r).wait()
            h2[pl.ds(r * X_ROWS, X_ROWS), :] = _modulate(xbuf[r % X_BUFS], nrm[...], scp[...], shp[...]).astype(BF16)
            if r + X_BUFS < n_xchunks:
                xcopy(i, r + X_BUFS).start()
        h2[srows, :] = _modulate(x1s[...], nrm[...], scs[...], shs[...]).astype(BF16)
        for r in range(POOL_STATE - 1):
            nps[r] = stp[r + 1]
        for r in range(CONV_STATE - 1):
            ncs[r] = stc[r + 1]

    def pool_stage(c):
        zbuf = zb.at[0]
        for gl in range(MIX_CH // POOL_GROUP_DIM):
            g = c * (MIX_CH // POOL_GROUP_DIM) + gl
            w = POOL_WINDOWS[g]
            gcols = pl.ds(g * POOL_GROUP_DIM, POOL_GROUP_DIM)
            lcols = pl.ds(gl * POOL_GROUP_DIM, POOL_GROUP_DIM)
            pg = zbuf[prows, lcols]
            pad = jnp.zeros((SUBLANES, POOL_GROUP_DIM), F32)
            e0[pl.ds(SUBLANES, SUBLANES), :] = pad
            e1[pl.ds(SUBLANES, SUBLANES), :] = pad
            e0[pl.ds(CARRY_P, CARRY_P), :] = pcar[:, gcols]
            e0[pl.ds(2 * CARRY_P, tm), :] = pg
            src, dst = e0, e1
            d = 1
            while d < w:
                dst[pl.ds(CARRY_P, tm + CARRY_P), :] = (src[pl.ds(CARRY_P, tm + CARRY_P), :]
                                                        + src[pl.ds(CARRY_P - d, tm + CARRY_P), :])
                src, dst = dst, src
                d *= 2
            wsum = src[pl.ds(2 * CARRY_P, tm), :]
            pos = (i % tps) * tm + lax.broadcasted_iota(jnp.int32, (CARRY_P, 1), 0)
            cnt = jnp.minimum(pos + 1, w).astype(F32)
            abf[prows, :] = (wsum * (1.0 / w) - pg).astype(BF16)
            head = pl.ds(0, CARRY_P)
            abf[head, :] = (wsum[:CARRY_P, :] / cnt - pg[:CARRY_P, :]).astype(BF16)
            tail = pg[tm - CARRY_P:, :]
            pcar[:, gcols] = tail
            pst[:, gcols] = tail
            ps = zbuf[srows, lcols]
            ssum = ps
            for r in range(POOL_STATE + 1 - w, POOL_STATE):
                ssum = ssum + stp[r, :, gcols]
            abf[srows, :] = (ssum / float(min(PAST_LEN + 1, w)) - ps).astype(BF16)
            nps[POOL_STATE - 1, :, gcols] = ps
            ag = _bdot(abf[...], grp[g].astype(BF16)) * pscale[:, gcols]
            ap[g] = ag.astype(BF16)

    def conv_stage(c):
        ccols = pl.ds(c * MIX_CH, MIX_CH)
        zbuf, zprev = zb.at[0], zb.at[1]
        qp = zprev[prows, :] * zbuf[prows, :]
        qext[pl.ds(0, CARRY_Q), :] = qcar[:, ccols]
        qext[pl.ds(CARRY_Q, tm), :] = qp
        y = cbias[:, ccols] + qext[pl.ds(CARRY_Q - 2, tm), :] * cw[0:1, ccols]
        y = y + qext[pl.ds(CARRY_Q - 1, tm), :] * cw[1:2, ccols]
        tail = qp[tm - CARRY_Q:, :]
        qcar[:, ccols] = tail
        cst[:, ccols] = tail
        qs = zprev[srows, :] * zbuf[srows, :]
        ys = cbias[:, ccols] + stc[0, :, ccols] * cw[0:1, ccols]
        ys = ys + stc[1, :, ccols] * cw[1:2, ccols]
        ncs[CONV_STATE - 1, :, ccols] = qs
        ybuf[prows, :] = y + qp * cw[2:3, ccols]
        ybuf[srows, :] = ys + qs * cw[2:3, ccols]

    def main_stage(s):
        kind, chunk, _ = _decode_stage(s)
        slot = lax.rem(s, MIX_NB)
        wwait(slot)
        wstart(s + (MIX_NB - 1))

        keep = jnp.logical_or(kind == K_CC, kind == K_GA)

        @pl.when(keep)
        def _():
            zb[1] = _bdot(h2[...], wbuf[slot].astype(BF16))

        @pl.when(kind == K_GB)
        def _():
            sa[...] = _sigmoid(zb[1])
            zb[0] = _bdot(h2[...], wbuf[slot].astype(BF16))

        @pl.when(jnp.logical_and(kind < K_GA, jnp.logical_not(keep)))
        def _():
            zb[0] = _bdot(h2[...], wbuf[slot].astype(BF16))

        for c in range(POOL_WIDTH // MIX_CH):
            @pl.when(jnp.logical_and(kind == K_P, chunk == c))
            def _(c=c):
                pool_stage(c)

        for c in range(_N_CONV_CHUNKS):
            @pl.when(jnp.logical_and(kind == K_CH, chunk == c))
            def _(c=c):
                conv_stage(c)

        @pl.when(kind == K_CB)
        def _():
            v[chunk] = (zb[0] * ybuf[...]).astype(BF16)

        @pl.when(kind == K_AB)
        def _():
            sb = _sigmoid(zb[0])
            for hc in range(MIX_CH // MIX_HALF):
                hcols = pl.ds(hc * MIX_HALF, MIX_HALF)
                ua = None
                for g in range(len(POOL_WINDOWS)):
                    part = _bdot(ap[g], wbuf[slot, pl.ds(g * POOL_GROUP_DIM, POOL_GROUP_DIM), hcols].astype(BF16))
                    ua = part if ua is None else ua + part
                ub = None
                for k in range(_N_CONV_CHUNKS):
                    part = _bdot(v[k], wbuf[slot, pl.ds(half + k * MIX_CH, MIX_CH), hcols].astype(BF16))
                    ub = part if ub is None else ub + part
                sbh = sb[:, hc * MIX_HALF:(hc + 1) * MIX_HALF]
                m[chunk, :, hcols] = (sa[:, hcols] * ua + sbh * ub).astype(BF16)

    assert _S_OUT + MIX_NB - 2 < _N_STAGES

    @pl.when(oc == 0)
    def _():
        @pl.when(i == 0)
        def _():
            for t in range(MIX_NB - 1):
                wstart(jnp.int32(t))
            xprefetch(0)

        @pl.when(i % tps == 0)
        def _():
            pcar[...] = jnp.zeros(pcar.shape, F32)
            qcar[...] = jnp.zeros(qcar.shape, F32)

        prologue()
        pl.loop(0, _S_OUT)(main_stage)

    s_out = _S_OUT + oc
    slot = lax.rem(s_out, MIX_NB)
    wwait(slot)
    nxt = s_out + (MIX_NB - 1)

    @pl.when(jnp.logical_or(nxt < _N_STAGES, i < n_tiles - 1))
    def _():
        wstart(jnp.where(nxt < _N_STAGES, nxt, nxt - _N_STAGES))

    @pl.when(jnp.logical_and(oc == 1, i < n_tiles - 1))
    def _():
        xprefetch(i + 1)

    for hc in range(MIX_CH // MIX_HALF):
        hcols = pl.ds(hc * MIX_HALF, MIX_HALF)
        acc = None
        for k in range(_N_OUT_CHUNKS):
            part = _bdot(m[k], wbuf[slot, pl.ds(k * MIX_CH, MIX_CH), hcols].astype(BF16))
            acc = part if acc is None else acc + part
        x2p[:, hcols] = x1pc[:, hcols] + g2pc[:, hcols] * acc[:tm]
        x2s[:, hcols] = x1sc[:, hcols] + g2sc[:, hcols] * acc[tm:]


def _mixer(x1p, x1s, mod_p3, mod_s, nrm, grp, pscale, cw, cbias, stp_t, stc_t, w_in, w_a, w_b, w_o, *, tm, ts, tps):
    n_tiles = x1p.shape[0] // tm
    n_seq = n_tiles // tps
    n_s = x1s.shape[0]
    assert n_s == n_tiles * ts
    rows = tm + ts
    any_spec = pl.BlockSpec(memory_space=pl.ANY)

    def const(shape):
        return pl.BlockSpec(shape, lambda i, s: (0,) * len(shape))

    def oc(s):
        return s

    g2_blk = (MOD_G2 * D_MODEL) // MIX_CH
    in_specs = [
        any_spec,
        pl.BlockSpec((ts, D_MODEL), lambda i, s: (i, 0)),
        pl.BlockSpec((tm, MIX_CH), lambda i, s: (i, oc(s))),
        pl.BlockSpec((ts, MIX_CH), lambda i, s: (i, oc(s))),
        pl.BlockSpec((None, 1, D_MODEL), lambda i, s: (i // tps, 0, MOD_SH2)),
        pl.BlockSpec((None, 1, D_MODEL), lambda i, s: (i // tps, 0, MOD_SC2)),
        pl.BlockSpec((None, 1, MIX_CH), lambda i, s: (i // tps, 0, g2_blk + oc(s))),
        pl.BlockSpec((ts, D_MODEL), lambda i, s: (i, MOD_SH2)),
        pl.BlockSpec((ts, D_MODEL), lambda i, s: (i, MOD_SC2)),
        pl.BlockSpec((ts, MIX_CH), lambda i, s: (i, g2_blk + oc(s))),
        const((1, D_MODEL)),
        const((len(POOL_WINDOWS), POOL_GROUP_DIM, POOL_GROUP_DIM)),
        const((1, POOL_WIDTH)),
        const((3, CONV_WIDTH)),
        const((1, CONV_WIDTH)),
        pl.BlockSpec((POOL_STATE, ts, POOL_WIDTH), lambda i, s: (0, i, 0)),
        pl.BlockSpec((CONV_STATE, ts, CONV_WIDTH), lambda i, s: (0, i, 0)),
        any_spec, any_spec, any_spec, any_spec,
    ]
    out_specs = [
        pl.BlockSpec((tm, MIX_CH), lambda i, s: (i, oc(s))),
        pl.BlockSpec((ts, MIX_CH), lambda i, s: (i, oc(s))),
        pl.BlockSpec((None, CARRY_P, POOL_WIDTH), lambda i, s: (i // tps, 0, 0)),
        pl.BlockSpec((None, CARRY_Q, CONV_WIDTH), lambda i, s: (i // tps, 0, 0)),
        pl.BlockSpec((POOL_STATE, ts, POOL_WIDTH), lambda i, s: (0, i, 0)),
        pl.BlockSpec((CONV_STATE, ts, CONV_WIDTH), lambda i, s: (0, i, 0)),
    ]
    out_shape = [
        jax.ShapeDtypeStruct(x1p.shape, F32),
        jax.ShapeDtypeStruct(x1s.shape, F32),
        jax.ShapeDtypeStruct((n_seq, CARRY_P, POOL_WIDTH), F32),
        jax.ShapeDtypeStruct((n_seq, CARRY_Q, CONV_WIDTH), F32),
        jax.ShapeDtypeStruct((POOL_STATE, n_s, POOL_WIDTH), F32),
        jax.ShapeDtypeStruct((CONV_STATE, n_s, CONV_WIDTH), F32),
    ]
    scratch = [
        pltpu.VMEM((MIX_NB, D_MODEL, MIX_CH), F32),
        pltpu.SemaphoreType.DMA((MIX_NB,)),
        pltpu.VMEM((X_BUFS, X_ROWS, D_MODEL), F32),
        pltpu.SemaphoreType.DMA((X_BUFS,)),
        pltpu.VMEM((rows, D_MODEL), BF16),
        pltpu.VMEM((len(POOL_WINDOWS), rows, POOL_GROUP_DIM), BF16),
        pltpu.VMEM((_N_CONV_CHUNKS, rows, MIX_CH), BF16),
        pltpu.VMEM((_N_OUT_CHUNKS, rows, MIX_CH), BF16),
        pltpu.VMEM((2, rows, MIX_CH), F32),
        pltpu.VMEM((rows, MIX_CH), F32),
        pltpu.VMEM((rows, MIX_CH), F32),
        pltpu.VMEM((rows, POOL_GROUP_DIM), BF16),
        pltpu.VMEM((tm + 2 * CARRY_P, POOL_GROUP_DIM), F32),
        pltpu.VMEM((tm + 2 * CARRY_P, POOL_GROUP_DIM), F32),
        pltpu.VMEM((tm + CARRY_Q, MIX_CH), F32),
        pltpu.VMEM((CARRY_P, POOL_WIDTH), F32),
        pltpu.VMEM((CARRY_Q, CONV_WIDTH), F32),
    ]
    return pl.pallas_call(
        functools.partial(_mixer_kernel, tm=tm, ts=ts, tps=tps, n_tiles=n_tiles),
        grid=(n_tiles, _N_OUT_CHUNKS),
        in_specs=in_specs,
        out_specs=out_specs,
        out_shape=out_shape,
        scratch_shapes=scratch,
        compiler_params=pltpu.CompilerParams(
            dimension_semantics=("arbitrary", "arbitrary"), vmem_limit_bytes=VMEM_LIMIT_BYTES),
        name="mixer",
    )(x1p, x1s, x1p, x1s, mod_p3, mod_p3, mod_p3, mod_s, mod_s, mod_s, nrm, grp, pscale, cw, cbias,
      stp_t, stc_t, w_in, w_a, w_b, w_o)


def kernel(x_prompt, x_sample, c_prompt, c_sample, state_pool, state_conv, w_ada, b_ada, norm1, ffn1_gate, ffn1_up, ffn1_down, norm2, w_in, pool_grp, pool_scale, w_branch_a, conv_w, conv_b, w_branch_b, w_o, norm3, ffn2_gate, ffn2_up, ffn2_down, norm_final):
    assert w_ada.shape[0] == 1, "single-layer step"
    batch, seq, d = x_prompt.shape
    dec_batch = x_sample.shape[0]
    assert d == D_MODEL and x_sample.shape[1] == 1

    tm = TM_PROMPT
    tps = seq // tm
    n_tiles = batch * tps
    ts = dec_batch // n_tiles
    assert seq % tm == 0 and dec_batch % n_tiles == 0 and ts % (2 * SUBLANES) == 0

    n_c = batch + dec_batch
    pad = (-n_c) % SUBLANES
    c_all = jnp.concatenate([c_sample, c_prompt, jnp.zeros((pad, d), F32)], axis=0)
    mod = _ada(c_all, w_ada[0], b_ada)
    mod_p3 = mod[dec_batch:n_c].reshape(batch, 1, N_MOD * d)

    row = lambda a: a.reshape(1, -1)
    xp = x_prompt.reshape(batch * seq, d)
    xs = x_sample.reshape(dec_batch, d)
    nf = row(norm_final)
    x1p, x1s = _ffn(xp, xs, mod_p3, mod, MOD_SH1, MOD_SC1, MOD_G1, row(norm1[0]), ffn1_gate[0], ffn1_up[0],
                    ffn1_down[0], nf, tm=tm, ts=ts, tps=tps, final_norm=False, name="ffn1")
    stp_t = jnp.transpose(state_pool[0], (1, 0, 2))
    stc_t = jnp.transpose(state_conv[0], (1, 0, 2))
    x2p, x2s, pst, cst, nps_t, ncs_t = _mixer(
        x1p, x1s, mod_p3, mod, row(norm2[0]), pool_grp[0], row(pool_scale[0]), conv_w[0], row(conv_b[0]),
        stp_t, stc_t, w_in[0], w_branch_a[0], w_branch_b[0], w_o[0], tm=tm, ts=ts, tps=tps)
    yp, ys = _ffn(x2p, x2s, mod_p3, mod, MOD_SH3, MOD_SC3, MOD_G3, row(norm3[0]), ffn2_gate[0], ffn2_up[0],
                  ffn2_down[0], nf, tm=tm, ts=ts, tps=tps, final_norm=True, name="ffn2")

    y_prompt = yp.reshape(batch, seq, d)
    y_sample = ys.reshape(dec_batch, 1, d)
    new_pool_prompt = pst[None, :, CARRY_P - POOL_STATE:, :]
    new_conv_prompt = cst[None, :, CARRY_Q - CONV_STATE:, :]
    new_pool_sample = jnp.transpose(nps_t, (1, 0, 2))[None]
    new_conv_sample = jnp.transpose(ncs_t, (1, 0, 2))[None]
    return (y_prompt, y_sample, new_pool_prompt, new_conv_prompt, new_pool_sample, new_conv_sample)
```

```python
import functools

import jax
import jax.numpy as jnp
from jax import lax
from jax.experimental import pallas as pl
from jax.experimental.pallas import tpu as pltpu

F32 = jnp.float32
BF16 = jnp.bfloat16

D_MODEL = 2048
D_FF = 5632
POOL_WIDTH = 1024
CONV_WIDTH = 1024
POOL_WINDOWS = (2, 4, 8, 16)
POOL_GROUP_DIM = 256
POOL_STATE = 15
CONV_STATE = 2
N_MOD = 9
EPS = 1e-6
PAST_LEN = 16384

V7X_VMEM_BYTES = 64 * 1024 * 1024
SUBLANES = 8
VMEM_LIMIT_BYTES = V7X_VMEM_BYTES - 2 * 1024 * 1024

TM_PROMPT = 1024

ADA_TN = 1024
ADA_NB = 3
FFN_TF = 256
FFN_DMA_SPLIT = 1
FFN_NB = 3
MIX_CH = 512
MIX_HALF = 256
MIX_NB = 3
X_ROWS = 128
FFN_ROWS = 128
X_BUFS = 4
MIX_DMA_SPLIT = 2
CARRY_P = 16
CARRY_Q = 8

COL_P, COL_CB, COL_CC, COL_CH, COL_GA, COL_GB = 0, 1024, 2048, 3072, 4096, 6144
MOD_SH1, MOD_SC1, MOD_G1, MOD_SH2, MOD_SC2, MOD_G2, MOD_SH3, MOD_SC3, MOD_G3 = range(9)


def _sigmoid(x):
    return 1.0 / (1.0 + jnp.exp(-x))


def _modulate(x, nrm, sc, sh):
    inv = lax.rsqrt(jnp.mean(x * x, axis=-1, keepdims=True) + EPS)
    return (x * inv) * nrm * (1.0 + sc) + sh


def _bdot(a, b):
    return jnp.dot(a, b, preferred_element_type=F32)


def _ada_kernel(c_ref, b_ref, w_hbm, o_hbm, s_ref, wbuf, obuf, wsem, osem, *, n_chunks):
    c = c_ref[...]
    s_ref[...] = (c * _sigmoid(c)).astype(BF16)

    def wcopy(j, slot):
        return pltpu.make_async_copy(w_hbm.at[:, pl.ds(pl.multiple_of(j * ADA_TN, ADA_TN), ADA_TN)],
                                     wbuf.at[slot], wsem.at[slot])

    def ocopy(j, slot):
        return pltpu.make_async_copy(obuf.at[slot],
                                     o_hbm.at[:, pl.ds(pl.multiple_of(j * ADA_TN, ADA_TN), ADA_TN)], osem.at[slot])

    for t in range(ADA_NB - 1):
        wcopy(jnp.int32(t), t).start()

    @pl.loop(0, n_chunks)
    def _(j):
        slot = lax.rem(j, ADA_NB)
        oslot = lax.rem(j, 2)
        wcopy(j, slot).wait()

        @pl.when(j + (ADA_NB - 1) < n_chunks)
        def _():
            wcopy(j + (ADA_NB - 1), lax.rem(j + (ADA_NB - 1), ADA_NB)).start()

        @pl.when(j >= 2)
        def _():
            ocopy(j - 2, oslot).wait()

        obuf[oslot] = _bdot(s_ref[...], wbuf[slot].astype(BF16)) + b_ref[j]
        ocopy(j, oslot).start()

    for t in range(2):
        j_last = n_chunks - 2 + t
        ocopy(jnp.int32(j_last), j_last % 2).wait()


def _ada(c_all, w_ada, b_ada):
    rows = c_all.shape[0]
    cols = w_ada.shape[1]
    n_chunks = cols // ADA_TN
    assert n_chunks >= ADA_NB and n_chunks >= 2
    return pl.pallas_call(
        functools.partial(_ada_kernel, n_chunks=n_chunks),
        grid=(1,),
        in_specs=[
            pl.BlockSpec((rows, D_MODEL), lambda g: (0, 0)),
            pl.BlockSpec((n_chunks, 1, ADA_TN), lambda g: (0, 0, 0)),
            pl.BlockSpec(memory_space=pl.ANY),
        ],
        out_specs=pl.BlockSpec(memory_space=pl.ANY),
        out_shape=jax.ShapeDtypeStruct((rows, cols), F32),
        scratch_shapes=[pltpu.VMEM((rows, D_MODEL), BF16),
                        pltpu.VMEM((ADA_NB, D_MODEL, ADA_TN), F32),
                        pltpu.VMEM((2, rows, ADA_TN), F32),
                        pltpu.SemaphoreType.DMA((ADA_NB,)), pltpu.SemaphoreType.DMA((2,))],
        compiler_params=pltpu.CompilerParams(dimension_semantics=("arbitrary",)),
        name="ada_mod",
    )(c_all, b_ada.reshape(n_chunks, 1, ADA_TN), w_ada)


def _ffn_kernel(xp_ref, xs_ref, shp, scp, gtp, shs, scs, gts, nrm_ref, wg_hbm, wu_hbm, wd_hbm, nf_ref,
                op_ref, os_ref, h_ref, inv_ref, wgb, wub, wdb, wsem, *, n_f, n_tiles, final_norm):
    i = pl.program_id(0)
    tm = xp_ref.shape[0]
    ts = xs_ref.shape[0]

    def wcopies(f, slot):
        col = pl.multiple_of(f * FFN_TF, FFN_TF)
        rows_in = D_MODEL // FFN_DMA_SPLIT
        rows_dn = FFN_TF // FFN_DMA_SPLIT
        out = []
        for k in range(FFN_DMA_SPLIT):
            r_in = pl.ds(k * rows_in, rows_in)
            out.append(pltpu.make_async_copy(wg_hbm.at[r_in, pl.ds(col, FFN_TF)], wgb.at[slot, r_in, :],
                                             wsem.at[slot]))
            out.append(pltpu.make_async_copy(wu_hbm.at[r_in, pl.ds(col, FFN_TF)], wub.at[slot, r_in, :],
                                             wsem.at[slot]))
            out.append(pltpu.make_async_copy(wd_hbm.at[pl.ds(col + k * rows_dn, rows_dn), :],
                                             wdb.at[slot, pl.ds(k * rows_dn, rows_dn), :], wsem.at[slot]))
        return out

    def wstart(f, slot):
        for cp in wcopies(f, slot):
            cp.start()

    def wwait(f, slot):
        for cp in wcopies(f, slot):
            cp.wait()

    @pl.when(i == 0)
    def _():
        for t in range(FFN_NB - 1):
            wstart(jnp.int32(t), t)

    def head(x, sc, sh):
        return _modulate(x, nrm_ref[...], sc, sh).astype(BF16)

    def tail(x, gt, acc):
        xo = x + 0.5 * gt * acc
        if final_norm:
            inv = lax.rsqrt(jnp.mean(xo * xo, axis=-1, keepdims=True) + EPS)
            xo = (xo * inv) * nf_ref[...]
        return xo

    @pl.loop(0, tm // FFN_ROWS)
    def _(r):
        rows = pl.ds(pl.multiple_of(r * FFN_ROWS, FFN_ROWS), FFN_ROWS)
        x = xp_ref[rows, :]
        inv_ref[rows, :] = lax.rsqrt(jnp.mean(x * x, axis=-1, keepdims=True) + EPS)

    @pl.loop(0, tm // FFN_ROWS)
    def _(r):
        rows = pl.ds(pl.multiple_of(r * FFN_ROWS, FFN_ROWS), FFN_ROWS)
        hx = (xp_ref[rows, :] * inv_ref[rows, :]) * nrm_ref[...] * (1.0 + scp[...]) + shp[...]
        h_ref[rows, :] = hx.astype(BF16)
        op_ref[rows, :] = jnp.zeros((FFN_ROWS, D_MODEL), F32)

    h_ref[pl.ds(tm, ts), :] = head(xs_ref[...], scs[...], shs[...])
    os_ref[...] = jnp.zeros((ts, D_MODEL), F32)

    @pl.loop(0, n_f)
    def _(f):
        g_idx = i * n_f + f
        slot = lax.rem(g_idx, FFN_NB)
        wwait(f, slot)
        nxt = f + (FFN_NB - 1)
        nslot = lax.rem(g_idx + (FFN_NB - 1), FFN_NB)

        @pl.when(jnp.logical_or(nxt < n_f, i < n_tiles - 1))
        def _():
            wstart(jnp.where(nxt < n_f, nxt, nxt - n_f), nslot)

        h = h_ref[...]
        g = _bdot(h, wgb[slot].astype(BF16))
        u = _bdot(h, wub[slot].astype(BF16))
        a = ((g * _sigmoid(g)) * u).astype(BF16)
        for half in range(2):
            hcols = pl.ds(half * (D_MODEL // 2), D_MODEL // 2)
            y = _bdot(a, wdb[slot, :, hcols].astype(BF16))
            op_ref[:, hcols] += y[:tm]
            os_ref[:, hcols] += y[tm:]

    @pl.loop(0, tm // FFN_ROWS)
    def _(r):
        rows = pl.ds(pl.multiple_of(r * FFN_ROWS, FFN_ROWS), FFN_ROWS)
        op_ref[rows, :] = tail(xp_ref[rows, :], gtp[...], op_ref[rows, :])

    os_ref[...] = tail(xs_ref[...], gts[...], os_ref[...])


def _ffn(xp, xs, mod_p3, mod_s, j_sh, j_sc, j_g, nrm, wg, wu, wd, nf, *, tm, ts, tps, final_norm, name):
    n_tiles = xp.shape[0] // tm
    assert xs.shape[0] == n_tiles * ts
    n_f = D_FF // FFN_TF
    assert n_f >= FFN_NB
    any_spec = pl.BlockSpec(memory_space=pl.ANY)

    def pspec(j):
        return pl.BlockSpec((None, 1, D_MODEL), lambda i: (i // tps, 0, j))

    def sspec(j):
        return pl.BlockSpec((ts, D_MODEL), lambda i: (i, j))

    return pl.pallas_call(
        functools.partial(_ffn_kernel, n_f=n_f, n_tiles=n_tiles, final_norm=final_norm),
        grid=(n_tiles,),
        in_specs=[
            pl.BlockSpec((tm, D_MODEL), lambda i: (i, 0)),
            pl.BlockSpec((ts, D_MODEL), lambda i: (i, 0)),
            pspec(j_sh), pspec(j_sc), pspec(j_g),
            sspec(j_sh), sspec(j_sc), sspec(j_g),
            pl.BlockSpec((1, D_MODEL), lambda i: (0, 0)),
            any_spec, any_spec, any_spec,
            pl.BlockSpec((1, D_MODEL), lambda i: (0, 0)),
        ],
        out_specs=[pl.BlockSpec((tm, D_MODEL), lambda i: (i, 0)),
                   pl.BlockSpec((ts, D_MODEL), lambda i: (i, 0))],
        out_shape=[jax.ShapeDtypeStruct(xp.shape, F32), jax.ShapeDtypeStruct(xs.shape, F32)],
        scratch_shapes=[pltpu.VMEM((tm + ts, D_MODEL), BF16), pltpu.VMEM((tm, 1), F32),
                        pltpu.VMEM((FFN_NB, D_MODEL, FFN_TF), F32), pltpu.VMEM((FFN_NB, D_MODEL, FFN_TF), F32),
                        pltpu.VMEM((FFN_NB, FFN_TF, D_MODEL), F32), pltpu.SemaphoreType.DMA((FFN_NB,))],
        compiler_params=pltpu.CompilerParams(
            dimension_semantics=("arbitrary",), vmem_limit_bytes=VMEM_LIMIT_BYTES),
        name=name,
    )(xp, xs, mod_p3, mod_p3, mod_p3, mod_s, mod_s, mod_s, nrm, wg, wu, wd, nf)


K_P, K_CC, K_CH, K_CB, K_GA, K_GB, K_AB, K_O = range(8)
_N_CONV_CHUNKS = CONV_WIDTH // MIX_CH
_N_OUT_CHUNKS = D_MODEL // MIX_CH
_S_CONV = POOL_WIDTH // MIX_CH
_S_GATE = _S_CONV + 3 * _N_CONV_CHUNKS
_S_OUT = _S_GATE + 3 * _N_OUT_CHUNKS
_N_STAGES = _S_OUT + _N_OUT_CHUNKS
_KIND_COL0 = (COL_P, COL_CC, COL_CH, COL_CB, COL_GA, COL_GB, 0, 0)
assert _N_STAGES % MIX_NB == 0


def _decode_stage(t):
    u1 = t - _S_CONV
    u2 = t - _S_GATE
    c1 = lax.div(jnp.maximum(u1, 0), 3)
    c2 = lax.div(jnp.maximum(u2, 0), 3)
    kind = jnp.where(t < _S_CONV, K_P,
                     jnp.where(t < _S_GATE, K_CC + (u1 - 3 * c1),
                               jnp.where(t < _S_OUT, K_GA + (u2 - 3 * c2), K_O)))
    chunk = jnp.where(t < _S_CONV, t, jnp.where(t < _S_GATE, c1, jnp.where(t < _S_OUT, c2, t - _S_OUT)))
    col0 = jnp.int32(0)
    for k, base in enumerate(_KIND_COL0):
        if base:
            col0 = jnp.where(kind == k, base, col0)
    return kind, chunk, col0 + chunk * MIX_CH


def _mixer_kernel(x1p_hbm, x1s, x1pc, x1sc, shp, scp, g2pc, shs, scs, g2sc, nrm, grp, pscale, cw, cbias,
                  stp, stc, w_in, w_a, w_b, w_o,
                  x2p, x2s, pst, cst, nps, ncs,
                  wbuf, wsem, xbuf, xsem, h2, ap, v, m, zb, ybuf, sa, abf, e0, e1, qext, pcar, qcar,
                  *, tm, ts, tps, n_tiles):
    i = pl.program_id(0)
    oc = pl.program_id(1)
    slab = D_MODEL // MIX_DMA_SPLIT
    half = D_MODEL // 2
    prows = pl.ds(0, tm)
    srows = pl.ds(tm, ts)

    def wcopy(src, src_row, col, dst_slot, dst_row):
        return pltpu.make_async_copy(src.at[pl.ds(src_row, slab), pl.ds(col, MIX_CH)],
                                     wbuf.at[dst_slot, pl.ds(dst_row, slab), :], wsem.at[dst_slot])

    def wstart(t):
        k_t, _, col_t = _decode_stage(t)
        col_t = pl.multiple_of(col_t, MIX_CH)
        slot_t = lax.rem(t, MIX_NB)

        @pl.when(k_t < K_AB)
        def _():
            for k in range(MIX_DMA_SPLIT):
                wcopy(w_in, k * slab, col_t, slot_t, k * slab).start()

        @pl.when(k_t == K_AB)
        def _():
            for k in range(MIX_DMA_SPLIT):
                r0 = k * slab
                src, s0 = (w_a, r0) if r0 < half else (w_b, r0 - half)
                wcopy(src, s0, col_t, slot_t, r0).start()

        @pl.when(k_t == K_O)
        def _():
            for k in range(MIX_DMA_SPLIT):
                wcopy(w_o, k * slab, col_t, slot_t, k * slab).start()

    def wwait(slot):
        for k in range(MIX_DMA_SPLIT):
            wcopy(w_in, k * slab, 0, slot, k * slab).wait()

    n_xchunks = tm // X_ROWS

    def xcopy(tile, r):
        return pltpu.make_async_copy(x1p_hbm.at[pl.ds(tile * tm + r * X_ROWS, X_ROWS), :],
                                     xbuf.at[r % X_BUFS], xsem.at[r % X_BUFS])

    def xprefetch(tile):
        for r in range(min(X_BUFS, n_xchunks)):
            xcopy(tile, r).start()

    def prologue():
        for r in range(n_xchunks):
            xcopy(i, r).wait()
            h2[pl.ds(r * X_ROWS, X_ROWS), :] = _modulate(xbuf[r % X_BUFS], nrm[...], scp[...], shp[...]).astype(BF16)
            if r + X_BUFS < n_xchunks:
                xcopy(i, r + X_BUFS).start()
        h2[srows, :] = _modulate(x1s[...], nrm[...], scs[...], shs[...]).astype(BF16)
        for r in range(POOL_STATE - 1):
            nps[r] = stp[r + 1]
        for r in range(CONV_STATE - 1):
            ncs[r] = stc[r + 1]

    def pool_stage(c):
        zbuf = zb.at[0]
        for gl in range(MIX_CH // POOL_GROUP_DIM):
            g = c * (MIX_CH // POOL_GROUP_DIM) + gl
            w = POOL_WINDOWS[g]
            gcols = pl.ds(g * POOL_GROUP_DIM, POOL_GROUP_DIM)
            lcols = pl.ds(gl * POOL_GROUP_DIM, POOL_GROUP_DIM)
            pg = zbuf[prows, lcols]
            pad = jnp.zeros((SUBLANES, POOL_GROUP_DIM), F32)
            e0[pl.ds(SUBLANES, SUBLANES), :] = pad
            e1[pl.ds(SUBLANES, SUBLANES), :] = pad
            e0[pl.ds(CARRY_P, CARRY_P), :] = pcar[:, gcols]
            e0[pl.ds(2 * CARRY_P, tm), :] = pg
            src, dst = e0, e1
            d = 1
            while d < w:
                dst[pl.ds(CARRY_P, tm + CARRY_P), :] = (src[pl.ds(CARRY_P, tm + CARRY_P), :]
                                                        + src[pl.ds(CARRY_P - d, tm + CARRY_P), :])
                src, dst = dst, src
                d *= 2
            wsum = src[pl.ds(2 * CARRY_P, tm), :]
            pos = (i % tps) * tm + lax.broadcasted_iota(jnp.int32, (CARRY_P, 1), 0)
            cnt = jnp.minimum(pos + 1, w).astype(F32)
            abf[prows, :] = (wsum * (1.0 / w) - pg).astype(BF16)
            head = pl.ds(0, CARRY_P)
            abf[head, :] = (wsum[:CARRY_P, :] / cnt - pg[:CARRY_P, :]).astype(BF16)
            tail = pg[tm - CARRY_P:, :]
            pcar[:, gcols] = tail
            pst[:, gcols] = tail
            ps = zbuf[srows, lcols]
            ssum = ps
            for r in range(POOL_STATE + 1 - w, POOL_STATE):
                ssum = ssum + stp[r, :, gcols]
            abf[srows, :] = (ssum / float(min(PAST_LEN + 1, w)) - ps).astype(BF16)
            nps[POOL_STATE - 1, :, gcols] = ps
            ag = _bdot(abf[...], grp[g].astype(BF16)) * pscale[:, gcols]
            ap[g] = ag.astype(BF16)

    def conv_stage(c):
        ccols = pl.ds(c * MIX_CH, MIX_CH)
        zbuf, zprev = zb.at[0], zb.at[1]
        qp = zprev[prows, :] * zbuf[prows, :]
        qext[pl.ds(0, CARRY_Q), :] = qcar[:, ccols]
        qext[pl.ds(CARRY_Q, tm), :] = qp
        y = cbias[:, ccols] + qext[pl.ds(CARRY_Q - 2, tm), :] * cw[0:1, ccols]
        y = y + qext[pl.ds(CARRY_Q - 1, tm), :] * cw[1:2, ccols]
        tail = qp[tm - CARRY_Q:, :]
        qcar[:, ccols] = tail
        cst[:, ccols] = tail
        qs = zprev[srows, :] * zbuf[srows, :]
        ys = cbias[:, ccols] + stc[0, :, ccols] * cw[0:1, ccols]
        ys = ys + stc[1, :, ccols] * cw[1:2, ccols]
        ncs[CONV_STATE - 1, :, ccols] = qs
        ybuf[prows, :] = y + qp * cw[2:3, ccols]
        ybuf[srows, :] = ys + qs * cw[2:3, ccols]

    def main_stage(s):
        kind, chunk, _ = _decode_stage(s)
        slot = lax.rem(s, MIX_NB)
        wwait(slot)
        wstart(s + (MIX_NB - 1))

        keep = jnp.logical_or(kind == K_CC, kind == K_GA)

        @pl.when(keep)
        def _():
            zb[1] = _bdot(h2[...], wbuf[slot].astype(BF16))

        @pl.when(kind == K_GB)
        def _():
            sa[...] = _sigmoid(zb[1])
            zb[0] = _bdot(h2[...], wbuf[slot].astype(BF16))

        @pl.when(jnp.logical_and(kind < K_GA, jnp.logical_not(keep)))
        def _():
            zb[0] = _bdot(h2[...], wbuf[slot].astype(BF16))

        for c in range(POOL_WIDTH // MIX_CH):
            @pl.when(jnp.logical_and(kind == K_P, chunk == c))
            def _(c=c):
                pool_stage(c)

        for c in range(_N_CONV_CHUNKS):
            @pl.when(jnp.logical_and(kind == K_CH, chunk == c))
            def _(c=c):
                conv_stage(c)

        @pl.when(kind == K_CB)
        def _():
            v[chunk] = (zb[0] * ybuf[...]).astype(BF16)

        @pl.when(kind == K_AB)
        def _():
            sb = _sigmoid(zb[0])
            for hc in range(MIX_CH // MIX_HALF):
                hcols = pl.ds(hc * MIX_HALF, MIX_HALF)
                ua = None
                for g in range(len(POOL_WINDOWS)):
                    part = _bdot(ap[g], wbuf[slot, pl.ds(g * POOL_GROUP_DIM, POOL_GROUP_DIM), hcols].astype(BF16))
                    ua = part if ua is None else ua + part
                ub = None
                for k in range(_N_CONV_CHUNKS):
                    part = _bdot(v[k], wbuf[slot, pl.ds(half + k * MIX_CH, MIX_CH), hcols].astype(BF16))
                    ub = part if ub is None else ub + part
                sbh = sb[:, hc * MIX_HALF:(hc + 1) * MIX_HALF]
                m[chunk, :, hcols] = (sa[:, hcols] * ua + sbh * ub).astype(BF16)

    assert _S_OUT + MIX_NB - 2 < _N_STAGES

    @pl.when(oc == 0)
    def _():
        @pl.when(i == 0)
        def _():
            for t in range(MIX_NB - 1):
                wstart(jnp.int32(t))
            xprefetch(0)

        @pl.when(i % tps == 0)
        def _():
            pcar[...] = jnp.zeros(pcar.shape, F32)
            qcar[...] = jnp.zeros(qcar.shape, F32)

        prologue()
        pl.loop(0, _S_OUT)(main_stage)

    s_out = _S_OUT + oc
    slot = lax.rem(s_out, MIX_NB)
    wwait(slot)
    nxt = s_out + (MIX_NB - 1)

    @pl.when(jnp.logical_or(nxt < _N_STAGES, i < n_tiles - 1))
    def _():
        wstart(jnp.where(nxt < _N_STAGES, nxt, nxt - _N_STAGES))

    @pl.when(jnp.logical_and(oc == 1, i < n_tiles - 1))
    def _():
        xprefetch(i + 1)

    for hc in range(MIX_CH // MIX_HALF):
        hcols = pl.ds(hc * MIX_HALF, MIX_HALF)
        acc = None
        for k in range(_N_OUT_CHUNKS):
            part = _bdot(m[k], wbuf[slot, pl.ds(k * MIX_CH, MIX_CH), hcols].astype(BF16))
            acc = part if acc is None else acc + part
        g2_row = (MOD_G2 * D_MODEL) // MIX_CH + oc
        x2p[:, hcols] = x1pc[:, hcols] + g2pc[pl.ds(g2_row, 1), hcols] * acc[:tm]
        x2s[:, oc, hcols] = x1sc[:, oc, hcols] + g2sc[:, g2_row, hcols] * acc[tm:]


def _mixer(x1p, x1s, mod_p3, mod_s, nrm, grp, pscale, cw, cbias, stp_t, stc_t, w_in, w_a, w_b, w_o, *, tm, ts, tps):
    n_tiles = x1p.shape[0] // tm
    n_seq = n_tiles // tps
    n_s = x1s.shape[0]
    assert n_s == n_tiles * ts
    rows = tm + ts
    any_spec = pl.BlockSpec(memory_space=pl.ANY)

    def const(shape):
        return pl.BlockSpec(shape, lambda i, s: (0,) * len(shape))

    def oc(s):
        return s

    mod_chunks = mod_s.shape[1] // MIX_CH
    in_specs = [
        any_spec,
        pl.BlockSpec((ts, D_MODEL), lambda i, s: (i, 0)),
        pl.BlockSpec((tm, MIX_CH), lambda i, s: (i, oc(s))),
        pl.BlockSpec((ts, _N_OUT_CHUNKS, MIX_CH), lambda i, s: (i, 0, 0)),
        pl.BlockSpec((None, 1, D_MODEL), lambda i, s: (i // tps, 0, MOD_SH2)),
        pl.BlockSpec((None, 1, D_MODEL), lambda i, s: (i // tps, 0, MOD_SC2)),
        pl.BlockSpec((None, mod_chunks, MIX_CH), lambda i, s: (i // tps, 0, 0)),
        pl.BlockSpec((ts, D_MODEL), lambda i, s: (i, MOD_SH2)),
        pl.BlockSpec((ts, D_MODEL), lambda i, s: (i, MOD_SC2)),
        pl.BlockSpec((ts, mod_chunks, MIX_CH), lambda i, s: (i, 0, 0)),
        const((1, D_MODEL)),
        const((len(POOL_WINDOWS), POOL_GROUP_DIM, POOL_GROUP_DIM)),
        const((1, POOL_WIDTH)),
        const((3, CONV_WIDTH)),
        const((1, CONV_WIDTH)),
        pl.BlockSpec((POOL_STATE, ts, POOL_WIDTH), lambda i, s: (0, i, 0)),
        pl.BlockSpec((CONV_STATE, ts, CONV_WIDTH), lambda i, s: (0, i, 0)),
        any_spec, any_spec, any_spec, any_spec,
    ]
    out_specs = [
        pl.BlockSpec((tm, MIX_CH), lambda i, s: (i, oc(s))),
        pl.BlockSpec((ts, _N_OUT_CHUNKS, MIX_CH), lambda i, s: (i, 0, 0)),
        pl.BlockSpec((None, CARRY_P, POOL_WIDTH), lambda i, s: (i // tps, 0, 0)),
        pl.BlockSpec((None, CARRY_Q, CONV_WIDTH), lambda i, s: (i // tps, 0, 0)),
        pl.BlockSpec((POOL_STATE, ts, POOL_WIDTH), lambda i, s: (0, i, 0)),
        pl.BlockSpec((CONV_STATE, ts, CONV_WIDTH), lambda i, s: (0, i, 0)),
    ]
    out_shape = [
        jax.ShapeDtypeStruct(x1p.shape, F32),
        jax.ShapeDtypeStruct((n_s, _N_OUT_CHUNKS, MIX_CH), F32),
        jax.ShapeDtypeStruct((n_seq, CARRY_P, POOL_WIDTH), F32),
        jax.ShapeDtypeStruct((n_seq, CARRY_Q, CONV_WIDTH), F32),
        jax.ShapeDtypeStruct((POOL_STATE, n_s, POOL_WIDTH), F32),
        jax.ShapeDtypeStruct((CONV_STATE, n_s, CONV_WIDTH), F32),
    ]
    scratch = [
        pltpu.VMEM((MIX_NB, D_MODEL, MIX_CH), F32),
        pltpu.SemaphoreType.DMA((MIX_NB,)),
        pltpu.VMEM((X_BUFS, X_ROWS, D_MODEL), F32),
        pltpu.SemaphoreType.DMA((X_BUFS,)),
        pltpu.VMEM((rows, D_MODEL), BF16),
        pltpu.VMEM((len(POOL_WINDOWS), rows, POOL_GROUP_DIM), BF16),
        pltpu.VMEM((_N_CONV_CHUNKS, rows, MIX_CH), BF16),
        pltpu.VMEM((_N_OUT_CHUNKS, rows, MIX_CH), BF16),
        pltpu.VMEM((2, rows, MIX_CH), F32),
        pltpu.VMEM((rows, MIX_CH), F32),
        pltpu.VMEM((rows, MIX_CH), F32),
        pltpu.VMEM((rows, POOL_GROUP_DIM), BF16),
        pltpu.VMEM((tm + 2 * CARRY_P, POOL_GROUP_DIM), F32),
        pltpu.VMEM((tm + 2 * CARRY_P, POOL_GROUP_DIM), F32),
        pltpu.VMEM((tm + CARRY_Q, MIX_CH), F32),
        pltpu.VMEM((CARRY_P, POOL_WIDTH), F32),
        pltpu.VMEM((CARRY_Q, CONV_WIDTH), F32),
    ]
    x1s_c = x1s.reshape(n_s, _N_OUT_CHUNKS, MIX_CH)
    mod_p_c = mod_p3.reshape(mod_p3.shape[0], mod_chunks, MIX_CH)
    mod_s_c = mod_s.reshape(mod_s.shape[0], mod_chunks, MIX_CH)
    outs = pl.pallas_call(
        functools.partial(_mixer_kernel, tm=tm, ts=ts, tps=tps, n_tiles=n_tiles),
        grid=(n_tiles, _N_OUT_CHUNKS),
        in_specs=in_specs,
        out_specs=out_specs,
        out_shape=out_shape,
        scratch_shapes=scratch,
        compiler_params=pltpu.CompilerParams(
            dimension_semantics=("arbitrary", "arbitrary"), vmem_limit_bytes=VMEM_LIMIT_BYTES),
        name="mixer",
    )(x1p, x1s, x1p, x1s_c, mod_p3, mod_p3, mod_p_c, mod_s, mod_s, mod_s_c, nrm, grp, pscale, cw, cbias,
      stp_t, stc_t, w_in, w_a, w_b, w_o)
    outs = list(outs)
    outs[1] = outs[1].reshape(x1s.shape)
    return outs


def kernel(x_prompt, x_sample, c_prompt, c_sample, state_pool, state_conv, w_ada, b_ada, norm1, ffn1_gate, ffn1_up, ffn1_down, norm2, w_in, pool_grp, pool_scale, w_branch_a, conv_w, conv_b, w_branch_b, w_o, norm3, ffn2_gate, ffn2_up, ffn2_down, norm_final):
    assert w_ada.shape[0] == 1, "single-layer step"
    batch, seq, d = x_prompt.shape
    dec_batch = x_sample.shape[0]
    assert d == D_MODEL and x_sample.shape[1] == 1

    tm = TM_PROMPT
    tps = seq // tm
    n_tiles = batch * tps
    ts = dec_batch // n_tiles
    assert seq % tm == 0 and dec_batch % n_tiles == 0 and ts % (2 * SUBLANES) == 0

    n_c = batch + dec_batch
    pad = (-n_c) % SUBLANES
    c_all = jnp.concatenate([c_sample, c_prompt, jnp.zeros((pad, d), F32)], axis=0)
    mod = _ada(c_all, w_ada[0], b_ada)
    mod_p3 = mod[dec_batch:n_c].reshape(batch, 1, N_MOD * d)

    row = lambda a: a.reshape(1, -1)
    xp = x_prompt.reshape(batch * seq, d)
    xs = x_sample.reshape(dec_batch, d)
    nf = row(norm_final)
    x1p, x1s = _ffn(xp, xs, mod_p3, mod, MOD_SH1, MOD_SC1, MOD_G1, row(norm1[0]), ffn1_gate[0], ffn1_up[0],
                    ffn1_down[0], nf, tm=tm, ts=ts, tps=tps, final_norm=False, name="ffn1")
    stp_t = jnp.transpose(state_pool[0], (1, 0, 2))
    stc_t = jnp.transpose(state_conv[0], (1, 0, 2))
    x2p, x2s, pst, cst, nps_t, ncs_t = _mixer(
        x1p, x1s, mod_p3, mod, row(norm2[0]), pool_grp[0], row(pool_scale[0]), conv_w[0], row(conv_b[0]),
        stp_t, stc_t, w_in[0], w_branch_a[0], w_branch_b[0], w_o[0], tm=tm, ts=ts, tps=tps)
    yp, ys = _ffn(x2p, x2s, mod_p3, mod, MOD_SH3, MOD_SC3, MOD_G3, row(norm3[0]), ffn2_gate[0], ffn2_up[0],
                  ffn2_down[0], nf, tm=tm, ts=ts, tps=tps, final_norm=True, name="ffn2")

    y_prompt = yp.reshape(batch, seq, d)
    y_sample = ys.reshape(dec_batch, 1, d)
    new_pool_prompt = pst[None, :, CARRY_P - POOL_STATE:, :]
    new_conv_prompt = cst[None, :, CARRY_Q - CONV_STATE:, :]
    new_pool_sample = jnp.transpose(nps_t, (1, 0, 2))[None]
    new_conv_sample = jnp.transpose(ncs_t, (1, 0, 2))[None]
    return (y_prompt, y_sample, new_pool_prompt, new_conv_prompt, new_pool_sample, new_conv_sample)
```

```python
import functools

import jax
import jax.numpy as jnp
from jax import lax
from jax.experimental import pallas as pl
from jax.experimental.pallas import tpu as pltpu

F32 = jnp.float32
BF16 = jnp.bfloat16

D_MODEL = 2048
D_FF = 5632
POOL_WIDTH = 1024
CONV_WIDTH = 1024
POOL_WINDOWS = (2, 4, 8, 16)
POOL_GROUP_DIM = 256
POOL_STATE = 15
CONV_STATE = 2
N_MOD = 9
EPS = 1e-6
PAST_LEN = 16384

V7X_VMEM_BYTES = 64 * 1024 * 1024
SUBLANES = 8
VMEM_LIMIT_BYTES = V7X_VMEM_BYTES - 2 * 1024 * 1024

TM_PROMPT = 1024

ADA_TN = 1024
ADA_NB = 3
FFN_TF = 256
FFN_DMA_SPLIT = 1
FFN_NB = 3
MIX_CH = 512
MIX_HALF = 256
MIX_NB = 3
X_ROWS = 128
FFN_ROWS = 128
X_BUFS = 4
MIX_DMA_SPLIT = 2
W_DMA_PRIORITY = 1
CARRY_P = 16
CARRY_Q = 8

COL_P, COL_CB, COL_CC, COL_CH, COL_GA, COL_GB = 0, 1024, 2048, 3072, 4096, 6144
MOD_SH1, MOD_SC1, MOD_G1, MOD_SH2, MOD_SC2, MOD_G2, MOD_SH3, MOD_SC3, MOD_G3 = range(9)


def _sigmoid(x):
    return 1.0 / (1.0 + jnp.exp(-x))


def _modulate(x, nrm, sc, sh):
    inv = lax.rsqrt(jnp.mean(x * x, axis=-1, keepdims=True) + EPS)
    return (x * inv) * nrm * (1.0 + sc) + sh


def _bdot(a, b):
    return jnp.dot(a, b, preferred_element_type=F32)


def _ada_kernel(c_ref, b_ref, w_hbm, o_hbm, s_ref, wbuf, obuf, wsem, osem, *, n_chunks):
    c = c_ref[...]
    s_ref[...] = (c * _sigmoid(c)).astype(BF16)

    def wcopy(j, slot):
        return pltpu.make_async_copy(w_hbm.at[:, pl.ds(pl.multiple_of(j * ADA_TN, ADA_TN), ADA_TN)],
                                     wbuf.at[slot], wsem.at[slot])

    def ocopy(j, slot):
        return pltpu.make_async_copy(obuf.at[slot],
                                     o_hbm.at[:, pl.ds(pl.multiple_of(j * ADA_TN, ADA_TN), ADA_TN)], osem.at[slot])

    for t in range(ADA_NB - 1):
        wcopy(jnp.int32(t), t).start(priority=W_DMA_PRIORITY)

    @pl.loop(0, n_chunks)
    def _(j):
        slot = lax.rem(j, ADA_NB)
        oslot = lax.rem(j, 2)
        wcopy(j, slot).wait()

        @pl.when(j + (ADA_NB - 1) < n_chunks)
        def _():
            wcopy(j + (ADA_NB - 1), lax.rem(j + (ADA_NB - 1), ADA_NB)).start(priority=W_DMA_PRIORITY)

        @pl.when(j >= 2)
        def _():
            ocopy(j - 2, oslot).wait()

        obuf[oslot] = _bdot(s_ref[...], wbuf[slot].astype(BF16)) + b_ref[j]
        ocopy(j, oslot).start()

    for t in range(2):
        j_last = n_chunks - 2 + t
        ocopy(jnp.int32(j_last), j_last % 2).wait()


def _ada(c_all, w_ada, b_ada):
    rows = c_all.shape[0]
    cols = w_ada.shape[1]
    n_chunks = cols // ADA_TN
    assert n_chunks >= ADA_NB and n_chunks >= 2
    return pl.pallas_call(
        functools.partial(_ada_kernel, n_chunks=n_chunks),
        grid=(1,),
        in_specs=[
            pl.BlockSpec((rows, D_MODEL), lambda g: (0, 0)),
            pl.BlockSpec((n_chunks, 1, ADA_TN), lambda g: (0, 0, 0)),
            pl.BlockSpec(memory_space=pl.ANY),
        ],
        out_specs=pl.BlockSpec(memory_space=pl.ANY),
        out_shape=jax.ShapeDtypeStruct((rows, cols), F32),
        scratch_shapes=[pltpu.VMEM((rows, D_MODEL), BF16),
                        pltpu.VMEM((ADA_NB, D_MODEL, ADA_TN), F32),
                        pltpu.VMEM((2, rows, ADA_TN), F32),
                        pltpu.SemaphoreType.DMA((ADA_NB,)), pltpu.SemaphoreType.DMA((2,))],
        compiler_params=pltpu.CompilerParams(dimension_semantics=("arbitrary",)),
        name="ada_mod",
    )(c_all, b_ada.reshape(n_chunks, 1, ADA_TN), w_ada)


def _ffn_kernel(xp_ref, xs_ref, shp, scp, gtp, shs, scs, gts, nrm_ref, wg_hbm, wu_hbm, wd_hbm, nf_ref,
                op_ref, os_ref, h_ref, inv_ref, wgb, wub, wdb, wsem, *, n_f, n_tiles, final_norm):
    i = pl.program_id(0)
    tm = xp_ref.shape[0]
    ts = xs_ref.shape[0]

    def wcopies(f, slot):
        col = pl.multiple_of(f * FFN_TF, FFN_TF)
        rows_in = D_MODEL // FFN_DMA_SPLIT
        rows_dn = FFN_TF // FFN_DMA_SPLIT
        out = []
        for k in range(FFN_DMA_SPLIT):
            r_in = pl.ds(k * rows_in, rows_in)
            out.append(pltpu.make_async_copy(wg_hbm.at[r_in, pl.ds(col, FFN_TF)], wgb.at[slot, r_in, :],
                                             wsem.at[slot]))
            out.append(pltpu.make_async_copy(wu_hbm.at[r_in, pl.ds(col, FFN_TF)], wub.at[slot, r_in, :],
                                             wsem.at[slot]))
            out.append(pltpu.make_async_copy(wd_hbm.at[pl.ds(col + k * rows_dn, rows_dn), :],
                                             wdb.at[slot, pl.ds(k * rows_dn, rows_dn), :], wsem.at[slot]))
        return out

    def wstart(f, slot):
        for cp in wcopies(f, slot):
            cp.start(priority=W_DMA_PRIORITY)

    def wwait(f, slot):
        for cp in wcopies(f, slot):
            cp.wait()

    @pl.when(i == 0)
    def _():
        for t in range(FFN_NB - 1):
            wstart(jnp.int32(t), t)

    def head(x, sc, sh):
        return _modulate(x, nrm_ref[...], sc, sh).astype(BF16)

    def tail(x, gt, acc):
        xo = x + 0.5 * gt * acc
        if final_norm:
            inv = lax.rsqrt(jnp.mean(xo * xo, axis=-1, keepdims=True) + EPS)
            xo = (xo * inv) * nf_ref[...]
        return xo

    @pl.loop(0, tm // FFN_ROWS)
    def _(r):
        rows = pl.ds(pl.multiple_of(r * FFN_ROWS, FFN_ROWS), FFN_ROWS)
        x = xp_ref[rows, :]
        inv_ref[rows, :] = lax.rsqrt(jnp.mean(x * x, axis=-1, keepdims=True) + EPS)

    @pl.loop(0, tm // FFN_ROWS)
    def _(r):
        rows = pl.ds(pl.multiple_of(r * FFN_ROWS, FFN_ROWS), FFN_ROWS)
        hx = (xp_ref[rows, :] * inv_ref[rows, :]) * nrm_ref[...] * (1.0 + scp[...]) + shp[...]
        h_ref[rows, :] = hx.astype(BF16)
        op_ref[rows, :] = jnp.zeros((FFN_ROWS, D_MODEL), F32)

    h_ref[pl.ds(tm, ts), :] = head(xs_ref[...], scs[...], shs[...])
    os_ref[...] = jnp.zeros((ts, D_MODEL), F32)

    @pl.loop(0, n_f)
    def _(f):
        g_idx = i * n_f + f
        slot = lax.rem(g_idx, FFN_NB)
        wwait(f, slot)
        nxt = f + (FFN_NB - 1)
        nslot = lax.rem(g_idx + (FFN_NB - 1), FFN_NB)

        @pl.when(jnp.logical_or(nxt < n_f, i < n_tiles - 1))
        def _():
            wstart(jnp.where(nxt < n_f, nxt, nxt - n_f), nslot)

        h = h_ref[...]
        g = _bdot(h, wgb[slot].astype(BF16))
        u = _bdot(h, wub[slot].astype(BF16))
        a = ((g * _sigmoid(g)) * u).astype(BF16)
        for half in range(2):
            hcols = pl.ds(half * (D_MODEL // 2), D_MODEL // 2)
            y = _bdot(a, wdb[slot, :, hcols].astype(BF16))
            op_ref[:, hcols] += y[:tm]
            os_ref[:, hcols] += y[tm:]

    @pl.loop(0, tm // FFN_ROWS)
    def _(r):
        rows = pl.ds(pl.multiple_of(r * FFN_ROWS, FFN_ROWS), FFN_ROWS)
        op_ref[rows, :] = tail(xp_ref[rows, :], gtp[...], op_ref[rows, :])

    os_ref[...] = tail(xs_ref[...], gts[...], os_ref[...])


def _ffn(xp, xs, mod_p3, mod_s, j_sh, j_sc, j_g, nrm, wg, wu, wd, nf, *, tm, ts, tps, final_norm, name):
    n_tiles = xp.shape[0] // tm
    assert xs.shape[0] == n_tiles * ts
    n_f = D_FF // FFN_TF
    assert n_f >= FFN_NB
    any_spec = pl.BlockSpec(memory_space=pl.ANY)

    def pspec(j):
        return pl.BlockSpec((None, 1, D_MODEL), lambda i: (i // tps, 0, j))

    def sspec(j):
        return pl.BlockSpec((ts, D_MODEL), lambda i: (i, j))

    return pl.pallas_call(
        functools.partial(_ffn_kernel, n_f=n_f, n_tiles=n_tiles, final_norm=final_norm),
        grid=(n_tiles,),
        in_specs=[
            pl.BlockSpec((tm, D_MODEL), lambda i: (i, 0)),
            pl.BlockSpec((ts, D_MODEL), lambda i: (i, 0)),
            pspec(j_sh), pspec(j_sc), pspec(j_g),
            sspec(j_sh), sspec(j_sc), sspec(j_g),
            pl.BlockSpec((1, D_MODEL), lambda i: (0, 0)),
            any_spec, any_spec, any_spec,
            pl.BlockSpec((1, D_MODEL), lambda i: (0, 0)),
        ],
        out_specs=[pl.BlockSpec((tm, D_MODEL), lambda i: (i, 0)),
                   pl.BlockSpec((ts, D_MODEL), lambda i: (i, 0))],
        out_shape=[jax.ShapeDtypeStruct(xp.shape, F32), jax.ShapeDtypeStruct(xs.shape, F32)],
        scratch_shapes=[pltpu.VMEM((tm + ts, D_MODEL), BF16), pltpu.VMEM((tm, 1), F32),
                        pltpu.VMEM((FFN_NB, D_MODEL, FFN_TF), F32), pltpu.VMEM((FFN_NB, D_MODEL, FFN_TF), F32),
                        pltpu.VMEM((FFN_NB, FFN_TF, D_MODEL), F32), pltpu.SemaphoreType.DMA((FFN_NB,))],
        compiler_params=pltpu.CompilerParams(
            dimension_semantics=("arbitrary",), vmem_limit_bytes=VMEM_LIMIT_BYTES),
        name=name,
    )(xp, xs, mod_p3, mod_p3, mod_p3, mod_s, mod_s, mod_s, nrm, wg, wu, wd, nf)


K_P, K_CC, K_CH, K_CB, K_GA, K_GB, K_AB, K_O = range(8)
_N_CONV_CHUNKS = CONV_WIDTH // MIX_CH
_N_OUT_CHUNKS = D_MODEL // MIX_CH
_S_CONV = POOL_WIDTH // MIX_CH
_S_GATE = _S_CONV + 3 * _N_CONV_CHUNKS
_S_OUT = _S_GATE + 3 * _N_OUT_CHUNKS
_N_STAGES = _S_OUT + _N_OUT_CHUNKS
_KIND_COL0 = (COL_P, COL_CC, COL_CH, COL_CB, COL_GA, COL_GB, 0, 0)
assert _N_STAGES % MIX_NB == 0


def _decode_stage(t):
    u1 = t - _S_CONV
    u2 = t - _S_GATE
    c1 = lax.div(jnp.maximum(u1, 0), 3)
    c2 = lax.div(jnp.maximum(u2, 0), 3)
    kind = jnp.where(t < _S_CONV, K_P,
                     jnp.where(t < _S_GATE, K_CC + (u1 - 3 * c1),
                               jnp.where(t < _S_OUT, K_GA + (u2 - 3 * c2), K_O)))
    chunk = jnp.where(t < _S_CONV, t, jnp.where(t < _S_GATE, c1, jnp.where(t < _S_OUT, c2, t - _S_OUT)))
    col0 = jnp.int32(0)
    for k, base in enumerate(_KIND_COL0):
        if base:
            col0 = jnp.where(kind == k, base, col0)
    return kind, chunk, col0 + chunk * MIX_CH


def _mixer_kernel(x1p_hbm, x1s, x1pc, x1sc, shp, scp, g2pc, shs, scs, g2sc, nrm, grp, pscale, cw, cbias,
                  stp, stc, w_in, w_a, w_b, w_o,
                  x2p, x2s, pst, cst, nps, ncs,
                  wbuf, wsem, xbuf, xsem, h2, ap, v, m, zb, ybuf, sa, abf, e0, e1, qext, pcar, qcar,
                  *, tm, ts, tps, n_tiles):
    i = pl.program_id(0)
    oc = pl.program_id(1)
    slab = D_MODEL // MIX_DMA_SPLIT
    half = D_MODEL // 2
    prows = pl.ds(0, tm)
    srows = pl.ds(tm, ts)

    def wcopy(src, src_row, col, dst_slot, dst_row):
        return pltpu.make_async_copy(src.at[pl.ds(src_row, slab), pl.ds(col, MIX_CH)],
                                     wbuf.at[dst_slot, pl.ds(dst_row, slab), :], wsem.at[dst_slot])

    def wstart(t):
        k_t, _, col_t = _decode_stage(t)
        col_t = pl.multiple_of(col_t, MIX_CH)
        slot_t = lax.rem(t, MIX_NB)

        @pl.when(k_t < K_AB)
        def _():
            for k in range(MIX_DMA_SPLIT):
                wcopy(w_in, k * slab, col_t, slot_t, k * slab).start(priority=W_DMA_PRIORITY)

        @pl.when(k_t == K_AB)
        def _():
            for k in range(MIX_DMA_SPLIT):
                r0 = k * slab
                src, s0 = (w_a, r0) if r0 < half else (w_b, r0 - half)
                wcopy(src, s0, col_t, slot_t, r0).start(priority=W_DMA_PRIORITY)

        @pl.when(k_t == K_O)
        def _():
            for k in range(MIX_DMA_SPLIT):
                wcopy(w_o, k * slab, col_t, slot_t, k * slab).start(priority=W_DMA_PRIORITY)

    def wwait(slot):
        for k in range(MIX_DMA_SPLIT):
            wcopy(w_in, k * slab, 0, slot, k * slab).wait()

    n_xchunks = tm // X_ROWS

    def xcopy(tile, r):
        return pltpu.make_async_copy(x1p_hbm.at[pl.ds(tile * tm + r * X_ROWS, X_ROWS), :],
                                     xbuf.at[r % X_BUFS], xsem.at[r % X_BUFS])

    def xprefetch(tile):
        for r in range(min(X_BUFS, n_xchunks)):
            xcopy(tile, r).start()

    def prologue():
        for r in range(n_xchunks):
            xcopy(i, r).wait()
            h2[pl.ds(r * X_ROWS, X_ROWS), :] = _modulate(xbuf[r % X_BUFS], nrm[...], scp[...], shp[...]).astype(BF16)
            if r + X_BUFS < n_xchunks:
                xcopy(i, r + X_BUFS).start()
        h2[srows, :] = _modulate(x1s[...], nrm[...], scs[...], shs[...]).astype(BF16)
        for r in range(POOL_STATE - 1):
            nps[r] = stp[r + 1]
        for r in range(CONV_STATE - 1):
            ncs[r] = stc[r + 1]

    def pool_stage(c):
        zbuf = zb.at[0]
        for gl in range(MIX_CH // POOL_GROUP_DIM):
            g = c * (MIX_CH // POOL_GROUP_DIM) + gl
            w = POOL_WINDOWS[g]
            gcols = pl.ds(g * POOL_GROUP_DIM, POOL_GROUP_DIM)
            lcols = pl.ds(gl * POOL_GROUP_DIM, POOL_GROUP_DIM)
            pg = zbuf[prows, lcols]
            pad = jnp.zeros((SUBLANES, POOL_GROUP_DIM), F32)
            e0[pl.ds(SUBLANES, SUBLANES), :] = pad
            e1[pl.ds(SUBLANES, SUBLANES), :] = pad
            e0[pl.ds(CARRY_P, CARRY_P), :] = pcar[:, gcols]
            e0[pl.ds(2 * CARRY_P, tm), :] = pg
            src, dst = e0, e1
            d = 1
            while d < w:
                dst[pl.ds(CARRY_P, tm + CARRY_P), :] = (src[pl.ds(CARRY_P, tm + CARRY_P), :]
                                                        + src[pl.ds(CARRY_P - d, tm + CARRY_P), :])
                src, dst = dst, src
                d *= 2
            wsum = src[pl.ds(2 * CARRY_P, tm), :]
            pos = (i % tps) * tm + lax.broadcasted_iota(jnp.int32, (CARRY_P, 1), 0)
            cnt = jnp.minimum(pos + 1, w).astype(F32)
            abf[prows, :] = (wsum * (1.0 / w) - pg).astype(BF16)
            head = pl.ds(0, CARRY_P)
            abf[head, :] = (wsum[:CARRY_P, :] / cnt - pg[:CARRY_P, :]).astype(BF16)
            tail = pg[tm - CARRY_P:, :]
            pcar[:, gcols] = tail
            pst[:, gcols] = tail
            ps = zbuf[srows, lcols]
            ssum = ps
            for r in range(POOL_STATE + 1 - w, POOL_STATE):
                ssum = ssum + stp[r, :, gcols]
            abf[srows, :] = (ssum / float(min(PAST_LEN + 1, w)) - ps).astype(BF16)
            nps[POOL_STATE - 1, :, gcols] = ps
            ag = _bdot(abf[...], grp[g].astype(BF16)) * pscale[:, gcols]
            ap[g] = ag.astype(BF16)

    def conv_stage(c):
        ccols = pl.ds(c * MIX_CH, MIX_CH)
        zbuf, zprev = zb.at[0], zb.at[1]
        qp = zprev[prows, :] * zbuf[prows, :]
        qext[pl.ds(0, CARRY_Q), :] = qcar[:, ccols]
        qext[pl.ds(CARRY_Q, tm), :] = qp
        y = cbias[:, ccols] + qext[pl.ds(CARRY_Q - 2, tm), :] * cw[0:1, ccols]
        y = y + qext[pl.ds(CARRY_Q - 1, tm), :] * cw[1:2, ccols]
        tail = qp[tm - CARRY_Q:, :]
        qcar[:, ccols] = tail
        cst[:, ccols] = tail
        qs = zprev[srows, :] * zbuf[srows, :]
        ys = cbias[:, ccols] + stc[0, :, ccols] * cw[0:1, ccols]
        ys = ys + stc[1, :, ccols] * cw[1:2, ccols]
        ncs[CONV_STATE - 1, :, ccols] = qs
        ybuf[prows, :] = y + qp * cw[2:3, ccols]
        ybuf[srows, :] = ys + qs * cw[2:3, ccols]

    def main_stage(s):
        kind, chunk, _ = _decode_stage(s)
        slot = lax.rem(s, MIX_NB)
        wwait(slot)
        wstart(s + (MIX_NB - 1))

        keep = jnp.logical_or(kind == K_CC, kind == K_GA)

        @pl.when(keep)
        def _():
            zb[1] = _bdot(h2[...], wbuf[slot].astype(BF16))

        @pl.when(kind == K_GB)
        def _():
            sa[...] = _sigmoid(zb[1])
            zb[0] = _bdot(h2[...], wbuf[slot].astype(BF16))

        @pl.when(jnp.logical_and(kind < K_GA, jnp.logical_not(keep)))
        def _():
            zb[0] = _bdot(h2[...], wbuf[slot].astype(BF16))

        for c in range(POOL_WIDTH // MIX_CH):
            @pl.when(jnp.logical_and(kind == K_P, chunk == c))
            def _(c=c):
                pool_stage(c)

        for c in range(_N_CONV_CHUNKS):
            @pl.when(jnp.logical_and(kind == K_CH, chunk == c))
            def _(c=c):
                conv_stage(c)

        @pl.when(kind == K_CB)
        def _():
            v[chunk] = (zb[0] * ybuf[...]).astype(BF16)

        @pl.when(kind == K_AB)
        def _():
            sb = _sigmoid(zb[0])
            for hc in range(MIX_CH // MIX_HALF):
                hcols = pl.ds(hc * MIX_HALF, MIX_HALF)
                ua = None
                for g in range(len(POOL_WINDOWS)):
                    part = _bdot(ap[g], wbuf[slot, pl.ds(g * POOL_GROUP_DIM, POOL_GROUP_DIM), hcols].astype(BF16))
                    ua = part if ua is None else ua + part
                ub = None
                for k in range(_N_CONV_CHUNKS):
                    part = _bdot(v[k], wbuf[slot, pl.ds(half + k * MIX_CH, MIX_CH), hcols].astype(BF16))
                    ub = part if ub is None else ub + part
                sbh = sb[:, hc * MIX_HALF:(hc + 1) * MIX_HALF]
                m[chunk, :, hcols] = (sa[:, hcols] * ua + sbh * ub).astype(BF16)

    assert _S_OUT + MIX_NB - 2 < _N_STAGES

    @pl.when(oc == 0)
    def _():
        @pl.when(i == 0)
        def _():
            for t in range(MIX_NB - 1):
                wstart(jnp.int32(t))
            xprefetch(0)

        @pl.when(i % tps == 0)
        def _():
            pcar[...] = jnp.zeros(pcar.shape, F32)
            qcar[...] = jnp.zeros(qcar.shape, F32)

        prologue()
        pl.loop(0, _S_OUT)(main_stage)

    s_out = _S_OUT + oc
    slot = lax.rem(s_out, MIX_NB)
    wwait(slot)
    nxt = s_out + (MIX_NB - 1)

    @pl.when(jnp.logical_or(nxt < _N_STAGES, i < n_tiles - 1))
    def _():
        wstart(jnp.where(nxt < _N_STAGES, nxt, nxt - _N_STAGES))

    @pl.when(jnp.logical_and(oc == 1, i < n_tiles - 1))
    def _():
        xprefetch(i + 1)

    for hc in range(MIX_CH // MIX_HALF):
        hcols = pl.ds(hc * MIX_HALF, MIX_HALF)
        acc = None
        for k in range(_N_OUT_CHUNKS):
            part = _bdot(m[k], wbuf[slot, pl.ds(k * MIX_CH, MIX_CH), hcols].astype(BF16))
            acc = part if acc is None else acc + part
        x2p[:, hcols] = x1pc[:, hcols] + g2pc[:, hcols] * acc[:tm]
        x2s[:, hcols] = x1sc[:, hcols] + g2sc[:, hcols] * acc[tm:]


def _mixer(x1p, x1s, mod_p3, mod_s, nrm, grp, pscale, cw, cbias, stp_t, stc_t, w_in, w_a, w_b, w_o, *, tm, ts, tps):
    n_tiles = x1p.shape[0] // tm
    n_seq = n_tiles // tps
    n_s = x1s.shape[0]
    assert n_s == n_tiles * ts
    rows = tm + ts
    any_spec = pl.BlockSpec(memory_space=pl.ANY)

    def const(shape):
        return pl.BlockSpec(shape, lambda i, s: (0,) * len(shape))

    def oc(s):
        return s

    g2_blk = (MOD_G2 * D_MODEL) // MIX_CH
    in_specs = [
        any_spec,
        pl.BlockSpec((ts, D_MODEL), lambda i, s: (i, 0)),
        pl.BlockSpec((tm, MIX_CH), lambda i, s: (i, oc(s))),
        pl.BlockSpec((ts, MIX_CH), lambda i, s: (i, oc(s))),
        pl.BlockSpec((None, 1, D_MODEL), lambda i, s: (i // tps, 0, MOD_SH2)),
        pl.BlockSpec((None, 1, D_MODEL), lambda i, s: (i // tps, 0, MOD_SC2)),
        pl.BlockSpec((None, 1, MIX_CH), lambda i, s: (i // tps, 0, g2_blk + oc(s))),
        pl.BlockSpec((ts, D_MODEL), lambda i, s: (i, MOD_SH2)),
        pl.BlockSpec((ts, D_MODEL), lambda i, s: (i, MOD_SC2)),
        pl.BlockSpec((ts, MIX_CH), lambda i, s: (i, g2_blk + oc(s))),
        const((1, D_MODEL)),
        const((len(POOL_WINDOWS), POOL_GROUP_DIM, POOL_GROUP_DIM)),
        const((1, POOL_WIDTH)),
        const((3, CONV_WIDTH)),
        const((1, CONV_WIDTH)),
        pl.BlockSpec((POOL_STATE, ts, POOL_WIDTH), lambda i, s: (0, i, 0)),
        pl.BlockSpec((CONV_STATE, ts, CONV_WIDTH), lambda i, s: (0, i, 0)),
        any_spec, any_spec, any_spec, any_spec,
    ]
    out_specs = [
        pl.BlockSpec((tm, MIX_CH), lambda i, s: (i, oc(s))),
        pl.BlockSpec((ts, MIX_CH), lambda i, s: (i, oc(s))),
        pl.BlockSpec((None, CARRY_P, POOL_WIDTH), lambda i, s: (i // tps, 0, 0)),
        pl.BlockSpec((None, CARRY_Q, CONV_WIDTH), lambda i, s: (i // tps, 0, 0)),
        pl.BlockSpec((POOL_STATE, ts, POOL_WIDTH), lambda i, s: (0, i, 0)),
        pl.BlockSpec((CONV_STATE, ts, CONV_WIDTH), lambda i, s: (0, i, 0)),
    ]
    out_shape = [
        jax.ShapeDtypeStruct(x1p.shape, F32),
        jax.ShapeDtypeStruct(x1s.shape, F32),
        jax.ShapeDtypeStruct((n_seq, CARRY_P, POOL_WIDTH), F32),
        jax.ShapeDtypeStruct((n_seq, CARRY_Q, CONV_WIDTH), F32),
        jax.ShapeDtypeStruct((POOL_STATE, n_s, POOL_WIDTH), F32),
        jax.ShapeDtypeStruct((CONV_STATE, n_s, CONV_WIDTH), F32),
    ]
    scratch = [
        pltpu.VMEM((MIX_NB, D_MODEL, MIX_CH), F32),
        pltpu.SemaphoreType.DMA((MIX_NB,)),
        pltpu.VMEM((X_BUFS, X_ROWS, D_MODEL), F32),
        pltpu.SemaphoreType.DMA((X_BUFS,)),
        pltpu.VMEM((rows, D_MODEL), BF16),
        pltpu.VMEM((len(POOL_WINDOWS), rows, POOL_GROUP_DIM), BF16),
        pltpu.VMEM((_N_CONV_CHUNKS, rows, MIX_CH), BF16),
        pltpu.VMEM((_N_OUT_CHUNKS, rows, MIX_CH), BF16),
        pltpu.VMEM((2, rows, MIX_CH), F32),
        pltpu.VMEM((rows, MIX_CH), F32),
        pltpu.VMEM((rows, MIX_CH), F32),
        pltpu.VMEM((rows, POOL_GROUP_DIM), BF16),
        pltpu.VMEM((tm + 2 * CARRY_P, POOL_GROUP_DIM), F32),
        pltpu.VMEM((tm + 2 * CARRY_P, POOL_GROUP_DIM), F32),
        pltpu.VMEM((tm + CARRY_Q, MIX_CH), F32),
        pltpu.VMEM((CARRY_P, POOL_WIDTH), F32),
        pltpu.VMEM((CARRY_Q, CONV_WIDTH), F32),
    ]
    return pl.pallas_call(
        functools.partial(_mixer_kernel, tm=tm, ts=ts, tps=tps, n_tiles=n_tiles),
        grid=(n_tiles, _N_OUT_CHUNKS),
        in_specs=in_specs,
        out_specs=out_specs,
        out_shape=out_shape,
        scratch_shapes=scratch,
        compiler_params=pltpu.CompilerParams(
            dimension_semantics=("arbitrary", "arbitrary"), vmem_limit_bytes=VMEM_LIMIT_BYTES),
        name="mixer",
    )(x1p, x1s, x1p, x1s, mod_p3, mod_p3, mod_p3, mod_s, mod_s, mod_s, nrm, grp, pscale, cw, cbias,
      stp_t, stc_t, w_in, w_a, w_b, w_o)


def kernel(x_prompt, x_sample, c_prompt, c_sample, state_pool, state_conv, w_ada, b_ada, norm1, ffn1_gate, ffn1_up, ffn1_down, norm2, w_in, pool_grp, pool_scale, w_branch_a, conv_w, conv_b, w_branch_b, w_o, norm3, ffn2_gate, ffn2_up, ffn2_down, norm_final):
    assert w_ada.shape[0] == 1, "single-layer step"
    batch, seq, d = x_prompt.shape
    dec_batch = x_sample.shape[0]
    assert d == D_MODEL and x_sample.shape[1] == 1

    tm = TM_PROMPT
    tps = seq // tm
    n_tiles = batch * tps
    ts = dec_batch // n_tiles
    assert seq % tm == 0 and dec_batch % n_tiles == 0 and ts % (2 * SUBLANES) == 0

    n_c = batch + dec_batch
    pad = (-n_c) % SUBLANES
    c_all = jnp.concatenate([c_sample, c_prompt, jnp.zeros((pad, d), F32)], axis=0)
    mod = _ada(c_all, w_ada[0], b_ada)
    mod_p3 = mod[dec_batch:n_c].reshape(batch, 1, N_MOD * d)

    row = lambda a: a.reshape(1, -1)
    xp = x_prompt.reshape(batch * seq, d)
    xs = x_sample.reshape(dec_batch, d)
    nf = row(norm_final)
    x1p, x1s = _ffn(xp, xs, mod_p3, mod, MOD_SH1, MOD_SC1, MOD_G1, row(norm1[0]), ffn1_gate[0], ffn1_up[0],
                    ffn1_down[0], nf, tm=tm, ts=ts, tps=tps, final_norm=False, name="ffn1")
    stp_t = jnp.transpose(state_pool[0], (1, 0, 2))
    stc_t = jnp.transpose(state_conv[0], (1, 0, 2))
    x2p, x2s, pst, cst, nps_t, ncs_t = _mixer(
        x1p, x1s, mod_p3, mod, row(norm2[0]), pool_grp[0], row(pool_scale[0]), conv_w[0], row(conv_b[0]),
        stp_t, stc_t, w_in[0], w_branch_a[0], w_branch_b[0], w_o[0], tm=tm, ts=ts, tps=tps)
    yp, ys = _ffn(x2p, x2s, mod_p3, mod, MOD_SH3, MOD_SC3, MOD_G3, row(norm3[0]), ffn2_gate[0], ffn2_up[0],
                  ffn2_down[0], nf, tm=tm, ts=ts, tps=tps, final_norm=True, name="ffn2")

    y_prompt = yp.reshape(batch, seq, d)
    y_sample = ys.reshape(dec_batch, 1, d)
    new_pool_prompt = pst[None, :, CARRY_P - POOL_STATE:, :]
    new_conv_prompt = cst[None, :, CARRY_Q - CONV_STATE:, :]
    new_pool_sample = jnp.transpose(nps_t, (1, 0, 2))[None]
    new_conv_sample = jnp.transpose(ncs_t, (1, 0, 2))[None]
    return (y_prompt, y_sample, new_pool_prompt, new_conv_prompt, new_pool_sample, new_conv_sample)
```

```python
import functools

import jax
import jax.numpy as jnp
from jax import lax
from jax.experimental import pallas as pl
from jax.experimental.pallas import tpu as pltpu

F32 = jnp.float32
BF16 = jnp.bfloat16

D_MODEL = 2048
D_FF = 5632
POOL_WIDTH = 1024
CONV_WIDTH = 1024
POOL_WINDOWS = (2, 4, 8, 16)
POOL_GROUP_DIM = 256
POOL_STATE = 15
CONV_STATE = 2
N_MOD = 9
EPS = 1e-6
PAST_LEN = 16384

V7X_VMEM_BYTES = 64 * 1024 * 1024
SUBLANES = 8
VMEM_LIMIT_BYTES = V7X_VMEM_BYTES - 2 * 1024 * 1024

TM_PROMPT = 1024

ADA_TN = 1024
ADA_NB = 3
FFN_TF = 256
FFN_DMA_SPLIT = 1
FFN_NB = 3
MIX_CH = 512
MIX_HALF = 256
MIX_NB = 3
X_ROWS = 128
FFN_ROWS = 128
X_BUFS = 4
MIX_DMA_SPLIT = 2
W_DMA_PRIORITY = 1
CARRY_P = 16
CARRY_Q = 8

COL_P, COL_CB, COL_CC, COL_CH, COL_GA, COL_GB = 0, 1024, 2048, 3072, 4096, 6144
MOD_SH1, MOD_SC1, MOD_G1, MOD_SH2, MOD_SC2, MOD_G2, MOD_SH3, MOD_SC3, MOD_G3 = range(9)


def _sigmoid(x):
    return 1.0 / (1.0 + jnp.exp(-x))


def _modulate(x, nrm, sc, sh):
    inv = lax.rsqrt(jnp.mean(x * x, axis=-1, keepdims=True) + EPS)
    return (x * inv) * nrm * (1.0 + sc) + sh


def _bdot(a, b):
    return jnp.dot(a, b, preferred_element_type=F32)


def _ada_kernel(c_ref, b_ref, w_hbm, o_hbm, s_ref, wbuf, obuf, wsem, osem, *, n_chunks):
    c = c_ref[...]
    s_ref[...] = (c * _sigmoid(c)).astype(BF16)

    def wcopy(j, slot):
        return pltpu.make_async_copy(w_hbm.at[:, pl.ds(pl.multiple_of(j * ADA_TN, ADA_TN), ADA_TN)],
                                     wbuf.at[slot], wsem.at[slot])

    def ocopy(j, slot):
        return pltpu.make_async_copy(obuf.at[slot],
                                     o_hbm.at[:, pl.ds(pl.multiple_of(j * ADA_TN, ADA_TN), ADA_TN)], osem.at[slot])

    for t in range(ADA_NB - 1):
        wcopy(jnp.int32(t), t).start(priority=W_DMA_PRIORITY)

    @pl.loop(0, n_chunks)
    def _(j):
        slot = lax.rem(j, ADA_NB)
        oslot = lax.rem(j, 2)
        wcopy(j, slot).wait()

        @pl.when(j + (ADA_NB - 1) < n_chunks)
        def _():
            wcopy(j + (ADA_NB - 1), lax.rem(j + (ADA_NB - 1), ADA_NB)).start(priority=W_DMA_PRIORITY)

        @pl.when(j >= 2)
        def _():
            ocopy(j - 2, oslot).wait()

        obuf[oslot] = _bdot(s_ref[...], wbuf[slot].astype(BF16)) + b_ref[j]
        ocopy(j, oslot).start()

    for t in range(2):
        j_last = n_chunks - 2 + t
        ocopy(jnp.int32(j_last), j_last % 2).wait()


def _ada(c_all, w_ada, b_ada):
    rows = c_all.shape[0]
    cols = w_ada.shape[1]
    n_chunks = cols // ADA_TN
    assert n_chunks >= ADA_NB and n_chunks >= 2
    return pl.pallas_call(
        functools.partial(_ada_kernel, n_chunks=n_chunks),
        grid=(1,),
        in_specs=[
            pl.BlockSpec((rows, D_MODEL), lambda g: (0, 0)),
            pl.BlockSpec((n_chunks, 1, ADA_TN), lambda g: (0, 0, 0)),
            pl.BlockSpec(memory_space=pl.ANY),
        ],
        out_specs=pl.BlockSpec(memory_space=pl.ANY),
        out_shape=jax.ShapeDtypeStruct((rows, cols), F32),
        scratch_shapes=[pltpu.VMEM((rows, D_MODEL), BF16),
                        pltpu.VMEM((ADA_NB, D_MODEL, ADA_TN), F32),
                        pltpu.VMEM((2, rows, ADA_TN), F32),
                        pltpu.SemaphoreType.DMA((ADA_NB,)), pltpu.SemaphoreType.DMA((2,))],
        compiler_params=pltpu.CompilerParams(dimension_semantics=("arbitrary",)),
        name="ada_mod",
    )(c_all, b_ada.reshape(n_chunks, 1, ADA_TN), w_ada)


def _ffn_kernel(xp_ref, xs_ref, shp, scp, gtp, shs, scs, gts, nrm_ref, wg_hbm, wu_hbm, wd_hbm, nf_ref,
                op_ref, os_ref, h_ref, inv_ref, wgb, wub, wdb, wsem, *, n_f, n_tiles, final_norm):
    i = pl.program_id(0)
    tm = xp_ref.shape[0]
    ts = xs_ref.shape[0]

    def wcopies(f, slot):
        col = pl.multiple_of(f * FFN_TF, FFN_TF)
        rows_in = D_MODEL // FFN_DMA_SPLIT
        rows_dn = FFN_TF // FFN_DMA_SPLIT
        out = []
        for k in range(FFN_DMA_SPLIT):
            r_in = pl.ds(k * rows_in, rows_in)
            out.append(pltpu.make_async_copy(wg_hbm.at[r_in, pl.ds(col, FFN_TF)], wgb.at[slot, r_in, :],
                                             wsem.at[slot]))
            out.append(pltpu.make_async_copy(wu_hbm.at[r_in, pl.ds(col, FFN_TF)], wub.at[slot, r_in, :],
                                             wsem.at[slot]))
            out.append(pltpu.make_async_copy(wd_hbm.at[pl.ds(col + k * rows_dn, rows_dn), :],
                                             wdb.at[slot, pl.ds(k * rows_dn, rows_dn), :], wsem.at[slot]))
        return out

    def wstart(f, slot):
        for cp in wcopies(f, slot):
            cp.start(priority=W_DMA_PRIORITY)

    def wwait(f, slot):
        for cp in wcopies(f, slot):
            cp.wait()

    @pl.when(i == 0)
    def _():
        for t in range(FFN_NB - 1):
            wstart(jnp.int32(t), t)

    def head(x, sc, sh):
        return _modulate(x, nrm_ref[...], sc, sh).astype(BF16)

    def tail(x, gt, acc):
        xo = x + 0.5 * gt * acc
        if final_norm:
            inv = lax.rsqrt(jnp.mean(xo * xo, axis=-1, keepdims=True) + EPS)
            xo = (xo * inv) * nf_ref[...]
        return xo

    @pl.loop(0, tm // FFN_ROWS)
    def _(r):
        rows = pl.ds(pl.multiple_of(r * FFN_ROWS, FFN_ROWS), FFN_ROWS)
        x = xp_ref[rows, :]
        inv_ref[rows, :] = lax.rsqrt(jnp.mean(x * x, axis=-1, keepdims=True) + EPS)

    @pl.loop(0, tm // FFN_ROWS)
    def _(r):
        rows = pl.ds(pl.multiple_of(r * FFN_ROWS, FFN_ROWS), FFN_ROWS)
        hx = (xp_ref[rows, :] * inv_ref[rows, :]) * nrm_ref[...] * (1.0 + scp[...]) + shp[...]
        h_ref[rows, :] = hx.astype(BF16)
        op_ref[rows, :] = jnp.zeros((FFN_ROWS, D_MODEL), F32)

    h_ref[pl.ds(tm, ts), :] = head(xs_ref[...], scs[...], shs[...])
    os_ref[...] = jnp.zeros((ts, D_MODEL), F32)

    @pl.loop(0, n_f)
    def _(f):
        g_idx = i * n_f + f
        slot = lax.rem(g_idx, FFN_NB)
        wwait(f, slot)
        nxt = f + (FFN_NB - 1)
        nslot = lax.rem(g_idx + (FFN_NB - 1), FFN_NB)

        wstart(jnp.where(nxt < n_f, nxt, nxt - n_f), nslot)

        h = h_ref[...]
        g = _bdot(h, wgb[slot].astype(BF16))
        u = _bdot(h, wub[slot].astype(BF16))
        a = ((g * _sigmoid(g)) * u).astype(BF16)
        for half in range(2):
            hcols = pl.ds(half * (D_MODEL // 2), D_MODEL // 2)
            y = _bdot(a, wdb[slot, :, hcols].astype(BF16))
            op_ref[:, hcols] += y[:tm]
            os_ref[:, hcols] += y[tm:]

    @pl.when(i == n_tiles - 1)
    def _():
        for t in range(FFN_NB - 1):
            wwait(jnp.int32(t), (n_tiles * n_f + t) % FFN_NB)

    @pl.loop(0, tm // FFN_ROWS)
    def _(r):
        rows = pl.ds(pl.multiple_of(r * FFN_ROWS, FFN_ROWS), FFN_ROWS)
        op_ref[rows, :] = tail(xp_ref[rows, :], gtp[...], op_ref[rows, :])

    os_ref[...] = tail(xs_ref[...], gts[...], os_ref[...])


def _ffn(xp, xs, mod_p3, mod_s, j_sh, j_sc, j_g, nrm, wg, wu, wd, nf, *, tm, ts, tps, final_norm, name):
    n_tiles = xp.shape[0] // tm
    assert xs.shape[0] == n_tiles * ts
    n_f = D_FF // FFN_TF
    assert n_f >= FFN_NB
    any_spec = pl.BlockSpec(memory_space=pl.ANY)

    def pspec(j):
        return pl.BlockSpec((None, 1, D_MODEL), lambda i: (i // tps, 0, j))

    def sspec(j):
        return pl.BlockSpec((ts, D_MODEL), lambda i: (i, j))

    return pl.pallas_call(
        functools.partial(_ffn_kernel, n_f=n_f, n_tiles=n_tiles, final_norm=final_norm),
        grid=(n_tiles,),
        in_specs=[
            pl.BlockSpec((tm, D_MODEL), lambda i: (i, 0)),
            pl.BlockSpec((ts, D_MODEL), lambda i: (i, 0)),
            pspec(j_sh), pspec(j_sc), pspec(j_g),
            sspec(j_sh), sspec(j_sc), sspec(j_g),
            pl.BlockSpec((1, D_MODEL), lambda i: (0, 0)),
            any_spec, any_spec, any_spec,
            pl.BlockSpec((1, D_MODEL), lambda i: (0, 0)),
        ],
        out_specs=[pl.BlockSpec((tm, D_MODEL), lambda i: (i, 0)),
                   pl.BlockSpec((ts, D_MODEL), lambda i: (i, 0))],
        out_shape=[jax.ShapeDtypeStruct(xp.shape, F32), jax.ShapeDtypeStruct(xs.shape, F32)],
        scratch_shapes=[pltpu.VMEM((tm + ts, D_MODEL), BF16), pltpu.VMEM((tm, 1), F32),
                        pltpu.VMEM((FFN_NB, D_MODEL, FFN_TF), F32), pltpu.VMEM((FFN_NB, D_MODEL, FFN_TF), F32),
                        pltpu.VMEM((FFN_NB, FFN_TF, D_MODEL), F32), pltpu.SemaphoreType.DMA((FFN_NB,))],
        compiler_params=pltpu.CompilerParams(
            dimension_semantics=("arbitrary",), vmem_limit_bytes=VMEM_LIMIT_BYTES),
        name=name,
    )(xp, xs, mod_p3, mod_p3, mod_p3, mod_s, mod_s, mod_s, nrm, wg, wu, wd, nf)


K_P, K_CC, K_CH, K_CB, K_GA, K_GB, K_AB, K_O = range(8)
_N_CONV_CHUNKS = CONV_WIDTH // MIX_CH
_N_OUT_CHUNKS = D_MODEL // MIX_CH
_S_CONV = POOL_WIDTH // MIX_CH
_S_GATE = _S_CONV + 3 * _N_CONV_CHUNKS
_S_OUT = _S_GATE + 3 * _N_OUT_CHUNKS
_N_STAGES = _S_OUT + _N_OUT_CHUNKS
_KIND_COL0 = (COL_P, COL_CC, COL_CH, COL_CB, COL_GA, COL_GB, 0, 0)
assert _N_STAGES % MIX_NB == 0


def _decode_stage(t):
    u1 = t - _S_CONV
    u2 = t - _S_GATE
    c1 = lax.div(jnp.maximum(u1, 0), 3)
    c2 = lax.div(jnp.maximum(u2, 0), 3)
    kind = jnp.where(t < _S_CONV, K_P,
                     jnp.where(t < _S_GATE, K_CC + (u1 - 3 * c1),
                               jnp.where(t < _S_OUT, K_GA + (u2 - 3 * c2), K_O)))
    chunk = jnp.where(t < _S_CONV, t, jnp.where(t < _S_GATE, c1, jnp.where(t < _S_OUT, c2, t - _S_OUT)))
    col0 = jnp.int32(0)
    for k, base in enumerate(_KIND_COL0):
        if base:
            col0 = jnp.where(kind == k, base, col0)
    return kind, chunk, col0 + chunk * MIX_CH


def _mixer_kernel(x1p_hbm, x1s, x1pc, x1sc, shp, scp, g2pc, shs, scs, g2sc, nrm, grp, pscale, cw, cbias,
                  stp, stc, w_in, w_a, w_b, w_o,
                  x2p, x2s, pst, cst, nps, ncs,
                  wbuf, wsem, xbuf, xsem, h2, ap, v, m, zb, ybuf, sa, abf, e0, e1, qext, pcar, qcar,
                  *, tm, ts, tps, n_tiles):
    i = pl.program_id(0)
    oc = pl.program_id(1)
    slab = D_MODEL // MIX_DMA_SPLIT
    half = D_MODEL // 2
    prows = pl.ds(0, tm)
    srows = pl.ds(tm, ts)

    def wcopy(src, src_row, col, dst_slot, dst_row):
        return pltpu.make_async_copy(src.at[pl.ds(src_row, slab), pl.ds(col, MIX_CH)],
                                     wbuf.at[dst_slot, pl.ds(dst_row, slab), :], wsem.at[dst_slot])

    def wstart(t):
        k_t, _, col_t = _decode_stage(t)
        col_t = pl.multiple_of(col_t, MIX_CH)
        slot_t = lax.rem(t, MIX_NB)

        @pl.when(k_t < K_AB)
        def _():
            for k in range(MIX_DMA_SPLIT):
                wcopy(w_in, k * slab, col_t, slot_t, k * slab).start(priority=W_DMA_PRIORITY)

        @pl.when(k_t == K_AB)
        def _():
            for k in range(MIX_DMA_SPLIT):
                r0 = k * slab
                src, s0 = (w_a, r0) if r0 < half else (w_b, r0 - half)
                wcopy(src, s0, col_t, slot_t, r0).start(priority=W_DMA_PRIORITY)

        @pl.when(k_t == K_O)
        def _():
            for k in range(MIX_DMA_SPLIT):
                wcopy(w_o, k * slab, col_t, slot_t, k * slab).start(priority=W_DMA_PRIORITY)

    def wwait(slot):
        for k in range(MIX_DMA_SPLIT):
            wcopy(w_in, k * slab, 0, slot, k * slab).wait()

    n_xchunks = tm // X_ROWS

    def xcopy(tile, r):
        return pltpu.make_async_copy(x1p_hbm.at[pl.ds(tile * tm + r * X_ROWS, X_ROWS), :],
                                     xbuf.at[r % X_BUFS], xsem.at[r % X_BUFS])

    def xprefetch(tile):
        for r in range(min(X_BUFS, n_xchunks)):
            xcopy(tile, r).start()

    def prologue():
        for r in range(n_xchunks):
            xcopy(i, r).wait()
            h2[pl.ds(r * X_ROWS, X_ROWS), :] = _modulate(xbuf[r % X_BUFS], nrm[...], scp[...], shp[...]).astype(BF16)
            if r + X_BUFS < n_xchunks:
                xcopy(i, r + X_BUFS).start()
        h2[srows, :] = _modulate(x1s[...], nrm[...], scs[...], shs[...]).astype(BF16)
        for r in range(POOL_STATE - 1):
            nps[r] = stp[r + 1]
        for r in range(CONV_STATE - 1):
            ncs[r] = stc[r + 1]

    def pool_stage(c):
        zbuf = zb.at[0]
        for gl in range(MIX_CH // POOL_GROUP_DIM):
            g = c * (MIX_CH // POOL_GROUP_DIM) + gl
            w = POOL_WINDOWS[g]
            gcols = pl.ds(g * POOL_GROUP_DIM, POOL_GROUP_DIM)
            lcols = pl.ds(gl * POOL_GROUP_DIM, POOL_GROUP_DIM)
            pg = zbuf[prows, lcols]
            pad = jnp.zeros((SUBLANES, POOL_GROUP_DIM), F32)
            e0[pl.ds(SUBLANES, SUBLANES), :] = pad
            e1[pl.ds(SUBLANES, SUBLANES), :] = pad
            e0[pl.ds(CARRY_P, CARRY_P), :] = pcar[:, gcols]
            e0[pl.ds(2 * CARRY_P, tm), :] = pg
            src, dst = e0, e1
            d = 1
            while d < w:
                dst[pl.ds(CARRY_P, tm + CARRY_P), :] = (src[pl.ds(CARRY_P, tm + CARRY_P), :]
                                                        + src[pl.ds(CARRY_P - d, tm + CARRY_P), :])
                src, dst = dst, src
                d *= 2
            wsum = src[pl.ds(2 * CARRY_P, tm), :]
            pos = (i % tps) * tm + lax.broadcasted_iota(jnp.int32, (CARRY_P, 1), 0)
            cnt = jnp.minimum(pos + 1, w).astype(F32)
            abf[prows, :] = (wsum * (1.0 / w) - pg).astype(BF16)
            head = pl.ds(0, CARRY_P)
            abf[head, :] = (wsum[:CARRY_P, :] / cnt - pg[:CARRY_P, :]).astype(BF16)
            tail = pg[tm - CARRY_P:, :]
            pcar[:, gcols] = tail
            pst[:, gcols] = tail
            ps = zbuf[srows, lcols]
            ssum = ps
            for r in range(POOL_STATE + 1 - w, POOL_STATE):
                ssum = ssum + stp[r, :, gcols]
            abf[srows, :] = (ssum / float(min(PAST_LEN + 1, w)) - ps).astype(BF16)
            nps[POOL_STATE - 1, :, gcols] = ps
            ag = _bdot(abf[...], grp[g].astype(BF16)) * pscale[:, gcols]
            ap[g] = ag.astype(BF16)

    def conv_stage(c):
        ccols = pl.ds(c * MIX_CH, MIX_CH)
        zbuf, zprev = zb.at[0], zb.at[1]
        qp = zprev[prows, :] * zbuf[prows, :]
        qext[pl.ds(0, CARRY_Q), :] = qcar[:, ccols]
        qext[pl.ds(CARRY_Q, tm), :] = qp
        y = cbias[:, ccols] + qext[pl.ds(CARRY_Q - 2, tm), :] * cw[0:1, ccols]
        y = y + qext[pl.ds(CARRY_Q - 1, tm), :] * cw[1:2, ccols]
        tail = qp[tm - CARRY_Q:, :]
        qcar[:, ccols] = tail
        cst[:, ccols] = tail
        qs = zprev[srows, :] * zbuf[srows, :]
        ys = cbias[:, ccols] + stc[0, :, ccols] * cw[0:1, ccols]
        ys = ys + stc[1, :, ccols] * cw[1:2, ccols]
        ncs[CONV_STATE - 1, :, ccols] = qs
        ybuf[prows, :] = y + qp * cw[2:3, ccols]
        ybuf[srows, :] = ys + qs * cw[2:3, ccols]

    def main_stage(s):
        kind, chunk, _ = _decode_stage(s)
        slot = lax.rem(s, MIX_NB)
        wwait(slot)
        wstart(s + (MIX_NB - 1))

        keep = jnp.logical_or(kind == K_CC, kind == K_GA)

        @pl.when(keep)
        def _():
            zb[1] = _bdot(h2[...], wbuf[slot].astype(BF16))

        @pl.when(kind == K_GB)
        def _():
            sa[...] = _sigmoid(zb[1])
            zb[0] = _bdot(h2[...], wbuf[slot].astype(BF16))

        @pl.when(jnp.logical_and(kind < K_GA, jnp.logical_not(keep)))
        def _():
            zb[0] = _bdot(h2[...], wbuf[slot].astype(BF16))

        for c in range(POOL_WIDTH // MIX_CH):
            @pl.when(jnp.logical_and(kind == K_P, chunk == c))
            def _(c=c):
                pool_stage(c)

        for c in range(_N_CONV_CHUNKS):
            @pl.when(jnp.logical_and(kind == K_CH, chunk == c))
            def _(c=c):
                conv_stage(c)

        @pl.when(kind == K_CB)
        def _():
            v[chunk] = (zb[0] * ybuf[...]).astype(BF16)

        @pl.when(kind == K_AB)
        def _():
            sb = _sigmoid(zb[0])
            for hc in range(MIX_CH // MIX_HALF):
                hcols = pl.ds(hc * MIX_HALF, MIX_HALF)
                ua = None
                for g in range(len(POOL_WINDOWS)):
                    part = _bdot(ap[g], wbuf[slot, pl.ds(g * POOL_GROUP_DIM, POOL_GROUP_DIM), hcols].astype(BF16))
                    ua = part if ua is None else ua + part
                ub = None
                for k in range(_N_CONV_CHUNKS):
                    part = _bdot(v[k], wbuf[slot, pl.ds(half + k * MIX_CH, MIX_CH), hcols].astype(BF16))
                    ub = part if ub is None else ub + part
                sbh = sb[:, hc * MIX_HALF:(hc + 1) * MIX_HALF]
                m[chunk, :, hcols] = (sa[:, hcols] * ua + sbh * ub).astype(BF16)

    assert _S_OUT + MIX_NB - 2 < _N_STAGES

    @pl.when(oc == 0)
    def _():
        @pl.when(i == 0)
        def _():
            for t in range(MIX_NB - 1):
                wstart(jnp.int32(t))
            xprefetch(0)

        @pl.when(i % tps == 0)
        def _():
            pcar[...] = jnp.zeros(pcar.shape, F32)
            qcar[...] = jnp.zeros(qcar.shape, F32)

        prologue()
        pl.loop(0, _S_OUT)(main_stage)

    s_out = _S_OUT + oc
    slot = lax.rem(s_out, MIX_NB)
    wwait(slot)
    nxt = s_out + (MIX_NB - 1)

    @pl.when(jnp.logical_or(nxt < _N_STAGES, i < n_tiles - 1))
    def _():
        wstart(jnp.where(nxt < _N_STAGES, nxt, nxt - _N_STAGES))

    @pl.when(jnp.logical_and(oc == 1, i < n_tiles - 1))
    def _():
        xprefetch(i + 1)

    for hc in range(MIX_CH // MIX_HALF):
        hcols = pl.ds(hc * MIX_HALF, MIX_HALF)
        acc = None
        for k in range(_N_OUT_CHUNKS):
            part = _bdot(m[k], wbuf[slot, pl.ds(k * MIX_CH, MIX_CH), hcols].astype(BF16))
            acc = part if acc is None else acc + part
        x2p[:, hcols] = x1pc[:, hcols] + g2pc[:, hcols] * acc[:tm]
        x2s[:, hcols] = x1sc[:, hcols] + g2sc[:, hcols] * acc[tm:]


def _mixer(x1p, x1s, mod_p3, mod_s, nrm, grp, pscale, cw, cbias, stp_t, stc_t, w_in, w_a, w_b, w_o, *, tm, ts, tps):
    n_tiles = x1p.shape[0] // tm
    n_seq = n_tiles // tps
    n_s = x1s.shape[0]
    assert n_s == n_tiles * ts
    rows = tm + ts
    any_spec = pl.BlockSpec(memory_space=pl.ANY)

    def const(shape):
        return pl.BlockSpec(shape, lambda i, s: (0,) * len(shape))

    def oc(s):
        return s

    g2_blk = (MOD_G2 * D_MODEL) // MIX_CH
    in_specs = [
        any_spec,
        pl.BlockSpec((ts, D_MODEL), lambda i, s: (i, 0)),
        pl.BlockSpec((tm, MIX_CH), lambda i, s: (i, oc(s))),
        pl.BlockSpec((ts, MIX_CH), lambda i, s: (i, oc(s))),
        pl.BlockSpec((None, 1, D_MODEL), lambda i, s: (i // tps, 0, MOD_SH2)),
        pl.BlockSpec((None, 1, D_MODEL), lambda i, s: (i // tps, 0, MOD_SC2)),
        pl.BlockSpec((None, 1, MIX_CH), lambda i, s: (i // tps, 0, g2_blk + oc(s))),
        pl.BlockSpec((ts, D_MODEL), lambda i, s: (i, MOD_SH2)),
        pl.BlockSpec((ts, D_MODEL), lambda i, s: (i, MOD_SC2)),
        pl.BlockSpec((ts, MIX_CH), lambda i, s: (i, g2_blk + oc(s))),
        const((1, D_MODEL)),
        const((len(POOL_WINDOWS), POOL_GROUP_DIM, POOL_GROUP_DIM)),
        const((1, POOL_WIDTH)),
        const((3, CONV_WIDTH)),
        const((1, CONV_WIDTH)),
        pl.BlockSpec((POOL_STATE, ts, POOL_WIDTH), lambda i, s: (0, i, 0)),
        pl.BlockSpec((CONV_STATE, ts, CONV_WIDTH), lambda i, s: (0, i, 0)),
        any_spec, any_spec, any_spec, any_spec,
    ]
    out_specs = [
        pl.BlockSpec((tm, MIX_CH), lambda i, s: (i, oc(s))),
        pl.BlockSpec((ts, MIX_CH), lambda i, s: (i, oc(s))),
        pl.BlockSpec((None, CARRY_P, POOL_WIDTH), lambda i, s: (i // tps, 0, 0)),
        pl.BlockSpec((None, CARRY_Q, CONV_WIDTH), lambda i, s: (i // tps, 0, 0)),
        pl.BlockSpec((POOL_STATE, ts, POOL_WIDTH), lambda i, s: (0, i, 0)),
        pl.BlockSpec((CONV_STATE, ts, CONV_WIDTH), lambda i, s: (0, i, 0)),
    ]
    out_shape = [
        jax.ShapeDtypeStruct(x1p.shape, F32),
        jax.ShapeDtypeStruct(x1s.shape, F32),
        jax.ShapeDtypeStruct((n_seq, CARRY_P, POOL_WIDTH), F32),
        jax.ShapeDtypeStruct((n_seq, CARRY_Q, CONV_WIDTH), F32),
        jax.ShapeDtypeStruct((POOL_STATE, n_s, POOL_WIDTH), F32),
        jax.ShapeDtypeStruct((CONV_STATE, n_s, CONV_WIDTH), F32),
    ]
    scratch = [
        pltpu.VMEM((MIX_NB, D_MODEL, MIX_CH), F32),
        pltpu.SemaphoreType.DMA((MIX_NB,)),
        pltpu.VMEM((X_BUFS, X_ROWS, D_MODEL), F32),
        pltpu.SemaphoreType.DMA((X_BUFS,)),
        pltpu.VMEM((rows, D_MODEL), BF16),
        pltpu.VMEM((len(POOL_WINDOWS), rows, POOL_GROUP_DIM), BF16),
        pltpu.VMEM((_N_CONV_CHUNKS, rows, MIX_CH), BF16),
        pltpu.VMEM((_N_OUT_CHUNKS, rows, MIX_CH), BF16),
        pltpu.VMEM((2, rows, MIX_CH), F32),
        pltpu.VMEM((rows, MIX_CH), F32),
        pltpu.VMEM((rows, MIX_CH), F32),
        pltpu.VMEM((rows, POOL_GROUP_DIM), BF16),
        pltpu.VMEM((tm + 2 * CARRY_P, POOL_GROUP_DIM), F32),
        pltpu.VMEM((tm + 2 * CARRY_P, POOL_GROUP_DIM), F32),
        pltpu.VMEM((tm + CARRY_Q, MIX_CH), F32),
        pltpu.VMEM((CARRY_P, POOL_WIDTH), F32),
        pltpu.VMEM((CARRY_Q, CONV_WIDTH), F32),
    ]
    return pl.pallas_call(
        functools.partial(_mixer_kernel, tm=tm, ts=ts, tps=tps, n_tiles=n_tiles),
        grid=(n_tiles, _N_OUT_CHUNKS),
        in_specs=in_specs,
        out_specs=out_specs,
        out_shape=out_shape,
        scratch_shapes=scratch,
        compiler_params=pltpu.CompilerParams(
            dimension_semantics=("arbitrary", "arbitrary"), vmem_limit_bytes=VMEM_LIMIT_BYTES),
        name="mixer",
    )(x1p, x1s, x1p, x1s, mod_p3, mod_p3, mod_p3, mod_s, mod_s, mod_s, nrm, grp, pscale, cw, cbias,
      stp_t, stc_t, w_in, w_a, w_b, w_o)


def kernel(x_prompt, x_sample, c_prompt, c_sample, state_pool, state_conv, w_ada, b_ada, norm1, ffn1_gate, ffn1_up, ffn1_down, norm2, w_in, pool_grp, pool_scale, w_branch_a, conv_w, conv_b, w_branch_b, w_o, norm3, ffn2_gate, ffn2_up, ffn2_down, norm_final):
    assert w_ada.shape[0] == 1, "single-layer step"
    batch, seq, d = x_prompt.shape
    dec_batch = x_sample.shape[0]
    assert d == D_MODEL and x_sample.shape[1] == 1

    tm = TM_PROMPT
    tps = seq // tm
    n_tiles = batch * tps
    ts = dec_batch // n_tiles
    assert seq % tm == 0 and dec_batch % n_tiles == 0 and ts % (2 * SUBLANES) == 0

    n_c = batch + dec_batch
    pad = (-n_c) % SUBLANES
    c_all = jnp.concatenate([c_sample, c_prompt, jnp.zeros((pad, d), F32)], axis=0)
    mod = _ada(c_all, w_ada[0], b_ada)
    mod_p3 = mod[dec_batch:n_c].reshape(batch, 1, N_MOD * d)

    row = lambda a: a.reshape(1, -1)
    xp = x_prompt.reshape(batch * seq, d)
    xs = x_sample.reshape(dec_batch, d)
    nf = row(norm_final)
    x1p, x1s = _ffn(xp, xs, mod_p3, mod, MOD_SH1, MOD_SC1, MOD_G1, row(norm1[0]), ffn1_gate[0], ffn1_up[0],
                    ffn1_down[0], nf, tm=tm, ts=ts, tps=tps, final_norm=False, name="ffn1")
    stp_t = jnp.transpose(state_pool[0], (1, 0, 2))
    stc_t = jnp.transpose(state_conv[0], (1, 0, 2))
    x2p, x2s, pst, cst, nps_t, ncs_t = _mixer(
        x1p, x1s, mod_p3, mod, row(norm2[0]), pool_grp[0], row(pool_scale[0]), conv_w[0], row(conv_b[0]),
        stp_t, stc_t, w_in[0], w_branch_a[0], w_branch_b[0], w_o[0], tm=tm, ts=ts, tps=tps)
    yp, ys = _ffn(x2p, x2s, mod_p3, mod, MOD_SH3, MOD_SC3, MOD_G3, row(norm3[0]), ffn2_gate[0], ffn2_up[0],
                  ffn2_down[0], nf, tm=tm, ts=ts, tps=tps, final_norm=True, name="ffn2")

    y_prompt = yp.reshape(batch, seq, d)
    y_sample = ys.reshape(dec_batch, 1, d)
    new_pool_prompt = pst[None, :, CARRY_P - POOL_STATE:, :]
    new_conv_prompt = cst[None, :, CARRY_Q - CONV_STATE:, :]
    new_pool_sample = jnp.transpose(nps_t, (1, 0, 2))[None]
    new_conv_sample = jnp.transpose(ncs_t, (1, 0, 2))[None]
    return (y_prompt, y_sample, new_pool_prompt, new_conv_prompt, new_pool_sample, new_conv_sample)
```
